```python
import math
import jax, jax.numpy as jnp
from jax import lax
import numpy as np

D_MODEL = 1024
BATCH = 4
SEQ = 8192
DEPTH = 2

ROPE_THETA = 500000.0
RMS_EPS = 1e-6
Q_BLOCK = 128
CONV_WIDTH = 3
D_CONV = D_MODEL // 2
MLA_HEADS = 8
MLA_NOPE = 64
MLA_ROPE = 32
MLA_V = 64
MLA_Q_LORA = 3 * D_MODEL // 8
MLA_KV_LORA = D_MODEL // 4
NSA_HEADS = 8
NSA_GROUPS = 2
NSA_HPG = NSA_HEADS // NSA_GROUPS
NSA_DIM = 64
NSA_ROT = NSA_DIM // 4
CMP_LEN = 32
CMP_STRIDE = 16
SEL_LEN = 64
N_SEL = 16
WINDOW = 512
FORCED_SCORE = 1e6
D_FF = 2816
IN_SPLITS = (D_MODEL, D_MODEL, D_MODEL,
             D_CONV, D_CONV, D_CONV,
             MLA_Q_LORA, MLA_KV_LORA, MLA_ROPE,
             NSA_HEADS * NSA_DIM,
             NSA_GROUPS * NSA_DIM, NSA_GROUPS * NSA_DIM,
             NSA_GROUPS * NSA_DIM, NSA_GROUPS * NSA_DIM,
             NSA_GROUPS * NSA_DIM, NSA_GROUPS * NSA_DIM,
             3 * NSA_HEADS)
N_IN = sum(IN_SPLITS)
SPLIT_POINTS = tuple(int(v) for v in np.cumsum(IN_SPLITS)[:-1])

kernel_name = 'hybrid_conv_mla_nsa_block'


def rms_norm(x, g):
    x32 = x.astype(jnp.float32)
    y = x32 * lax.rsqrt(jnp.mean(x32 * x32, axis=-1, keepdims=True) + RMS_EPS)
    return (y * g.astype(jnp.float32)).astype(x.dtype)


def masked_softmax(s, mask):
    s = jnp.where(mask, s.astype(jnp.float32), -1e30)
    return jnp.where(mask, jax.nn.softmax(s, axis=-1), 0.0)


def rope(x, positions, rot_dim):
    half = rot_dim // 2
    inv_freq = ROPE_THETA ** (-jnp.arange(half, dtype=jnp.float32) / half)
    ang = positions.astype(jnp.float32)[..., None] * inv_freq
    cos = jnp.cos(ang)[:, :, None, :]
    sin = jnp.sin(ang)[:, :, None, :]
    x1 = x[..., :half].astype(jnp.float32)
    x2 = x[..., half:rot_dim].astype(jnp.float32)
    rot = jnp.concatenate([x1 * cos - x2 * sin, x2 * cos + x1 * sin], axis=-1).astype(x.dtype)
    return jnp.concatenate([rot, x[..., rot_dim:]], axis=-1)


def causal_dwconv(u, w, b=None):
    S = u.shape[1]
    up = jnp.pad(u, ((0, 0), (CONV_WIDTH - 1, 0), (0, 0)))
    y = up[:, 0:S] * w[0]
    for k in range(1, CONV_WIDTH):
        y = y + up[:, k:k + S] * w[k]
    return y if b is None else y + b


def short_conv_mixer(b_gate, c_gate, x_in, conv_w):
    return b_gate * causal_dwconv(c_gate * x_in, conv_w)


def mla_mixer(c_q, c_kv, k_r, positions, q_norm, w_uq, kv_norm, w_ukv):
    B, S, _ = c_q.shape
    q = (rms_norm(c_q, q_norm) @ w_uq).reshape(B, S, MLA_HEADS, MLA_NOPE + MLA_ROPE)
    q_nope = q[..., :MLA_NOPE]
    q_rope = rope(q[..., MLA_NOPE:], positions, MLA_ROPE)
    kv = (rms_norm(c_kv, kv_norm) @ w_ukv).reshape(B, S, MLA_HEADS, MLA_NOPE + MLA_V)
    k_nope = kv[..., :MLA_NOPE]
    v = kv[..., MLA_NOPE:]
    k_rope = rope(k_r[:, :, None, :], positions, MLA_ROPE)[:, :, 0, :]
    scale = (MLA_NOPE + MLA_ROPE) ** -0.5
    kpos = jnp.arange(S)

    def block(i):
        s0 = i * Q_BLOCK
        qn = lax.dynamic_slice_in_dim(q_nope, s0, Q_BLOCK, axis=1)
        qr = lax.dynamic_slice_in_dim(q_rope, s0, Q_BLOCK, axis=1)
        s = (jnp.einsum('bqhd,bkhd->bhqk', qn, k_nope)
             + jnp.einsum('bqhd,bkd->bhqk', qr, k_rope)) * scale
        t = s0 + jnp.arange(Q_BLOCK)
        p = masked_softmax(s, kpos[None, :] <= t[:, None])
        return jnp.einsum('bhqk,bkhd->bqhd', p.astype(v.dtype), v)

    o = lax.map(block, jnp.arange(S // Q_BLOCK))
    return o.transpose(1, 0, 2, 3, 4).reshape(B, S, MLA_HEADS * MLA_V)


def nsa_mixer(q, k_cmp, v_cmp, k_slc, v_slc, k_win, v_win, gate_logits, positions,
              pos_k, pos_v, w_ck, w_cv):
    B, S, _ = q.shape
    G, J, dk = NSA_GROUPS, NSA_HPG, NSA_DIM
    q = rope(q.reshape(B, S, NSA_HEADS, dk), positions, NSA_ROT).reshape(B, S, G, J, dk)

    def kv_heads(t):
        return t.reshape(B, S, G, dk)

    k_cmp = rope(kv_heads(k_cmp), positions, NSA_ROT)
    k_slc = rope(kv_heads(k_slc), positions, NSA_ROT)
    k_win = rope(kv_heads(k_win), positions, NSA_ROT)
    v_cmp, v_slc, v_win = kv_heads(v_cmp), kv_heads(v_slc), kv_heads(v_win)

    n_cmp = (S - CMP_LEN) // CMP_STRIDE + 1
    idx = jnp.arange(n_cmp)[:, None] * CMP_STRIDE + jnp.arange(CMP_LEN)[None, :]

    def compress(t, pos, w):
        blocks = t[:, idx] + pos[None, None, :, None, :]
        return jnp.einsum('bnrgd,rde->bnge', blocks, w.reshape(CMP_LEN, dk, dk))

    kc = compress(k_cmp, pos_k, w_ck)
    vc = compress(v_cmp, pos_v, w_cv)
    cmp_end = jnp.arange(n_cmp) * CMP_STRIDE + CMP_LEN - 1

    n_sel = S // SEL_LEN
    n_top = min(N_SEL, n_sel)
    kb = k_slc.reshape(B, n_sel, SEL_LEN, G, dk).transpose(0, 3, 1, 2, 4)
    vb = v_slc.reshape(B, n_sel, SEL_LEN, G, dk).transpose(0, 3, 1, 2, 4)
    cmp_start = jnp.arange(n_cmp)[:, None] * CMP_STRIDE
    sel_start = jnp.arange(n_sel)[None, :] * SEL_LEN
    overlap = ((cmp_start <= sel_start + SEL_LEN - 1)
               & (cmp_start + CMP_LEN - 1 >= sel_start)).astype(jnp.float32)
    sel_id = jnp.arange(n_sel)
    bi = jnp.arange(B)[:, None, None, None]
    gi = jnp.arange(G)[None, :, None, None]

    pad = ((0, 0), (WINDOW, 0), (0, 0), (0, 0))
    kw = jnp.pad(k_win, pad)
    vw = jnp.pad(v_win, pad)

    gates = jax.nn.sigmoid(gate_logits.reshape(B, S, G, J, 3))
    scale = dk ** -0.5

    def block(i):
        s0 = i * Q_BLOCK
        qb = lax.dynamic_slice_in_dim(q, s0, Q_BLOCK, axis=1)
        gb = lax.dynamic_slice_in_dim(gates, s0, Q_BLOCK, axis=1)
        t = s0 + jnp.arange(Q_BLOCK)
        sc = jnp.einsum('bqgjd,bngd->bgjqn', qb, kc) * scale
        pc = masked_softmax(sc, cmp_end[None, :] <= t[:, None])
        o_cmp = jnp.einsum('bgjqn,bngd->bqgjd', pc.astype(vc.dtype), vc)
        imp = jnp.einsum('bgjqn,nm->bgqm', pc, overlap)
        cur = t // SEL_LEN
        forced = ((sel_id[None, :] == 0) | (sel_id[None, :] == cur[:, None])
                  | (sel_id[None, :] == cur[:, None] - 1))
        causal = sel_id[None, :] * SEL_LEN <= t[:, None]
        imp = jnp.where(forced, FORCED_SCORE, jnp.where(causal, imp, -1.0))
        top_val, top_idx = lax.top_k(imp, n_top)
        ks = kb[bi, gi, top_idx]
        vs = vb[bi, gi, top_idx]
        kpos = top_idx[..., None] * SEL_LEN + jnp.arange(SEL_LEN)
        smask = (top_val >= 0)[..., None] & (kpos <= t[None, None, :, None, None])
        ss = jnp.einsum('bqgjd,bgqnrd->bgjqnr', qb, ks) * scale
        ps = masked_softmax(ss.reshape(B, G, J, Q_BLOCK, n_top * SEL_LEN),
                            smask.reshape(B, G, 1, Q_BLOCK, n_top * SEL_LEN)).reshape(ss.shape)
        o_slc = jnp.einsum('bgjqnr,bgqnrd->bqgjd', ps.astype(vs.dtype), vs)
        kwb = lax.dynamic_slice_in_dim(kw, s0, Q_BLOCK + WINDOW, axis=1)
        vwb = lax.dynamic_slice_in_dim(vw, s0, Q_BLOCK + WINDOW, axis=1)
        kp = s0 - WINDOW + jnp.arange(Q_BLOCK + WINDOW)
        wmask = ((kp[None, :] >= 0) & (kp[None, :] <= t[:, None])
                 & (kp[None, :] > t[:, None] - WINDOW))
        sw = jnp.einsum('bqgjd,bkgd->bgjqk', qb, kwb) * scale
        pw = masked_softmax(sw, wmask)
        o_win = jnp.einsum('bgjqk,bkgd->bqgjd', pw.astype(vwb.dtype), vwb)
        o = gb[..., 0:1] * o_cmp + gb[..., 1:2] * o_slc + gb[..., 2:3] * o_win
        return o.reshape(B, Q_BLOCK, NSA_HEADS * dk)

    o = lax.map(block, jnp.arange(S // Q_BLOCK))
    return o.transpose(1, 0, 2, 3).reshape(B, S, NSA_HEADS * dk)


def gated_conv_ffn(h, w_up, conv_w, conv_b, w_down):
    a, b = jnp.split(h @ w_up, 2, axis=-1)
    a = causal_dwconv(a, conv_w, conv_b)
    return (jax.nn.gelu(a) * b) @ w_down


def setup_inputs(seed: int = 0) -> dict:
    key = jax.random.key(seed)
    keys = jax.random.split(key, 24)

    def dense(k, shape, fan_in):
        return jax.random.normal(k, shape, jnp.float32) * fan_in ** -0.5

    def gain(k, n):
        return 1.0 + 0.05 * jax.random.normal(k, (DEPTH, n), jnp.float32)

    x = jax.random.normal(keys[0], (BATCH, SEQ, D_MODEL), jnp.float32)
    positions = jnp.broadcast_to(jnp.arange(SEQ, dtype=jnp.int32)[None, :], (BATCH, SEQ))
    return {
        'x': x,
        'positions': positions,
        'norm_mix_pre': gain(keys[1], D_MODEL),
        'norm_mix_post': gain(keys[2], D_MODEL),
        'w_in': dense(keys[3], (DEPTH, D_MODEL, N_IN), D_MODEL),
        'conv_w': dense(keys[4], (DEPTH, CONV_WIDTH, D_CONV), CONV_WIDTH),
        'mla_q_norm': gain(keys[5], MLA_Q_LORA),
        'mla_w_uq': dense(keys[6], (DEPTH, MLA_Q_LORA, MLA_HEADS * (MLA_NOPE + MLA_ROPE)), MLA_Q_LORA),
        'mla_kv_norm': gain(keys[7], MLA_KV_LORA),
        'mla_w_ukv': dense(keys[8], (DEPTH, MLA_KV_LORA, MLA_HEADS * (MLA_NOPE + MLA_V)), MLA_KV_LORA),
        'nsa_cmp_pos_k': 0.1 * jax.random.normal(keys[9], (DEPTH, CMP_LEN, NSA_DIM), jnp.float32),
        'nsa_cmp_pos_v': 0.1 * jax.random.normal(keys[10], (DEPTH, CMP_LEN, NSA_DIM), jnp.float32),
        'nsa_cmp_w_k': dense(keys[11], (DEPTH, CMP_LEN * NSA_DIM, NSA_DIM), CMP_LEN * NSA_DIM),
        'nsa_cmp_w_v': dense(keys[12], (DEPTH, CMP_LEN * NSA_DIM, NSA_DIM), CMP_LEN * NSA_DIM),
        'w_branch_conv': dense(keys[13], (DEPTH, D_CONV, D_MODEL), D_CONV),
        'w_branch_mla': dense(keys[14], (DEPTH, MLA_HEADS * MLA_V, D_MODEL), MLA_HEADS * MLA_V),
        'w_branch_nsa': dense(keys[15], (DEPTH, NSA_HEADS * NSA_DIM, D_MODEL), NSA_HEADS * NSA_DIM),
        'w_out': dense(keys[16], (DEPTH, D_MODEL, D_MODEL), D_MODEL),
        'norm_ffn_pre': gain(keys[17], D_MODEL),
        'norm_ffn_post': gain(keys[18], D_MODEL),
        'ffn_w_up': dense(keys[19], (DEPTH, D_MODEL, 2 * D_FF), D_MODEL),
        'ffn_conv_w': dense(keys[20], (DEPTH, CONV_WIDTH, D_FF), CONV_WIDTH),
        'ffn_conv_b': 0.01 * jax.random.normal(keys[21], (DEPTH, D_FF), jnp.float32),
        'ffn_w_down': dense(keys[22], (DEPTH, D_FF, D_MODEL), D_FF),
    }


def reference(x, positions, norm_mix_pre, norm_mix_post, w_in, conv_w, mla_q_norm, mla_w_uq,
              mla_kv_norm, mla_w_ukv, nsa_cmp_pos_k, nsa_cmp_pos_v, nsa_cmp_w_k, nsa_cmp_w_v,
              w_branch_conv, w_branch_mla, w_branch_nsa, w_out, norm_ffn_pre, norm_ffn_post,
              ffn_w_up, ffn_conv_w, ffn_conv_b, ffn_w_down):
    for l in range(DEPTH):
        h = rms_norm(x, norm_mix_pre[l])
        (g_conv, g_mla, g_nsa, c_b, c_c, c_x, mla_cq, mla_ckv, mla_kr,
         nsa_q, nsa_kc, nsa_vc, nsa_ks, nsa_vs, nsa_kw, nsa_vw, nsa_g) = jnp.split(
            h @ w_in[l], SPLIT_POINTS, axis=-1)
        y_conv = short_conv_mixer(c_b, c_c, c_x, conv_w[l])
        y_mla = mla_mixer(mla_cq, mla_ckv, mla_kr, positions, mla_q_norm[l], mla_w_uq[l],
                          mla_kv_norm[l], mla_w_ukv[l])
        y_nsa = nsa_mixer(nsa_q, nsa_kc, nsa_vc, nsa_ks, nsa_vs, nsa_kw, nsa_vw, nsa_g, positions,
                          nsa_cmp_pos_k[l], nsa_cmp_pos_v[l], nsa_cmp_w_k[l], nsa_cmp_w_v[l])
        merged = (jax.nn.sigmoid(g_conv) * (y_conv @ w_branch_conv[l])
                  + jax.nn.sigmoid(g_mla) * (y_mla @ w_branch_mla[l])
                  + jax.nn.sigmoid(g_nsa) * (y_nsa @ w_branch_nsa[l]))
        x = x + rms_norm(merged @ w_out[l], norm_mix_post[l])
        h = rms_norm(x, norm_ffn_pre[l])
        x = x + rms_norm(gated_conv_ffn(h, ffn_w_up[l], ffn_conv_w[l], ffn_conv_b[l], ffn_w_down[l]),
                         norm_ffn_post[l])
    return x
```

```python
import functools

import numpy as np
import jax
import jax.numpy as jnp
from jax import lax
from jax.experimental import pallas as pl
from jax.experimental.pallas import tpu as pltpu

F32 = jnp.float32
BF16 = jnp.bfloat16

ROPE_THETA = 500000.0
RMS_EPS = 1e-6
CONV_WIDTH = 3
MLA_HEADS = 8
MLA_NOPE = 64
MLA_ROPE = 32
MLA_V = 64
NSA_HEADS = 8
NSA_GROUPS = 2
NSA_HPG = NSA_HEADS // NSA_GROUPS
NSA_DIM = 64
NSA_ROT = NSA_DIM // 4
CMP_LEN = 32
CMP_STRIDE = 16
SEL_LEN = 64
N_SEL = 16
WINDOW = 512

LANES = 128
SUBLANES = 8
VMEM_LIMIT = 56 * 1024 * 1024
MASKED = -1e30
M_FLOOR = -1e20


def _cparams(sem):
    return pltpu.CompilerParams(dimension_semantics=sem, vmem_limit_bytes=VMEM_LIMIT)


def _const_spec(shape):
    nd = len(shape)
    return pl.BlockSpec(shape, lambda *_: (0,) * nd, pipeline_mode=pl.Buffered(1))


def _rms(x, g):
    return x * lax.rsqrt(jnp.mean(x * x, axis=-1, keepdims=True) + RMS_EPS) * g


def _dot(a, b):
    return jnp.dot(a, b, preferred_element_type=F32)


def _dot_nt(a, b):
    return lax.dot_general(a, b, (((1,), (1,)), ((), ())), preferred_element_type=F32)


def _dot_split(a, b, terms):
    out = None
    rem = a
    for _ in range(terms):
        piece = rem.astype(BF16)
        rem = rem - piece.astype(F32)
        d = _dot(piece, b)
        out = d if out is None else out + d
    return out


def _rope(x, cos, sin_up, sin_dn, half):
    return (x * cos + pltpu.roll(x, half, axis=1) * sin_up
            + pltpu.roll(x, LANES - half, axis=1) * sin_dn)


def _proj_kernel(x_ref, pos_ref, g_ref, watt_ref, qg_ref, wuq_ref, kvg_ref, wukv_ref, rc_ref,
                 qm_ref, km_ref, vm_ref, nq_ref, kc_ref, vc_ref, ks_ref, vs_ref, kw_ref, vw_ref,
                 gl_ref, *, mla_scale, nsa_scale):
    x = x_ref[...]
    h = _rms(x, g_ref[...]).astype(BF16)
    p = _dot(h, watt_ref[...])
    pos = pos_ref[...].astype(F32)
    rc = rc_ref[...]
    ang_m = pos * rc[0:1]
    cos_m, sin_m = jnp.cos(ang_m), jnp.sin(ang_m)
    sup_m, sdn_m = sin_m * rc[1:2], sin_m * rc[2:3]
    ang_n = pos * rc[3:4]
    cos_n, sin_n = jnp.cos(ang_n), jnp.sin(ang_n)
    sup_n, sdn_n = sin_n * rc[4:5], sin_n * rc[5:6]
    lane = lax.broadcasted_iota(jnp.int32, (x.shape[0], LANES), 1)

    cqn = _rms(p[:, 0:384], qg_ref[...]).astype(BF16)
    qf = _dot(cqn, wuq_ref[...])
    for hd in range(MLA_HEADS):
        slab = qf[:, hd * LANES:(hd + 1) * LANES]
        slab = _rope(slab, cos_m, sup_m, sdn_m, MLA_ROPE // 2) * mla_scale
        qm_ref[hd] = slab.astype(BF16)
    ckvn = _rms(p[:, 384:640], kvg_ref[...]).astype(BF16)
    kv = _dot(ckvn, wukv_ref[...])
    kr = _rope(p[:, 640:768], cos_m, sup_m, sdn_m, MLA_ROPE // 2)
    for hd in range(MLA_HEADS):
        slab = kv[:, hd * LANES:(hd + 1) * LANES]
        km_ref[hd] = jnp.where(lane < MLA_NOPE, slab, kr).astype(BF16)
        vm_ref[hd] = pltpu.roll(slab, LANES - MLA_NOPE, axis=1)[:, :MLA_V].astype(BF16)
    for pr in range(NSA_HEADS // 2):
        slab = p[:, 768 + pr * LANES:768 + (pr + 1) * LANES]
        slab = (_rope(slab, cos_n, sup_n, sdn_n, NSA_ROT // 2) * nsa_scale)
        nq_ref[2 * pr] = slab[:, :NSA_DIM].astype(BF16)
        nq_ref[2 * pr + 1] = pltpu.roll(slab, NSA_DIM, axis=1)[:, :NSA_DIM].astype(BF16)
    for idx, (ref, roped) in enumerate(((kc_ref, True), (vc_ref, False), (ks_ref, True),
                                        (vs_ref, False), (kw_ref, True), (vw_ref, False))):
        slab = p[:, 1280 + idx * LANES:1280 + (idx + 1) * LANES]
        if roped:
            slab = _rope(slab, cos_n, sup_n, sdn_n, NSA_ROT // 2)
        ref[0] = slab[:, :NSA_DIM].astype(BF16)
        ref[1] = pltpu.roll(slab, NSA_DIM, axis=1)[:, :NSA_DIM].astype(BF16)
    gl_ref[...] = p[:, 2048:2176]


def _rope_consts():
    rc = np.zeros((SUBLANES, LANES), np.float32)
    hm = MLA_ROPE // 2
    fm = ROPE_THETA ** (-np.arange(hm, dtype=np.float32) / hm)
    rc[0, MLA_NOPE:MLA_NOPE + hm] = fm
    rc[0, MLA_NOPE + hm:MLA_NOPE + 2 * hm] = fm
    rc[1, MLA_NOPE + hm:MLA_NOPE + 2 * hm] = 1.0
    rc[2, MLA_NOPE:MLA_NOPE + hm] = -1.0
    hn = NSA_ROT // 2
    fn = ROPE_THETA ** (-np.arange(hn, dtype=np.float32) / hn)
    for base in (0, NSA_DIM):
        rc[3, base:base + hn] = fn
        rc[3, base + hn:base + 2 * hn] = fn
        rc[4, base + hn:base + 2 * hn] = 1.0
        rc[5, base:base + hn] = -1.0
    return rc


def _project(x2, pos2, g_pre, w_att, q_g, w_uq, kv_g, w_ukv, rc, tm):
    T, D = x2.shape
    n_att = w_att.shape[1]
    hs = lambda n, d: jax.ShapeDtypeStruct((n, T, d), BF16)
    hspec = lambda n, d: pl.BlockSpec((n, tm, d), lambda i: (0, i, 0))
    kern = functools.partial(_proj_kernel,
                             mla_scale=float((MLA_NOPE + MLA_ROPE) ** -0.5),
                             nsa_scale=float(NSA_DIM ** -0.5))
    return pl.pallas_call(
        kern,
        grid=(T // tm,),
        in_specs=[
            pl.BlockSpec((tm, D), lambda i: (i, 0)),
            pl.BlockSpec((tm, 1), lambda i: (i, 0)),
            _const_spec((1, D)),
            _const_spec((D, n_att)),
            _const_spec((1, q_g.shape[1])),
            _const_spec(w_uq.shape),
            _const_spec((1, kv_g.shape[1])),
            _const_spec(w_ukv.shape),
            _const_spec((SUBLANES, LANES)),
        ],
        out_specs=[hspec(MLA_HEADS, LANES), hspec(MLA_HEADS, LANES), hspec(MLA_HEADS, MLA_V),
                   hspec(NSA_HEADS, NSA_DIM)] + [hspec(NSA_GROUPS, NSA_DIM)] * 6
                  + [pl.BlockSpec((tm, LANES), lambda i: (i, 0))],
        out_shape=[hs(MLA_HEADS, LANES), hs(MLA_HEADS, LANES), hs(MLA_HEADS, MLA_V),
                   hs(NSA_HEADS, NSA_DIM)] + [hs(NSA_GROUPS, NSA_DIM)] * 6
                  + [jax.ShapeDtypeStruct((T, LANES), F32)],
        compiler_params=_cparams(("parallel",)),
        name="proj",
    )(x2, pos2, g_pre, w_att, q_g, w_uq, kv_g, w_ukv, rc)


def _compress_kernel(xk_ref, xv_ref, pk_ref, pv_ref, wk_ref, wv_ref, ok_ref, ov_ref):
    for x_ref, p_ref, w_ref, o_ref in ((xk_ref, pk_ref, wk_ref, ok_ref),
                                       (xv_ref, pv_ref, wv_ref, ov_ref)):
        x = x_ref[0].astype(F32)
        n = x.shape[0]
        lo = _dot((x + p_ref[0:1]).astype(BF16), w_ref[0])
        hi = _dot((x + p_ref[1:2]).astype(BF16), w_ref[1])
        o_ref[0] = (lo + pltpu.roll(hi, n - 1, axis=0)).astype(BF16)


def _compress(xk, xv, pk, pv, wk, wv, B):
    G, TC, CW = xk.shape
    nch = TC // B
    xspec = pl.BlockSpec((1, nch, CW), lambda g, b: (g, b, 0))
    ospec = pl.BlockSpec((1, nch, NSA_DIM), lambda g, b: (g, b, 0))
    oshape = jax.ShapeDtypeStruct((G, TC, NSA_DIM), BF16)
    return pl.pallas_call(
        _compress_kernel,
        grid=(G, B),
        in_specs=[xspec, xspec, _const_spec(pk.shape), _const_spec(pv.shape),
                  _const_spec(wk.shape), _const_spec(wv.shape)],
        out_specs=[ospec, ospec],
        out_shape=[oshape, oshape],
        compiler_params=_cparams(("parallel", "parallel")),
        name="compress",
    )(xk, xv, pk, pv, wk, wv)


def _cmp_select_kernel(q_ref, kc_ref, vc_ref, ov_ref, o_ref, sel_ref, *, tq, n_pick):
    q0 = pl.program_id(1) * tq
    nch = kc_ref.shape[1]
    nblk = ov_ref.shape[1]
    t_c = q0 + lax.broadcasted_iota(jnp.int32, (tq, nch), 0)
    n_c = lax.broadcasted_iota(jnp.int32, (tq, nch), 1)
    cmask = (n_c * CMP_STRIDE + (CMP_LEN - 1)) <= t_c
    t_b = q0 + lax.broadcasted_iota(jnp.int32, (tq, nblk), 0)
    lane = lax.broadcasted_iota(jnp.int32, (tq, nblk), 1)
    lane_f = lane.astype(F32)
    cur = lax.shift_right_logical(t_b, int(np.log2(SEL_LEN)))
    forced = (lane == 0) | (lane == cur) | (lane == cur - 1)
    causal = lane * SEL_LEN <= t_b
    for g in range(NSA_GROUPS):
        q = q_ref[g * NSA_HPG:(g + 1) * NSA_HPG].reshape(NSA_HPG * tq, NSA_DIM)
        s = _dot_nt(q, kc_ref[g]).reshape(NSA_HPG, tq, nch)
        s = jnp.where(cmask[None], s, MASKED)
        m = jnp.max(s, axis=-1, keepdims=True)
        p = jnp.where(cmask[None], jnp.exp(s - m), 0.0)
        l = jnp.sum(p, axis=-1, keepdims=True)
        pn = p * jnp.where(l > 0.0, 1.0 / l, 0.0)
        o = _dot(pn.reshape(NSA_HPG * tq, nch).astype(BF16), vc_ref[g])
        for j in range(NSA_HPG):
            hd = g * NSA_HPG + j
            o_ref[:, hd * NSA_DIM:(hd + 1) * NSA_DIM] = o[j * tq:(j + 1) * tq].astype(BF16)
        imp = _dot_split(jnp.sum(pn, axis=0), ov_ref[...], 3)
        v = jnp.where(forced | jnp.logical_not(causal), -1.0, imp)
        sel = jnp.where(forced, 1.0, 0.0)
        for _ in range(n_pick):
            mx = jnp.max(v, axis=-1, keepdims=True)
            first = jnp.min(jnp.where(v == mx, lane_f, float(nblk)), axis=-1, keepdims=True)
            hit = lane_f == first
            sel = jnp.where(hit & (mx >= 0.0), 1.0, sel)
            v = jnp.where(hit, -2.0, v)
        sel_ref[g] = sel.astype(BF16)


def _cmp_select(nq, kc, vc, ov, B, S, tq):
    T = nq.shape[1]
    nch = kc.shape[1] // B
    nblk = ov.shape[1]
    nqt = S // tq
    kern = functools.partial(_cmp_select_kernel, tq=tq, n_pick=N_SEL - 3)
    return pl.pallas_call(
        kern,
        grid=(B, nqt),
        in_specs=[
            pl.BlockSpec((NSA_HEADS, tq, NSA_DIM), lambda b, i: (0, b * nqt + i, 0)),
            pl.BlockSpec((NSA_GROUPS, nch, NSA_DIM), lambda b, i: (0, b, 0)),
            pl.BlockSpec((NSA_GROUPS, nch, NSA_DIM), lambda b, i: (0, b, 0)),
            _const_spec(ov.shape),
        ],
        out_specs=[pl.BlockSpec((tq, NSA_HEADS * NSA_DIM), lambda b, i: (b * nqt + i, 0)),
                   pl.BlockSpec((NSA_GROUPS, tq, nblk), lambda b, i: (0, b * nqt + i, 0))],
        out_shape=[jax.ShapeDtypeStruct((T, NSA_HEADS * NSA_DIM), BF16),
                   jax.ShapeDtypeStruct((NSA_GROUPS, T, nblk), BF16)],
        compiler_params=_cparams(("parallel", "parallel")),
        name="cmp_select",
    )(nq, kc, vc, ov)


def _flash_update(q, k, v, mask, rows, m_ref, l_ref, acc_ref):
    s = _dot_nt(q, k)
    if mask is not None:
        tq, tk = mask.shape
        s = jnp.where(mask[None], s.reshape(-1, tq, tk), MASKED).reshape(s.shape)
    m_old = m_ref[rows]
    m_new = jnp.maximum(m_old, jnp.max(s, axis=-1, keepdims=True))
    alpha = jnp.exp(m_old - m_new)
    p = jnp.exp(s - m_new)
    l_ref[rows] = alpha * l_ref[rows] + jnp.sum(p, axis=-1, keepdims=True)
    acc_ref[rows] = alpha * acc_ref[rows] + _dot(p.astype(BF16), v)
    m_ref[rows] = m_new


def _flash_init(m_ref, l_ref, acc_ref):
    m_ref[...] = jnp.full(m_ref.shape, M_FLOOR, F32)
    l_ref[...] = jnp.zeros(l_ref.shape, F32)
    acc_ref[...] = jnp.zeros(acc_ref.shape, F32)


def _mla_kernel(q_ref, k_ref, v_ref, o_ref, m_ref, l_ref, acc_ref, *, tq):
    i, j = pl.program_id(1), pl.program_id(2)

    @pl.when(j == 0)
    def _():
        _flash_init(m_ref, l_ref, acc_ref)

    def step(masked):
        mask = None
        if masked:
            r = lax.broadcasted_iota(jnp.int32, (tq, tq), 0)
            c = lax.broadcasted_iota(jnp.int32, (tq, tq), 1)
            mask = c <= r
        for hd in range(MLA_HEADS):
            rows = pl.ds(hd * tq, tq)
            _flash_update(q_ref[hd], k_ref[hd], v_ref[hd], mask, rows, m_ref, l_ref, acc_ref)

    @pl.when(j < i)
    def _():
        step(False)

    @pl.when(j == i)
    def _():
        step(True)
        for hd in range(MLA_HEADS):
            rows = pl.ds(hd * tq, tq)
            o_ref[:, hd * MLA_V:(hd + 1) * MLA_V] = (acc_ref[rows] / l_ref[rows]).astype(BF16)


def _mla_attention(qm, km, vm, B, S, tq):
    T = qm.shape[1]
    nqt = S // tq
    kern = functools.partial(_mla_kernel, tq=tq)
    qmap = lambda b, i, j: (0, b * nqt + i, 0)
    kmap = lambda b, i, j: (0, b * nqt + jnp.minimum(i, j), 0)
    return pl.pallas_call(
        kern,
        grid=(B, nqt, nqt),
        in_specs=[pl.BlockSpec((MLA_HEADS, tq, LANES), qmap),
                  pl.BlockSpec((MLA_HEADS, tq, LANES), kmap),
                  pl.BlockSpec((MLA_HEADS, tq, MLA_V), kmap)],
        out_specs=pl.BlockSpec((tq, MLA_HEADS * MLA_V), lambda b, i, j: (b * nqt + i, 0)),
        out_shape=jax.ShapeDtypeStruct((T, MLA_HEADS * MLA_V), BF16),
        scratch_shapes=[pltpu.VMEM((MLA_HEADS * tq, 1), F32),
                        pltpu.VMEM((MLA_HEADS * tq, 1), F32),
                        pltpu.VMEM((MLA_HEADS * tq, MLA_V), F32)],
        compiler_params=_cparams(("parallel", "parallel", "arbitrary")),
        name="mla_attn",
    )(qm, km, vm)


def _nsa_finalize(o_ref, l_ref, acc_ref, tq):
    for g in range(NSA_GROUPS):
        for jh in range(NSA_HPG):
            hd = g * NSA_HPG + jh
            rows = pl.ds(hd * tq, tq)
            o_ref[:, hd * NSA_DIM:(hd + 1) * NSA_DIM] = (acc_ref[rows] / l_ref[rows]).astype(BF16)


def _slc_kernel(q_ref, k_ref, v_ref, sel_ref, e_ref, o_ref, m_ref, l_ref, acc_ref, *, tq):
    i, j = pl.program_id(1), pl.program_id(2)

    @pl.when(j == 0)
    def _():
        _flash_init(m_ref, l_ref, acc_ref)

    def step(diag):
        for g in range(NSA_GROUPS):
            mask = _dot(sel_ref[g], e_ref[...]) > 0.5
            if diag:
                r = lax.broadcasted_iota(jnp.int32, (tq, tq), 0)
                c = lax.broadcasted_iota(jnp.int32, (tq, tq), 1)
                mask = mask & (c <= r)
            q = q_ref[g * NSA_HPG:(g + 1) * NSA_HPG].reshape(NSA_HPG * tq, NSA_DIM)
            rows = pl.ds(g * NSA_HPG * tq, NSA_HPG * tq)
            _flash_update(q, k_ref[g], v_ref[g], mask, rows, m_ref, l_ref, acc_ref)

    @pl.when(j < i)
    def _():
        step(False)

    @pl.when(j == i)
    def _():
        step(True)
        _nsa_finalize(o_ref, l_ref, acc_ref, tq)


def _slc_attention(nq, ks, vs, sel, e, B, S, tq):
    T = nq.shape[1]
    nqt = S // tq
    nblk = sel.shape[2]
    kern = functools.partial(_slc_kernel, tq=tq)
    qmap = lambda b, i, j: (0, b * nqt + i, 0)
    kmap = lambda b, i, j: (0, b * nqt + jnp.minimum(i, j), 0)
    return pl.pallas_call(
        kern,
        grid=(B, nqt, nqt),
        in_specs=[pl.BlockSpec((NSA_HEADS, tq, NSA_DIM), qmap),
                  pl.BlockSpec((NSA_GROUPS, tq, NSA_DIM), kmap),
                  pl.BlockSpec((NSA_GROUPS, tq, NSA_DIM), kmap),
                  pl.BlockSpec((NSA_GROUPS, tq, nblk), qmap),
                  pl.BlockSpec((nblk, tq), lambda b, i, j: (0, jnp.minimum(i, j)))],
        out_specs=pl.BlockSpec((tq, NSA_HEADS * NSA_DIM), lambda b, i, j: (b * nqt + i, 0)),
        out_shape=jax.ShapeDtypeStruct((T, NSA_HEADS * NSA_DIM), BF16),
        scratch_shapes=[pltpu.VMEM((NSA_HEADS * tq, 1), F32),
                        pltpu.VMEM((NSA_HEADS * tq, 1), F32),
                        pltpu.VMEM((NSA_HEADS * tq, NSA_DIM), F32)],
        compiler_params=_cparams(("parallel", "parallel", "arbitrary")),
        name="slc_attn",
    )(nq, ks, vs, sel, e)


def _win_kernel(q_ref, k_ref, v_ref, o_ref, m_ref, l_ref, acc_ref, *, tq):
    i, j = pl.program_id(1), pl.program_id(2)

    @pl.when(j == 0)
    def _():
        _flash_init(m_ref, l_ref, acc_ref)

    def step(prev):
        r = lax.broadcasted_iota(jnp.int32, (tq, tq), 0)
        c = lax.broadcasted_iota(jnp.int32, (tq, tq), 1)
        mask = (c > r) if prev else (c <= r)
        for g in range(NSA_GROUPS):
            q = q_ref[g * NSA_HPG:(g + 1) * NSA_HPG].reshape(NSA_HPG * tq, NSA_DIM)
            rows = pl.ds(g * NSA_HPG * tq, NSA_HPG * tq)
            _flash_update(q, k_ref[g], v_ref[g], mask, rows, m_ref, l_ref, acc_ref)

    @pl.when((j == 0) & (i > 0))
    def _():
        step(True)

    @pl.when(j == 1)
    def _():
        step(False)
        _nsa_finalize(o_ref, l_ref, acc_ref, tq)


def _win_attention(nq, kw, vw, B, S, tq):
    T = nq.shape[1]
    nqt = S // tq
    kern = functools.partial(_win_kernel, tq=tq)
    qmap = lambda b, i, j: (0, b * nqt + i, 0)
    kmap = lambda b, i, j: (0, b * nqt + jnp.maximum(i - 1 + j, 0), 0)
    return pl.pallas_call(
        kern,
        grid=(B, nqt, 2),
        in_specs=[pl.BlockSpec((NSA_HEADS, tq, NSA_DIM), qmap),
                  pl.BlockSpec((NSA_GROUPS, tq, NSA_DIM), kmap),
                  pl.BlockSpec((NSA_GROUPS, tq, NSA_DIM), kmap)],
        out_specs=pl.BlockSpec((tq, NSA_HEADS * NSA_DIM), lambda b, i, j: (b * nqt + i, 0)),
        out_shape=jax.ShapeDtypeStruct((T, NSA_HEADS * NSA_DIM), BF16),
        scratch_shapes=[pltpu.VMEM((NSA_HEADS * tq, 1), F32),
                        pltpu.VMEM((NSA_HEADS * tq, 1), F32),
                        pltpu.VMEM((NSA_HEADS * tq, NSA_DIM), F32)],
        compiler_params=_cparams(("parallel", "parallel", "arbitrary")),
        name="win_attn",
    )(nq, kw, vw)


def _causal_conv(u, carry, w_ref):
    row = lax.broadcasted_iota(jnp.int32, u.shape, 0)
    c1 = carry[SUBLANES - 1:SUBLANES]
    c2 = carry[SUBLANES - 2:SUBLANES - 1]
    u1 = jnp.where(row == 0, c1, pltpu.roll(u, 1, axis=0))
    u2 = jnp.where(row == 0, c2, jnp.where(row == 1, c1, pltpu.roll(u, 2, axis=0)))
    return w_ref[0:1] * u2 + w_ref[1:2] * u1 + w_ref[2:3] * u


def _merge_kernel(x_ref, g_ref, wgc_ref, cw_ref, ym_ref, oc_ref, os_ref, ow_ref, gl_ref, ex_ref,
                  wbc_ref, wbm_ref, wbn_ref, wo_ref, gp_ref, out_ref, carry_ref, *, tiles_per_seq):
    D = x_ref.shape[1]
    DC = wbc_ref.shape[0]

    @pl.when(pl.program_id(0) % tiles_per_seq == 0)
    def _():
        carry_ref[...] = jnp.zeros(carry_ref.shape, F32)

    x = x_ref[...]
    h = _rms(x, g_ref[...]).astype(BF16)
    p = _dot(h, wgc_ref[...])
    u = p[:, 3 * D + DC:3 * D + 2 * DC] * p[:, 3 * D + 2 * DC:3 * D + 3 * DC]
    y_conv = p[:, 3 * D:3 * D + DC] * _causal_conv(u, carry_ref[...], cw_ref)
    carry_ref[...] = u[u.shape[0] - SUBLANES:]
    gate = jax.nn.sigmoid(_dot_split(gl_ref[...], ex_ref[...], 2))
    HD = oc_ref.shape[1]
    y_nsa = (gate[:, 0:HD] * oc_ref[...].astype(F32) + gate[:, HD:2 * HD] * os_ref[...].astype(F32)
             + gate[:, 2 * HD:3 * HD] * ow_ref[...].astype(F32))
    merged = (jax.nn.sigmoid(p[:, 0:D]) * _dot(y_conv.astype(BF16), wbc_ref[...])
              + jax.nn.sigmoid(p[:, D:2 * D]) * _dot(ym_ref[...], wbm_ref[...])
              + jax.nn.sigmoid(p[:, 2 * D:3 * D]) * _dot(y_nsa.astype(BF16), wbn_ref[...]))
    out_ref[...] = x + _rms(_dot(merged.astype(BF16), wo_ref[...]), gp_ref[...])


def _merge(x2, g_pre, w_gc, conv_w, y_mla, o_cmp, o_slc, o_win, gl, ex, wbc, wbm, wbn, wo, g_post,
           S, tm):
    T, D = x2.shape
    HD = y_mla.shape[1]
    row = lambda w: pl.BlockSpec((tm, w), lambda i: (i, 0))
    kern = functools.partial(_merge_kernel, tiles_per_seq=S // tm)
    return pl.pallas_call(
        kern,
        grid=(T // tm,),
        in_specs=[row(D), _const_spec((1, D)), _const_spec(w_gc.shape), _const_spec(conv_w.shape),
                  row(HD), row(HD), row(HD), row(HD), row(LANES), _const_spec(ex.shape),
                  _const_spec(wbc.shape), _const_spec(wbm.shape), _const_spec(wbn.shape),
                  _const_spec(wo.shape), _const_spec((1, D))],
        out_specs=row(D),
        out_shape=jax.ShapeDtypeStruct((T, D), F32),
        scratch_shapes=[pltpu.VMEM((SUBLANES, wbc.shape[0]), F32)],
        compiler_params=_cparams(("arbitrary",)),
        name="merge",
    )(x2, g_pre, w_gc, conv_w, y_mla, o_cmp, o_slc, o_win, gl, ex, wbc, wbm, wbn, wo, g_post)


def _ffn_kernel(x_ref, g_ref, wup_ref, cw_ref, cb_ref, wdn_ref, gp_ref, out_ref, carry_ref,
                *, tiles_per_seq, chunk):
    F = wdn_ref.shape[0]

    @pl.when(pl.program_id(0) % tiles_per_seq == 0)
    def _():
        carry_ref[...] = jnp.zeros(carry_ref.shape, F32)

    x = x_ref[...]
    h = _rms(x, g_ref[...]).astype(BF16)
    acc = None
    for c0 in range(0, F, chunk):
        a = _dot(h, wup_ref[:, c0:c0 + chunk])
        b = _dot(h, wup_ref[:, F + c0:F + c0 + chunk])
        ac = _causal_conv(a, carry_ref[:, c0:c0 + chunk], cw_ref.at[:, c0:c0 + chunk])
        ac = ac + cb_ref[:, c0:c0 + chunk]
        carry_ref[:, c0:c0 + chunk] = a[a.shape[0] - SUBLANES:]
        d = _dot((jax.nn.gelu(ac) * b).astype(BF16), wdn_ref[c0:c0 + chunk, :])
        acc = d if acc is None else acc + d
    out_ref[...] = x + _rms(acc, gp_ref[...])


def _ffn(x2, g_pre, w_up, conv_w, conv_b, w_dn, g_post, S, tm, chunk):
    T, D = x2.shape
    F = w_dn.shape[0]
    row = pl.BlockSpec((tm, D), lambda i: (i, 0))
    kern = functools.partial(_ffn_kernel, tiles_per_seq=S // tm, chunk=chunk)
    return pl.pallas_call(
        kern,
        grid=(T // tm,),
        in_specs=[row, _const_spec((1, D)), _const_spec(w_up.shape), _const_spec(conv_w.shape),
                  _const_spec((1, F)), _const_spec(w_dn.shape), _const_spec((1, D))],
        out_specs=row,
        out_shape=jax.ShapeDtypeStruct((T, D), F32),
        scratch_shapes=[pltpu.VMEM((SUBLANES, F), F32)],
        compiler_params=_cparams(("arbitrary",)),
        name="ffn",
    )(x2, g_pre, w_up, conv_w, conv_b, w_dn, g_post)


def _overlap_matrix(S):
    nch = S // CMP_STRIDE
    n_cmp = (S - CMP_LEN) // CMP_STRIDE + 1
    cs = np.arange(nch)[:, None] * CMP_STRIDE
    ss = np.arange(S // SEL_LEN)[None, :] * SEL_LEN
    ov = (cs <= ss + SEL_LEN - 1) & (cs + CMP_LEN - 1 >= ss) & (np.arange(nch)[:, None] < n_cmp)
    return ov.astype(np.float32)


def _block_expand(S):
    return (np.arange(S)[None, :] // SEL_LEN == np.arange(S // SEL_LEN)[:, None]).astype(np.float32)


def _gate_expand():
    ex = np.zeros((LANES, 3 * NSA_HEADS * NSA_DIM), np.float32)
    for hd in range(NSA_HEADS):
        for c in range(3):
            ex[hd * 3 + c, c * NSA_HEADS * NSA_DIM + hd * NSA_DIM:
               c * NSA_HEADS * NSA_DIM + (hd + 1) * NSA_DIM] = 1.0
    return ex


def _pad_rows(w, rows):
    return jnp.concatenate([w, jnp.zeros((rows - w.shape[0],) + w.shape[1:], w.dtype)], axis=0)


def kernel(x, positions, norm_mix_pre, norm_mix_post, w_in, conv_w, mla_q_norm, mla_w_uq, mla_kv_norm, mla_w_ukv, nsa_cmp_pos_k, nsa_cmp_pos_v, nsa_cmp_w_k, nsa_cmp_w_v, w_branch_conv, w_branch_mla, w_branch_nsa, w_out, norm_ffn_pre, norm_ffn_post, ffn_w_up, ffn_conv_w, ffn_conv_b, ffn_w_down):
    B, S, D = x.shape
    T = B * S
    depth = w_in.shape[0]
    DC = conv_w.shape[2]
    F = ffn_w_down.shape[1]
    QL = mla_q_norm.shape[1]
    KL = mla_kv_norm.shape[1]
    tq = 512
    assert S % tq == 0 and WINDOW == tq and S // SEL_LEN >= N_SEL
    tm_proj, tm_merge, tm_ffn, ffn_chunk = 512, 256, 256, 1408
    assert F % ffn_chunk == 0 and ffn_chunk % LANES == 0

    x2 = x.reshape(T, D)
    pos2 = positions.reshape(T, 1)
    rc = jnp.asarray(_rope_consts())
    ov = jnp.asarray(_overlap_matrix(S), BF16)
    e = jnp.asarray(_block_expand(S), BF16)
    ex = jnp.asarray(_gate_expand(), BF16)
    half = CMP_LEN // 2 * NSA_DIM
    o_att = 3 * D + 3 * DC

    for l in range(depth):
        w = w_in[l]
        o_kr = o_att + QL + KL
        z = lambda n: jnp.zeros((D, n), F32)
        w_att = jnp.concatenate(
            [w[:, o_att:o_kr], z(MLA_NOPE), w[:, o_kr:o_kr + MLA_ROPE], z(LANES - MLA_NOPE - MLA_ROPE),
             w[:, o_kr + MLA_ROPE:o_kr + MLA_ROPE + NSA_HEADS * NSA_DIM + 6 * NSA_GROUPS * NSA_DIM],
             w[:, w.shape[1] - 3 * NSA_HEADS:], z(LANES - 3 * NSA_HEADS)], axis=1).astype(BF16)
        dq = MLA_NOPE + MLA_ROPE
        w_uq = jnp.pad(mla_w_uq[l].reshape(QL, MLA_HEADS, dq),
                       ((0, 0), (0, 0), (0, LANES - dq))).reshape(QL, MLA_HEADS * LANES).astype(BF16)
        (qm, km, vm, nq, kc_in, vc_in, ks, vs, kw, vw, gl) = _project(
            x2, pos2, norm_mix_pre[l][None], w_att, mla_q_norm[l][None], w_uq,
            mla_kv_norm[l][None], mla_w_ukv[l].astype(BF16), rc, tm_proj)

        chunks = lambda a: a.reshape(NSA_GROUPS, T // CMP_STRIDE, CMP_STRIDE * NSA_DIM)
        kc, vc = _compress(chunks(kc_in), chunks(vc_in),
                           nsa_cmp_pos_k[l].reshape(2, half), nsa_cmp_pos_v[l].reshape(2, half),
                           nsa_cmp_w_k[l].reshape(2, half, NSA_DIM).astype(BF16),
                           nsa_cmp_w_v[l].reshape(2, half, NSA_DIM).astype(BF16), B)

        y_mla = _mla_attention(qm, km, vm, B, S, tq)
        o_cmp, sel = _cmp_select(nq, kc, vc, ov, B, S, tq // 2)
        o_slc = _slc_attention(nq, ks, vs, sel, e, B, S, tq)
        o_win = _win_attention(nq, kw, vw, B, S, tq)

        x2 = _merge(x2, norm_mix_pre[l][None], w[:, :o_att].astype(BF16),
                    _pad_rows(conv_w[l], SUBLANES), y_mla, o_cmp, o_slc, o_win, gl, ex,
                    w_branch_conv[l].astype(BF16), w_branch_mla[l].astype(BF16),
                    w_branch_nsa[l].astype(BF16), w_out[l].astype(BF16), norm_mix_post[l][None],
                    S, tm_merge)
        x2 = _ffn(x2, norm_ffn_pre[l][None], ffn_w_up[l].astype(BF16),
                  _pad_rows(ffn_conv_w[l], SUBLANES), ffn_conv_b[l][None],
                  ffn_w_down[l].astype(BF16), norm_ffn_post[l][None], S, tm_ffn, ffn_chunk)
    return x2.reshape(B, S, D)
```

```python
import functools
import math

import numpy as np
import jax
import jax.numpy as jnp
from jax import lax
from jax.experimental import pallas as pl
from jax.experimental.pallas import tpu as pltpu

F32 = jnp.float32
BF16 = jnp.bfloat16

ROPE_THETA = 500000.0
RMS_EPS = 1e-6
CONV_WIDTH = 3
MLA_HEADS = 8
MLA_NOPE = 64
MLA_ROPE = 32
MLA_V = 64
NSA_HEADS = 8
NSA_GROUPS = 2
NSA_HPG = NSA_HEADS // NSA_GROUPS
NSA_DIM = 64
NSA_ROT = NSA_DIM // 4
CMP_LEN = 32
CMP_STRIDE = 16
SEL_LEN = 64
N_SEL = 16
WINDOW = 512

LANES = 128
SUBLANES = 8
VMEM_LIMIT = 56 * 1024 * 1024
MASKED = -1e30
M_FLOOR = -1e20
SEL_BIAS = -32768.0
LOG2E = math.log2(math.e)


def _cparams(sem):
    return pltpu.CompilerParams(dimension_semantics=sem, vmem_limit_bytes=VMEM_LIMIT)


def _const_spec(shape):
    nd = len(shape)
    return pl.BlockSpec(shape, lambda *_: (0,) * nd, pipeline_mode=pl.Buffered(1))


def _rms(x, g):
    return x * lax.rsqrt(jnp.mean(x * x, axis=-1, keepdims=True) + RMS_EPS) * g


def _dot(a, b):
    return jnp.dot(a, b, preferred_element_type=F32)


def _dot_nt(a, b):
    return lax.dot_general(a, b, (((1,), (1,)), ((), ())), preferred_element_type=F32)


def _split_bf16(a, terms):
    pieces, rem = [], a
    for _ in range(terms):
        piece = rem.astype(BF16)
        rem = rem - piece.astype(F32)
        pieces.append(piece)
    return pieces


def _rope(x, cos, sin_up, sin_dn, half):
    return (x * cos + pltpu.roll(x, half, axis=1) * sin_up
            + pltpu.roll(x, LANES - half, axis=1) * sin_dn)


def _proj_kernel(x_ref, pos_ref, g_ref, watt_ref, qg_ref, wuq_ref, kvg_ref, wukv_ref, rc_ref,
                 qm_ref, km_ref, vmt_ref, nq_ref, kc_ref, vc_ref, ksa_ref, vst_ref, kw_ref, vwt_ref,
                 gl_ref, *, mla_scale, nsa_scale, tiles_per_seq):
    tm = x_ref.shape[0]
    x = x_ref[...]
    h = _rms(x, g_ref[...]).astype(BF16)
    p = _dot(h, watt_ref[...])
    pos = pos_ref[...].astype(F32)
    rc = rc_ref[...]
    ang_m = pos * rc[0:1]
    cos_m, sin_m = jnp.cos(ang_m), jnp.sin(ang_m)
    sup_m, sdn_m = sin_m * rc[1:2], sin_m * rc[2:3]
    ang_n = pos * rc[3:4]
    cos_n, sin_n = jnp.cos(ang_n), jnp.sin(ang_n)
    sup_n, sdn_n = sin_n * rc[4:5], sin_n * rc[5:6]
    lane = lax.broadcasted_iota(jnp.int32, (tm, LANES), 1)
    low = lane < NSA_DIM

    def halves(slab):
        return (jnp.where(low, slab, 0.0), jnp.where(low, pltpu.roll(slab, NSA_DIM, axis=1), 0.0))

    cqn = _rms(p[:, 0:384], qg_ref[...]).astype(BF16)
    qf = _dot(cqn, wuq_ref[...])
    for hd in range(MLA_HEADS):
        slab = qf[:, hd * LANES:(hd + 1) * LANES]
        slab = _rope(slab, cos_m, sup_m, sdn_m, MLA_ROPE // 2) * mla_scale
        qm_ref[hd] = slab.astype(BF16)
    ckvn = _rms(p[:, 384:640], kvg_ref[...]).astype(BF16)
    kv = _dot(ckvn, wukv_ref[...])
    kr = _rope(p[:, 640:768], cos_m, sup_m, sdn_m, MLA_ROPE // 2)
    for hd in range(MLA_HEADS):
        slab = kv[:, hd * LANES:(hd + 1) * LANES]
        km_ref[hd] = jnp.where(lane < MLA_NOPE, slab, kr).astype(BF16)
        vmt_ref[hd] = slab.T[MLA_NOPE:].astype(BF16)
    for pr in range(NSA_HEADS // 2):
        slab = p[:, 768 + pr * LANES:768 + (pr + 1) * LANES]
        slab = _rope(slab, cos_n, sup_n, sdn_n, NSA_ROT // 2) * nsa_scale
        q0, q1 = halves(slab)
        nq_ref[2 * pr] = q0.astype(BF16)
        nq_ref[2 * pr + 1] = q1.astype(BF16)
    nsa = lambda idx: p[:, 1280 + idx * LANES:1280 + (idx + 1) * LANES]
    kcs = _rope(nsa(0), cos_n, sup_n, sdn_n, NSA_ROT // 2)
    kc_ref[0] = kcs[:, :NSA_DIM].astype(BF16)
    kc_ref[1] = pltpu.roll(kcs, NSA_DIM, axis=1)[:, :NSA_DIM].astype(BF16)
    vcs = nsa(1)
    vc_ref[0] = vcs[:, :NSA_DIM].astype(BF16)
    vc_ref[1] = pltpu.roll(vcs, NSA_DIM, axis=1)[:, :NSA_DIM].astype(BF16)
    t_seq = (pl.program_id(0) % tiles_per_seq) * tm + lax.broadcasted_iota(jnp.int32, (tm, LANES), 0)
    onehot = jnp.where(lane == lax.shift_right_logical(t_seq, int(math.log2(SEL_LEN))), 1.0, 0.0)
    k0, k1 = halves(_rope(nsa(2), cos_n, sup_n, sdn_n, NSA_ROT // 2))
    for g, kg in enumerate((k0, k1)):
        ksa_ref[g, :, 0:LANES] = onehot.astype(BF16)
        ksa_ref[g, :, LANES:2 * LANES] = kg.astype(BF16)
    vst = nsa(3).T
    vst_ref[0] = vst[:NSA_DIM].astype(BF16)
    vst_ref[1] = vst[NSA_DIM:].astype(BF16)
    kws = _rope(nsa(4), cos_n, sup_n, sdn_n, NSA_ROT // 2)
    kw_ref[0] = kws[:, :NSA_DIM].astype(BF16)
    kw_ref[1] = pltpu.roll(kws, NSA_DIM, axis=1)[:, :NSA_DIM].astype(BF16)
    vwt = nsa(5).T
    vwt_ref[0] = vwt[:NSA_DIM].astype(BF16)
    vwt_ref[1] = vwt[NSA_DIM:].astype(BF16)
    gl_ref[...] = p[:, 2048:2176]


def _rope_consts():
    rc = np.zeros((SUBLANES, LANES), np.float32)
    hm = MLA_ROPE // 2
    fm = ROPE_THETA ** (-np.arange(hm, dtype=np.float32) / hm)
    rc[0, MLA_NOPE:MLA_NOPE + hm] = fm
    rc[0, MLA_NOPE + hm:MLA_NOPE + 2 * hm] = fm
    rc[1, MLA_NOPE + hm:MLA_NOPE + 2 * hm] = 1.0
    rc[2, MLA_NOPE:MLA_NOPE + hm] = -1.0
    hn = NSA_ROT // 2
    fn = ROPE_THETA ** (-np.arange(hn, dtype=np.float32) / hn)
    for base in (0, NSA_DIM):
        rc[3, base:base + hn] = fn
        rc[3, base + hn:base + 2 * hn] = fn
        rc[4, base + hn:base + 2 * hn] = 1.0
        rc[5, base:base + hn] = -1.0
    return rc


def _project(x2, pos2, g_pre, w_att, q_g, w_uq, kv_g, w_ukv, rc, S, tm):
    T, D = x2.shape
    n_att = w_att.shape[1]
    rows = lambda n, d: (jax.ShapeDtypeStruct((n, T, d), BF16), pl.BlockSpec((n, tm, d), lambda i: (0, i, 0)))
    cols = lambda n, d: (jax.ShapeDtypeStruct((n, d, T), BF16), pl.BlockSpec((n, d, tm), lambda i: (0, 0, i)))
    outs = [rows(MLA_HEADS, LANES), rows(MLA_HEADS, LANES), cols(MLA_HEADS, MLA_V),
            rows(NSA_HEADS, LANES), rows(NSA_GROUPS, NSA_DIM), rows(NSA_GROUPS, NSA_DIM),
            rows(NSA_GROUPS, 2 * LANES), cols(NSA_GROUPS, NSA_DIM), rows(NSA_GROUPS, NSA_DIM),
            cols(NSA_GROUPS, NSA_DIM),
            (jax.ShapeDtypeStruct((T, LANES), F32), pl.BlockSpec((tm, LANES), lambda i: (i, 0)))]
    kern = functools.partial(_proj_kernel,
                             mla_scale=float((MLA_NOPE + MLA_ROPE) ** -0.5 * LOG2E),
                             nsa_scale=float(NSA_DIM ** -0.5 * LOG2E),
                             tiles_per_seq=S // tm)
    return pl.pallas_call(
        kern,
        grid=(T // tm,),
        in_specs=[
            pl.BlockSpec((tm, D), lambda i: (i, 0)),
            pl.BlockSpec((tm, 1), lambda i: (i, 0)),
            _const_spec((1, D)),
            _const_spec((D, n_att)),
            _const_spec((1, q_g.shape[1])),
            _const_spec(w_uq.shape),
            _const_spec((1, kv_g.shape[1])),
            _const_spec(w_ukv.shape),
            _const_spec((SUBLANES, LANES)),
        ],
        out_specs=[o[1] for o in outs],
        out_shape=[o[0] for o in outs],
        compiler_params=_cparams(("parallel",)),
        name="proj",
    )(x2, pos2, g_pre, w_att, q_g, w_uq, kv_g, w_ukv, rc)


def _compress_kernel(xk_ref, xv_ref, pk_ref, pv_ref, wk_ref, wv_ref, ok_ref, ovt_ref):
    n = xk_ref.shape[1]
    vboth = None
    for g in range(NSA_GROUPS):
        xk = xk_ref[g].astype(F32)
        lo = _dot((xk + pk_ref[0:1]).astype(BF16), wk_ref[0])
        hi = _dot((xk + pk_ref[1:2]).astype(BF16), wk_ref[1])
        ok_ref[g] = (lo + pltpu.roll(hi, n - 1, axis=0)).astype(BF16)
        xv = xv_ref[g].astype(F32)
        lo = _dot((xv + pv_ref[0:1]).astype(BF16), wv_ref[g, 0])
        hi = _dot((xv + pv_ref[1:2]).astype(BF16), wv_ref[g, 1])
        vg = lo + pltpu.roll(hi, n - 1, axis=0)
        vboth = vg if vboth is None else vboth + vg
    ovt_ref[0] = vboth.T.astype(BF16)


def _compress(xk, xv, pk, pv, wk, wv, B):
    G, TC, CW = xk.shape
    nch = TC // B
    xspec = pl.BlockSpec((G, nch, CW), lambda b: (0, b, 0))
    return pl.pallas_call(
        _compress_kernel,
        grid=(B,),
        in_specs=[xspec, xspec, _const_spec(pk.shape), _const_spec(pv.shape),
                  _const_spec(wk.shape), _const_spec(wv.shape)],
        out_specs=[pl.BlockSpec((G, nch, NSA_DIM), lambda b: (0, b, 0)),
                   pl.BlockSpec((1, G * NSA_DIM, nch), lambda b: (b, 0, 0))],
        out_shape=[jax.ShapeDtypeStruct((G, TC, NSA_DIM), BF16),
                   jax.ShapeDtypeStruct((B, G * NSA_DIM, nch), BF16)],
        compiler_params=_cparams(("parallel",)),
        name="compress",
    )(xk, xv, pk, pv, wk, wv)


def _cmp_select_kernel(q_ref, kc_ref, vct_ref, ovt_ref, o_ref, bias_ref, *, tq, n_pick):
    q0 = pl.program_id(1) * tq
    nch = kc_ref.shape[1]
    nblk = ovt_ref.shape[0]
    t_c = q0 + lax.broadcasted_iota(jnp.int32, (nch, tq), 1)
    n_c = lax.broadcasted_iota(jnp.int32, (nch, tq), 0)
    cmask = (n_c * CMP_STRIDE + (CMP_LEN - 1)) <= t_c
    t_b = q0 + lax.broadcasted_iota(jnp.int32, (nblk, tq), 1)
    blk = lax.broadcasted_iota(jnp.int32, (nblk, tq), 0)
    blk_f = blk.astype(F32)
    cur = lax.shift_right_logical(t_b, int(math.log2(SEL_LEN)))
    forced = (blk == 0) | (blk == cur) | (blk == cur - 1)
    causal = blk * SEL_LEN <= t_b
    outs = []
    for g in range(NSA_GROUPS):
        psum = None
        for j in range(NSA_HPG):
            q = q_ref[g * NSA_HPG + j][:, :NSA_DIM]
            s = jnp.where(cmask, _dot_nt(kc_ref[g], q), MASKED)
            m = jnp.max(s, axis=0, keepdims=True)
            p = jnp.where(cmask, jnp.exp2(s - m), 0.0)
            l = jnp.sum(p, axis=0, keepdims=True)
            pn = p * jnp.where(l > 0.0, 1.0 / l, 0.0)
            outs.append(_dot(vct_ref[0, g * NSA_DIM:(g + 1) * NSA_DIM, :], pn.astype(BF16)))
            psum = pn if psum is None else psum + pn
        imp = None
        for piece in _split_bf16(psum, 3):
            d = _dot(ovt_ref[...], piece)
            imp = d if imp is None else imp + d
        v = jnp.where(forced | jnp.logical_not(causal), -1.0, imp)
        sel = jnp.where(forced, 1.0, 0.0)
        for _ in range(n_pick):
            mx = jnp.max(v, axis=0, keepdims=True)
            first = jnp.min(jnp.where(v == mx, blk_f, float(nblk)), axis=0, keepdims=True)
            hit = blk_f == first
            sel = jnp.where(hit & (mx >= 0.0), 1.0, sel)
            v = jnp.where(hit, -2.0, v)
        bias_ref[g] = ((sel - 1.0) * -SEL_BIAS).T.astype(BF16)
    o_ref[...] = jnp.concatenate(outs, axis=0).T.astype(BF16)


def _cmp_select(nq, kc, vct, ovt, B, S, tq):
    T = nq.shape[1]
    nch = kc.shape[1] // B
    nblk = ovt.shape[0]
    nqt = S // tq
    kern = functools.partial(_cmp_select_kernel, tq=tq, n_pick=N_SEL - 3)
    return pl.pallas_call(
        kern,
        grid=(B, nqt),
        in_specs=[
            pl.BlockSpec((NSA_HEADS, tq, LANES), lambda b, i: (0, b * nqt + i, 0)),
            pl.BlockSpec((NSA_GROUPS, nch, NSA_DIM), lambda b, i: (0, b, 0)),
            pl.BlockSpec((1, NSA_GROUPS * NSA_DIM, nch), lambda b, i: (b, 0, 0)),
            _const_spec(ovt.shape),
        ],
        out_specs=[pl.BlockSpec((tq, NSA_HEADS * NSA_DIM), lambda b, i: (b * nqt + i, 0)),
                   pl.BlockSpec((NSA_GROUPS, tq, nblk), lambda b, i: (0, b * nqt + i, 0))],
        out_shape=[jax.ShapeDtypeStruct((T, NSA_HEADS * NSA_DIM), BF16),
                   jax.ShapeDtypeStruct((NSA_GROUPS, T, nblk), BF16)],
        compiler_params=_cparams(("parallel", "parallel")),
        name="cmp_select",
    )(nq, kc, vct, ovt)


def _flash_update(k, q, vt, mask, hd, m_ref, l_ref, acc_ref):
    s = _dot_nt(k, q)
    if mask is not None:
        s = jnp.where(mask, s, MASKED)
    m_old = m_ref[hd]
    m_new = jnp.maximum(m_old, jnp.max(s, axis=0, keepdims=True))
    alpha = jnp.exp2(m_old - m_new)
    p = jnp.exp2(s - m_new)
    l_ref[hd] = alpha * l_ref[hd] + jnp.sum(p, axis=0, keepdims=True)
    acc_ref[hd] = alpha * acc_ref[hd] + _dot(vt, p.astype(BF16))
    m_ref[hd] = m_new


def _flash_init(m_ref, l_ref, acc_ref):
    m_ref[...] = jnp.full(m_ref.shape, M_FLOOR, F32)
    l_ref[...] = jnp.zeros(l_ref.shape, F32)
    acc_ref[...] = jnp.zeros(acc_ref.shape, F32)


def _flash_finalize(o_ref, l_ref, acc_ref):
    nh, dv, tq = acc_ref.shape
    o = acc_ref[...] * (1.0 / l_ref[...])
    o_ref[...] = o.reshape(nh * dv, tq).T.astype(BF16)


def _flash_scratch(nh, dv, tq):
    return [pltpu.VMEM((nh, 1, tq), F32), pltpu.VMEM((nh, 1, tq), F32), pltpu.VMEM((nh, dv, tq), F32)]


def _tile_iota(tq):
    return (lax.broadcasted_iota(jnp.int32, (tq, tq), 0), lax.broadcasted_iota(jnp.int32, (tq, tq), 1))


def _mla_kernel(q_ref, k_ref, vt_ref, o_ref, m_ref, l_ref, acc_ref, *, tq):
    i, j = pl.program_id(1), pl.program_id(2)

    @pl.when(j == 0)
    def _():
        _flash_init(m_ref, l_ref, acc_ref)

    def step(diag):
        mask = None
        if diag:
            key, qry = _tile_iota(tq)
            mask = key <= qry
        for hd in range(MLA_HEADS):
            _flash_update(k_ref[hd], q_ref[hd], vt_ref[hd], mask, hd, m_ref, l_ref, acc_ref)

    @pl.when(j < i)
    def _():
        step(False)

    @pl.when(j == i)
    def _():
        step(True)
        _flash_finalize(o_ref, l_ref, acc_ref)


def _mla_attention(qm, km, vmt, B, S, tq):
    T = qm.shape[1]
    nqt = S // tq
    kern = functools.partial(_mla_kernel, tq=tq)
    return pl.pallas_call(
        kern,
        grid=(B, nqt, nqt),
        in_specs=[pl.BlockSpec((MLA_HEADS, tq, LANES), lambda b, i, j: (0, b * nqt + i, 0)),
                  pl.BlockSpec((MLA_HEADS, tq, LANES), lambda b, i, j: (0, b * nqt + jnp.minimum(i, j), 0)),
                  pl.BlockSpec((MLA_HEADS, MLA_V, tq), lambda b, i, j: (0, 0, b * nqt + jnp.minimum(i, j)))],
        out_specs=pl.BlockSpec((tq, MLA_HEADS * MLA_V), lambda b, i, j: (b * nqt + i, 0)),
        out_shape=jax.ShapeDtypeStruct((T, MLA_HEADS * MLA_V), BF16),
        scratch_shapes=_flash_scratch(MLA_HEADS, MLA_V, tq),
        compiler_params=_cparams(("parallel", "parallel", "arbitrary")),
        name="mla_attn",
    )(qm, km, vmt)


def _slc_kernel(q_ref, bias_ref, k_ref, vt_ref, o_ref, qa_ref, m_ref, l_ref, acc_ref, *, tq):
    i, j = pl.program_id(1), pl.program_id(2)
    nblk = bias_ref.shape[2]

    @pl.when(j == 0)
    def _():
        _flash_init(m_ref, l_ref, acc_ref)
        if nblk < LANES:
            qa_ref[...] = jnp.zeros(qa_ref.shape, BF16)
        for hd in range(NSA_HEADS):
            qa_ref[hd, :, 0:nblk] = bias_ref[hd // NSA_HPG]
            qa_ref[hd, :, LANES:2 * LANES] = q_ref[hd]

    def step(diag):
        mask = None
        if diag:
            key, qry = _tile_iota(tq)
            mask = key <= qry
        for hd in range(NSA_HEADS):
            g = hd // NSA_HPG
            _flash_update(k_ref[g], qa_ref[hd], vt_ref[g], mask, hd, m_ref, l_ref, acc_ref)

    @pl.when(j < i)
    def _():
        step(False)

    @pl.when(j == i)
    def _():
        step(True)
        _flash_finalize(o_ref, l_ref, acc_ref)


def _slc_attention(nq, bias, ksa, vst, B, S, tq):
    T = nq.shape[1]
    nqt = S // tq
    nblk = bias.shape[2]
    kern = functools.partial(_slc_kernel, tq=tq)
    qmap = lambda b, i, j: (0, b * nqt + i, 0)
    return pl.pallas_call(
        kern,
        grid=(B, nqt, nqt),
        in_specs=[pl.BlockSpec((NSA_HEADS, tq, LANES), qmap),
                  pl.BlockSpec((NSA_GROUPS, tq, nblk), qmap),
                  pl.BlockSpec((NSA_GROUPS, tq, 2 * LANES), lambda b, i, j: (0, b * nqt + jnp.minimum(i, j), 0)),
                  pl.BlockSpec((NSA_GROUPS, NSA_DIM, tq), lambda b, i, j: (0, 0, b * nqt + jnp.minimum(i, j)))],
        out_specs=pl.BlockSpec((tq, NSA_HEADS * NSA_DIM), lambda b, i, j: (b * nqt + i, 0)),
        out_shape=jax.ShapeDtypeStruct((T, NSA_HEADS * NSA_DIM), BF16),
        scratch_shapes=[pltpu.VMEM((NSA_HEADS, tq, 2 * LANES), BF16)] + _flash_scratch(NSA_HEADS, NSA_DIM, tq),
        compiler_params=_cparams(("parallel", "parallel", "arbitrary")),
        name="slc_attn",
    )(nq, bias, ksa, vst)


def _win_kernel(q_ref, k_ref, vt_ref, o_ref, m_ref, l_ref, acc_ref, *, tq):
    i, j = pl.program_id(1), pl.program_id(2)

    @pl.when(j == 0)
    def _():
        _flash_init(m_ref, l_ref, acc_ref)

    def step(prev):
        key, qry = _tile_iota(tq)
        mask = (key > qry) if prev else (key <= qry)
        for hd in range(NSA_HEADS):
            g = hd // NSA_HPG
            _flash_update(k_ref[g], q_ref[hd][:, :NSA_DIM], vt_ref[g], mask, hd, m_ref, l_ref, acc_ref)

    @pl.when((j == 0) & (i > 0))
    def _():
        step(True)

    @pl.when(j == 1)
    def _():
        step(False)
        _flash_finalize(o_ref, l_ref, acc_ref)


def _win_attention(nq, kw, vwt, B, S, tq):
    T = nq.shape[1]
    nqt = S // tq
    kern = functools.partial(_win_kernel, tq=tq)
    kidx = lambda b, i, j: b * nqt + jnp.maximum(i - 1 + j, 0)
    return pl.pallas_call(
        kern,
        grid=(B, nqt, 2),
        in_specs=[pl.BlockSpec((NSA_HEADS, tq, LANES), lambda b, i, j: (0, b * nqt + i, 0)),
                  pl.BlockSpec((NSA_GROUPS, tq, NSA_DIM), lambda b, i, j: (0, kidx(b, i, j), 0)),
                  pl.BlockSpec((NSA_GROUPS, NSA_DIM, tq), lambda b, i, j: (0, 0, kidx(b, i, j)))],
        out_specs=pl.BlockSpec((tq, NSA_HEADS * NSA_DIM), lambda b, i, j: (b * nqt + i, 0)),
        out_shape=jax.ShapeDtypeStruct((T, NSA_HEADS * NSA_DIM), BF16),
        scratch_shapes=_flash_scratch(NSA_HEADS, NSA_DIM, tq),
        compiler_params=_cparams(("parallel", "parallel", "arbitrary")),
        name="win_attn",
    )(nq, kw, vwt)


def _causal_conv(u, carry, w_ref):
    row = lax.broadcasted_iota(jnp.int32, u.shape, 0)
    c1 = carry[SUBLANES - 1:SUBLANES]
    c2 = carry[SUBLANES - 2:SUBLANES - 1]
    u1 = jnp.where(row == 0, c1, pltpu.roll(u, 1, axis=0))
    u2 = jnp.where(row == 0, c2, jnp.where(row == 1, c1, pltpu.roll(u, 2, axis=0)))
    return w_ref[0:1] * u2 + w_ref[1:2] * u1 + w_ref[2:3] * u


def _merge_kernel(x_ref, g_ref, wgc_ref, cw_ref, ym_ref, oc_ref, os_ref, ow_ref, gl_ref, ex_ref,
                  wbc_ref, wbm_ref, wbn_ref, wo_ref, gp_ref, out_ref, carry_ref, *, tiles_per_seq):
    D = x_ref.shape[1]
    DC = wbc_ref.shape[0]

    @pl.when(pl.program_id(0) % tiles_per_seq == 0)
    def _():
        carry_ref[...] = jnp.zeros(carry_ref.shape, F32)

    x = x_ref[...]
    h = _rms(x, g_ref[...]).astype(BF16)
    p = _dot(h, wgc_ref[...])
    u = p[:, 3 * D + DC:3 * D + 2 * DC] * p[:, 3 * D + 2 * DC:3 * D + 3 * DC]
    y_conv = p[:, 3 * D:3 * D + DC] * _causal_conv(u, carry_ref[...], cw_ref)
    carry_ref[...] = u[u.shape[0] - SUBLANES:]
    hi, lo = _split_bf16(gl_ref[...], 2)
    gate = jax.nn.sigmoid(_dot(hi, ex_ref[...]) + _dot(lo, ex_ref[...]))
    HD = oc_ref.shape[1]
    y_nsa = (gate[:, 0:HD] * oc_ref[...].astype(F32) + gate[:, HD:2 * HD] * os_ref[...].astype(F32)
             + gate[:, 2 * HD:3 * HD] * ow_ref[...].astype(F32))
    merged = (jax.nn.sigmoid(p[:, 0:D]) * _dot(y_conv.astype(BF16), wbc_ref[...])
              + jax.nn.sigmoid(p[:, D:2 * D]) * _dot(ym_ref[...], wbm_ref[...])
              + jax.nn.sigmoid(p[:, 2 * D:3 * D]) * _dot(y_nsa.astype(BF16), wbn_ref[...]))
    out_ref[...] = x + _rms(_dot(merged.astype(BF16), wo_ref[...]), gp_ref[...])


def _merge(x2, g_pre, w_gc, conv_w, y_mla, o_cmp, o_slc, o_win, gl, ex, wbc, wbm, wbn, wo, g_post,
           S, tm):
    T, D = x2.shape
    HD = y_mla.shape[1]
    row = lambda w: pl.BlockSpec((tm, w), lambda i: (i, 0))
    kern = functools.partial(_merge_kernel, tiles_per_seq=S // tm)
    return pl.pallas_call(
        kern,
        grid=(T // tm,),
        in_specs=[row(D), _const_spec((1, D)), _const_spec(w_gc.shape), _const_spec(conv_w.shape),
                  row(HD), row(HD), row(HD), row(HD), row(LANES), _const_spec(ex.shape),
                  _const_spec(wbc.shape), _const_spec(wbm.shape), _const_spec(wbn.shape),
                  _const_spec(wo.shape), _const_spec((1, D))],
        out_specs=row(D),
        out_shape=jax.ShapeDtypeStruct((T, D), F32),
        scratch_shapes=[pltpu.VMEM((SUBLANES, wbc.shape[0]), F32)],
        compiler_params=_cparams(("arbitrary",)),
        name="merge",
    )(x2, g_pre, w_gc, conv_w, y_mla, o_cmp, o_slc, o_win, gl, ex, wbc, wbm, wbn, wo, g_post)


def _ffn_kernel(x_ref, g_ref, wup_ref, cw_ref, cb_ref, wdn_ref, gp_ref, out_ref, carry_ref,
                *, tiles_per_seq, chunk):
    F = wdn_ref.shape[0]

    @pl.when(pl.program_id(0) % tiles_per_seq == 0)
    def _():
        carry_ref[...] = jnp.zeros(carry_ref.shape, F32)

    x = x_ref[...]
    h = _rms(x, g_ref[...]).astype(BF16)
    acc = None
    for c0 in range(0, F, chunk):
        a = _dot(h, wup_ref[:, c0:c0 + chunk])
        b = _dot(h, wup_ref[:, F + c0:F + c0 + chunk])
        ac = _causal_conv(a, carry_ref[:, c0:c0 + chunk], cw_ref.at[:, c0:c0 + chunk])
        ac = ac + cb_ref[:, c0:c0 + chunk]
        carry_ref[:, c0:c0 + chunk] = a[a.shape[0] - SUBLANES:]
        d = _dot((jax.nn.gelu(ac) * b).astype(BF16), wdn_ref[c0:c0 + chunk, :])
        acc = d if acc is None else acc + d
    out_ref[...] = x + _rms(acc, gp_ref[...])


def _ffn(x2, g_pre, w_up, conv_w, conv_b, w_dn, g_post, S, tm, chunk):
    T, D = x2.shape
    F = w_dn.shape[0]
    row = pl.BlockSpec((tm, D), lambda i: (i, 0))
    kern = functools.partial(_ffn_kernel, tiles_per_seq=S // tm, chunk=chunk)
    return pl.pallas_call(
        kern,
        grid=(T // tm,),
        in_specs=[row, _const_spec((1, D)), _const_spec(w_up.shape), _const_spec(conv_w.shape),
                  _const_spec((1, F)), _const_spec(w_dn.shape), _const_spec((1, D))],
        out_specs=row,
        out_shape=jax.ShapeDtypeStruct((T, D), F32),
        scratch_shapes=[pltpu.VMEM((SUBLANES, F), F32)],
        compiler_params=_cparams(("arbitrary",)),
        name="ffn",
    )(x2, g_pre, w_up, conv_w, conv_b, w_dn, g_post)


def _overlap_matrix_t(S):
    nch = S // CMP_STRIDE
    n_cmp = (S - CMP_LEN) // CMP_STRIDE + 1
    cs = np.arange(nch)[None, :] * CMP_STRIDE
    ss = np.arange(S // SEL_LEN)[:, None] * SEL_LEN
    ov = (cs <= ss + SEL_LEN - 1) & (cs + CMP_LEN - 1 >= ss) & (np.arange(nch)[None, :] < n_cmp)
    return ov.astype(np.float32)


def _gate_expand():
    ex = np.zeros((LANES, 3 * NSA_HEADS * NSA_DIM), np.float32)
    for hd in range(NSA_HEADS):
        for c in range(3):
            ex[hd * 3 + c, c * NSA_HEADS * NSA_DIM + hd * NSA_DIM:
               c * NSA_HEADS * NSA_DIM + (hd + 1) * NSA_DIM] = 1.0
    return ex


def _pad_rows(w, rows):
    return jnp.concatenate([w, jnp.zeros((rows - w.shape[0],) + w.shape[1:], w.dtype)], axis=0)


def kernel(x, positions, norm_mix_pre, norm_mix_post, w_in, conv_w, mla_q_norm, mla_w_uq, mla_kv_norm, mla_w_ukv, nsa_cmp_pos_k, nsa_cmp_pos_v, nsa_cmp_w_k, nsa_cmp_w_v, w_branch_conv, w_branch_mla, w_branch_nsa, w_out, norm_ffn_pre, norm_ffn_post, ffn_w_up, ffn_conv_w, ffn_conv_b, ffn_w_down):
    B, S, D = x.shape
    T = B * S
    depth = w_in.shape[0]
    DC = conv_w.shape[2]
    F = ffn_w_down.shape[1]
    QL = mla_q_norm.shape[1]
    KL = mla_kv_norm.shape[1]
    tq = 512
    assert S % tq == 0 and WINDOW == tq and N_SEL <= S // SEL_LEN <= LANES
    tm_proj, tm_merge, tm_ffn, ffn_chunk = 512, 256, 256, 1408
    assert F % ffn_chunk == 0 and ffn_chunk % LANES == 0

    x2 = x.reshape(T, D)
    pos2 = positions.reshape(T, 1)
    rc = jnp.asarray(_rope_consts())
    ovt = jnp.asarray(_overlap_matrix_t(S), BF16)
    ex = jnp.asarray(_gate_expand(), BF16)
    half = CMP_LEN // 2 * NSA_DIM
    o_att = 3 * D + 3 * DC

    for l in range(depth):
        w = w_in[l]
        o_kr = o_att + QL + KL
        z = lambda n: jnp.zeros((D, n), F32)
        w_att = jnp.concatenate(
            [w[:, o_att:o_kr], z(MLA_NOPE), w[:, o_kr:o_kr + MLA_ROPE], z(LANES - MLA_NOPE - MLA_ROPE),
             w[:, o_kr + MLA_ROPE:o_kr + MLA_ROPE + NSA_HEADS * NSA_DIM + 6 * NSA_GROUPS * NSA_DIM],
             w[:, w.shape[1] - 3 * NSA_HEADS:], z(LANES - 3 * NSA_HEADS)], axis=1).astype(BF16)
        dq = MLA_NOPE + MLA_ROPE
        w_uq = jnp.pad(mla_w_uq[l].reshape(QL, MLA_HEADS, dq),
                       ((0, 0), (0, 0), (0, LANES - dq))).reshape(QL, MLA_HEADS * LANES).astype(BF16)
        (qm, km, vmt, nq, kc_in, vc_in, ksa, vst, kw, vwt, gl) = _project(
            x2, pos2, norm_mix_pre[l][None], w_att, mla_q_norm[l][None], w_uq,
            mla_kv_norm[l][None], mla_w_ukv[l].astype(BF16), rc, S, tm_proj)

        chunks = lambda a: a.reshape(NSA_GROUPS, T // CMP_STRIDE, CMP_STRIDE * NSA_DIM)
        wv = nsa_cmp_w_v[l].reshape(2, half, NSA_DIM)
        zv = jnp.zeros_like(wv)
        wv_pad = jnp.stack([jnp.concatenate([wv, zv], axis=2), jnp.concatenate([zv, wv], axis=2)])
        kc, vct = _compress(chunks(kc_in), chunks(vc_in),
                            nsa_cmp_pos_k[l].reshape(2, half), nsa_cmp_pos_v[l].reshape(2, half),
                            nsa_cmp_w_k[l].reshape(2, half, NSA_DIM).astype(BF16),
                            wv_pad.astype(BF16), B)

        y_mla = _mla_attention(qm, km, vmt, B, S, tq)
        o_cmp, bias = _cmp_select(nq, kc, vct, ovt, B, S, tq)
        o_slc = _slc_attention(nq, bias, ksa, vst, B, S, tq)
        o_win = _win_attention(nq, kw, vwt, B, S, tq)

        x2 = _merge(x2, norm_mix_pre[l][None], w[:, :o_att].astype(BF16),
                    _pad_rows(conv_w[l], SUBLANES), y_mla, o_cmp, o_slc, o_win, gl, ex,
                    w_branch_conv[l].astype(BF16), w_branch_mla[l].astype(BF16),
                    w_branch_nsa[l].astype(BF16), w_out[l].astype(BF16), norm_mix_post[l][None],
                    S, tm_merge)
        x2 = _ffn(x2, norm_ffn_pre[l][None], ffn_w_up[l].astype(BF16),
                  _pad_rows(ffn_conv_w[l], SUBLANES), ffn_conv_b[l][None],
                  ffn_w_down[l].astype(BF16), norm_ffn_post[l][None], S, tm_ffn, ffn_chunk)
    return x2.reshape(B, S, D)
```

```python
import functools
import math

import numpy as np
import jax
import jax.numpy as jnp
from jax import lax
from jax.experimental import pallas as pl
from jax.experimental.pallas import tpu as pltpu

F32 = jnp.float32
BF16 = jnp.bfloat16

ROPE_THETA = 500000.0
RMS_EPS = 1e-6
CONV_WIDTH = 3
MLA_HEADS = 8
MLA_NOPE = 64
MLA_ROPE = 32
MLA_V = 64
NSA_HEADS = 8
NSA_GROUPS = 2
NSA_HPG = NSA_HEADS // NSA_GROUPS
NSA_DIM = 64
NSA_ROT = NSA_DIM // 4
CMP_LEN = 32
CMP_STRIDE = 16
SEL_LEN = 64
N_SEL = 16
WINDOW = 512

LANES = 128
SUBLANES = 8
VMEM_LIMIT = 56 * 1024 * 1024
MASKED = -1e30
M_FLOOR = -1e20
SEL_BIAS = -32768.0
LOG2E = math.log2(math.e)


def _cparams(sem):
    return pltpu.CompilerParams(dimension_semantics=sem, vmem_limit_bytes=VMEM_LIMIT)


def _const_spec(shape):
    nd = len(shape)
    return pl.BlockSpec(shape, lambda *_: (0,) * nd, pipeline_mode=pl.Buffered(1))


def _rms(x, g):
    return x * lax.rsqrt(jnp.mean(x * x, axis=-1, keepdims=True) + RMS_EPS) * g


def _dot(a, b):
    return jnp.dot(a, b, preferred_element_type=F32)


def _dot_nt(a, b):
    return lax.dot_general(a, b, (((1,), (1,)), ((), ())), preferred_element_type=F32)


def _split_bf16(a, terms):
    pieces, rem = [], a
    for _ in range(terms):
        piece = rem.astype(BF16)
        rem = rem - piece.astype(F32)
        pieces.append(piece)
    return pieces


def _rope(x, cos, sin_up, sin_dn, half):
    return (x * cos + pltpu.roll(x, half, axis=1) * sin_up
            + pltpu.roll(x, LANES - half, axis=1) * sin_dn)


def _proj_kernel(x_ref, pos_ref, g_ref, watt_ref, qg_ref, wuq_ref, kvg_ref, wukv_ref, rc_ref,
                 qm_ref, km_ref, vmt_ref, nq_ref, kc_ref, vc_ref, ksa_ref, vst_ref, kw_ref, vwt_ref,
                 gl_ref, *, mla_scale, nsa_scale, tiles_per_seq):
    tm = x_ref.shape[0]
    x = x_ref[...]
    h = _rms(x, g_ref[...]).astype(BF16)
    p = _dot(h, watt_ref[...])
    pos = pos_ref[...].astype(F32)
    rc = rc_ref[...]
    ang_m = pos * rc[0:1]
    cos_m, sin_m = jnp.cos(ang_m), jnp.sin(ang_m)
    sup_m, sdn_m = sin_m * rc[1:2], sin_m * rc[2:3]
    ang_n = pos * rc[3:4]
    cos_n, sin_n = jnp.cos(ang_n), jnp.sin(ang_n)
    sup_n, sdn_n = sin_n * rc[4:5], sin_n * rc[5:6]
    lane = lax.broadcasted_iota(jnp.int32, (tm, LANES), 1)
    low = lane < NSA_DIM

    def halves(slab):
        return (jnp.where(low, slab, 0.0), jnp.where(low, pltpu.roll(slab, NSA_DIM, axis=1), 0.0))

    cqn = _rms(p[:, 0:384], qg_ref[...]).astype(BF16)
    qf = _dot(cqn, wuq_ref[...])
    for hd in range(MLA_HEADS):
        slab = qf[:, hd * LANES:(hd + 1) * LANES]
        slab = _rope(slab, cos_m, sup_m, sdn_m, MLA_ROPE // 2) * mla_scale
        qm_ref[hd] = slab.astype(BF16)
    ckvn = _rms(p[:, 384:640], kvg_ref[...]).astype(BF16)
    kv = _dot(ckvn, wukv_ref[...])
    kr = _rope(p[:, 640:768], cos_m, sup_m, sdn_m, MLA_ROPE // 2)
    for hd in range(MLA_HEADS):
        slab = kv[:, hd * LANES:(hd + 1) * LANES]
        km_ref[hd] = jnp.where(lane < MLA_NOPE, slab, kr).astype(BF16)
        vmt_ref[hd] = slab.T[MLA_NOPE:].astype(BF16)
    for pr in range(NSA_HEADS // 2):
        slab = p[:, 768 + pr * LANES:768 + (pr + 1) * LANES]
        slab = _rope(slab, cos_n, sup_n, sdn_n, NSA_ROT // 2) * nsa_scale
        q0, q1 = halves(slab)
        nq_ref[2 * pr] = q0.astype(BF16)
        nq_ref[2 * pr + 1] = q1.astype(BF16)
    nsa = lambda idx: p[:, 1280 + idx * LANES:1280 + (idx + 1) * LANES]
    kcs = _rope(nsa(0), cos_n, sup_n, sdn_n, NSA_ROT // 2)
    kc_ref[0] = kcs[:, :NSA_DIM].astype(BF16)
    kc_ref[1] = pltpu.roll(kcs, NSA_DIM, axis=1)[:, :NSA_DIM].astype(BF16)
    vcs = nsa(1)
    vc_ref[0] = vcs[:, :NSA_DIM].astype(BF16)
    vc_ref[1] = pltpu.roll(vcs, NSA_DIM, axis=1)[:, :NSA_DIM].astype(BF16)
    t_seq = (pl.program_id(0) % tiles_per_seq) * tm + lax.broadcasted_iota(jnp.int32, (tm, LANES), 0)
    onehot = jnp.where(lane == lax.shift_right_logical(t_seq, int(math.log2(SEL_LEN))), 1.0, 0.0)
    k0, k1 = halves(_rope(nsa(2), cos_n, sup_n, sdn_n, NSA_ROT // 2))
    for g, kg in enumerate((k0, k1)):
        ksa_ref[g, :, 0:LANES] = onehot.astype(BF16)
        ksa_ref[g, :, LANES:2 * LANES] = kg.astype(BF16)
    vst = nsa(3).T
    vst_ref[0] = vst[:NSA_DIM].astype(BF16)
    vst_ref[1] = vst[NSA_DIM:].astype(BF16)
    kws = _rope(nsa(4), cos_n, sup_n, sdn_n, NSA_ROT // 2)
    kw_ref[0] = kws[:, :NSA_DIM].astype(BF16)
    kw_ref[1] = pltpu.roll(kws, NSA_DIM, axis=1)[:, :NSA_DIM].astype(BF16)
    vwt = nsa(5).T
    vwt_ref[0] = vwt[:NSA_DIM].astype(BF16)
    vwt_ref[1] = vwt[NSA_DIM:].astype(BF16)
    gl_ref[...] = p[:, 2048:2176]


def _rope_consts():
    rc = np.zeros((SUBLANES, LANES), np.float32)
    hm = MLA_ROPE // 2
    fm = ROPE_THETA ** (-np.arange(hm, dtype=np.float32) / hm)
    rc[0, MLA_NOPE:MLA_NOPE + hm] = fm
    rc[0, MLA_NOPE + hm:MLA_NOPE + 2 * hm] = fm
    rc[1, MLA_NOPE + hm:MLA_NOPE + 2 * hm] = 1.0
    rc[2, MLA_NOPE:MLA_NOPE + hm] = -1.0
    hn = NSA_ROT // 2
    fn = ROPE_THETA ** (-np.arange(hn, dtype=np.float32) / hn)
    for base in (0, NSA_DIM):
        rc[3, base:base + hn] = fn
        rc[3, base + hn:base + 2 * hn] = fn
        rc[4, base + hn:base + 2 * hn] = 1.0
        rc[5, base:base + hn] = -1.0
    return rc


def _project(x2, pos2, g_pre, w_att, q_g, w_uq, kv_g, w_ukv, rc, S, tm):
    T, D = x2.shape
    n_att = w_att.shape[1]
    rows = lambda n, d: (jax.ShapeDtypeStruct((n, T, d), BF16), pl.BlockSpec((n, tm, d), lambda i: (0, i, 0)))
    cols = lambda n, d: (jax.ShapeDtypeStruct((n, d, T), BF16), pl.BlockSpec((n, d, tm), lambda i: (0, 0, i)))
    outs = [rows(MLA_HEADS, LANES), rows(MLA_HEADS, LANES), cols(MLA_HEADS, MLA_V),
            rows(NSA_HEADS, LANES), rows(NSA_GROUPS, NSA_DIM), rows(NSA_GROUPS, NSA_DIM),
            rows(NSA_GROUPS, 2 * LANES), cols(NSA_GROUPS, NSA_DIM), rows(NSA_GROUPS, NSA_DIM),
            cols(NSA_GROUPS, NSA_DIM),
            (jax.ShapeDtypeStruct((T, LANES), F32), pl.BlockSpec((tm, LANES), lambda i: (i, 0)))]
    kern = functools.partial(_proj_kernel,
                             mla_scale=float((MLA_NOPE + MLA_ROPE) ** -0.5 * LOG2E),
                             nsa_scale=float(NSA_DIM ** -0.5 * LOG2E),
                             tiles_per_seq=S // tm)
    return pl.pallas_call(
        kern,
        grid=(T // tm,),
        in_specs=[
            pl.BlockSpec((tm, D), lambda i: (i, 0)),
            pl.BlockSpec((tm, 1), lambda i: (i, 0)),
            _const_spec((1, D)),
            _const_spec((D, n_att)),
            _const_spec((1, q_g.shape[1])),
            _const_spec(w_uq.shape),
            _const_spec((1, kv_g.shape[1])),
            _const_spec(w_ukv.shape),
            _const_spec((SUBLANES, LANES)),
        ],
        out_specs=[o[1] for o in outs],
        out_shape=[o[0] for o in outs],
        compiler_params=_cparams(("parallel",)),
        name="proj",
    )(x2, pos2, g_pre, w_att, q_g, w_uq, kv_g, w_ukv, rc)


def _compress_kernel(xk_ref, xv_ref, pk_ref, pv_ref, wk_ref, wv_ref, ok_ref, ovt_ref):
    n = xk_ref.shape[1]
    vboth = None
    for g in range(NSA_GROUPS):
        xk = xk_ref[g].astype(F32)
        lo = _dot((xk + pk_ref[0:1]).astype(BF16), wk_ref[0])
        hi = _dot((xk + pk_ref[1:2]).astype(BF16), wk_ref[1])
        ok_ref[g] = (lo + pltpu.roll(hi, n - 1, axis=0)).astype(BF16)
        xv = xv_ref[g].astype(F32)
        lo = _dot((xv + pv_ref[0:1]).astype(BF16), wv_ref[g, 0])
        hi = _dot((xv + pv_ref[1:2]).astype(BF16), wv_ref[g, 1])
        vg = lo + pltpu.roll(hi, n - 1, axis=0)
        vboth = vg if vboth is None else vboth + vg
    ovt_ref[0] = vboth.T.astype(BF16)


def _compress(xk, xv, pk, pv, wk, wv, B):
    G, TC, CW = xk.shape
    nch = TC // B
    xspec = pl.BlockSpec((G, nch, CW), lambda b: (0, b, 0))
    return pl.pallas_call(
        _compress_kernel,
        grid=(B,),
        in_specs=[xspec, xspec, _const_spec(pk.shape), _const_spec(pv.shape),
                  _const_spec(wk.shape), _const_spec(wv.shape)],
        out_specs=[pl.BlockSpec((G, nch, NSA_DIM), lambda b: (0, b, 0)),
                   pl.BlockSpec((1, G * NSA_DIM, nch), lambda b: (b, 0, 0))],
        out_shape=[jax.ShapeDtypeStruct((G, TC, NSA_DIM), BF16),
                   jax.ShapeDtypeStruct((B, G * NSA_DIM, nch), BF16)],
        compiler_params=_cparams(("parallel",)),
        name="compress",
    )(xk, xv, pk, pv, wk, wv)


def _cmp_select_kernel(q_ref, kc_ref, vct_ref, ovt_ref, o_ref, bias_ref, *, tq, n_pick):
    q0 = pl.program_id(1) * tq
    nch = kc_ref.shape[1]
    nblk = ovt_ref.shape[0]
    t_c = q0 + lax.broadcasted_iota(jnp.int32, (nch, tq), 1)
    n_c = lax.broadcasted_iota(jnp.int32, (nch, tq), 0)
    cmask = (n_c * CMP_STRIDE + (CMP_LEN - 1)) <= t_c
    t_b = q0 + lax.broadcasted_iota(jnp.int32, (nblk, tq), 1)
    blk = lax.broadcasted_iota(jnp.int32, (nblk, tq), 0)
    blk_f = blk.astype(F32)
    cur = lax.shift_right_logical(t_b, int(math.log2(SEL_LEN)))
    forced = (blk == 0) | (blk == cur) | (blk == cur - 1)
    causal = blk * SEL_LEN <= t_b
    outs = []
    for g in range(NSA_GROUPS):
        psum = None
        for j in range(NSA_HPG):
            q = q_ref[g * NSA_HPG + j][:, :NSA_DIM]
            s = jnp.where(cmask, _dot_nt(kc_ref[g], q), MASKED)
            m = jnp.max(s, axis=0, keepdims=True)
            p = jnp.where(cmask, jnp.exp2(s - m), 0.0)
            l = jnp.sum(p, axis=0, keepdims=True)
            pn = p * jnp.where(l > 0.0, 1.0 / l, 0.0)
            outs.append(_dot(vct_ref[0, g * NSA_DIM:(g + 1) * NSA_DIM, :], pn.astype(BF16)))
            psum = pn if psum is None else psum + pn
        imp = None
        for piece in _split_bf16(psum, 3):
            d = _dot(ovt_ref[...], piece)
            imp = d if imp is None else imp + d
        v = jnp.where(forced | jnp.logical_not(causal), -1.0, imp)
        sel = jnp.where(forced, 1.0, 0.0)
        for _ in range(n_pick):
            mx = jnp.max(v, axis=0, keepdims=True)
            first = jnp.min(jnp.where(v == mx, blk_f, float(nblk)), axis=0, keepdims=True)
            hit = blk_f == first
            sel = jnp.where(hit & (mx >= 0.0), 1.0, sel)
            v = jnp.where(hit, -2.0, v)
        bias_ref[g] = ((sel - 1.0) * -SEL_BIAS).T.astype(BF16)
    o_ref[...] = jnp.concatenate(outs, axis=0).T.astype(BF16)


def _cmp_select(nq, kc, vct, ovt, B, S, tq):
    T = nq.shape[1]
    nch = kc.shape[1] // B
    nblk = ovt.shape[0]
    nqt = S // tq
    kern = functools.partial(_cmp_select_kernel, tq=tq, n_pick=N_SEL - 3)
    return pl.pallas_call(
        kern,
        grid=(B, nqt),
        in_specs=[
            pl.BlockSpec((NSA_HEADS, tq, LANES), lambda b, i: (0, b * nqt + i, 0)),
            pl.BlockSpec((NSA_GROUPS, nch, NSA_DIM), lambda b, i: (0, b, 0)),
            pl.BlockSpec((1, NSA_GROUPS * NSA_DIM, nch), lambda b, i: (b, 0, 0)),
            _const_spec(ovt.shape),
        ],
        out_specs=[pl.BlockSpec((tq, NSA_HEADS * NSA_DIM), lambda b, i: (b * nqt + i, 0)),
                   pl.BlockSpec((NSA_GROUPS, tq, nblk), lambda b, i: (0, b * nqt + i, 0))],
        out_shape=[jax.ShapeDtypeStruct((T, NSA_HEADS * NSA_DIM), BF16),
                   jax.ShapeDtypeStruct((NSA_GROUPS, T, nblk), BF16)],
        compiler_params=_cparams(("parallel", "parallel")),
        name="cmp_select",
    )(nq, kc, vct, ovt)


def _flash_heads(nheads, kqv, mask, m_ref, l_ref, acc_ref):
    def scores(hd):
        k, q, _ = kqv(hd)
        return _dot_nt(k, q)

    s_next = scores(0)
    for hd in range(nheads):
        s = s_next
        if hd + 1 < nheads:
            s_next = scores(hd + 1)
        if mask is not None:
            s = jnp.where(mask, s, MASKED)
        m_old = m_ref[hd]
        m_new = jnp.maximum(m_old, jnp.max(s, axis=0, keepdims=True))
        alpha = jnp.exp2(m_old - m_new)
        p = jnp.exp2(s - m_new)
        l_ref[hd] = alpha * l_ref[hd] + jnp.sum(p, axis=0, keepdims=True)
        acc_ref[hd] = alpha * acc_ref[hd] + _dot(kqv(hd)[2], p.astype(BF16))
        m_ref[hd] = m_new


def _flash_init(m_ref, l_ref, acc_ref):
    m_ref[...] = jnp.full(m_ref.shape, M_FLOOR, F32)
    l_ref[...] = jnp.zeros(l_ref.shape, F32)
    acc_ref[...] = jnp.zeros(acc_ref.shape, F32)


def _flash_finalize(o_ref, l_ref, acc_ref):
    nh, dv, tq = acc_ref.shape
    o = acc_ref[...] * (1.0 / l_ref[...])
    o_ref[...] = o.reshape(nh * dv, tq).T.astype(BF16)


def _flash_scratch(nh, dv, tq):
    return [pltpu.VMEM((nh, 1, tq), F32), pltpu.VMEM((nh, 1, tq), F32), pltpu.VMEM((nh, dv, tq), F32)]


def _causal_pairs(nqt):
    pairs = [(i, j) for i in range(nqt) for j in range(i + 1)]
    return (jnp.asarray([p[0] for p in pairs], jnp.int32), jnp.asarray([p[1] for p in pairs], jnp.int32))


def _tile_iota(tq):
    return (lax.broadcasted_iota(jnp.int32, (tq, tq), 0), lax.broadcasted_iota(jnp.int32, (tq, tq), 1))


def _mla_kernel(qi_ref, ki_ref, q_ref, k_ref, vt_ref, o_ref, m_ref, l_ref, acc_ref, *, tq):
    i, j = qi_ref[pl.program_id(1)], ki_ref[pl.program_id(1)]

    @pl.when(j == 0)
    def _():
        _flash_init(m_ref, l_ref, acc_ref)

    def step(diag):
        mask = None
        if diag:
            key, qry = _tile_iota(tq)
            mask = key <= qry
        _flash_heads(MLA_HEADS, lambda hd: (k_ref[hd], q_ref[hd], vt_ref[hd]), mask,
                     m_ref, l_ref, acc_ref)

    @pl.when(j < i)
    def _():
        step(False)

    @pl.when(j == i)
    def _():
        step(True)
        _flash_finalize(o_ref, l_ref, acc_ref)


def _mla_attention(qm, km, vmt, B, S, tq):
    T = qm.shape[1]
    nqt = S // tq
    kern = functools.partial(_mla_kernel, tq=tq)
    qi, ki = _causal_pairs(nqt)
    return pl.pallas_call(
        kern,
        grid_spec=pltpu.PrefetchScalarGridSpec(
            num_scalar_prefetch=2,
            grid=(B, qi.shape[0]),
            in_specs=[pl.BlockSpec((MLA_HEADS, tq, LANES), lambda b, p, qi, ki: (0, b * nqt + qi[p], 0)),
                      pl.BlockSpec((MLA_HEADS, tq, LANES), lambda b, p, qi, ki: (0, b * nqt + ki[p], 0)),
                      pl.BlockSpec((MLA_HEADS, MLA_V, tq), lambda b, p, qi, ki: (0, 0, b * nqt + ki[p]))],
            out_specs=pl.BlockSpec((tq, MLA_HEADS * MLA_V), lambda b, p, qi, ki: (b * nqt + qi[p], 0)),
            scratch_shapes=_flash_scratch(MLA_HEADS, MLA_V, tq)),
        out_shape=jax.ShapeDtypeStruct((T, MLA_HEADS * MLA_V), BF16),
        compiler_params=_cparams(("parallel", "arbitrary")),
        name="mla_attn",
    )(qi, ki, qm, km, vmt)


def _slc_kernel(qi_ref, ki_ref, q_ref, bias_ref, k_ref, vt_ref, o_ref, qa_ref, m_ref, l_ref, acc_ref,
                *, tq):
    i, j = qi_ref[pl.program_id(1)], ki_ref[pl.program_id(1)]
    nblk = bias_ref.shape[2]

    @pl.when(j == 0)
    def _():
        _flash_init(m_ref, l_ref, acc_ref)
        if nblk < LANES:
            qa_ref[...] = jnp.zeros(qa_ref.shape, BF16)
        for hd in range(NSA_HEADS):
            qa_ref[hd, :, 0:nblk] = bias_ref[hd // NSA_HPG]
            qa_ref[hd, :, LANES:2 * LANES] = q_ref[hd]

    def step(diag):
        mask = None
        if diag:
            key, qry = _tile_iota(tq)
            mask = key <= qry
        _flash_heads(NSA_HEADS, lambda hd: (k_ref[hd // NSA_HPG], qa_ref[hd], vt_ref[hd // NSA_HPG]),
                     mask, m_ref, l_ref, acc_ref)

    @pl.when(j < i)
    def _():
        step(False)

    @pl.when(j == i)
    def _():
        step(True)
        _flash_finalize(o_ref, l_ref, acc_ref)


def _slc_attention(nq, bias, ksa, vst, B, S, tq):
    T = nq.shape[1]
    nqt = S // tq
    nblk = bias.shape[2]
    kern = functools.partial(_slc_kernel, tq=tq)
    qi, ki = _causal_pairs(nqt)
    qmap = lambda b, p, qi, ki: (0, b * nqt + qi[p], 0)
    return pl.pallas_call(
        kern,
        grid_spec=pltpu.PrefetchScalarGridSpec(
            num_scalar_prefetch=2,
            grid=(B, qi.shape[0]),
            in_specs=[pl.BlockSpec((NSA_HEADS, tq, LANES), qmap),
                      pl.BlockSpec((NSA_GROUPS, tq, nblk), qmap),
                      pl.BlockSpec((NSA_GROUPS, tq, 2 * LANES), lambda b, p, qi, ki: (0, b * nqt + ki[p], 0)),
                      pl.BlockSpec((NSA_GROUPS, NSA_DIM, tq), lambda b, p, qi, ki: (0, 0, b * nqt + ki[p]))],
            out_specs=pl.BlockSpec((tq, NSA_HEADS * NSA_DIM), lambda b, p, qi, ki: (b * nqt + qi[p], 0)),
            scratch_shapes=[pltpu.VMEM((NSA_HEADS, tq, 2 * LANES), BF16)]
                           + _flash_scratch(NSA_HEADS, NSA_DIM, tq)),
        out_shape=jax.ShapeDtypeStruct((T, NSA_HEADS * NSA_DIM), BF16),
        compiler_params=_cparams(("parallel", "arbitrary")),
        name="slc_attn",
    )(qi, ki, nq, bias, ksa, vst)


def _win_kernel(q_ref, k_ref, vt_ref, o_ref, m_ref, l_ref, acc_ref, *, tq):
    i, j = pl.program_id(1), pl.program_id(2)

    @pl.when(j == 0)
    def _():
        _flash_init(m_ref, l_ref, acc_ref)

    def step(prev):
        key, qry = _tile_iota(tq)
        mask = (key > qry) if prev else (key <= qry)
        _flash_heads(NSA_HEADS, lambda hd: (k_ref[hd // NSA_HPG], q_ref[hd][:, :NSA_DIM],
                                            vt_ref[hd // NSA_HPG]), mask, m_ref, l_ref, acc_ref)

    @pl.when((j == 0) & (i > 0))
    def _():
        step(True)

    @pl.when(j == 1)
    def _():
        step(False)
        _flash_finalize(o_ref, l_ref, acc_ref)


def _win_attention(nq, kw, vwt, B, S, tq):
    T = nq.shape[1]
    nqt = S // tq
    kern = functools.partial(_win_kernel, tq=tq)
    kidx = lambda b, i, j: b * nqt + jnp.maximum(i - 1 + j, 0)
    return pl.pallas_call(
        kern,
        grid=(B, nqt, 2),
        in_specs=[pl.BlockSpec((NSA_HEADS, tq, LANES), lambda b, i, j: (0, b * nqt + i, 0)),
                  pl.BlockSpec((NSA_GROUPS, tq, NSA_DIM), lambda b, i, j: (0, kidx(b, i, j), 0)),
                  pl.BlockSpec((NSA_GROUPS, NSA_DIM, tq), lambda b, i, j: (0, 0, kidx(b, i, j)))],
        out_specs=pl.BlockSpec((tq, NSA_HEADS * NSA_DIM), lambda b, i, j: (b * nqt + i, 0)),
        out_shape=jax.ShapeDtypeStruct((T, NSA_HEADS * NSA_DIM), BF16),
        scratch_shapes=_flash_scratch(NSA_HEADS, NSA_DIM, tq),
        compiler_params=_cparams(("parallel", "parallel", "arbitrary")),
        name="win_attn",
    )(nq, kw, vwt)


def _causal_conv(u, carry, w_ref):
    row = lax.broadcasted_iota(jnp.int32, u.shape, 0)
    c1 = carry[SUBLANES - 1:SUBLANES]
    c2 = carry[SUBLANES - 2:SUBLANES - 1]
    u1 = jnp.where(row == 0, c1, pltpu.roll(u, 1, axis=0))
    u2 = jnp.where(row == 0, c2, jnp.where(row == 1, c1, pltpu.roll(u, 2, axis=0)))
    return w_ref[0:1] * u2 + w_ref[1:2] * u1 + w_ref[2:3] * u


def _merge_kernel(x_ref, g_ref, wgc_ref, cw_ref, ym_ref, oc_ref, os_ref, ow_ref, gl_ref, ex_ref,
                  wbc_ref, wbm_ref, wbn_ref, wo_ref, gp_ref, out_ref, carry_ref, *, tiles_per_seq):
    D = x_ref.shape[1]
    DC = wbc_ref.shape[0]

    @pl.when(pl.program_id(0) % tiles_per_seq == 0)
    def _():
        carry_ref[...] = jnp.zeros(carry_ref.shape, F32)

    x = x_ref[...]
    h = _rms(x, g_ref[...]).astype(BF16)
    p = _dot(h, wgc_ref[...])
    u = p[:, 3 * D + DC:3 * D + 2 * DC] * p[:, 3 * D + 2 * DC:3 * D + 3 * DC]
    y_conv = p[:, 3 * D:3 * D + DC] * _causal_conv(u, carry_ref[...], cw_ref)
    carry_ref[...] = u[u.shape[0] - SUBLANES:]
    hi, lo = _split_bf16(gl_ref[...], 2)
    gate = jax.nn.sigmoid(_dot(hi, ex_ref[...]) + _dot(lo, ex_ref[...]))
    HD = oc_ref.shape[1]
    y_nsa = (gate[:, 0:HD] * oc_ref[...].astype(F32) + gate[:, HD:2 * HD] * os_ref[...].astype(F32)
             + gate[:, 2 * HD:3 * HD] * ow_ref[...].astype(F32))
    merged = (jax.nn.sigmoid(p[:, 0:D]) * _dot(y_conv.astype(BF16), wbc_ref[...])
              + jax.nn.sigmoid(p[:, D:2 * D]) * _dot(ym_ref[...], wbm_ref[...])
              + jax.nn.sigmoid(p[:, 2 * D:3 * D]) * _dot(y_nsa.astype(BF16), wbn_ref[...]))
    out_ref[...] = x + _rms(_dot(merged.astype(BF16), wo_ref[...]), gp_ref[...])


def _merge(x2, g_pre, w_gc, conv_w, y_mla, o_cmp, o_slc, o_win, gl, ex, wbc, wbm, wbn, wo, g_post,
           S, tm):
    T, D = x2.shape
    HD = y_mla.shape[1]
    row = lambda w: pl.BlockSpec((tm, w), lambda i: (i, 0))
    kern = functools.partial(_merge_kernel, tiles_per_seq=S // tm)
    return pl.pallas_call(
        kern,
        grid=(T // tm,),
        in_specs=[row(D), _const_spec((1, D)), _const_spec(w_gc.shape), _const_spec(conv_w.shape),
                  row(HD), row(HD), row(HD), row(HD), row(LANES), _const_spec(ex.shape),
                  _const_spec(wbc.shape), _const_spec(wbm.shape), _const_spec(wbn.shape),
                  _const_spec(wo.shape), _const_spec((1, D))],
        out_specs=row(D),
        out_shape=jax.ShapeDtypeStruct((T, D), F32),
        scratch_shapes=[pltpu.VMEM((SUBLANES, wbc.shape[0]), F32)],
        compiler_params=_cparams(("arbitrary",)),
        name="merge",
    )(x2, g_pre, w_gc, conv_w, y_mla, o_cmp, o_slc, o_win, gl, ex, wbc, wbm, wbn, wo, g_post)


def _ffn_kernel(x_ref, g_ref, wup_ref, cw_ref, cb_ref, wdn_ref, gp_ref, out_ref, carry_ref,
                *, tiles_per_seq, chunk):
    F = wdn_ref.shape[0]

    @pl.when(pl.program_id(0) % tiles_per_seq == 0)
    def _():
        carry_ref[...] = jnp.zeros(carry_ref.shape, F32)

    x = x_ref[...]
    h = _rms(x, g_ref[...]).astype(BF16)
    acc = None
    for c0 in range(0, F, chunk):
        a = _dot(h, wup_ref[:, c0:c0 + chunk])
        b = _dot(h, wup_ref[:, F + c0:F + c0 + chunk])
        ac = _causal_conv(a, carry_ref[:, c0:c0 + chunk], cw_ref.at[:, c0:c0 + chunk])
        ac = ac + cb_ref[:, c0:c0 + chunk]
        carry_ref[:, c0:c0 + chunk] = a[a.shape[0] - SUBLANES:]
        d = _dot((jax.nn.gelu(ac) * b).astype(BF16), wdn_ref[c0:c0 + chunk, :])
        acc = d if acc is None else acc + d
    out_ref[...] = x + _rms(acc, gp_ref[...])


def _ffn(x2, g_pre, w_up, conv_w, conv_b, w_dn, g_post, S, tm, chunk):
    T, D = x2.shape
    F = w_dn.shape[0]
    row = pl.BlockSpec((tm, D), lambda i: (i, 0))
    kern = functools.partial(_ffn_kernel, tiles_per_seq=S // tm, chunk=chunk)
    return pl.pallas_call(
        kern,
        grid=(T // tm,),
        in_specs=[row, _const_spec((1, D)), _const_spec(w_up.shape), _const_spec(conv_w.shape),
                  _const_spec((1, F)), _const_spec(w_dn.shape), _const_spec((1, D))],
        out_specs=row,
        out_shape=jax.ShapeDtypeStruct((T, D), F32),
        scratch_shapes=[pltpu.VMEM((SUBLANES, F), F32)],
        compiler_params=_cparams(("arbitrary",)),
        name="ffn",
    )(x2, g_pre, w_up, conv_w, conv_b, w_dn, g_post)


def _overlap_matrix_t(S):
    nch = S // CMP_STRIDE
    n_cmp = (S - CMP_LEN) // CMP_STRIDE + 1
    cs = np.arange(nch)[None, :] * CMP_STRIDE
    ss = np.arange(S // SEL_LEN)[:, None] * SEL_LEN
    ov = (cs <= ss + SEL_LEN - 1) & (cs + CMP_LEN - 1 >= ss) & (np.arange(nch)[None, :] < n_cmp)
    return ov.astype(np.float32)


def _gate_expand():
    ex = np.zeros((LANES, 3 * NSA_HEADS * NSA_DIM), np.float32)
    for hd in range(NSA_HEADS):
        for c in range(3):
            ex[hd * 3 + c, c * NSA_HEADS * NSA_DIM + hd * NSA_DIM:
               c * NSA_HEADS * NSA_DIM + (hd + 1) * NSA_DIM] = 1.0
    return ex


def _pad_rows(w, rows):
    return jnp.concatenate([w, jnp.zeros((rows - w.shape[0],) + w.shape[1:], w.dtype)], axis=0)


def kernel(x, positions, norm_mix_pre, norm_mix_post, w_in, conv_w, mla_q_norm, mla_w_uq, mla_kv_norm, mla_w_ukv, nsa_cmp_pos_k, nsa_cmp_pos_v, nsa_cmp_w_k, nsa_cmp_w_v, w_branch_conv, w_branch_mla, w_branch_nsa, w_out, norm_ffn_pre, norm_ffn_post, ffn_w_up, ffn_conv_w, ffn_conv_b, ffn_w_down):
    B, S, D = x.shape
    T = B * S
    depth = w_in.shape[0]
    DC = conv_w.shape[2]
    F = ffn_w_down.shape[1]
    QL = mla_q_norm.shape[1]
    KL = mla_kv_norm.shape[1]
    tq = 512
    assert S % tq == 0 and WINDOW == tq and N_SEL <= S // SEL_LEN <= LANES
    tm_proj, tm_merge, tm_ffn, ffn_chunk = 512, 256, 256, 1408
    assert F % ffn_chunk == 0 and ffn_chunk % LANES == 0

    x2 = x.reshape(T, D)
    pos2 = positions.reshape(T, 1)
    rc = jnp.asarray(_rope_consts())
    ovt = jnp.asarray(_overlap_matrix_t(S), BF16)
    ex = jnp.asarray(_gate_expand(), BF16)
    half = CMP_LEN // 2 * NSA_DIM
    o_att = 3 * D + 3 * DC

    for l in range(depth):
        w = w_in[l]
        o_kr = o_att + QL + KL
        z = lambda n: jnp.zeros((D, n), F32)
        w_att = jnp.concatenate(
            [w[:, o_att:o_kr], z(MLA_NOPE), w[:, o_kr:o_kr + MLA_ROPE], z(LANES - MLA_NOPE - MLA_ROPE),
             w[:, o_kr + MLA_ROPE:o_kr + MLA_ROPE + NSA_HEADS * NSA_DIM + 6 * NSA_GROUPS * NSA_DIM],
             w[:, w.shape[1] - 3 * NSA_HEADS:], z(LANES - 3 * NSA_HEADS)], axis=1).astype(BF16)
        dq = MLA_NOPE + MLA_ROPE
        w_uq = jnp.pad(mla_w_uq[l].reshape(QL, MLA_HEADS, dq),
                       ((0, 0), (0, 0), (0, LANES - dq))).reshape(QL, MLA_HEADS * LANES).astype(BF16)
        (qm, km, vmt, nq, kc_in, vc_in, ksa, vst, kw, vwt, gl) = _project(
            x2, pos2, norm_mix_pre[l][None], w_att, mla_q_norm[l][None], w_uq,
            mla_kv_norm[l][None], mla_w_ukv[l].astype(BF16), rc, S, tm_proj)

        chunks = lambda a: a.reshape(NSA_GROUPS, T // CMP_STRIDE, CMP_STRIDE * NSA_DIM)
        wv = nsa_cmp_w_v[l].reshape(2, half, NSA_DIM)
        zv = jnp.zeros_like(wv)
        wv_pad = jnp.stack([jnp.concatenate([wv, zv], axis=2), jnp.concatenate([zv, wv], axis=2)])
        kc, vct = _compress(chunks(kc_in), chunks(vc_in),
                            nsa_cmp_pos_k[l].reshape(2, half), nsa_cmp_pos_v[l].reshape(2, half),
                            nsa_cmp_w_k[l].reshape(2, half, NSA_DIM).astype(BF16),
                            wv_pad.astype(BF16), B)

        y_mla = _mla_attention(qm, km, vmt, B, S, tq)
        o_cmp, bias = _cmp_select(nq, kc, vct, ovt, B, S, tq)
        o_slc = _slc_attention(nq, bias, ksa, vst, B, S, tq)
        o_win = _win_attention(nq, kw, vwt, B, S, tq)

        x2 = _merge(x2, norm_mix_pre[l][None], w[:, :o_att].astype(BF16),
                    _pad_rows(conv_w[l], SUBLANES), y_mla, o_cmp, o_slc, o_win, gl, ex,
                    w_branch_conv[l].astype(BF16), w_branch_mla[l].astype(BF16),
                    w_branch_nsa[l].astype(BF16), w_out[l].astype(BF16), norm_mix_post[l][None],
                    S, tm_merge)
        x2 = _ffn(x2, norm_ffn_pre[l][None], ffn_w_up[l].astype(BF16),
                  _pad_rows(ffn_conv_w[l], SUBLANES), ffn_conv_b[l][None],
                  ffn_w_down[l].astype(BF16), norm_ffn_post[l][None], S, tm_ffn, ffn_chunk)
    return x2.reshape(B, S, D)
```

```python
import functools
import math

import numpy as np
import jax
import jax.numpy as jnp
from jax import lax
from jax.experimental import pallas as pl
from jax.experimental.pallas import tpu as pltpu

F32 = jnp.float32
BF16 = jnp.bfloat16

ROPE_THETA = 500000.0
RMS_EPS = 1e-6
CONV_WIDTH = 3
MLA_HEADS = 8
MLA_NOPE = 64
MLA_ROPE = 32
MLA_V = 64
NSA_HEADS = 8
NSA_GROUPS = 2
NSA_HPG = NSA_HEADS // NSA_GROUPS
NSA_DIM = 64
NSA_ROT = NSA_DIM // 4
CMP_LEN = 32
CMP_STRIDE = 16
SEL_LEN = 64
N_SEL = 16
WINDOW = 512

LANES = 128
SUBLANES = 8
VMEM_LIMIT = 56 * 1024 * 1024
MASKED = -1e30
M_FLOOR = -1e20
SEL_BIAS = -32768.0
LOG2E = math.log2(math.e)
SUM_ROWS = 16


def _cparams(sem):
    return pltpu.CompilerParams(dimension_semantics=sem, vmem_limit_bytes=VMEM_LIMIT)


def _const_spec(shape):
    nd = len(shape)
    return pl.BlockSpec(shape, lambda *_: (0,) * nd, pipeline_mode=pl.Buffered(1))


def _rms(x, g):
    return x * lax.rsqrt(jnp.mean(x * x, axis=-1, keepdims=True) + RMS_EPS) * g


def _dot(a, b):
    return jnp.dot(a, b, preferred_element_type=F32)


def _dot_nt(a, b):
    return lax.dot_general(a, b, (((1,), (1,)), ((), ())), preferred_element_type=F32)


def _split_bf16(a, terms):
    pieces, rem = [], a
    for _ in range(terms):
        piece = rem.astype(BF16)
        rem = rem - piece.astype(F32)
        pieces.append(piece)
    return pieces


def _rope(x, cos, sin_up, sin_dn, half):
    return (x * cos + pltpu.roll(x, half, axis=1) * sin_up
            + pltpu.roll(x, LANES - half, axis=1) * sin_dn)


def _proj_kernel(x_ref, pos_ref, g_ref, watt_ref, qg_ref, wuq_ref, kvg_ref, wukv_ref, rc_ref,
                 qm_ref, km_ref, vmt_ref, nq_ref, kc_ref, vc_ref, ksa_ref, vst_ref, kw_ref, vwt_ref,
                 gl_ref, *, mla_scale, nsa_scale, tiles_per_seq):
    tm = x_ref.shape[0]
    x = x_ref[...]
    h = _rms(x, g_ref[...]).astype(BF16)
    p = _dot(h, watt_ref[...])
    pos = pos_ref[...].astype(F32)
    rc = rc_ref[...]
    ang_m = pos * rc[0:1]
    cos_m, sin_m = jnp.cos(ang_m), jnp.sin(ang_m)
    sup_m, sdn_m = sin_m * rc[1:2], sin_m * rc[2:3]
    ang_n = pos * rc[3:4]
    cos_n, sin_n = jnp.cos(ang_n), jnp.sin(ang_n)
    sup_n, sdn_n = sin_n * rc[4:5], sin_n * rc[5:6]
    lane = lax.broadcasted_iota(jnp.int32, (tm, LANES), 1)
    low = lane < NSA_DIM
    ones_rows = jnp.where(lax.broadcasted_iota(jnp.int32, (SUM_ROWS, tm), 0) == 0, 1.0, 0.0)

    def with_sum_rows(vt):
        return jnp.concatenate([vt, ones_rows], axis=0).astype(BF16)

    def halves(slab):
        return (jnp.where(low, slab, 0.0), jnp.where(low, pltpu.roll(slab, NSA_DIM, axis=1), 0.0))

    cqn = _rms(p[:, 0:384], qg_ref[...]).astype(BF16)
    qf = _dot(cqn, wuq_ref[...])
    for hd in range(MLA_HEADS):
        slab = qf[:, hd * LANES:(hd + 1) * LANES]
        slab = _rope(slab, cos_m, sup_m, sdn_m, MLA_ROPE // 2) * mla_scale
        qm_ref[hd] = slab.astype(BF16)
    ckvn = _rms(p[:, 384:640], kvg_ref[...]).astype(BF16)
    kv = _dot(ckvn, wukv_ref[...])
    kr = _rope(p[:, 640:768], cos_m, sup_m, sdn_m, MLA_ROPE // 2)
    for hd in range(MLA_HEADS):
        slab = kv[:, hd * LANES:(hd + 1) * LANES]
        km_ref[hd] = jnp.where(lane < MLA_NOPE, slab, kr).astype(BF16)
        vmt_ref[hd] = with_sum_rows(slab.T[MLA_NOPE:])
    for pr in range(NSA_HEADS // 2):
        slab = p[:, 768 + pr * LANES:768 + (pr + 1) * LANES]
        slab = _rope(slab, cos_n, sup_n, sdn_n, NSA_ROT // 2) * nsa_scale
        q0, q1 = halves(slab)
        nq_ref[2 * pr] = q0.astype(BF16)
        nq_ref[2 * pr + 1] = q1.astype(BF16)
    nsa = lambda idx: p[:, 1280 + idx * LANES:1280 + (idx + 1) * LANES]
    kcs = _rope(nsa(0), cos_n, sup_n, sdn_n, NSA_ROT // 2)
    kc_ref[0] = kcs[:, :NSA_DIM].astype(BF16)
    kc_ref[1] = pltpu.roll(kcs, NSA_DIM, axis=1)[:, :NSA_DIM].astype(BF16)
    vcs = nsa(1)
    vc_ref[0] = vcs[:, :NSA_DIM].astype(BF16)
    vc_ref[1] = pltpu.roll(vcs, NSA_DIM, axis=1)[:, :NSA_DIM].astype(BF16)
    t_seq = (pl.program_id(0) % tiles_per_seq) * tm + lax.broadcasted_iota(jnp.int32, (tm, LANES), 0)
    onehot = jnp.where(lane == lax.shift_right_logical(t_seq, int(math.log2(SEL_LEN))), 1.0, 0.0)
    k0, k1 = halves(_rope(nsa(2), cos_n, sup_n, sdn_n, NSA_ROT // 2))
    for g, kg in enumerate((k0, k1)):
        ksa_ref[g, :, 0:LANES] = onehot.astype(BF16)
        ksa_ref[g, :, LANES:2 * LANES] = kg.astype(BF16)
    vst = nsa(3).T
    vst_ref[0] = with_sum_rows(vst[:NSA_DIM])
    vst_ref[1] = with_sum_rows(vst[NSA_DIM:])
    kws = _rope(nsa(4), cos_n, sup_n, sdn_n, NSA_ROT // 2)
    kw_ref[0] = kws[:, :NSA_DIM].astype(BF16)
    kw_ref[1] = pltpu.roll(kws, NSA_DIM, axis=1)[:, :NSA_DIM].astype(BF16)
    vwt = nsa(5).T
    vwt_ref[0] = with_sum_rows(vwt[:NSA_DIM])
    vwt_ref[1] = with_sum_rows(vwt[NSA_DIM:])
    gl_ref[...] = p[:, 2048:2176]


def _rope_consts():
    rc = np.zeros((SUBLANES, LANES), np.float32)
    hm = MLA_ROPE // 2
    fm = ROPE_THETA ** (-np.arange(hm, dtype=np.float32) / hm)
    rc[0, MLA_NOPE:MLA_NOPE + hm] = fm
    rc[0, MLA_NOPE + hm:MLA_NOPE + 2 * hm] = fm
    rc[1, MLA_NOPE + hm:MLA_NOPE + 2 * hm] = 1.0
    rc[2, MLA_NOPE:MLA_NOPE + hm] = -1.0
    hn = NSA_ROT // 2
    fn = ROPE_THETA ** (-np.arange(hn, dtype=np.float32) / hn)
    for base in (0, NSA_DIM):
        rc[3, base:base + hn] = fn
        rc[3, base + hn:base + 2 * hn] = fn
        rc[4, base + hn:base + 2 * hn] = 1.0
        rc[5, base:base + hn] = -1.0
    return rc


def _project(x2, pos2, g_pre, w_att, q_g, w_uq, kv_g, w_ukv, rc, S, tm):
    T, D = x2.shape
    n_att = w_att.shape[1]
    rows = lambda n, d: (jax.ShapeDtypeStruct((n, T, d), BF16), pl.BlockSpec((n, tm, d), lambda i: (0, i, 0)))
    cols = lambda n, d: (jax.ShapeDtypeStruct((n, d, T), BF16), pl.BlockSpec((n, d, tm), lambda i: (0, 0, i)))
    outs = [rows(MLA_HEADS, LANES), rows(MLA_HEADS, LANES), cols(MLA_HEADS, MLA_V + SUM_ROWS),
            rows(NSA_HEADS, LANES), rows(NSA_GROUPS, NSA_DIM), rows(NSA_GROUPS, NSA_DIM),
            rows(NSA_GROUPS, 2 * LANES), cols(NSA_GROUPS, NSA_DIM + SUM_ROWS), rows(NSA_GROUPS, NSA_DIM),
            cols(NSA_GROUPS, NSA_DIM + SUM_ROWS),
            (jax.ShapeDtypeStruct((T, LANES), F32), pl.BlockSpec((tm, LANES), lambda i: (i, 0)))]
    kern = functools.partial(_proj_kernel,
                             mla_scale=float((MLA_NOPE + MLA_ROPE) ** -0.5 * LOG2E),
                             nsa_scale=float(NSA_DIM ** -0.5 * LOG2E),
                             tiles_per_seq=S // tm)
    return pl.pallas_call(
        kern,
        grid=(T // tm,),
        in_specs=[
            pl.BlockSpec((tm, D), lambda i: (i, 0)),
            pl.BlockSpec((tm, 1), lambda i: (i, 0)),
            _const_spec((1, D)),
            _const_spec((D, n_att)),
            _const_spec((1, q_g.shape[1])),
            _const_spec(w_uq.shape),
            _const_spec((1, kv_g.shape[1])),
            _const_spec(w_ukv.shape),
            _const_spec((SUBLANES, LANES)),
        ],
        out_specs=[o[1] for o in outs],
        out_shape=[o[0] for o in outs],
        compiler_params=_cparams(("parallel",)),
        name="proj",
    )(x2, pos2, g_pre, w_att, q_g, w_uq, kv_g, w_ukv, rc)


def _compress_kernel(xk_ref, xv_ref, pk_ref, pv_ref, wk_ref, wv_ref, ok_ref, ovt_ref):
    n = xk_ref.shape[1]
    vboth = None
    for g in range(NSA_GROUPS):
        xk = xk_ref[g].astype(F32)
        lo = _dot((xk + pk_ref[0:1]).astype(BF16), wk_ref[0])
        hi = _dot((xk + pk_ref[1:2]).astype(BF16), wk_ref[1])
        ok_ref[g] = (lo + pltpu.roll(hi, n - 1, axis=0)).astype(BF16)
        xv = xv_ref[g].astype(F32)
        lo = _dot((xv + pv_ref[0:1]).astype(BF16), wv_ref[g, 0])
        hi = _dot((xv + pv_ref[1:2]).astype(BF16), wv_ref[g, 1])
        vg = lo + pltpu.roll(hi, n - 1, axis=0)
        vboth = vg if vboth is None else vboth + vg
    ovt_ref[0] = vboth.T.astype(BF16)


def _compress(xk, xv, pk, pv, wk, wv, B):
    G, TC, CW = xk.shape
    nch = TC // B
    xspec = pl.BlockSpec((G, nch, CW), lambda b: (0, b, 0))
    return pl.pallas_call(
        _compress_kernel,
        grid=(B,),
        in_specs=[xspec, xspec, _const_spec(pk.shape), _const_spec(pv.shape),
                  _const_spec(wk.shape), _const_spec(wv.shape)],
        out_specs=[pl.BlockSpec((G, nch, NSA_DIM), lambda b: (0, b, 0)),
                   pl.BlockSpec((1, G * NSA_DIM, nch), lambda b: (b, 0, 0))],
        out_shape=[jax.ShapeDtypeStruct((G, TC, NSA_DIM), BF16),
                   jax.ShapeDtypeStruct((B, G * NSA_DIM, nch), BF16)],
        compiler_params=_cparams(("parallel",)),
        name="compress",
    )(xk, xv, pk, pv, wk, wv)


def _cmp_select_kernel(q_ref, kc_ref, vct_ref, ovt_ref, o_ref, bias_ref, *, tq, n_pick):
    q0 = pl.program_id(1) * tq
    nch = kc_ref.shape[1]
    nblk = ovt_ref.shape[0]
    t_c = q0 + lax.broadcasted_iota(jnp.int32, (nch, tq), 1)
    n_c = lax.broadcasted_iota(jnp.int32, (nch, tq), 0)
    cmask = (n_c * CMP_STRIDE + (CMP_LEN - 1)) <= t_c
    t_b = q0 + lax.broadcasted_iota(jnp.int32, (nblk, tq), 1)
    blk = lax.broadcasted_iota(jnp.int32, (nblk, tq), 0)
    blk_f = blk.astype(F32)
    cur = lax.shift_right_logical(t_b, int(math.log2(SEL_LEN)))
    forced = (blk == 0) | (blk == cur) | (blk == cur - 1)
    causal = blk * SEL_LEN <= t_b
    outs = []
    for g in range(NSA_GROUPS):
        psum = None
        for j in range(NSA_HPG):
            q = q_ref[g * NSA_HPG + j][:, :NSA_DIM]
            s = jnp.where(cmask, _dot_nt(kc_ref[g], q), MASKED)
            m = jnp.max(s, axis=0, keepdims=True)
            p = jnp.where(cmask, jnp.exp2(s - m), 0.0)
            l = jnp.sum(p, axis=0, keepdims=True)
            pn = p * jnp.where(l > 0.0, 1.0 / l, 0.0)
            outs.append(_dot(vct_ref[0, g * NSA_DIM:(g + 1) * NSA_DIM, :], pn.astype(BF16)))
            psum = pn if psum is None else psum + pn
        imp = None
        for piece in _split_bf16(psum, 3):
            d = _dot(ovt_ref[...], piece)
            imp = d if imp is None else imp + d
        v = jnp.where(forced | jnp.logical_not(causal), -1.0, imp)
        sel = jnp.where(forced, 1.0, 0.0)
        for _ in range(n_pick):
            mx = jnp.max(v, axis=0, keepdims=True)
            first = jnp.min(jnp.where(v == mx, blk_f, float(nblk)), axis=0, keepdims=True)
            hit = blk_f == first
            sel = jnp.where(hit & (mx >= 0.0), 1.0, sel)
            v = jnp.where(hit, -2.0, v)
        bias_ref[g] = ((sel - 1.0) * -SEL_BIAS).T.astype(BF16)
    o_ref[...] = jnp.concatenate(outs, axis=0).T.astype(BF16)


def _cmp_select(nq, kc, vct, ovt, B, S, tq):
    T = nq.shape[1]
    nch = kc.shape[1] // B
    nblk = ovt.shape[0]
    nqt = S // tq
    kern = functools.partial(_cmp_select_kernel, tq=tq, n_pick=N_SEL - 3)
    return pl.pallas_call(
        kern,
        grid=(B, nqt),
        in_specs=[
            pl.BlockSpec((NSA_HEADS, tq, LANES), lambda b, i: (0, b * nqt + i, 0)),
            pl.BlockSpec((NSA_GROUPS, nch, NSA_DIM), lambda b, i: (0, b, 0)),
            pl.BlockSpec((1, NSA_GROUPS * NSA_DIM, nch), lambda b, i: (b, 0, 0)),
            _const_spec(ovt.shape),
        ],
        out_specs=[pl.BlockSpec((tq, NSA_HEADS * NSA_DIM), lambda b, i: (b * nqt + i, 0)),
                   pl.BlockSpec((NSA_GROUPS, tq, nblk), lambda b, i: (0, b * nqt + i, 0))],
        out_shape=[jax.ShapeDtypeStruct((T, NSA_HEADS * NSA_DIM), BF16),
                   jax.ShapeDtypeStruct((NSA_GROUPS, T, nblk), BF16)],
        compiler_params=_cparams(("parallel", "parallel")),
        name="cmp_select",
    )(nq, kc, vct, ovt)


def _flash_heads(nheads, kqv, mask, m_ref, acc_ref):
    def scores(hd):
        k, q, _ = kqv(hd)
        return _dot_nt(k, q)

    ahead = 2
    pending = [scores(hd) for hd in range(min(ahead, nheads))]
    for hd in range(nheads):
        s = pending.pop(0)
        if hd + ahead < nheads:
            pending.append(scores(hd + ahead))
        if mask is not None:
            s = jnp.where(mask, s, MASKED)
        m_old = m_ref[hd]
        m_new = jnp.maximum(m_old, jnp.max(s, axis=0, keepdims=True))
        alpha = jnp.exp2(m_old - m_new)
        p = jnp.exp2(s - m_new)
        acc_ref[hd] = alpha * acc_ref[hd] + _dot(kqv(hd)[2], p.astype(BF16))
        m_ref[hd] = m_new


def _flash_init(m_ref, acc_ref):
    m_ref[...] = jnp.full(m_ref.shape, M_FLOOR, F32)
    acc_ref[...] = jnp.zeros(acc_ref.shape, F32)


def _flash_finalize(o_ref, acc_ref):
    nh, rows, tq = acc_ref.shape
    dv = rows - SUM_ROWS
    o = acc_ref[:, :dv, :] * (1.0 / acc_ref[:, dv:dv + 1, :])
    o_ref[...] = o.reshape(nh * dv, tq).T.astype(BF16)


def _flash_scratch(nh, dv, tq):
    return [pltpu.VMEM((nh, 1, tq), F32), pltpu.VMEM((nh, dv + SUM_ROWS, tq), F32)]


def _causal_pairs(nqt):
    pairs = [(i, j) for i in range(nqt) for j in range(i + 1)]
    return (jnp.asarray([p[0] for p in pairs], jnp.int32), jnp.asarray([p[1] for p in pairs], jnp.int32))


def _tile_iota(tq):
    return (lax.broadcasted_iota(jnp.int32, (tq, tq), 0), lax.broadcasted_iota(jnp.int32, (tq, tq), 1))


def _mla_kernel(qi_ref, ki_ref, q_ref, k_ref, vt_ref, o_ref, m_ref, acc_ref, *, tq):
    i, j = qi_ref[pl.program_id(1)], ki_ref[pl.program_id(1)]

    @pl.when(j == 0)
    def _():
        _flash_init(m_ref, acc_ref)

    def step(diag):
        mask = None
        if diag:
            key, qry = _tile_iota(tq)
            mask = key <= qry
        _flash_heads(MLA_HEADS, lambda hd: (k_ref[hd], q_ref[hd], vt_ref[hd]), mask,
                     m_ref, acc_ref)

    @pl.when(j < i)
    def _():
        step(False)

    @pl.when(j == i)
    def _():
        step(True)
        _flash_finalize(o_ref, acc_ref)


def _mla_attention(qm, km, vmt, B, S, tq):
    T = qm.shape[1]
    nqt = S // tq
    kern = functools.partial(_mla_kernel, tq=tq)
    qi, ki = _causal_pairs(nqt)
    return pl.pallas_call(
        kern,
        grid_spec=pltpu.PrefetchScalarGridSpec(
            num_scalar_prefetch=2,
            grid=(B, qi.shape[0]),
            in_specs=[pl.BlockSpec((MLA_HEADS, tq, LANES), lambda b, p, qi, ki: (0, b * nqt + qi[p], 0)),
                      pl.BlockSpec((MLA_HEADS, tq, LANES), lambda b, p, qi, ki: (0, b * nqt + ki[p], 0)),
                      pl.BlockSpec((MLA_HEADS, MLA_V + SUM_ROWS, tq), lambda b, p, qi, ki: (0, 0, b * nqt + ki[p]))],
            out_specs=pl.BlockSpec((tq, MLA_HEADS * MLA_V), lambda b, p, qi, ki: (b * nqt + qi[p], 0)),
            scratch_shapes=_flash_scratch(MLA_HEADS, MLA_V, tq)),
        out_shape=jax.ShapeDtypeStruct((T, MLA_HEADS * MLA_V), BF16),
        compiler_params=_cparams(("parallel", "arbitrary")),
        name="mla_attn",
    )(qi, ki, qm, km, vmt)


def _slc_kernel(qi_ref, ki_ref, q_ref, bias_ref, k_ref, vt_ref, o_ref, qa_ref, m_ref, acc_ref,
                *, tq):
    i, j = qi_ref[pl.program_id(1)], ki_ref[pl.program_id(1)]
    nblk = bias_ref.shape[2]

    @pl.when(j == 0)
    def _():
        _flash_init(m_ref, acc_ref)
        if nblk < LANES:
            qa_ref[...] = jnp.zeros(qa_ref.shape, BF16)
        for hd in range(NSA_HEADS):
            qa_ref[hd, :, 0:nblk] = bias_ref[hd // NSA_HPG]
            qa_ref[hd, :, LANES:2 * LANES] = q_ref[hd]

    def step(diag):
        mask = None
        if diag:
            key, qry = _tile_iota(tq)
            mask = key <= qry
        _flash_heads(NSA_HEADS, lambda hd: (k_ref[hd // NSA_HPG], qa_ref[hd], vt_ref[hd // NSA_HPG]),
                     mask, m_ref, acc_ref)

    @pl.when(j < i)
    def _():
        step(False)

    @pl.when(j == i)
    def _():
        step(True)
        _flash_finalize(o_ref, acc_ref)


def _slc_attention(nq, bias, ksa, vst, B, S, tq):
    T = nq.shape[1]
    nqt = S // tq
    nblk = bias.shape[2]
    kern = functools.partial(_slc_kernel, tq=tq)
    qi, ki = _causal_pairs(nqt)
    qmap = lambda b, p, qi, ki: (0, b * nqt + qi[p], 0)
    return pl.pallas_call(
        kern,
        grid_spec=pltpu.PrefetchScalarGridSpec(
            num_scalar_prefetch=2,
            grid=(B, qi.shape[0]),
            in_specs=[pl.BlockSpec((NSA_HEADS, tq, LANES), qmap),
                      pl.BlockSpec((NSA_GROUPS, tq, nblk), qmap),
                      pl.BlockSpec((NSA_GROUPS, tq, 2 * LANES), lambda b, p, qi, ki: (0, b * nqt + ki[p], 0)),
                      pl.BlockSpec((NSA_GROUPS, NSA_DIM + SUM_ROWS, tq), lambda b, p, qi, ki: (0, 0, b * nqt + ki[p]))],
            out_specs=pl.BlockSpec((tq, NSA_HEADS * NSA_DIM), lambda b, p, qi, ki: (b * nqt + qi[p], 0)),
            scratch_shapes=[pltpu.VMEM((NSA_HEADS, tq, 2 * LANES), BF16)]
                           + _flash_scratch(NSA_HEADS, NSA_DIM, tq)),
        out_shape=jax.ShapeDtypeStruct((T, NSA_HEADS * NSA_DIM), BF16),
        compiler_params=_cparams(("parallel", "arbitrary")),
        name="slc_attn",
    )(qi, ki, nq, bias, ksa, vst)


def _win_kernel(q_ref, k_ref, vt_ref, o_ref, m_ref, acc_ref, *, tq):
    i, j = pl.program_id(1), pl.program_id(2)

    @pl.when(j == 0)
    def _():
        _flash_init(m_ref, acc_ref)

    def step(prev):
        key, qry = _tile_iota(tq)
        mask = (key > qry) if prev else (key <= qry)
        _flash_heads(NSA_HEADS, lambda hd: (k_ref[hd // NSA_HPG], q_ref[hd][:, :NSA_DIM],
                                            vt_ref[hd // NSA_HPG]), mask, m_ref, acc_ref)

    @pl.when((j == 0) & (i > 0))
    def _():
        step(True)

    @pl.when(j == 1)
    def _():
        step(False)
        _flash_finalize(o_ref, acc_ref)


def _win_attention(nq, kw, vwt, B, S, tq):
    T = nq.shape[1]
    nqt = S // tq
    kern = functools.partial(_win_kernel, tq=tq)
    kidx = lambda b, i, j: b * nqt + jnp.maximum(i - 1 + j, 0)
    return pl.pallas_call(
        kern,
        grid=(B, nqt, 2),
        in_specs=[pl.BlockSpec((NSA_HEADS, tq, LANES), lambda b, i, j: (0, b * nqt + i, 0)),
                  pl.BlockSpec((NSA_GROUPS, tq, NSA_DIM), lambda b, i, j: (0, kidx(b, i, j), 0)),
                  pl.BlockSpec((NSA_GROUPS, NSA_DIM + SUM_ROWS, tq), lambda b, i, j: (0, 0, kidx(b, i, j)))],
        out_specs=pl.BlockSpec((tq, NSA_HEADS * NSA_DIM), lambda b, i, j: (b * nqt + i, 0)),
        out_shape=jax.ShapeDtypeStruct((T, NSA_HEADS * NSA_DIM), BF16),
        scratch_shapes=_flash_scratch(NSA_HEADS, NSA_DIM, tq),
        compiler_params=_cparams(("parallel", "parallel", "arbitrary")),
        name="win_attn",
    )(nq, kw, vwt)


def _causal_conv(u, carry, w_ref):
    row = lax.broadcasted_iota(jnp.int32, u.shape, 0)
    c1 = carry[SUBLANES - 1:SUBLANES]
    c2 = carry[SUBLANES - 2:SUBLANES - 1]
    u1 = jnp.where(row == 0, c1, pltpu.roll(u, 1, axis=0))
    u2 = jnp.where(row == 0, c2, jnp.where(row == 1, c1, pltpu.roll(u, 2, axis=0)))
    return w_ref[0:1] * u2 + w_ref[1:2] * u1 + w_ref[2:3] * u


def _merge_kernel(x_ref, g_ref, wgc_ref, cw_ref, ym_ref, oc_ref, os_ref, ow_ref, gl_ref, ex_ref,
                  wbc_ref, wbm_ref, wbn_ref, wo_ref, gp_ref, out_ref, carry_ref, *, tiles_per_seq):
    D = x_ref.shape[1]
    DC = wbc_ref.shape[0]

    @pl.when(pl.program_id(0) % tiles_per_seq == 0)
    def _():
        carry_ref[...] = jnp.zeros(carry_ref.shape, F32)

    x = x_ref[...]
    h = _rms(x, g_ref[...]).astype(BF16)
    p = _dot(h, wgc_ref[...])
    u = p[:, 3 * D + DC:3 * D + 2 * DC] * p[:, 3 * D + 2 * DC:3 * D + 3 * DC]
    y_conv = p[:, 3 * D:3 * D + DC] * _causal_conv(u, carry_ref[...], cw_ref)
    carry_ref[...] = u[u.shape[0] - SUBLANES:]
    hi, lo = _split_bf16(gl_ref[...], 2)
    gate = jax.nn.sigmoid(_dot(hi, ex_ref[...]) + _dot(lo, ex_ref[...]))
    HD = oc_ref.shape[1]
    y_nsa = (gate[:, 0:HD] * oc_ref[...].astype(F32) + gate[:, HD:2 * HD] * os_ref[...].astype(F32)
             + gate[:, 2 * HD:3 * HD] * ow_ref[...].astype(F32))
    merged = (jax.nn.sigmoid(p[:, 0:D]) * _dot(y_conv.astype(BF16), wbc_ref[...])
              + jax.nn.sigmoid(p[:, D:2 * D]) * _dot(ym_ref[...], wbm_ref[...])
              + jax.nn.sigmoid(p[:, 2 * D:3 * D]) * _dot(y_nsa.astype(BF16), wbn_ref[...]))
    out_ref[...] = x + _rms(_dot(merged.astype(BF16), wo_ref[...]), gp_ref[...])


def _merge(x2, g_pre, w_gc, conv_w, y_mla, o_cmp, o_slc, o_win, gl, ex, wbc, wbm, wbn, wo, g_post,
           S, tm):
    T, D = x2.shape
    HD = y_mla.shape[1]
    row = lambda w: pl.BlockSpec((tm, w), lambda i: (i, 0))
    kern = functools.partial(_merge_kernel, tiles_per_seq=S // tm)
    return pl.pallas_call(
        kern,
        grid=(T // tm,),
        in_specs=[row(D), _const_spec((1, D)), _const_spec(w_gc.shape), _const_spec(conv_w.shape),
                  row(HD), row(HD), row(HD), row(HD), row(LANES), _const_spec(ex.shape),
                  _const_spec(wbc.shape), _const_spec(wbm.shape), _const_spec(wbn.shape),
                  _const_spec(wo.shape), _const_spec((1, D))],
        out_specs=row(D),
        out_shape=jax.ShapeDtypeStruct((T, D), F32),
        scratch_shapes=[pltpu.VMEM((SUBLANES, wbc.shape[0]), F32)],
        compiler_params=_cparams(("arbitrary",)),
        name="merge",
    )(x2, g_pre, w_gc, conv_w, y_mla, o_cmp, o_slc, o_win, gl, ex, wbc, wbm, wbn, wo, g_post)


def _ffn_kernel(x_ref, g_ref, wup_ref, cw_ref, cb_ref, wdn_ref, gp_ref, out_ref, carry_ref,
                *, tiles_per_seq, chunk):
    F = wdn_ref.shape[0]

    @pl.when(pl.program_id(0) % tiles_per_seq == 0)
    def _():
        carry_ref[...] = jnp.zeros(carry_ref.shape, F32)

    x = x_ref[...]
    h = _rms(x, g_ref[...]).astype(BF16)
    acc = None
    for c0 in range(0, F, chunk):
        a = _dot(h, wup_ref[:, c0:c0 + chunk])
        b = _dot(h, wup_ref[:, F + c0:F + c0 + chunk])
        ac = _causal_conv(a, carry_ref[:, c0:c0 + chunk], cw_ref.at[:, c0:c0 + chunk])
        ac = ac + cb_ref[:, c0:c0 + chunk]
        carry_ref[:, c0:c0 + chunk] = a[a.shape[0] - SUBLANES:]
        d = _dot((jax.nn.gelu(ac) * b).astype(BF16), wdn_ref[c0:c0 + chunk, :])
        acc = d if acc is None else acc + d
    out_ref[...] = x + _rms(acc, gp_ref[...])


def _ffn(x2, g_pre, w_up, conv_w, conv_b, w_dn, g_post, S, tm, chunk):
    T, D = x2.shape
    F = w_dn.shape[0]
    row = pl.BlockSpec((tm, D), lambda i: (i, 0))
    kern = functools.partial(_ffn_kernel, tiles_per_seq=S // tm, chunk=chunk)
    return pl.pallas_call(
        kern,
        grid=(T // tm,),
        in_specs=[row, _const_spec((1, D)), _const_spec(w_up.shape), _const_spec(conv_w.shape),
                  _const_spec((1, F)), _const_spec(w_dn.shape), _const_spec((1, D))],
        out_specs=row,
        out_shape=jax.ShapeDtypeStruct((T, D), F32),
        scratch_shapes=[pltpu.VMEM((SUBLANES, F), F32)],
        compiler_params=_cparams(("arbitrary",)),
        name="ffn",
    )(x2, g_pre, w_up, conv_w, conv_b, w_dn, g_post)


def _overlap_matrix_t(S):
    nch = S // CMP_STRIDE
    n_cmp = (S - CMP_LEN) // CMP_STRIDE + 1
    cs = np.arange(nch)[None, :] * CMP_STRIDE
    ss = np.arange(S // SEL_LEN)[:, None] * SEL_LEN
    ov = (cs <= ss + SEL_LEN - 1) & (cs + CMP_LEN - 1 >= ss) & (np.arange(nch)[None, :] < n_cmp)
    return ov.astype(np.float32)


def _gate_expand():
    ex = np.zeros((LANES, 3 * NSA_HEADS * NSA_DIM), np.float32)
    for hd in range(NSA_HEADS):
        for c in range(3):
            ex[hd * 3 + c, c * NSA_HEADS * NSA_DIM + hd * NSA_DIM:
               c * NSA_HEADS * NSA_DIM + (hd + 1) * NSA_DIM] = 1.0
    return ex


def _pad_rows(w, rows):
    return jnp.concatenate([w, jnp.zeros((rows - w.shape[0],) + w.shape[1:], w.dtype)], axis=0)


def kernel(x, positions, norm_mix_pre, norm_mix_post, w_in, conv_w, mla_q_norm, mla_w_uq, mla_kv_norm, mla_w_ukv, nsa_cmp_pos_k, nsa_cmp_pos_v, nsa_cmp_w_k, nsa_cmp_w_v, w_branch_conv, w_branch_mla, w_branch_nsa, w_out, norm_ffn_pre, norm_ffn_post, ffn_w_up, ffn_conv_w, ffn_conv_b, ffn_w_down):
    B, S, D = x.shape
    T = B * S
    depth = w_in.shape[0]
    DC = conv_w.shape[2]
    F = ffn_w_down.shape[1]
    QL = mla_q_norm.shape[1]
    KL = mla_kv_norm.shape[1]
    tq = 512
    assert S % tq == 0 and WINDOW == tq and N_SEL <= S // SEL_LEN <= LANES
    tm_proj, tm_merge, tm_ffn, ffn_chunk = 512, 256, 256, 1408
    assert F % ffn_chunk == 0 and ffn_chunk % LANES == 0

    x2 = x.reshape(T, D)
    pos2 = positions.reshape(T, 1)
    rc = jnp.asarray(_rope_consts())
    ovt = jnp.asarray(_overlap_matrix_t(S), BF16)
    ex = jnp.asarray(_gate_expand(), BF16)
    half = CMP_LEN // 2 * NSA_DIM
    o_att = 3 * D + 3 * DC

    for l in range(depth):
        w = w_in[l]
        o_kr = o_att + QL + KL
        z = lambda n: jnp.zeros((D, n), F32)
        w_att = jnp.concatenate(
            [w[:, o_att:o_kr], z(MLA_NOPE), w[:, o_kr:o_kr + MLA_ROPE], z(LANES - MLA_NOPE - MLA_ROPE),
             w[:, o_kr + MLA_ROPE:o_kr + MLA_ROPE + NSA_HEADS * NSA_DIM + 6 * NSA_GROUPS * NSA_DIM],
             w[:, w.shape[1] - 3 * NSA_HEADS:], z(LANES - 3 * NSA_HEADS)], axis=1).astype(BF16)
        dq = MLA_NOPE + MLA_ROPE
        w_uq = jnp.pad(mla_w_uq[l].reshape(QL, MLA_HEADS, dq),
                       ((0, 0), (0, 0), (0, LANES - dq))).reshape(QL, MLA_HEADS * LANES).astype(BF16)
        (qm, km, vmt, nq, kc_in, vc_in, ksa, vst, kw, vwt, gl) = _project(
            x2, pos2, norm_mix_pre[l][None], w_att, mla_q_norm[l][None], w_uq,
            mla_kv_norm[l][None], mla_w_ukv[l].astype(BF16), rc, S, tm_proj)

        chunks = lambda a: a.reshape(NSA_GROUPS, T // CMP_STRIDE, CMP_STRIDE * NSA_DIM)
        wv = nsa_cmp_w_v[l].reshape(2, half, NSA_DIM)
        zv = jnp.zeros_like(wv)
        wv_pad = jnp.stack([jnp.concatenate([wv, zv], axis=2), jnp.concatenate([zv, wv], axis=2)])
        kc, vct = _compress(chunks(kc_in), chunks(vc_in),
                            nsa_cmp_pos_k[l].reshape(2, half), nsa_cmp_pos_v[l].reshape(2, half),
                            nsa_cmp_w_k[l].reshape(2, half, NSA_DIM).astype(BF16),
                            wv_pad.astype(BF16), B)

        y_mla = _mla_attention(qm, km, vmt, B, S, tq)
        o_cmp, bias = _cmp_select(nq, kc, vct, ovt, B, S, tq)
        o_slc = _slc_attention(nq, bias, ksa, vst, B, S, tq)
        o_win = _win_attention(nq, kw, vwt, B, S, tq)

        x2 = _merge(x2, norm_mix_pre[l][None], w[:, :o_att].astype(BF16),
                    _pad_rows(conv_w[l], SUBLANES), y_mla, o_cmp, o_slc, o_win, gl, ex,
                    w_branch_conv[l].astype(BF16), w_branch_mla[l].astype(BF16),
                    w_branch_nsa[l].astype(BF16), w_out[l].astype(BF16), norm_mix_post[l][None],
                    S, tm_merge)
        x2 = _ffn(x2, norm_ffn_pre[l][None], ffn_w_up[l].astype(BF16),
                  _pad_rows(ffn_conv_w[l], SUBLANES), ffn_conv_b[l][None],
                  ffn_w_down[l].astype(BF16), norm_ffn_post[l][None], S, tm_ffn, ffn_chunk)
    return x2.reshape(B, S, D)
```

```python
import functools
import math

import numpy as np
import jax
import jax.numpy as jnp
from jax import lax
from jax.experimental import pallas as pl
from jax.experimental.pallas import tpu as pltpu

F32 = jnp.float32
BF16 = jnp.bfloat16

ROPE_THETA = 500000.0
RMS_EPS = 1e-6
CONV_WIDTH = 3
MLA_HEADS = 8
MLA_NOPE = 64
MLA_ROPE = 32
MLA_V = 64
NSA_HEADS = 8
NSA_GROUPS = 2
NSA_HPG = NSA_HEADS // NSA_GROUPS
NSA_DIM = 64
NSA_ROT = NSA_DIM // 4
CMP_LEN = 32
CMP_STRIDE = 16
SEL_LEN = 64
N_SEL = 16
WINDOW = 512

LANES = 128
SUBLANES = 8
VMEM_LIMIT = 56 * 1024 * 1024
MASKED = -1e30
M_FLOOR = -1e20
SEL_BIAS = -2.0 ** 100
LAZY_SPAN = 64.0
LOG2E = math.log2(math.e)
SUM_ROWS = 16


def _cparams(sem):
    return pltpu.CompilerParams(dimension_semantics=sem, vmem_limit_bytes=VMEM_LIMIT)


def _const_spec(shape):
    nd = len(shape)
    return pl.BlockSpec(shape, lambda *_: (0,) * nd, pipeline_mode=pl.Buffered(1))


def _rms(x, g):
    return x * lax.rsqrt(jnp.mean(x * x, axis=-1, keepdims=True) + RMS_EPS) * g


def _dot(a, b):
    return jnp.dot(a, b, preferred_element_type=F32)


def _dot_nt(a, b):
    return lax.dot_general(a, b, (((1,), (1,)), ((), ())), preferred_element_type=F32)


def _split_bf16(a, terms):
    pieces, rem = [], a
    for _ in range(terms):
        piece = rem.astype(BF16)
        rem = rem - piece.astype(F32)
        pieces.append(piece)
    return pieces


def _rope(x, cos, sin_up, sin_dn, half):
    return (x * cos + pltpu.roll(x, half, axis=1) * sin_up
            + pltpu.roll(x, LANES - half, axis=1) * sin_dn)


def _proj_kernel(x_ref, pos_ref, g_ref, watt_ref, qg_ref, wuq_ref, kvg_ref, wukv_ref, rc_ref,
                 qm_ref, km_ref, vmt_ref, nq_ref, kc_ref, vc_ref, ksa_ref, vst_ref, kw_ref, vwt_ref,
                 gl_ref, *, mla_scale, nsa_scale, tiles_per_seq):
    tm = x_ref.shape[0]
    x = x_ref[...]
    h = _rms(x, g_ref[...]).astype(BF16)
    p = _dot(h, watt_ref[...])
    pos = pos_ref[...].astype(F32)
    rc = rc_ref[...]
    ang_m = pos * rc[0:1]
    cos_m, sin_m = jnp.cos(ang_m), jnp.sin(ang_m)
    sup_m, sdn_m = sin_m * rc[1:2], sin_m * rc[2:3]
    ang_n = pos * rc[3:4]
    cos_n, sin_n = jnp.cos(ang_n), jnp.sin(ang_n)
    sup_n, sdn_n = sin_n * rc[4:5], sin_n * rc[5:6]
    lane = lax.broadcasted_iota(jnp.int32, (tm, LANES), 1)
    low = lane < NSA_DIM
    ones_rows = jnp.where(lax.broadcasted_iota(jnp.int32, (SUM_ROWS, tm), 0) == 0, 1.0, 0.0)

    def with_sum_rows(vt):
        return jnp.concatenate([vt, ones_rows], axis=0).astype(BF16)

    def halves(slab):
        return (jnp.where(low, slab, 0.0), jnp.where(low, pltpu.roll(slab, NSA_DIM, axis=1), 0.0))

    cqn = _rms(p[:, 0:384], qg_ref[...]).astype(BF16)
    qf = _dot(cqn, wuq_ref[...])
    for hd in range(MLA_HEADS):
        slab = qf[:, hd * LANES:(hd + 1) * LANES]
        slab = _rope(slab, cos_m, sup_m, sdn_m, MLA_ROPE // 2) * mla_scale
        qm_ref[hd] = slab.astype(BF16)
    ckvn = _rms(p[:, 384:640], kvg_ref[...]).astype(BF16)
    kv = _dot(ckvn, wukv_ref[...])
    kr = _rope(p[:, 640:768], cos_m, sup_m, sdn_m, MLA_ROPE // 2)
    for hd in range(MLA_HEADS):
        slab = kv[:, hd * LANES:(hd + 1) * LANES]
        km_ref[hd] = jnp.where(lane < MLA_NOPE, slab, kr).astype(BF16)
        vmt_ref[hd] = with_sum_rows(slab.T[MLA_NOPE:])
    for pr in range(NSA_HEADS // 2):
        slab = p[:, 768 + pr * LANES:768 + (pr + 1) * LANES]
        slab = _rope(slab, cos_n, sup_n, sdn_n, NSA_ROT // 2) * nsa_scale
        q0, q1 = halves(slab)
        nq_ref[2 * pr] = q0.astype(BF16)
        nq_ref[2 * pr + 1] = q1.astype(BF16)
    nsa = lambda idx: p[:, 1280 + idx * LANES:1280 + (idx + 1) * LANES]
    kcs = _rope(nsa(0), cos_n, sup_n, sdn_n, NSA_ROT // 2)
    kc_ref[0] = kcs[:, :NSA_DIM].astype(BF16)
    kc_ref[1] = pltpu.roll(kcs, NSA_DIM, axis=1)[:, :NSA_DIM].astype(BF16)
    vcs = nsa(1)
    vc_ref[0] = vcs[:, :NSA_DIM].astype(BF16)
    vc_ref[1] = pltpu.roll(vcs, NSA_DIM, axis=1)[:, :NSA_DIM].astype(BF16)
    t_seq = (pl.program_id(0) % tiles_per_seq) * tm + lax.broadcasted_iota(jnp.int32, (tm, LANES), 0)
    onehot = jnp.where(lane == lax.shift_right_logical(t_seq, int(math.log2(SEL_LEN))), 1.0, 0.0)
    k0, k1 = halves(_rope(nsa(2), cos_n, sup_n, sdn_n, NSA_ROT // 2))
    for g, kg in enumerate((k0, k1)):
        ksa_ref[g, :, 0:LANES] = onehot.astype(BF16)
        ksa_ref[g, :, LANES:2 * LANES] = kg.astype(BF16)
    vst = nsa(3).T
    vst_ref[0] = with_sum_rows(vst[:NSA_DIM])
    vst_ref[1] = with_sum_rows(vst[NSA_DIM:])
    kws = _rope(nsa(4), cos_n, sup_n, sdn_n, NSA_ROT // 2)
    kw_ref[0] = kws[:, :NSA_DIM].astype(BF16)
    kw_ref[1] = pltpu.roll(kws, NSA_DIM, axis=1)[:, :NSA_DIM].astype(BF16)
    vwt = nsa(5).T
    vwt_ref[0] = with_sum_rows(vwt[:NSA_DIM])
    vwt_ref[1] = with_sum_rows(vwt[NSA_DIM:])
    gl_ref[...] = p[:, 2048:2176]


def _rope_consts():
    rc = np.zeros((SUBLANES, LANES), np.float32)
    hm = MLA_ROPE // 2
    fm = ROPE_THETA ** (-np.arange(hm, dtype=np.float32) / hm)
    rc[0, MLA_NOPE:MLA_NOPE + hm] = fm
    rc[0, MLA_NOPE + hm:MLA_NOPE + 2 * hm] = fm
    rc[1, MLA_NOPE + hm:MLA_NOPE + 2 * hm] = 1.0
    rc[2, MLA_NOPE:MLA_NOPE + hm] = -1.0
    hn = NSA_ROT // 2
    fn = ROPE_THETA ** (-np.arange(hn, dtype=np.float32) / hn)
    for base in (0, NSA_DIM):
        rc[3, base:base + hn] = fn
        rc[3, base + hn:base + 2 * hn] = fn
        rc[4, base + hn:base + 2 * hn] = 1.0
        rc[5, base:base + hn] = -1.0
    return rc


def _project(x2, pos2, g_pre, w_att, q_g, w_uq, kv_g, w_ukv, rc, S, tm):
    T, D = x2.shape
    n_att = w_att.shape[1]
    rows = lambda n, d: (jax.ShapeDtypeStruct((n, T, d), BF16), pl.BlockSpec((n, tm, d), lambda i: (0, i, 0)))
    cols = lambda n, d: (jax.ShapeDtypeStruct((n, d, T), BF16), pl.BlockSpec((n, d, tm), lambda i: (0, 0, i)))
    outs = [rows(MLA_HEADS, LANES), rows(MLA_HEADS, LANES), cols(MLA_HEADS, MLA_V + SUM_ROWS),
            rows(NSA_HEADS, LANES), rows(NSA_GROUPS, NSA_DIM), rows(NSA_GROUPS, NSA_DIM),
            rows(NSA_GROUPS, 2 * LANES), cols(NSA_GROUPS, NSA_DIM + SUM_ROWS), rows(NSA_GROUPS, NSA_DIM),
            cols(NSA_GROUPS, NSA_DIM + SUM_ROWS),
            (jax.ShapeDtypeStruct((T, LANES), F32), pl.BlockSpec((tm, LANES), lambda i: (i, 0)))]
    kern = functools.partial(_proj_kernel,
                             mla_scale=float((MLA_NOPE + MLA_ROPE) ** -0.5 * LOG2E),
                             nsa_scale=float(NSA_DIM ** -0.5 * LOG2E),
                             tiles_per_seq=S // tm)
    return pl.pallas_call(
        kern,
        grid=(T // tm,),
        in_specs=[
            pl.BlockSpec((tm, D), lambda i: (i, 0)),
            pl.BlockSpec((tm, 1), lambda i: (i, 0)),
            _const_spec((1, D)),
            _const_spec((D, n_att)),
            _const_spec((1, q_g.shape[1])),
            _const_spec(w_uq.shape),
            _const_spec((1, kv_g.shape[1])),
            _const_spec(w_ukv.shape),
            _const_spec((SUBLANES, LANES)),
        ],
        out_specs=[o[1] for o in outs],
        out_shape=[o[0] for o in outs],
        compiler_params=_cparams(("parallel",)),
        name="proj",
    )(x2, pos2, g_pre, w_att, q_g, w_uq, kv_g, w_ukv, rc)


def _compress_kernel(xk_ref, xv_ref, pk_ref, pv_ref, wk_ref, wv_ref, ok_ref, ovt_ref):
    n = xk_ref.shape[1]
    vboth = None
    for g in range(NSA_GROUPS):
        xk = xk_ref[g].astype(F32)
        lo = _dot((xk + pk_ref[0:1]).astype(BF16), wk_ref[0])
        hi = _dot((xk + pk_ref[1:2]).astype(BF16), wk_ref[1])
        ok_ref[g] = (lo + pltpu.roll(hi, n - 1, axis=0)).astype(BF16)
        xv = xv_ref[g].astype(F32)
        lo = _dot((xv + pv_ref[0:1]).astype(BF16), wv_ref[g, 0])
        hi = _dot((xv + pv_ref[1:2]).astype(BF16), wv_ref[g, 1])
        vg = lo + pltpu.roll(hi, n - 1, axis=0)
        vboth = vg if vboth is None else vboth + vg
    ovt_ref[0] = vboth.T.astype(BF16)


def _compress(xk, xv, pk, pv, wk, wv, B):
    G, TC, CW = xk.shape
    nch = TC // B
    xspec = pl.BlockSpec((G, nch, CW), lambda b: (0, b, 0))
    return pl.pallas_call(
        _compress_kernel,
        grid=(B,),
        in_specs=[xspec, xspec, _const_spec(pk.shape), _const_spec(pv.shape),
                  _const_spec(wk.shape), _const_spec(wv.shape)],
        out_specs=[pl.BlockSpec((G, nch, NSA_DIM), lambda b: (0, b, 0)),
                   pl.BlockSpec((1, G * NSA_DIM, nch), lambda b: (b, 0, 0))],
        out_shape=[jax.ShapeDtypeStruct((G, TC, NSA_DIM), BF16),
                   jax.ShapeDtypeStruct((B, G * NSA_DIM, nch), BF16)],
        compiler_params=_cparams(("parallel",)),
        name="compress",
    )(xk, xv, pk, pv, wk, wv)


def _cmp_select_kernel(q_ref, kc_ref, vct_ref, ovt_ref, o_ref, bias_ref, *, tq, n_pick):
    q0 = pl.program_id(1) * tq
    nch = kc_ref.shape[1]
    nblk = ovt_ref.shape[0]
    t_c = q0 + lax.broadcasted_iota(jnp.int32, (nch, tq), 1)
    n_c = lax.broadcasted_iota(jnp.int32, (nch, tq), 0)
    cmask = (n_c * CMP_STRIDE + (CMP_LEN - 1)) <= t_c
    t_b = q0 + lax.broadcasted_iota(jnp.int32, (nblk, tq), 1)
    blk = lax.broadcasted_iota(jnp.int32, (nblk, tq), 0)
    blk_f = blk.astype(F32)
    cur = lax.shift_right_logical(t_b, int(math.log2(SEL_LEN)))
    forced = (blk == 0) | (blk == cur) | (blk == cur - 1)
    causal = blk * SEL_LEN <= t_b
    outs = []
    for g in range(NSA_GROUPS):
        psum = None
        for j in range(NSA_HPG):
            q = q_ref[g * NSA_HPG + j][:, :NSA_DIM]
            s = jnp.where(cmask, _dot_nt(kc_ref[g], q), MASKED)
            m = jnp.max(s, axis=0, keepdims=True)
            p = jnp.where(cmask, jnp.exp2(s - m), 0.0)
            l = jnp.sum(p, axis=0, keepdims=True)
            pn = p * jnp.where(l > 0.0, 1.0 / l, 0.0)
            outs.append(_dot(vct_ref[0, g * NSA_DIM:(g + 1) * NSA_DIM, :], pn.astype(BF16)))
            psum = pn if psum is None else psum + pn
        imp = None
        for piece in _split_bf16(psum, 3):
            d = _dot(ovt_ref[...], piece)
            imp = d if imp is None else imp + d
        v = jnp.where(forced | jnp.logical_not(causal), -1.0, imp)
        sel = jnp.where(forced, 1.0, 0.0)
        for _ in range(n_pick):
            mx = jnp.max(v, axis=0, keepdims=True)
            first = jnp.min(jnp.where(v == mx, blk_f, float(nblk)), axis=0, keepdims=True)
            hit = blk_f == first
            sel = jnp.where(hit & (mx >= 0.0), 1.0, sel)
            v = jnp.where(hit, -2.0, v)
        bias_ref[g] = ((sel - 1.0) * -SEL_BIAS).T.astype(BF16)
    o_ref[...] = jnp.concatenate(outs, axis=0).T.astype(BF16)


def _cmp_select(nq, kc, vct, ovt, B, S, tq):
    T = nq.shape[1]
    nch = kc.shape[1] // B
    nblk = ovt.shape[0]
    nqt = S // tq
    kern = functools.partial(_cmp_select_kernel, tq=tq, n_pick=N_SEL - 3)
    return pl.pallas_call(
        kern,
        grid=(B, nqt),
        in_specs=[
            pl.BlockSpec((NSA_HEADS, tq, LANES), lambda b, i: (0, b * nqt + i, 0)),
            pl.BlockSpec((NSA_GROUPS, nch, NSA_DIM), lambda b, i: (0, b, 0)),
            pl.BlockSpec((1, NSA_GROUPS * NSA_DIM, nch), lambda b, i: (b, 0, 0)),
            _const_spec(ovt.shape),
        ],
        out_specs=[pl.BlockSpec((tq, NSA_HEADS * NSA_DIM), lambda b, i: (b * nqt + i, 0)),
                   pl.BlockSpec((NSA_GROUPS, tq, nblk), lambda b, i: (0, b * nqt + i, 0))],
        out_shape=[jax.ShapeDtypeStruct((T, NSA_HEADS * NSA_DIM), BF16),
                   jax.ShapeDtypeStruct((NSA_GROUPS, T, nblk), BF16)],
        compiler_params=_cparams(("parallel", "parallel")),
        name="cmp_select",
    )(nq, kc, vct, ovt)


def _flash_heads(nheads, kqv, mask, src, dst, lazy=False, unroll=True):
    def scores(hd):
        k, q, _ = kqv(hd)
        return _dot_nt(k, q)

    def update(hd, s):
        if mask is not None:
            s = jnp.where(mask, s, MASKED)
        tmax = jnp.max(s, axis=0, keepdims=True)
        vt = kqv(hd)[2]
        rise = None
        if src is None:
            m_new = jnp.maximum(tmax, M_FLOOR)
            acc = _dot(vt, jnp.exp2(s - m_new).astype(BF16))
        else:
            m_old = src[0][hd]
            m_new = jnp.maximum(m_old, tmax)
            alpha = jnp.exp2(m_old - m_new)
            if lazy:
                acc = (src[1][hd] + _dot(vt, jnp.exp2(s - m_old).astype(BF16))) * alpha
                rise = tmax - m_old
            else:
                acc = alpha * src[1][hd] + _dot(vt, jnp.exp2(s - m_new).astype(BF16))
        dst[0][hd] = m_new
        dst[1][hd] = acc
        return rise

    if not unroll:
        def body(hd, carry):
            update(hd, scores(hd))
            return carry
        lax.fori_loop(0, nheads, body, 0)
        return None
    ahead = 2
    pending = [scores(hd) for hd in range(min(ahead, nheads))]
    worst = None
    for hd in range(nheads):
        s = pending.pop(0)
        if hd + ahead < nheads:
            pending.append(scores(hd + ahead))
        rise = update(hd, s)
        if rise is not None:
            worst = rise if worst is None else jnp.maximum(worst, rise)
    return worst


def _flash_init(m_ref, acc_ref):
    m_ref[...] = jnp.full(m_ref.shape, M_FLOOR, F32)
    acc_ref[...] = jnp.zeros(acc_ref.shape, F32)


def _flash_finalize(o_ref, acc_ref):
    nh, rows, tq = acc_ref.shape
    dv = rows - SUM_ROWS
    o = acc_ref[:, :dv, :] * (1.0 / acc_ref[:, dv:dv + 1, :])
    o_ref[...] = o.reshape(nh * dv, tq).T.astype(BF16)


def _flash_scratch(nh, dv, tq, slots=None):
    lead = (nh,) if slots is None else (slots, nh)
    return [pltpu.VMEM(lead + (1, tq), F32), pltpu.VMEM(lead + (dv + SUM_ROWS, tq), F32)]


def _causal_pairs(nqt):
    pairs = [(i, j) for i in range(nqt) for j in range(i + 1)]
    return (jnp.asarray([p[0] for p in pairs], jnp.int32), jnp.asarray([p[1] for p in pairs], jnp.int32))


def _tile_iota(tq):
    return (lax.broadcasted_iota(jnp.int32, (tq, tq), 0), lax.broadcasted_iota(jnp.int32, (tq, tq), 1))


def _causal_flash_step(i, j, nheads, kqv, tq, o_ref, m_ref, acc_ref, slot_ref):
    key, qry = _tile_iota(tq)

    @pl.when(j == 0)
    def _():
        mask = key <= qry + jnp.where(i > 0, tq, 0)
        _flash_heads(nheads, kqv, mask, None, (m_ref.at[0], acc_ref.at[0]))
        slot_ref[0] = 0

    def lazy_step(mask):
        cur = slot_ref[0]
        src = (m_ref.at[cur], acc_ref.at[cur])
        dst = (m_ref.at[1 - cur], acc_ref.at[1 - cur])
        worst = _flash_heads(nheads, kqv, mask, src, dst, lazy=True)

        @pl.when(jnp.max(worst) > LAZY_SPAN)
        def _():
            _flash_heads(nheads, kqv, mask, src, dst, unroll=False)

        slot_ref[0] = 1 - cur

    @pl.when((j > 0) & (j < i))
    def _():
        lazy_step(None)

    @pl.when((j > 0) & (j == i))
    def _():
        lazy_step(key <= qry)

    @pl.when(j == i)
    def _():
        _flash_finalize(o_ref, acc_ref.at[slot_ref[0]])


def _mla_kernel(qi_ref, ki_ref, q_ref, k_ref, vt_ref, o_ref, m_ref, acc_ref, slot_ref, *, tq):
    i, j = qi_ref[pl.program_id(1)], ki_ref[pl.program_id(1)]
    _causal_flash_step(i, j, MLA_HEADS, lambda hd: (k_ref[hd], q_ref[hd], vt_ref[hd]), tq,
                       o_ref, m_ref, acc_ref, slot_ref)


def _mla_attention(qm, km, vmt, B, S, tq):
    T = qm.shape[1]
    nqt = S // tq
    kern = functools.partial(_mla_kernel, tq=tq)
    qi, ki = _causal_pairs(nqt)
    return pl.pallas_call(
        kern,
        grid_spec=pltpu.PrefetchScalarGridSpec(
            num_scalar_prefetch=2,
            grid=(B, qi.shape[0]),
            in_specs=[pl.BlockSpec((MLA_HEADS, tq, LANES), lambda b, p, qi, ki: (0, b * nqt + qi[p], 0)),
                      pl.BlockSpec((MLA_HEADS, tq, LANES), lambda b, p, qi, ki: (0, b * nqt + ki[p], 0)),
                      pl.BlockSpec((MLA_HEADS, MLA_V + SUM_ROWS, tq), lambda b, p, qi, ki: (0, 0, b * nqt + ki[p]))],
            out_specs=pl.BlockSpec((tq, MLA_HEADS * MLA_V), lambda b, p, qi, ki: (b * nqt + qi[p], 0)),
            scratch_shapes=_flash_scratch(MLA_HEADS, MLA_V, tq, slots=2) + [pltpu.SMEM((1,), jnp.int32)]),
        out_shape=jax.ShapeDtypeStruct((T, MLA_HEADS * MLA_V), BF16),
        compiler_params=_cparams(("parallel", "arbitrary")),
        name="mla_attn",
    )(qi, ki, qm, km, vmt)


def _slc_kernel(qi_ref, ki_ref, q_ref, bias_ref, k_ref, vt_ref, o_ref, qa_ref, m_ref, acc_ref,
                slot_ref, *, tq):
    i, j = qi_ref[pl.program_id(1)], ki_ref[pl.program_id(1)]
    nblk = bias_ref.shape[2]

    @pl.when(j == 0)
    def _():
        if nblk < LANES:
            qa_ref[...] = jnp.zeros(qa_ref.shape, BF16)
        for hd in range(NSA_HEADS):
            qa_ref[hd, :, 0:nblk] = bias_ref[hd // NSA_HPG]
            qa_ref[hd, :, LANES:2 * LANES] = q_ref[hd]

    def kqv(hd):
        g = hd // NSA_HPG if isinstance(hd, int) else lax.div(hd, NSA_HPG)
        return k_ref[g], qa_ref[hd], vt_ref[g]

    _causal_flash_step(i, j, NSA_HEADS, kqv, tq, o_ref, m_ref, acc_ref, slot_ref)


def _slc_attention(nq, bias, ksa, vst, B, S, tq):
    T = nq.shape[1]
    nqt = S // tq
    nblk = bias.shape[2]
    kern = functools.partial(_slc_kernel, tq=tq)
    qi, ki = _causal_pairs(nqt)
    qmap = lambda b, p, qi, ki: (0, b * nqt + qi[p], 0)
    return pl.pallas_call(
        kern,
        grid_spec=pltpu.PrefetchScalarGridSpec(
            num_scalar_prefetch=2,
            grid=(B, qi.shape[0]),
            in_specs=[pl.BlockSpec((NSA_HEADS, tq, LANES), qmap),
                      pl.BlockSpec((NSA_GROUPS, tq, nblk), qmap),
                      pl.BlockSpec((NSA_GROUPS, tq, 2 * LANES), lambda b, p, qi, ki: (0, b * nqt + ki[p], 0)),
                      pl.BlockSpec((NSA_GROUPS, NSA_DIM + SUM_ROWS, tq), lambda b, p, qi, ki: (0, 0, b * nqt + ki[p]))],
            out_specs=pl.BlockSpec((tq, NSA_HEADS * NSA_DIM), lambda b, p, qi, ki: (b * nqt + qi[p], 0)),
            scratch_shapes=[pltpu.VMEM((NSA_HEADS, tq, 2 * LANES), BF16)]
                           + _flash_scratch(NSA_HEADS, NSA_DIM, tq, slots=2)
                           + [pltpu.SMEM((1,), jnp.int32)]),
        out_shape=jax.ShapeDtypeStruct((T, NSA_HEADS * NSA_DIM), BF16),
        compiler_params=_cparams(("parallel", "arbitrary")),
        name="slc_attn",
    )(qi, ki, nq, bias, ksa, vst)


def _win_kernel(q_ref, k_ref, vt_ref, o_ref, m_ref, acc_ref, *, tq):
    i, j = pl.program_id(1), pl.program_id(2)

    @pl.when(j == 0)
    def _():
        _flash_init(m_ref, acc_ref)

    def step(prev):
        key, qry = _tile_iota(tq)
        mask = (key > qry) if prev else (key <= qry)
        _flash_heads(NSA_HEADS, lambda hd: (k_ref[hd // NSA_HPG], q_ref[hd][:, :NSA_DIM],
                                            vt_ref[hd // NSA_HPG]), mask, (m_ref, acc_ref), (m_ref, acc_ref))

    @pl.when((j == 0) & (i > 0))
    def _():
        step(True)

    @pl.when(j == 1)
    def _():
        step(False)
        _flash_finalize(o_ref, acc_ref)


def _win_attention(nq, kw, vwt, B, S, tq):
    T = nq.shape[1]
    nqt = S // tq
    kern = functools.partial(_win_kernel, tq=tq)
    kidx = lambda b, i, j: b * nqt + jnp.maximum(i - 1 + j, 0)
    return pl.pallas_call(
        kern,
        grid=(B, nqt, 2),
        in_specs=[pl.BlockSpec((NSA_HEADS, tq, LANES), lambda b, i, j: (0, b * nqt + i, 0)),
                  pl.BlockSpec((NSA_GROUPS, tq, NSA_DIM), lambda b, i, j: (0, kidx(b, i, j), 0)),
                  pl.BlockSpec((NSA_GROUPS, NSA_DIM + SUM_ROWS, tq), lambda b, i, j: (0, 0, kidx(b, i, j)))],
        out_specs=pl.BlockSpec((tq, NSA_HEADS * NSA_DIM), lambda b, i, j: (b * nqt + i, 0)),
        out_shape=jax.ShapeDtypeStruct((T, NSA_HEADS * NSA_DIM), BF16),
        scratch_shapes=_flash_scratch(NSA_HEADS, NSA_DIM, tq),
        compiler_params=_cparams(("parallel", "parallel", "arbitrary")),
        name="win_attn",
    )(nq, kw, vwt)


def _causal_conv(u, carry, w_ref):
    row = lax.broadcasted_iota(jnp.int32, u.shape, 0)
    c1 = carry[SUBLANES - 1:SUBLANES]
    c2 = carry[SUBLANES - 2:SUBLANES - 1]
    u1 = jnp.where(row == 0, c1, pltpu.roll(u, 1, axis=0))
    u2 = jnp.where(row == 0, c2, jnp.where(row == 1, c1, pltpu.roll(u, 2, axis=0)))
    return w_ref[0:1] * u2 + w_ref[1:2] * u1 + w_ref[2:3] * u


def _merge_kernel(x_ref, g_ref, wgc_ref, cw_ref, ym_ref, oc_ref, os_ref, ow_ref, gl_ref, ex_ref,
                  wbc_ref, wbm_ref, wbn_ref, wo_ref, gp_ref, out_ref, carry_ref, *, tiles_per_seq):
    D = x_ref.shape[1]
    DC = wbc_ref.shape[0]

    @pl.when(pl.program_id(0) % tiles_per_seq == 0)
    def _():
        carry_ref[...] = jnp.zeros(carry_ref.shape, F32)

    x = x_ref[...]
    h = _rms(x, g_ref[...]).astype(BF16)
    p = _dot(h, wgc_ref[...])
    u = p[:, 3 * D + DC:3 * D + 2 * DC] * p[:, 3 * D + 2 * DC:3 * D + 3 * DC]
    y_conv = p[:, 3 * D:3 * D + DC] * _causal_conv(u, carry_ref[...], cw_ref)
    carry_ref[...] = u[u.shape[0] - SUBLANES:]
    hi, lo = _split_bf16(gl_ref[...], 2)
    gate = jax.nn.sigmoid(_dot(hi, ex_ref[...]) + _dot(lo, ex_ref[...]))
    HD = oc_ref.shape[1]
    y_nsa = (gate[:, 0:HD] * oc_ref[...].astype(F32) + gate[:, HD:2 * HD] * os_ref[...].astype(F32)
             + gate[:, 2 * HD:3 * HD] * ow_ref[...].astype(F32))
    merged = (jax.nn.sigmoid(p[:, 0:D]) * _dot(y_conv.astype(BF16), wbc_ref[...])
              + jax.nn.sigmoid(p[:, D:2 * D]) * _dot(ym_ref[...], wbm_ref[...])
              + jax.nn.sigmoid(p[:, 2 * D:3 * D]) * _dot(y_nsa.astype(BF16), wbn_ref[...]))
    out_ref[...] = x + _rms(_dot(merged.astype(BF16), wo_ref[...]), gp_ref[...])


def _merge(x2, g_pre, w_gc, conv_w, y_mla, o_cmp, o_slc, o_win, gl, ex, wbc, wbm, wbn, wo, g_post,
           S, tm):
    T, D = x2.shape
    HD = y_mla.shape[1]
    row = lambda w: pl.BlockSpec((tm, w), lambda i: (i, 0))
    kern = functools.partial(_merge_kernel, tiles_per_seq=S // tm)
    return pl.pallas_call(
        kern,
        grid=(T // tm,),
        in_specs=[row(D), _const_spec((1, D)), _const_spec(w_gc.shape), _const_spec(conv_w.shape),
                  row(HD), row(HD), row(HD), row(HD), row(LANES), _const_spec(ex.shape),
                  _const_spec(wbc.shape), _const_spec(wbm.shape), _const_spec(wbn.shape),
                  _const_spec(wo.shape), _const_spec((1, D))],
        out_specs=row(D),
        out_shape=jax.ShapeDtypeStruct((T, D), F32),
        scratch_shapes=[pltpu.VMEM((SUBLANES, wbc.shape[0]), F32)],
        compiler_params=_cparams(("arbitrary",)),
        name="merge",
    )(x2, g_pre, w_gc, conv_w, y_mla, o_cmp, o_slc, o_win, gl, ex, wbc, wbm, wbn, wo, g_post)


def _ffn_kernel(x_ref, g_ref, wup_ref, cw_ref, cb_ref, wdn_ref, gp_ref, out_ref, carry_ref,
                *, tiles_per_seq, chunk):
    F = wdn_ref.shape[0]

    @pl.when(pl.program_id(0) % tiles_per_seq == 0)
    def _():
        carry_ref[...] = jnp.zeros(carry_ref.shape, F32)

    x = x_ref[...]
    h = _rms(x, g_ref[...]).astype(BF16)
    acc = None
    for c0 in range(0, F, chunk):
        a = _dot(h, wup_ref[:, c0:c0 + chunk])
        b = _dot(h, wup_ref[:, F + c0:F + c0 + chunk])
        ac = _causal_conv(a, carry_ref[:, c0:c0 + chunk], cw_ref.at[:, c0:c0 + chunk])
        ac = ac + cb_ref[:, c0:c0 + chunk]
        carry_ref[:, c0:c0 + chunk] = a[a.shape[0] - SUBLANES:]
        d = _dot((jax.nn.gelu(ac) * b).astype(BF16), wdn_ref[c0:c0 + chunk, :])
        acc = d if acc is None else acc + d
    out_ref[...] = x + _rms(acc, gp_ref[...])


def _ffn(x2, g_pre, w_up, conv_w, conv_b, w_dn, g_post, S, tm, chunk):
    T, D = x2.shape
    F = w_dn.shape[0]
    row = pl.BlockSpec((tm, D), lambda i: (i, 0))
    kern = functools.partial(_ffn_kernel, tiles_per_seq=S // tm, chunk=chunk)
    return pl.pallas_call(
        kern,
        grid=(T // tm,),
        in_specs=[row, _const_spec((1, D)), _const_spec(w_up.shape), _const_spec(conv_w.shape),
                  _const_spec((1, F)), _const_spec(w_dn.shape), _const_spec((1, D))],
        out_specs=row,
        out_shape=jax.ShapeDtypeStruct((T, D), F32),
        scratch_shapes=[pltpu.VMEM((SUBLANES, F), F32)],
        compiler_params=_cparams(("arbitrary",)),
        name="ffn",
    )(x2, g_pre, w_up, conv_w, conv_b, w_dn, g_post)


def _overlap_matrix_t(S):
    nch = S // CMP_STRIDE
    n_cmp = (S - CMP_LEN) // CMP_STRIDE + 1
    cs = np.arange(nch)[None, :] * CMP_STRIDE
    ss = np.arange(S // SEL_LEN)[:, None] * SEL_LEN
    ov = (cs <= ss + SEL_LEN - 1) & (cs + CMP_LEN - 1 >= ss) & (np.arange(nch)[None, :] < n_cmp)
    return ov.astype(np.float32)


def _gate_expand():
    ex = np.zeros((LANES, 3 * NSA_HEADS * NSA_DIM), np.float32)
    for hd in range(NSA_HEADS):
        for c in range(3):
            ex[hd * 3 + c, c * NSA_HEADS * NSA_DIM + hd * NSA_DIM:
               c * NSA_HEADS * NSA_DIM + (hd + 1) * NSA_DIM] = 1.0
    return ex


def _pad_rows(w, rows):
    return jnp.concatenate([w, jnp.zeros((rows - w.shape[0],) + w.shape[1:], w.dtype)], axis=0)


def kernel(x, positions, norm_mix_pre, norm_mix_post, w_in, conv_w, mla_q_norm, mla_w_uq, mla_kv_norm, mla_w_ukv, nsa_cmp_pos_k, nsa_cmp_pos_v, nsa_cmp_w_k, nsa_cmp_w_v, w_branch_conv, w_branch_mla, w_branch_nsa, w_out, norm_ffn_pre, norm_ffn_post, ffn_w_up, ffn_conv_w, ffn_conv_b, ffn_w_down):
    B, S, D = x.shape
    T = B * S
    depth = w_in.shape[0]
    DC = conv_w.shape[2]
    F = ffn_w_down.shape[1]
    QL = mla_q_norm.shape[1]
    KL = mla_kv_norm.shape[1]
    tq = 512
    assert S % tq == 0 and WINDOW == tq and N_SEL <= S // SEL_LEN <= LANES
    tm_proj, tm_merge, tm_ffn, ffn_chunk = 512, 256, 256, 1408
    assert F % ffn_chunk == 0 and ffn_chunk % LANES == 0

    x2 = x.reshape(T, D)
    pos2 = positions.reshape(T, 1)
    rc = jnp.asarray(_rope_consts())
    ovt = jnp.asarray(_overlap_matrix_t(S), BF16)
    ex = jnp.asarray(_gate_expand(), BF16)
    half = CMP_LEN // 2 * NSA_DIM
    o_att = 3 * D + 3 * DC

    for l in range(depth):
        w = w_in[l]
        o_kr = o_att + QL + KL
        z = lambda n: jnp.zeros((D, n), F32)
        w_att = jnp.concatenate(
            [w[:, o_att:o_kr], z(MLA_NOPE), w[:, o_kr:o_kr + MLA_ROPE], z(LANES - MLA_NOPE - MLA_ROPE),
             w[:, o_kr + MLA_ROPE:o_kr + MLA_ROPE + NSA_HEADS * NSA_DIM + 6 * NSA_GROUPS * NSA_DIM],
             w[:, w.shape[1] - 3 * NSA_HEADS:], z(LANES - 3 * NSA_HEADS)], axis=1).astype(BF16)
        dq = MLA_NOPE + MLA_ROPE
        w_uq = jnp.pad(mla_w_uq[l].reshape(QL, MLA_HEADS, dq),
                       ((0, 0), (0, 0), (0, LANES - dq))).reshape(QL, MLA_HEADS * LANES).astype(BF16)
        (qm, km, vmt, nq, kc_in, vc_in, ksa, vst, kw, vwt, gl) = _project(
            x2, pos2, norm_mix_pre[l][None], w_att, mla_q_norm[l][None], w_uq,
            mla_kv_norm[l][None], mla_w_ukv[l].astype(BF16), rc, S, tm_proj)

        chunks = lambda a: a.reshape(NSA_GROUPS, T // CMP_STRIDE, CMP_STRIDE * NSA_DIM)
        wv = nsa_cmp_w_v[l].reshape(2, half, NSA_DIM)
        zv = jnp.zeros_like(wv)
        wv_pad = jnp.stack([jnp.concatenate([wv, zv], axis=2), jnp.concatenate([zv, wv], axis=2)])
        kc, vct = _compress(chunks(kc_in), chunks(vc_in),
                            nsa_cmp_pos_k[l].reshape(2, half), nsa_cmp_pos_v[l].reshape(2, half),
                            nsa_cmp_w_k[l].reshape(2, half, NSA_DIM).astype(BF16),
                            wv_pad.astype(BF16), B)

        y_mla = _mla_attention(qm, km, vmt, B, S, tq)
        o_cmp, bias = _cmp_select(nq, kc, vct, ovt, B, S, tq)
        o_slc = _slc_attention(nq, bias, ksa, vst, B, S, tq)
        o_win = _win_attention(nq, kw, vwt, B, S, tq)

        x2 = _merge(x2, norm_mix_pre[l][None], w[:, :o_att].astype(BF16),
                    _pad_rows(conv_w[l], SUBLANES), y_mla, o_cmp, o_slc, o_win, gl, ex,
                    w_branch_conv[l].astype(BF16), w_branch_mla[l].astype(BF16),
                    w_branch_nsa[l].astype(BF16), w_out[l].astype(BF16), norm_mix_post[l][None],
                    S, tm_merge)
        x2 = _ffn(x2, norm_ffn_pre[l][None], ffn_w_up[l].astype(BF16),
                  _pad_rows(ffn_conv_w[l], SUBLANES), ffn_conv_b[l][None],
                  ffn_w_down[l].astype(BF16), norm_ffn_post[l][None], S, tm_ffn, ffn_chunk)
    return x2.reshape(B, S, D)
```

```python
import functools
import math

import numpy as np
import jax
import jax.numpy as jnp
from jax import lax
from jax.experimental import pallas as pl
from jax.experimental.pallas import tpu as pltpu

F32 = jnp.float32
BF16 = jnp.bfloat16

ROPE_THETA = 500000.0
RMS_EPS = 1e-6
CONV_WIDTH = 3
MLA_HEADS = 8
MLA_NOPE = 64
MLA_ROPE = 32
MLA_V = 64
NSA_HEADS = 8
NSA_GROUPS = 2
NSA_HPG = NSA_HEADS // NSA_GROUPS
NSA_DIM = 64
NSA_ROT = NSA_DIM // 4
CMP_LEN = 32
CMP_STRIDE = 16
SEL_LEN = 64
N_SEL = 16
WINDOW = 512

LANES = 128
SUBLANES = 8
VMEM_LIMIT = 56 * 1024 * 1024
MASKED = -1e30
M_FLOOR = -1e20
SEL_BIAS = -2.0 ** 100
LAZY_SPAN = 64.0
LOG2E = math.log2(math.e)
SUM_ROWS = 16


def _cparams(sem):
    return pltpu.CompilerParams(dimension_semantics=sem, vmem_limit_bytes=VMEM_LIMIT)


def _const_spec(shape):
    nd = len(shape)
    return pl.BlockSpec(shape, lambda *_: (0,) * nd, pipeline_mode=pl.Buffered(1))


def _rms(x, g):
    return x * lax.rsqrt(jnp.mean(x * x, axis=-1, keepdims=True) + RMS_EPS) * g


def _dot(a, b):
    return jnp.dot(a, b, preferred_element_type=F32)


def _dot_nt(a, b):
    return lax.dot_general(a, b, (((1,), (1,)), ((), ())), preferred_element_type=F32)


def _split_bf16(a, terms):
    pieces, rem = [], a
    for _ in range(terms):
        piece = rem.astype(BF16)
        rem = rem - piece.astype(F32)
        pieces.append(piece)
    return pieces


def _rope(x, cos, sin_up, sin_dn, half):
    return (x * cos + pltpu.roll(x, half, axis=1) * sin_up
            + pltpu.roll(x, LANES - half, axis=1) * sin_dn)


def _proj_kernel(x_ref, pos_ref, g_ref, watt_ref, qg_ref, wuq_ref, kvg_ref, wukv_ref, rc_ref,
                 qm_ref, km_ref, vmt_ref, nq_ref, kc_ref, vc_ref, ksa_ref, vst_ref, kw_ref, vwt_ref,
                 gl_ref, *, mla_scale, nsa_scale, tiles_per_seq):
    tm = x_ref.shape[0]
    x = x_ref[...]
    h = _rms(x, g_ref[...]).astype(BF16)
    p = _dot(h, watt_ref[...])
    pos = pos_ref[...].astype(F32)
    rc = rc_ref[...]
    ang_m = pos * rc[0:1]
    cos_m, sin_m = jnp.cos(ang_m), jnp.sin(ang_m)
    sup_m, sdn_m = sin_m * rc[1:2], sin_m * rc[2:3]
    ang_n = pos * rc[3:4]
    cos_n, sin_n = jnp.cos(ang_n), jnp.sin(ang_n)
    sup_n, sdn_n = sin_n * rc[4:5], sin_n * rc[5:6]
    lane = lax.broadcasted_iota(jnp.int32, (tm, LANES), 1)
    low = lane < NSA_DIM
    ones_rows = jnp.where(lax.broadcasted_iota(jnp.int32, (SUM_ROWS, tm), 0) == 0, 1.0, 0.0)

    def with_sum_rows(vt):
        return jnp.concatenate([vt, ones_rows], axis=0).astype(BF16)

    def halves(slab):
        return (jnp.where(low, slab, 0.0), jnp.where(low, pltpu.roll(slab, NSA_DIM, axis=1), 0.0))

    cqn = _rms(p[:, 0:384], qg_ref[...]).astype(BF16)
    qf = _dot(cqn, wuq_ref[...])
    for hd in range(MLA_HEADS):
        slab = qf[:, hd * LANES:(hd + 1) * LANES]
        slab = _rope(slab, cos_m, sup_m, sdn_m, MLA_ROPE // 2) * mla_scale
        qm_ref[0, hd] = slab.astype(BF16)
    ckvn = _rms(p[:, 384:640], kvg_ref[...]).astype(BF16)
    kv = _dot(ckvn, wukv_ref[...])
    kr = _rope(p[:, 640:768], cos_m, sup_m, sdn_m, MLA_ROPE // 2)
    for hd in range(MLA_HEADS):
        slab = kv[:, hd * LANES:(hd + 1) * LANES]
        km_ref[0, hd] = jnp.where(lane < MLA_NOPE, slab, kr).astype(BF16)
        vmt_ref[0, hd] = with_sum_rows(slab.T[MLA_NOPE:])
    for pr in range(NSA_HEADS // 2):
        slab = p[:, 768 + pr * LANES:768 + (pr + 1) * LANES]
        slab = _rope(slab, cos_n, sup_n, sdn_n, NSA_ROT // 2) * nsa_scale
        q0, q1 = halves(slab)
        nq_ref[0, 2 * pr] = q0.astype(BF16)
        nq_ref[0, 2 * pr + 1] = q1.astype(BF16)
    nsa = lambda idx: p[:, 1280 + idx * LANES:1280 + (idx + 1) * LANES]
    kcs = _rope(nsa(0), cos_n, sup_n, sdn_n, NSA_ROT // 2)
    kc_ref[0] = kcs[:, :NSA_DIM].astype(BF16)
    kc_ref[1] = pltpu.roll(kcs, NSA_DIM, axis=1)[:, :NSA_DIM].astype(BF16)
    vcs = nsa(1)
    vc_ref[0] = vcs[:, :NSA_DIM].astype(BF16)
    vc_ref[1] = pltpu.roll(vcs, NSA_DIM, axis=1)[:, :NSA_DIM].astype(BF16)
    t_seq = (pl.program_id(0) % tiles_per_seq) * tm + lax.broadcasted_iota(jnp.int32, (tm, LANES), 0)
    onehot = jnp.where(lane == lax.shift_right_logical(t_seq, int(math.log2(SEL_LEN))), 1.0, 0.0)
    k0, k1 = halves(_rope(nsa(2), cos_n, sup_n, sdn_n, NSA_ROT // 2))
    for g, kg in enumerate((k0, k1)):
        ksa_ref[0, g, :, 0:LANES] = onehot.astype(BF16)
        ksa_ref[0, g, :, LANES:2 * LANES] = kg.astype(BF16)
    vst = nsa(3).T
    vst_ref[0, 0] = with_sum_rows(vst[:NSA_DIM])
    vst_ref[0, 1] = with_sum_rows(vst[NSA_DIM:])
    kws = _rope(nsa(4), cos_n, sup_n, sdn_n, NSA_ROT // 2)
    kw_ref[0] = kws[:, :NSA_DIM].astype(BF16)
    kw_ref[1] = pltpu.roll(kws, NSA_DIM, axis=1)[:, :NSA_DIM].astype(BF16)
    vwt = nsa(5).T
    vwt_ref[0] = with_sum_rows(vwt[:NSA_DIM])
    vwt_ref[1] = with_sum_rows(vwt[NSA_DIM:])
    gl_ref[...] = p[:, 2048:2176]


def _rope_consts():
    rc = np.zeros((SUBLANES, LANES), np.float32)
    hm = MLA_ROPE // 2
    fm = ROPE_THETA ** (-np.arange(hm, dtype=np.float32) / hm)
    rc[0, MLA_NOPE:MLA_NOPE + hm] = fm
    rc[0, MLA_NOPE + hm:MLA_NOPE + 2 * hm] = fm
    rc[1, MLA_NOPE + hm:MLA_NOPE + 2 * hm] = 1.0
    rc[2, MLA_NOPE:MLA_NOPE + hm] = -1.0
    hn = NSA_ROT // 2
    fn = ROPE_THETA ** (-np.arange(hn, dtype=np.float32) / hn)
    for base in (0, NSA_DIM):
        rc[3, base:base + hn] = fn
        rc[3, base + hn:base + 2 * hn] = fn
        rc[4, base + hn:base + 2 * hn] = 1.0
        rc[5, base:base + hn] = -1.0
    return rc


def _project(x2, pos2, g_pre, w_att, q_g, w_uq, kv_g, w_ukv, rc, S, tm):
    T, D = x2.shape
    n_att = w_att.shape[1]
    tps = S // tm
    B = T // S
    rows = lambda n, d: (jax.ShapeDtypeStruct((n, T, d), BF16), pl.BlockSpec((n, tm, d), lambda i: (0, i, 0)))
    cols = lambda n, d: (jax.ShapeDtypeStruct((n, d, T), BF16), pl.BlockSpec((n, d, tm), lambda i: (0, 0, i)))
    brows = lambda n, d: (jax.ShapeDtypeStruct((B, n, S, d), BF16),
                          pl.BlockSpec((1, n, tm, d), lambda i: (i // tps, 0, i % tps, 0)))
    bcols = lambda n, d: (jax.ShapeDtypeStruct((B, n, d, S), BF16),
                          pl.BlockSpec((1, n, d, tm), lambda i: (i // tps, 0, 0, i % tps)))
    outs = [brows(MLA_HEADS, LANES), brows(MLA_HEADS, LANES), bcols(MLA_HEADS, MLA_V + SUM_ROWS),
            brows(NSA_HEADS, LANES), rows(NSA_GROUPS, NSA_DIM), rows(NSA_GROUPS, NSA_DIM),
            brows(NSA_GROUPS, 2 * LANES), bcols(NSA_GROUPS, NSA_DIM + SUM_ROWS), rows(NSA_GROUPS, NSA_DIM),
            cols(NSA_GROUPS, NSA_DIM + SUM_ROWS),
            (jax.ShapeDtypeStruct((T, LANES), F32), pl.BlockSpec((tm, LANES), lambda i: (i, 0)))]
    kern = functools.partial(_proj_kernel,
                             mla_scale=float((MLA_NOPE + MLA_ROPE) ** -0.5 * LOG2E),
                             nsa_scale=float(NSA_DIM ** -0.5 * LOG2E),
                             tiles_per_seq=S // tm)
    return pl.pallas_call(
        kern,
        grid=(T // tm,),
        in_specs=[
            pl.BlockSpec((tm, D), lambda i: (i, 0)),
            pl.BlockSpec((tm, 1), lambda i: (i, 0)),
            _const_spec((1, D)),
            _const_spec((D, n_att)),
            _const_spec((1, q_g.shape[1])),
            _const_spec(w_uq.shape),
            _const_spec((1, kv_g.shape[1])),
            _const_spec(w_ukv.shape),
            _const_spec((SUBLANES, LANES)),
        ],
        out_specs=[o[1] for o in outs],
        out_shape=[o[0] for o in outs],
        compiler_params=_cparams(("parallel",)),
        name="proj",
    )(x2, pos2, g_pre, w_att, q_g, w_uq, kv_g, w_ukv, rc)


def _compress_kernel(xk_ref, xv_ref, pk_ref, pv_ref, wk_ref, wv_ref, ok_ref, ovt_ref):
    n = xk_ref.shape[1]
    vboth = None
    for g in range(NSA_GROUPS):
        xk = xk_ref[g].astype(F32)
        lo = _dot((xk + pk_ref[0:1]).astype(BF16), wk_ref[0])
        hi = _dot((xk + pk_ref[1:2]).astype(BF16), wk_ref[1])
        ok_ref[g] = (lo + pltpu.roll(hi, n - 1, axis=0)).astype(BF16)
        xv = xv_ref[g].astype(F32)
        lo = _dot((xv + pv_ref[0:1]).astype(BF16), wv_ref[g, 0])
        hi = _dot((xv + pv_ref[1:2]).astype(BF16), wv_ref[g, 1])
        vg = lo + pltpu.roll(hi, n - 1, axis=0)
        vboth = vg if vboth is None else vboth + vg
    ovt_ref[0] = vboth.T.astype(BF16)


def _compress(xk, xv, pk, pv, wk, wv, B):
    G, TC, CW = xk.shape
    nch = TC // B
    xspec = pl.BlockSpec((G, nch, CW), lambda b: (0, b, 0))
    return pl.pallas_call(
        _compress_kernel,
        grid=(B,),
        in_specs=[xspec, xspec, _const_spec(pk.shape), _const_spec(pv.shape),
                  _const_spec(wk.shape), _const_spec(wv.shape)],
        out_specs=[pl.BlockSpec((G, nch, NSA_DIM), lambda b: (0, b, 0)),
                   pl.BlockSpec((1, G * NSA_DIM, nch), lambda b: (b, 0, 0))],
        out_shape=[jax.ShapeDtypeStruct((G, TC, NSA_DIM), BF16),
                   jax.ShapeDtypeStruct((B, G * NSA_DIM, nch), BF16)],
        compiler_params=_cparams(("parallel",)),
        name="compress",
    )(xk, xv, pk, pv, wk, wv)


def _cmp_select_kernel(q_ref, kc_ref, vct_ref, ovt_ref, o_ref, bias_ref, *, tq, n_pick):
    q0 = pl.program_id(1) * tq
    nch = kc_ref.shape[1]
    nblk = ovt_ref.shape[0]
    t_c = q0 + lax.broadcasted_iota(jnp.int32, (nch, tq), 1)
    n_c = lax.broadcasted_iota(jnp.int32, (nch, tq), 0)
    cmask = (n_c * CMP_STRIDE + (CMP_LEN - 1)) <= t_c
    t_b = q0 + lax.broadcasted_iota(jnp.int32, (nblk, tq), 1)
    blk = lax.broadcasted_iota(jnp.int32, (nblk, tq), 0)
    blk_f = blk.astype(F32)
    cur = lax.shift_right_logical(t_b, int(math.log2(SEL_LEN)))
    forced = (blk == 0) | (blk == cur) | (blk == cur - 1)
    causal = blk * SEL_LEN <= t_b
    outs = []
    for g in range(NSA_GROUPS):
        psum = None
        for j in range(NSA_HPG):
            q = q_ref[0, g * NSA_HPG + j][:, :NSA_DIM]
            s = jnp.where(cmask, _dot_nt(kc_ref[g], q), MASKED)
            m = jnp.max(s, axis=0, keepdims=True)
            p = jnp.where(cmask, jnp.exp2(s - m), 0.0)
            l = jnp.sum(p, axis=0, keepdims=True)
            pn = p * jnp.where(l > 0.0, 1.0 / l, 0.0)
            outs.append(_dot(vct_ref[0, g * NSA_DIM:(g + 1) * NSA_DIM, :], pn.astype(BF16)))
            psum = pn if psum is None else psum + pn
        imp = None
        for piece in _split_bf16(psum, 3):
            d = _dot(ovt_ref[...], piece)
            imp = d if imp is None else imp + d
        v = jnp.where(forced | jnp.logical_not(causal), -1.0, imp)
        sel = jnp.where(forced, 1.0, 0.0)
        for _ in range(n_pick):
            mx = jnp.max(v, axis=0, keepdims=True)
            first = jnp.min(jnp.where(v == mx, blk_f, float(nblk)), axis=0, keepdims=True)
            hit = blk_f == first
            sel = jnp.where(hit & (mx >= 0.0), 1.0, sel)
            v = jnp.where(hit, -2.0, v)
        bias_ref[0, g] = ((sel - 1.0) * -SEL_BIAS).T.astype(BF16)
    o_ref[...] = jnp.concatenate(outs, axis=0).T.astype(BF16)


def _cmp_select(nq, kc, vct, ovt, B, S, tq):
    T = B * S
    nch = kc.shape[1] // B
    nblk = ovt.shape[0]
    nqt = S // tq
    kern = functools.partial(_cmp_select_kernel, tq=tq, n_pick=N_SEL - 3)
    return pl.pallas_call(
        kern,
        grid=(B, nqt),
        in_specs=[
            pl.BlockSpec((1, NSA_HEADS, tq, LANES), lambda b, i: (b, 0, i, 0)),
            pl.BlockSpec((NSA_GROUPS, nch, NSA_DIM), lambda b, i: (0, b, 0)),
            pl.BlockSpec((1, NSA_GROUPS * NSA_DIM, nch), lambda b, i: (b, 0, 0)),
            _const_spec(ovt.shape),
        ],
        out_specs=[pl.BlockSpec((tq, NSA_HEADS * NSA_DIM), lambda b, i: (b * nqt + i, 0)),
                   pl.BlockSpec((1, NSA_GROUPS, tq, nblk), lambda b, i: (b, 0, i, 0))],
        out_shape=[jax.ShapeDtypeStruct((T, NSA_HEADS * NSA_DIM), BF16),
                   jax.ShapeDtypeStruct((B, NSA_GROUPS, S, nblk), BF16)],
        compiler_params=_cparams(("parallel", "parallel")),
        name="cmp_select",
    )(nq, kc, vct, ovt)


def _flash_heads(nheads, kqv, mask, src, dst, lazy=False, unroll=True):
    def scores(hd):
        k, q, _ = kqv(hd)
        return _dot_nt(k, q)

    def update(hd, s):
        if mask is not None:
            s = jnp.where(mask, s, MASKED)
        tmax = jnp.max(s, axis=0, keepdims=True)
        vt = kqv(hd)[2]
        rise = None
        if src is None:
            m_new = jnp.maximum(tmax, M_FLOOR)
            acc = _dot(vt, jnp.exp2(s - m_new).astype(BF16))
        else:
            m_old = src[0][hd]
            m_new = jnp.maximum(m_old, tmax)
            alpha = jnp.exp2(m_old - m_new)
            if lazy:
                acc = (src[1][hd] + _dot(vt, jnp.exp2(s - m_old).astype(BF16))) * alpha
                rise = tmax - m_old
            else:
                acc = alpha * src[1][hd] + _dot(vt, jnp.exp2(s - m_new).astype(BF16))
        dst[0][hd] = m_new
        dst[1][hd] = acc
        return rise

    if not unroll:
        def body(hd, carry):
            update(hd, scores(hd))
            return carry
        lax.fori_loop(0, nheads, body, 0)
        return None
    ahead = 2
    pending = [scores(hd) for hd in range(min(ahead, nheads))]
    worst = None
    for hd in range(nheads):
        s = pending.pop(0)
        if hd + ahead < nheads:
            pending.append(scores(hd + ahead))
        rise = update(hd, s)
        if rise is not None:
            worst = rise if worst is None else jnp.maximum(worst, rise)
    return worst


def _flash_init(m_ref, acc_ref):
    m_ref[...] = jnp.full(m_ref.shape, M_FLOOR, F32)
    acc_ref[...] = jnp.zeros(acc_ref.shape, F32)


def _flash_finalize(o_ref, acc_ref):
    nv, rows, tq = acc_ref.shape
    dv = rows - SUM_ROWS
    groups = o_ref.shape[0] if len(o_ref.shape) == 3 else 1
    nh = nv // groups
    for gb in range(groups):
        sl = slice(gb * nh, (gb + 1) * nh)
        o = acc_ref[sl, :dv, :] * (1.0 / acc_ref[sl, dv:dv + 1, :])
        o = o.reshape(nh * dv, tq).T.astype(BF16)
        if len(o_ref.shape) == 3:
            o_ref[gb] = o
        else:
            o_ref[...] = o


def _flash_scratch(nh, dv, tq, slots=None):
    lead = (nh,) if slots is None else (slots, nh)
    return [pltpu.VMEM(lead + (1, tq), F32), pltpu.VMEM(lead + (dv + SUM_ROWS, tq), F32)]


def _causal_pairs(nqt):
    pairs = [(i, j) for i in range(nqt) for j in range(i + 1)]
    return (jnp.asarray([p[0] for p in pairs], jnp.int32), jnp.asarray([p[1] for p in pairs], jnp.int32))


def _tile_iota(tq):
    return (lax.broadcasted_iota(jnp.int32, (tq, tq), 0), lax.broadcasted_iota(jnp.int32, (tq, tq), 1))


def _causal_flash_step(i, j, nheads, kqv, tq, o_ref, m_ref, acc_ref, slot_ref):
    key, qry = _tile_iota(tq)

    @pl.when(j == 0)
    def _():
        mask = key <= qry + jnp.where(i > 0, tq, 0)
        _flash_heads(nheads, kqv, mask, None, (m_ref.at[0], acc_ref.at[0]))
        slot_ref[0] = 0

    def lazy_step(mask):
        cur = slot_ref[0]
        src = (m_ref.at[cur], acc_ref.at[cur])
        dst = (m_ref.at[1 - cur], acc_ref.at[1 - cur])
        worst = _flash_heads(nheads, kqv, mask, src, dst, lazy=True)

        @pl.when(jnp.max(worst) > LAZY_SPAN)
        def _():
            _flash_heads(nheads, kqv, mask, src, dst, unroll=False)

        slot_ref[0] = 1 - cur

    @pl.when((j > 0) & (j < i))
    def _():
        lazy_step(None)

    @pl.when((j > 0) & (j == i))
    def _():
        lazy_step(key <= qry)

    @pl.when(j == i)
    def _():
        _flash_finalize(o_ref, acc_ref.at[slot_ref[0]])


def _split_virtual(v, nheads):
    if isinstance(v, int):
        return v // nheads, v % nheads
    return lax.div(v, nheads), lax.rem(v, nheads)


def _mla_kernel(qi_ref, ki_ref, q_ref, k_ref, vt_ref, o_ref, m_ref, acc_ref, slot_ref, *, tq):
    i, j = qi_ref[pl.program_id(0)], ki_ref[pl.program_id(0)]

    def kqv(v):
        bb, hd = _split_virtual(v, MLA_HEADS)
        return k_ref[bb, hd], q_ref[bb, hd], vt_ref[bb, hd]

    _causal_flash_step(i, j, q_ref.shape[0] * MLA_HEADS, kqv, tq, o_ref, m_ref, acc_ref, slot_ref)


def _mla_attention(qm, km, vmt, B, S, tq):
    nqt = S // tq
    kern = functools.partial(_mla_kernel, tq=tq)
    qi, ki = _causal_pairs(nqt)
    return pl.pallas_call(
        kern,
        grid_spec=pltpu.PrefetchScalarGridSpec(
            num_scalar_prefetch=2,
            grid=(qi.shape[0],),
            in_specs=[pl.BlockSpec((B, MLA_HEADS, tq, LANES), lambda p, qi, ki: (0, 0, qi[p], 0)),
                      pl.BlockSpec((B, MLA_HEADS, tq, LANES), lambda p, qi, ki: (0, 0, ki[p], 0)),
                      pl.BlockSpec((B, MLA_HEADS, MLA_V + SUM_ROWS, tq), lambda p, qi, ki: (0, 0, 0, ki[p]))],
            out_specs=pl.BlockSpec((B, tq, MLA_HEADS * MLA_V), lambda p, qi, ki: (0, qi[p], 0)),
            scratch_shapes=_flash_scratch(B * MLA_HEADS, MLA_V, tq, slots=2) + [pltpu.SMEM((1,), jnp.int32)]),
        out_shape=jax.ShapeDtypeStruct((B, S, MLA_HEADS * MLA_V), BF16),
        compiler_params=_cparams(("arbitrary",)),
        name="mla_attn",
    )(qi, ki, qm, km, vmt).reshape(B * S, MLA_HEADS * MLA_V)


def _slc_kernel(qi_ref, ki_ref, q_ref, bias_ref, k_ref, vt_ref, o_ref, qa_ref, m_ref, acc_ref,
                slot_ref, *, tq):
    i, j = qi_ref[pl.program_id(0)], ki_ref[pl.program_id(0)]
    nb = q_ref.shape[0]
    nblk = bias_ref.shape[3]

    @pl.when(j == 0)
    def _():
        if nblk < LANES:
            qa_ref[...] = jnp.zeros(qa_ref.shape, BF16)
        for bb in range(nb):
            for hd in range(NSA_HEADS):
                qa_ref[bb * NSA_HEADS + hd, :, 0:nblk] = bias_ref[bb, hd // NSA_HPG]
                qa_ref[bb * NSA_HEADS + hd, :, LANES:2 * LANES] = q_ref[bb, hd]

    def kqv(v):
        bb, hd = _split_virtual(v, NSA_HEADS)
        g = hd // NSA_HPG if isinstance(hd, int) else lax.div(hd, NSA_HPG)
        return k_ref[bb, g], qa_ref[v], vt_ref[bb, g]

    _causal_flash_step(i, j, nb * NSA_HEADS, kqv, tq, o_ref, m_ref, acc_ref, slot_ref)


def _slc_attention(nq, bias, ksa, vst, B, S, tq):
    nqt = S // tq
    nblk = bias.shape[3]
    kern = functools.partial(_slc_kernel, tq=tq)
    qi, ki = _causal_pairs(nqt)
    qmap = lambda p, qi, ki: (0, 0, qi[p], 0)
    return pl.pallas_call(
        kern,
        grid_spec=pltpu.PrefetchScalarGridSpec(
            num_scalar_prefetch=2,
            grid=(qi.shape[0],),
            in_specs=[pl.BlockSpec((B, NSA_HEADS, tq, LANES), qmap),
                      pl.BlockSpec((B, NSA_GROUPS, tq, nblk), qmap),
                      pl.BlockSpec((B, NSA_GROUPS, tq, 2 * LANES), lambda p, qi, ki: (0, 0, ki[p], 0)),
                      pl.BlockSpec((B, NSA_GROUPS, NSA_DIM + SUM_ROWS, tq), lambda p, qi, ki: (0, 0, 0, ki[p]))],
            out_specs=pl.BlockSpec((B, tq, NSA_HEADS * NSA_DIM), lambda p, qi, ki: (0, qi[p], 0)),
            scratch_shapes=[pltpu.VMEM((B * NSA_HEADS, tq, 2 * LANES), BF16)]
                           + _flash_scratch(B * NSA_HEADS, NSA_DIM, tq, slots=2)
                           + [pltpu.SMEM((1,), jnp.int32)]),
        out_shape=jax.ShapeDtypeStruct((B, S, NSA_HEADS * NSA_DIM), BF16),
        compiler_params=_cparams(("arbitrary",)),
        name="slc_attn",
    )(qi, ki, nq, bias, ksa, vst).reshape(B * S, NSA_HEADS * NSA_DIM)


def _win_kernel(q_ref, k_ref, vt_ref, o_ref, m_ref, acc_ref, *, tq):
    i, j = pl.program_id(1), pl.program_id(2)

    @pl.when(j == 0)
    def _():
        _flash_init(m_ref, acc_ref)

    def step(prev):
        key, qry = _tile_iota(tq)
        mask = (key > qry) if prev else (key <= qry)
        _flash_heads(NSA_HEADS, lambda hd: (k_ref[hd // NSA_HPG], q_ref[0, hd][:, :NSA_DIM],
                                            vt_ref[hd // NSA_HPG]), mask, (m_ref, acc_ref), (m_ref, acc_ref))

    @pl.when((j == 0) & (i > 0))
    def _():
        step(True)

    @pl.when(j == 1)
    def _():
        step(False)
        _flash_finalize(o_ref, acc_ref)


def _win_attention(nq, kw, vwt, B, S, tq):
    T = B * S
    nqt = S // tq
    kern = functools.partial(_win_kernel, tq=tq)
    kidx = lambda b, i, j: b * nqt + jnp.maximum(i - 1 + j, 0)
    return pl.pallas_call(
        kern,
        grid=(B, nqt, 2),
        in_specs=[pl.BlockSpec((1, NSA_HEADS, tq, LANES), lambda b, i, j: (b, 0, i, 0)),
                  pl.BlockSpec((NSA_GROUPS, tq, NSA_DIM), lambda b, i, j: (0, kidx(b, i, j), 0)),
                  pl.BlockSpec((NSA_GROUPS, NSA_DIM + SUM_ROWS, tq), lambda b, i, j: (0, 0, kidx(b, i, j)))],
        out_specs=pl.BlockSpec((tq, NSA_HEADS * NSA_DIM), lambda b, i, j: (b * nqt + i, 0)),
        out_shape=jax.ShapeDtypeStruct((T, NSA_HEADS * NSA_DIM), BF16),
        scratch_shapes=_flash_scratch(NSA_HEADS, NSA_DIM, tq),
        compiler_params=_cparams(("parallel", "parallel", "arbitrary")),
        name="win_attn",
    )(nq, kw, vwt)


def _causal_conv(u, carry, w_ref):
    row = lax.broadcasted_iota(jnp.int32, u.shape, 0)
    c1 = carry[SUBLANES - 1:SUBLANES]
    c2 = carry[SUBLANES - 2:SUBLANES - 1]
    u1 = jnp.where(row == 0, c1, pltpu.roll(u, 1, axis=0))
    u2 = jnp.where(row == 0, c2, jnp.where(row == 1, c1, pltpu.roll(u, 2, axis=0)))
    return w_ref[0:1] * u2 + w_ref[1:2] * u1 + w_ref[2:3] * u


def _merge_kernel(x_ref, g_ref, wgc_ref, cw_ref, ym_ref, oc_ref, os_ref, ow_ref, gl_ref, ex_ref,
                  wbc_ref, wbm_ref, wbn_ref, wo_ref, gp_ref, out_ref, carry_ref, *, tiles_per_seq):
    D = x_ref.shape[1]
    DC = wbc_ref.shape[0]

    @pl.when(pl.program_id(0) % tiles_per_seq == 0)
    def _():
        carry_ref[...] = jnp.zeros(carry_ref.shape, F32)

    x = x_ref[...]
    h = _rms(x, g_ref[...]).astype(BF16)
    p = _dot(h, wgc_ref[...])
    u = p[:, 3 * D + DC:3 * D + 2 * DC] * p[:, 3 * D + 2 * DC:3 * D + 3 * DC]
    y_conv = p[:, 3 * D:3 * D + DC] * _causal_conv(u, carry_ref[...], cw_ref)
    carry_ref[...] = u[u.shape[0] - SUBLANES:]
    hi, lo = _split_bf16(gl_ref[...], 2)
    gate = jax.nn.sigmoid(_dot(hi, ex_ref[...]) + _dot(lo, ex_ref[...]))
    HD = oc_ref.shape[1]
    y_nsa = (gate[:, 0:HD] * oc_ref[...].astype(F32) + gate[:, HD:2 * HD] * os_ref[...].astype(F32)
             + gate[:, 2 * HD:3 * HD] * ow_ref[...].astype(F32))
    merged = (jax.nn.sigmoid(p[:, 0:D]) * _dot(y_conv.astype(BF16), wbc_ref[...])
              + jax.nn.sigmoid(p[:, D:2 * D]) * _dot(ym_ref[...], wbm_ref[...])
              + jax.nn.sigmoid(p[:, 2 * D:3 * D]) * _dot(y_nsa.astype(BF16), wbn_ref[...]))
    out_ref[...] = x + _rms(_dot(merged.astype(BF16), wo_ref[...]), gp_ref[...])


def _merge(x2, g_pre, w_gc, conv_w, y_mla, o_cmp, o_slc, o_win, gl, ex, wbc, wbm, wbn, wo, g_post,
           S, tm):
    T, D = x2.shape
    HD = y_mla.shape[1]
    row = lambda w: pl.BlockSpec((tm, w), lambda i: (i, 0))
    kern = functools.partial(_merge_kernel, tiles_per_seq=S // tm)
    return pl.pallas_call(
        kern,
        grid=(T // tm,),
        in_specs=[row(D), _const_spec((1, D)), _const_spec(w_gc.shape), _const_spec(conv_w.shape),
                  row(HD), row(HD), row(HD), row(HD), row(LANES), _const_spec(ex.shape),
                  _const_spec(wbc.shape), _const_spec(wbm.shape), _const_spec(wbn.shape),
                  _const_spec(wo.shape), _const_spec((1, D))],
        out_specs=row(D),
        out_shape=jax.ShapeDtypeStruct((T, D), F32),
        scratch_shapes=[pltpu.VMEM((SUBLANES, wbc.shape[0]), F32)],
        compiler_params=_cparams(("arbitrary",)),
        name="merge",
    )(x2, g_pre, w_gc, conv_w, y_mla, o_cmp, o_slc, o_win, gl, ex, wbc, wbm, wbn, wo, g_post)


def _ffn_kernel(x_ref, g_ref, wup_ref, cw_ref, cb_ref, wdn_ref, gp_ref, out_ref, carry_ref,
                *, tiles_per_seq, chunk):
    F = wdn_ref.shape[0]

    @pl.when(pl.program_id(0) % tiles_per_seq == 0)
    def _():
        carry_ref[...] = jnp.zeros(carry_ref.shape, F32)

    x = x_ref[...]
    h = _rms(x, g_ref[...]).astype(BF16)
    acc = None
    for c0 in range(0, F, chunk):
        a = _dot(h, wup_ref[:, c0:c0 + chunk])
        b = _dot(h, wup_ref[:, F + c0:F + c0 + chunk])
        ac = _causal_conv(a, carry_ref[:, c0:c0 + chunk], cw_ref.at[:, c0:c0 + chunk])
        ac = ac + cb_ref[:, c0:c0 + chunk]
        carry_ref[:, c0:c0 + chunk] = a[a.shape[0] - SUBLANES:]
        d = _dot((jax.nn.gelu(ac) * b).astype(BF16), wdn_ref[c0:c0 + chunk, :])
        acc = d if acc is None else acc + d
    out_ref[...] = x + _rms(acc, gp_ref[...])


def _ffn(x2, g_pre, w_up, conv_w, conv_b, w_dn, g_post, S, tm, chunk):
    T, D = x2.shape
    F = w_dn.shape[0]
    row = pl.BlockSpec((tm, D), lambda i: (i, 0))
    kern = functools.partial(_ffn_kernel, tiles_per_seq=S // tm, chunk=chunk)
    return pl.pallas_call(
        kern,
        grid=(T // tm,),
        in_specs=[row, _const_spec((1, D)), _const_spec(w_up.shape), _const_spec(conv_w.shape),
                  _const_spec((1, F)), _const_spec(w_dn.shape), _const_spec((1, D))],
        out_specs=row,
        out_shape=jax.ShapeDtypeStruct((T, D), F32),
        scratch_shapes=[pltpu.VMEM((SUBLANES, F), F32)],
        compiler_params=_cparams(("arbitrary",)),
        name="ffn",
    )(x2, g_pre, w_up, conv_w, conv_b, w_dn, g_post)


def _overlap_matrix_t(S):
    nch = S // CMP_STRIDE
    n_cmp = (S - CMP_LEN) // CMP_STRIDE + 1
    cs = np.arange(nch)[None, :] * CMP_STRIDE
    ss = np.arange(S // SEL_LEN)[:, None] * SEL_LEN
    ov = (cs <= ss + SEL_LEN - 1) & (cs + CMP_LEN - 1 >= ss) & (np.arange(nch)[None, :] < n_cmp)
    return ov.astype(np.float32)


def _gate_expand():
    ex = np.zeros((LANES, 3 * NSA_HEADS * NSA_DIM), np.float32)
    for hd in range(NSA_HEADS):
        for c in range(3):
            ex[hd * 3 + c, c * NSA_HEADS * NSA_DIM + hd * NSA_DIM:
               c * NSA_HEADS * NSA_DIM + (hd + 1) * NSA_DIM] = 1.0
    return ex


def _pad_rows(w, rows):
    return jnp.concatenate([w, jnp.zeros((rows - w.shape[0],) + w.shape[1:], w.dtype)], axis=0)


def kernel(x, positions, norm_mix_pre, norm_mix_post, w_in, conv_w, mla_q_norm, mla_w_uq, mla_kv_norm, mla_w_ukv, nsa_cmp_pos_k, nsa_cmp_pos_v, nsa_cmp_w_k, nsa_cmp_w_v, w_branch_conv, w_branch_mla, w_branch_nsa, w_out, norm_ffn_pre, norm_ffn_post, ffn_w_up, ffn_conv_w, ffn_conv_b, ffn_w_down):
    B, S, D = x.shape
    T = B * S
    depth = w_in.shape[0]
    DC = conv_w.shape[2]
    F = ffn_w_down.shape[1]
    QL = mla_q_norm.shape[1]
    KL = mla_kv_norm.shape[1]
    tq = 512
    assert S % tq == 0 and WINDOW == tq and N_SEL <= S // SEL_LEN <= LANES
    tm_proj, tm_merge, tm_ffn, ffn_chunk = 512, 256, 256, 1408
    assert F % ffn_chunk == 0 and ffn_chunk % LANES == 0

    x2 = x.reshape(T, D)
    pos2 = positions.reshape(T, 1)
    rc = jnp.asarray(_rope_consts())
    ovt = jnp.asarray(_overlap_matrix_t(S), BF16)
    ex = jnp.asarray(_gate_expand(), BF16)
    half = CMP_LEN // 2 * NSA_DIM
    o_att = 3 * D + 3 * DC

    for l in range(depth):
        w = w_in[l]
        o_kr = o_att + QL + KL
        z = lambda n: jnp.zeros((D, n), F32)
        w_att = jnp.concatenate(
            [w[:, o_att:o_kr], z(MLA_NOPE), w[:, o_kr:o_kr + MLA_ROPE], z(LANES - MLA_NOPE - MLA_ROPE),
             w[:, o_kr + MLA_ROPE:o_kr + MLA_ROPE + NSA_HEADS * NSA_DIM + 6 * NSA_GROUPS * NSA_DIM],
             w[:, w.shape[1] - 3 * NSA_HEADS:], z(LANES - 3 * NSA_HEADS)], axis=1).astype(BF16)
        dq = MLA_NOPE + MLA_ROPE
        w_uq = jnp.pad(mla_w_uq[l].reshape(QL, MLA_HEADS, dq),
                       ((0, 0), (0, 0), (0, LANES - dq))).reshape(QL, MLA_HEADS * LANES).astype(BF16)
        (qm, km, vmt, nq, kc_in, vc_in, ksa, vst, kw, vwt, gl) = _project(
            x2, pos2, norm_mix_pre[l][None], w_att, mla_q_norm[l][None], w_uq,
            mla_kv_norm[l][None], mla_w_ukv[l].astype(BF16), rc, S, tm_proj)

        chunks = lambda a: a.reshape(NSA_GROUPS, T // CMP_STRIDE, CMP_STRIDE * NSA_DIM)
        wv = nsa_cmp_w_v[l].reshape(2, half, NSA_DIM)
        zv = jnp.zeros_like(wv)
        wv_pad = jnp.stack([jnp.concatenate([wv, zv], axis=2), jnp.concatenate([zv, wv], axis=2)])
        kc, vct = _compress(chunks(kc_in), chunks(vc_in),
                            nsa_cmp_pos_k[l].reshape(2, half), nsa_cmp_pos_v[l].reshape(2, half),
                            nsa_cmp_w_k[l].reshape(2, half, NSA_DIM).astype(BF16),
                            wv_pad.astype(BF16), B)

        y_mla = _mla_attention(qm, km, vmt, B, S, tq)
        o_cmp, bias = _cmp_select(nq, kc, vct, ovt, B, S, tq)
        o_slc = _slc_attention(nq, bias, ksa, vst, B, S, tq)
        o_win = _win_attention(nq, kw, vwt, B, S, tq)

        x2 = _merge(x2, norm_mix_pre[l][None], w[:, :o_att].astype(BF16),
                    _pad_rows(conv_w[l], SUBLANES), y_mla, o_cmp, o_slc, o_win, gl, ex,
                    w_branch_conv[l].astype(BF16), w_branch_mla[l].astype(BF16),
                    w_branch_nsa[l].astype(BF16), w_out[l].astype(BF16), norm_mix_post[l][None],
                    S, tm_merge)
        x2 = _ffn(x2, norm_ffn_pre[l][None], ffn_w_up[l].astype(BF16),
                  _pad_rows(ffn_conv_w[l], SUBLANES), ffn_conv_b[l][None],
                  ffn_w_down[l].astype(BF16), norm_ffn_post[l][None], S, tm_ffn, ffn_chunk)
    return x2.reshape(B, S, D)
```

```python
import functools
import math

import numpy as np
import jax
import jax.numpy as jnp
from jax import lax
from jax.experimental import pallas as pl
from jax.experimental.pallas import tpu as pltpu

F32 = jnp.float32
BF16 = jnp.bfloat16

ROPE_THETA = 500000.0
RMS_EPS = 1e-6
CONV_WIDTH = 3
MLA_HEADS = 8
MLA_NOPE = 64
MLA_ROPE = 32
MLA_V = 64
NSA_HEADS = 8
NSA_GROUPS = 2
NSA_HPG = NSA_HEADS // NSA_GROUPS
NSA_DIM = 64
NSA_ROT = NSA_DIM // 4
CMP_LEN = 32
CMP_STRIDE = 16
SEL_LEN = 64
N_SEL = 16
WINDOW = 512

LANES = 128
SUBLANES = 8
VMEM_LIMIT = 56 * 1024 * 1024
MASKED = -1e30
M_FLOOR = -1e20
SEL_BIAS = -2.0 ** 100
LAZY_SPAN = 64.0
LOG2E = math.log2(math.e)
SUM_ROWS = 16


def _cparams(sem):
    return pltpu.CompilerParams(dimension_semantics=sem, vmem_limit_bytes=VMEM_LIMIT)


def _const_spec(shape):
    nd = len(shape)
    return pl.BlockSpec(shape, lambda *_: (0,) * nd, pipeline_mode=pl.Buffered(1))


def _rms(x, g):
    return x * lax.rsqrt(jnp.mean(x * x, axis=-1, keepdims=True) + RMS_EPS) * g


def _dot(a, b):
    return jnp.dot(a, b, preferred_element_type=F32)


def _dot_nt(a, b):
    return lax.dot_general(a, b, (((1,), (1,)), ((), ())), preferred_element_type=F32)


def _split_bf16(a, terms):
    pieces, rem = [], a
    for _ in range(terms):
        piece = rem.astype(BF16)
        rem = rem - piece.astype(F32)
        pieces.append(piece)
    return pieces


def _rope(x, cos, sin_up, sin_dn, half):
    return (x * cos + pltpu.roll(x, half, axis=1) * sin_up
            + pltpu.roll(x, LANES - half, axis=1) * sin_dn)


def _proj_kernel(x_ref, pos_ref, g_ref, watt_ref, qg_ref, wuq_ref, kvg_ref, wukv_ref, rc_ref,
                 qm_ref, km_ref, vmt_ref, nq_ref, kc_ref, vc_ref, ksa_ref, vst_ref, kw_ref, vwt_ref,
                 gl_ref, *, mla_scale, nsa_scale, tiles_per_seq):
    tm = x_ref.shape[0]
    x = x_ref[...]
    h = _rms(x, g_ref[...]).astype(BF16)
    p = _dot(h, watt_ref[...])
    pos = pos_ref[...].astype(F32)
    rc = rc_ref[...]
    ang_m = pos * rc[0:1]
    cos_m, sin_m = jnp.cos(ang_m), jnp.sin(ang_m)
    sup_m, sdn_m = sin_m * rc[1:2], sin_m * rc[2:3]
    ang_n = pos * rc[3:4]
    cos_n, sin_n = jnp.cos(ang_n), jnp.sin(ang_n)
    sup_n, sdn_n = sin_n * rc[4:5], sin_n * rc[5:6]
    lane = lax.broadcasted_iota(jnp.int32, (tm, LANES), 1)
    low = lane < NSA_DIM
    ones_rows = jnp.where(lax.broadcasted_iota(jnp.int32, (SUM_ROWS, tm), 0) == 0, 1.0, 0.0)

    def with_sum_rows(vt):
        return jnp.concatenate([vt, ones_rows], axis=0).astype(BF16)

    def halves(slab):
        return (jnp.where(low, slab, 0.0), jnp.where(low, pltpu.roll(slab, NSA_DIM, axis=1), 0.0))

    cqn = _rms(p[:, 0:384], qg_ref[...]).astype(BF16)
    qf = _dot(cqn, wuq_ref[...])
    for hd in range(MLA_HEADS):
        slab = qf[:, hd * LANES:(hd + 1) * LANES]
        slab = _rope(slab, cos_m, sup_m, sdn_m, MLA_ROPE // 2) * mla_scale
        qm_ref[0, hd] = slab.astype(BF16)
    ckvn = _rms(p[:, 384:640], kvg_ref[...]).astype(BF16)
    kv = _dot(ckvn, wukv_ref[...])
    kr = _rope(p[:, 640:768], cos_m, sup_m, sdn_m, MLA_ROPE // 2)
    for hd in range(MLA_HEADS):
        slab = kv[:, hd * LANES:(hd + 1) * LANES]
        km_ref[0, hd] = jnp.where(lane < MLA_NOPE, slab, kr).astype(BF16)
        vmt_ref[0, hd] = with_sum_rows(slab.T[MLA_NOPE:])
    for pr in range(NSA_HEADS // 2):
        slab = p[:, 768 + pr * LANES:768 + (pr + 1) * LANES]
        slab = _rope(slab, cos_n, sup_n, sdn_n, NSA_ROT // 2) * nsa_scale
        q0, q1 = halves(slab)
        nq_ref[0, 2 * pr] = q0.astype(BF16)
        nq_ref[0, 2 * pr + 1] = q1.astype(BF16)
    nsa = lambda idx: p[:, 1280 + idx * LANES:1280 + (idx + 1) * LANES]
    kcs = _rope(nsa(0), cos_n, sup_n, sdn_n, NSA_ROT // 2)
    kc_ref[0] = kcs[:, :NSA_DIM].astype(BF16)
    kc_ref[1] = pltpu.roll(kcs, NSA_DIM, axis=1)[:, :NSA_DIM].astype(BF16)
    vcs = nsa(1)
    vc_ref[0] = vcs[:, :NSA_DIM].astype(BF16)
    vc_ref[1] = pltpu.roll(vcs, NSA_DIM, axis=1)[:, :NSA_DIM].astype(BF16)
    t_seq = (pl.program_id(0) % tiles_per_seq) * tm + lax.broadcasted_iota(jnp.int32, (tm, LANES), 0)
    onehot = jnp.where(lane == lax.shift_right_logical(t_seq, int(math.log2(SEL_LEN))), 1.0, 0.0)
    k0, k1 = halves(_rope(nsa(2), cos_n, sup_n, sdn_n, NSA_ROT // 2))
    for g, kg in enumerate((k0, k1)):
        ksa_ref[0, g, :, 0:LANES] = onehot.astype(BF16)
        ksa_ref[0, g, :, LANES:2 * LANES] = kg.astype(BF16)
    vst = nsa(3).T
    vst_ref[0, 0] = with_sum_rows(vst[:NSA_DIM])
    vst_ref[0, 1] = with_sum_rows(vst[NSA_DIM:])
    kws = _rope(nsa(4), cos_n, sup_n, sdn_n, NSA_ROT // 2)
    kw_ref[0] = kws[:, :NSA_DIM].astype(BF16)
    kw_ref[1] = pltpu.roll(kws, NSA_DIM, axis=1)[:, :NSA_DIM].astype(BF16)
    vwt = nsa(5).T
    vwt_ref[0] = with_sum_rows(vwt[:NSA_DIM])
    vwt_ref[1] = with_sum_rows(vwt[NSA_DIM:])
    gl_ref[...] = p[:, 2048:2176]


def _rope_consts():
    rc = np.zeros((SUBLANES, LANES), np.float32)
    hm = MLA_ROPE // 2
    fm = ROPE_THETA ** (-np.arange(hm, dtype=np.float32) / hm)
    rc[0, MLA_NOPE:MLA_NOPE + hm] = fm
    rc[0, MLA_NOPE + hm:MLA_NOPE + 2 * hm] = fm
    rc[1, MLA_NOPE + hm:MLA_NOPE + 2 * hm] = 1.0
    rc[2, MLA_NOPE:MLA_NOPE + hm] = -1.0
    hn = NSA_ROT // 2
    fn = ROPE_THETA ** (-np.arange(hn, dtype=np.float32) / hn)
    for base in (0, NSA_DIM):
        rc[3, base:base + hn] = fn
        rc[3, base + hn:base + 2 * hn] = fn
        rc[4, base + hn:base + 2 * hn] = 1.0
        rc[5, base:base + hn] = -1.0
    return rc


def _project(x2, pos2, g_pre, w_att, q_g, w_uq, kv_g, w_ukv, rc, S, tm):
    T, D = x2.shape
    n_att = w_att.shape[1]
    tps = S // tm
    B = T // S
    rows = lambda n, d: (jax.ShapeDtypeStruct((n, T, d), BF16), pl.BlockSpec((n, tm, d), lambda i: (0, i, 0)))
    cols = lambda n, d: (jax.ShapeDtypeStruct((n, d, T), BF16), pl.BlockSpec((n, d, tm), lambda i: (0, 0, i)))
    brows = lambda n, d: (jax.ShapeDtypeStruct((B, n, S, d), BF16),
                          pl.BlockSpec((1, n, tm, d), lambda i: (i // tps, 0, i % tps, 0)))
    bcols = lambda n, d: (jax.ShapeDtypeStruct((B, n, d, S), BF16),
                          pl.BlockSpec((1, n, d, tm), lambda i: (i // tps, 0, 0, i % tps)))
    outs = [brows(MLA_HEADS, LANES), brows(MLA_HEADS, LANES), bcols(MLA_HEADS, MLA_V + SUM_ROWS),
            brows(NSA_HEADS, LANES), rows(NSA_GROUPS, NSA_DIM), rows(NSA_GROUPS, NSA_DIM),
            brows(NSA_GROUPS, 2 * LANES), bcols(NSA_GROUPS, NSA_DIM + SUM_ROWS), rows(NSA_GROUPS, NSA_DIM),
            cols(NSA_GROUPS, NSA_DIM + SUM_ROWS),
            (jax.ShapeDtypeStruct((T, LANES), F32), pl.BlockSpec((tm, LANES), lambda i: (i, 0)))]
    kern = functools.partial(_proj_kernel,
                             mla_scale=float((MLA_NOPE + MLA_ROPE) ** -0.5 * LOG2E),
                             nsa_scale=float(NSA_DIM ** -0.5 * LOG2E),
                             tiles_per_seq=S // tm)
    return pl.pallas_call(
        kern,
        grid=(T // tm,),
        in_specs=[
            pl.BlockSpec((tm, D), lambda i: (i, 0)),
            pl.BlockSpec((tm, 1), lambda i: (i, 0)),
            _const_spec((1, D)),
            _const_spec((D, n_att)),
            _const_spec((1, q_g.shape[1])),
            _const_spec(w_uq.shape),
            _const_spec((1, kv_g.shape[1])),
            _const_spec(w_ukv.shape),
            _const_spec((SUBLANES, LANES)),
        ],
        out_specs=[o[1] for o in outs],
        out_shape=[o[0] for o in outs],
        compiler_params=_cparams(("parallel",)),
        name="proj",
    )(x2, pos2, g_pre, w_att, q_g, w_uq, kv_g, w_ukv, rc)


def _compress_kernel(xk_ref, xv_ref, pk_ref, pv_ref, wk_ref, wv_ref, ok_ref, ovt_ref):
    n = xk_ref.shape[1]
    vboth = None
    for g in range(NSA_GROUPS):
        xk = xk_ref[g].astype(F32)
        lo = _dot((xk + pk_ref[0:1]).astype(BF16), wk_ref[0])
        hi = _dot((xk + pk_ref[1:2]).astype(BF16), wk_ref[1])
        ok_ref[g] = (lo + pltpu.roll(hi, n - 1, axis=0)).astype(BF16)
        xv = xv_ref[g].astype(F32)
        lo = _dot((xv + pv_ref[0:1]).astype(BF16), wv_ref[g, 0])
        hi = _dot((xv + pv_ref[1:2]).astype(BF16), wv_ref[g, 1])
        vg = lo + pltpu.roll(hi, n - 1, axis=0)
        vboth = vg if vboth is None else vboth + vg
    ovt_ref[0] = vboth.T.astype(BF16)


def _compress(xk, xv, pk, pv, wk, wv, B):
    G, TC, CW = xk.shape
    nch = TC // B
    xspec = pl.BlockSpec((G, nch, CW), lambda b: (0, b, 0))
    return pl.pallas_call(
        _compress_kernel,
        grid=(B,),
        in_specs=[xspec, xspec, _const_spec(pk.shape), _const_spec(pv.shape),
                  _const_spec(wk.shape), _const_spec(wv.shape)],
        out_specs=[pl.BlockSpec((G, nch, NSA_DIM), lambda b: (0, b, 0)),
                   pl.BlockSpec((1, G * NSA_DIM, nch), lambda b: (b, 0, 0))],
        out_shape=[jax.ShapeDtypeStruct((G, TC, NSA_DIM), BF16),
                   jax.ShapeDtypeStruct((B, G * NSA_DIM, nch), BF16)],
        compiler_params=_cparams(("parallel",)),
        name="compress",
    )(xk, xv, pk, pv, wk, wv)


def _cmp_select_kernel(q_ref, kc_ref, vct_ref, ovt_ref, o_ref, bias_ref, *, tq, n_pick, nqt, n_var):
    i = pl.program_id(1)
    q0 = i * tq
    nch_all = kc_ref.shape[1]
    nblk_all = ovt_ref.shape[0]
    tiles_per_var = nqt // n_var

    def variant(nch, nblk):
        t_c = q0 + lax.broadcasted_iota(jnp.int32, (nch, tq), 1)
        n_c = lax.broadcasted_iota(jnp.int32, (nch, tq), 0)
        cmask = (n_c * CMP_STRIDE + (CMP_LEN - 1)) <= t_c
        t_b = q0 + lax.broadcasted_iota(jnp.int32, (nblk, tq), 1)
        blk = lax.broadcasted_iota(jnp.int32, (nblk, tq), 0)
        blk_f = blk.astype(F32)
        cur = lax.shift_right_logical(t_b, int(math.log2(SEL_LEN)))
        forced = (blk == 0) | (blk == cur) | (blk == cur - 1)
        causal = blk * SEL_LEN <= t_b
        outs = []
        for g in range(NSA_GROUPS):
            psum = None
            for j in range(NSA_HPG):
                q = q_ref[0, g * NSA_HPG + j][:, :NSA_DIM]
                s = jnp.where(cmask, _dot_nt(kc_ref[g, :nch, :], q), MASKED)
                m = jnp.maximum(jnp.max(s, axis=0, keepdims=True), M_FLOOR)
                p = jnp.exp2(s - m)
                l = jnp.sum(p, axis=0, keepdims=True)
                pn = p * jnp.where(l > 0.0, 1.0 / l, 0.0)
                outs.append(_dot(vct_ref[0, g * NSA_DIM:(g + 1) * NSA_DIM, :nch], pn.astype(BF16)))
                psum = pn if psum is None else psum + pn
            imp = None
            for piece in _split_bf16(psum, 3):
                d = _dot(ovt_ref[:nblk, :nch], piece)
                imp = d if imp is None else imp + d
            v0 = jnp.where(forced | jnp.logical_not(causal), -1.0, imp)
            v = v0
            for _ in range(n_pick):
                mx = jnp.max(v, axis=0, keepdims=True)
                first = jnp.min(jnp.where(v == mx, blk_f, float(nblk)), axis=0, keepdims=True)
                v = jnp.where(blk_f == first, -2.0, v)
            sel = forced | ((v == -2.0) & (v0 >= 0.0))
            bias = jnp.where(sel, 0.0, SEL_BIAS)
            if nblk < nblk_all:
                bias = jnp.concatenate([bias, jnp.full((nblk_all - nblk, tq), SEL_BIAS, F32)], axis=0)
            bias_ref[0, g] = bias.T.astype(BF16)
        o_ref[...] = jnp.concatenate(outs, axis=0).T.astype(BF16)

    for k in range(n_var):
        @pl.when((i >= k * tiles_per_var) & (i < (k + 1) * tiles_per_var))
        def _():
            variant(nch_all * (k + 1) // n_var, nblk_all * (k + 1) // n_var)


def _cmp_select(nq, kc, vct, ovt, B, S, tq):
    T = B * S
    nch = kc.shape[1] // B
    nblk = ovt.shape[0]
    nqt = S // tq
    n_var = 4 if nqt % 4 == 0 and (nch // 4) % LANES == 0 and (nblk // 4) % SUBLANES == 0 else 1
    kern = functools.partial(_cmp_select_kernel, tq=tq, n_pick=N_SEL - 3, nqt=nqt, n_var=n_var)
    return pl.pallas_call(
        kern,
        grid=(B, nqt),
        in_specs=[
            pl.BlockSpec((1, NSA_HEADS, tq, LANES), lambda b, i: (b, 0, i, 0)),
            pl.BlockSpec((NSA_GROUPS, nch, NSA_DIM), lambda b, i: (0, b, 0)),
            pl.BlockSpec((1, NSA_GROUPS * NSA_DIM, nch), lambda b, i: (b, 0, 0)),
            _const_spec(ovt.shape),
        ],
        out_specs=[pl.BlockSpec((tq, NSA_HEADS * NSA_DIM), lambda b, i: (b * nqt + i, 0)),
                   pl.BlockSpec((1, NSA_GROUPS, tq, nblk), lambda b, i: (b, 0, i, 0))],
        out_shape=[jax.ShapeDtypeStruct((T, NSA_HEADS * NSA_DIM), BF16),
                   jax.ShapeDtypeStruct((B, NSA_GROUPS, S, nblk), BF16)],
        compiler_params=_cparams(("parallel", "parallel")),
        name="cmp_select",
    )(nq, kc, vct, ovt)


def _flash_heads(nheads, kqv, mask, src, dst, lazy=False, unroll=True):
    def scores(hd):
        k, q, _ = kqv(hd)
        return _dot_nt(k, q)

    def update(hd, s):
        if mask is not None:
            s = jnp.where(mask, s, MASKED)
        tmax = jnp.max(s, axis=0, keepdims=True)
        vt = kqv(hd)[2]
        rise = None
        if src is None:
            m_new = jnp.maximum(tmax, M_FLOOR)
            acc = _dot(vt, jnp.exp2(s - m_new).astype(BF16))
        else:
            m_old = src[0][hd]
            m_new = jnp.maximum(m_old, tmax)
            alpha = jnp.exp2(m_old - m_new)
            if lazy:
                acc = (src[1][hd] + _dot(vt, jnp.exp2(s - m_old).astype(BF16))) * alpha
                rise = tmax - m_old
            else:
                acc = alpha * src[1][hd] + _dot(vt, jnp.exp2(s - m_new).astype(BF16))
        dst[0][hd] = m_new
        dst[1][hd] = acc
        return rise

    if not unroll:
        def body(hd, carry):
            update(hd, scores(hd))
            return carry
        lax.fori_loop(0, nheads, body, 0)
        return None
    ahead = 2
    pending = [scores(hd) for hd in range(min(ahead, nheads))]
    worst = None
    for hd in range(nheads):
        s = pending.pop(0)
        if hd + ahead < nheads:
            pending.append(scores(hd + ahead))
        rise = update(hd, s)
        if rise is not None:
            worst = rise if worst is None else jnp.maximum(worst, rise)
    return worst


def _flash_init(m_ref, acc_ref):
    m_ref[...] = jnp.full(m_ref.shape, M_FLOOR, F32)
    acc_ref[...] = jnp.zeros(acc_ref.shape, F32)


def _flash_finalize(o_ref, acc_ref):
    nv, rows, tq = acc_ref.shape
    dv = rows - SUM_ROWS
    groups = o_ref.shape[0] if len(o_ref.shape) == 3 else 1
    nh = nv // groups
    for gb in range(groups):
        sl = slice(gb * nh, (gb + 1) * nh)
        o = acc_ref[sl, :dv, :] * (1.0 / acc_ref[sl, dv:dv + 1, :])
        o = o.reshape(nh * dv, tq).T.astype(BF16)
        if len(o_ref.shape) == 3:
            o_ref[gb] = o
        else:
            o_ref[...] = o


def _flash_scratch(nh, dv, tq, slots=None):
    lead = (nh,) if slots is None else (slots, nh)
    return [pltpu.VMEM(lead + (1, tq), F32), pltpu.VMEM(lead + (dv + SUM_ROWS, tq), F32)]


def _causal_pairs(nqt):
    pairs = [(i, j) for i in range(nqt) for j in range(i + 1)]
    return (jnp.asarray([p[0] for p in pairs], jnp.int32), jnp.asarray([p[1] for p in pairs], jnp.int32))


def _tile_iota(tq):
    return (lax.broadcasted_iota(jnp.int32, (tq, tq), 0), lax.broadcasted_iota(jnp.int32, (tq, tq), 1))


def _causal_flash_step(i, j, nheads, kqv, tq, o_ref, m_ref, acc_ref, slot_ref):
    key, qry = _tile_iota(tq)

    @pl.when(j == 0)
    def _():
        mask = key <= qry + jnp.where(i > 0, tq, 0)
        _flash_heads(nheads, kqv, mask, None, (m_ref.at[0], acc_ref.at[0]))
        slot_ref[0] = 0

    def lazy_step(mask):
        cur = slot_ref[0]
        src = (m_ref.at[cur], acc_ref.at[cur])
        dst = (m_ref.at[1 - cur], acc_ref.at[1 - cur])
        worst = _flash_heads(nheads, kqv, mask, src, dst, lazy=True)

        @pl.when(jnp.max(worst) > LAZY_SPAN)
        def _():
            _flash_heads(nheads, kqv, mask, src, dst, unroll=False)

        slot_ref[0] = 1 - cur

    @pl.when((j > 0) & (j < i))
    def _():
        lazy_step(None)

    @pl.when((j > 0) & (j == i))
    def _():
        lazy_step(key <= qry)

    @pl.when(j == i)
    def _():
        _flash_finalize(o_ref, acc_ref.at[slot_ref[0]])


def _split_virtual(v, nheads):
    if isinstance(v, int):
        return v // nheads, v % nheads
    return lax.div(v, nheads), lax.rem(v, nheads)


def _mla_kernel(qi_ref, ki_ref, q_ref, k_ref, vt_ref, o_ref, m_ref, acc_ref, slot_ref, *, tq):
    i, j = qi_ref[pl.program_id(0)], ki_ref[pl.program_id(0)]

    def kqv(v):
        bb, hd = _split_virtual(v, MLA_HEADS)
        return k_ref[bb, hd], q_ref[bb, hd], vt_ref[bb, hd]

    _causal_flash_step(i, j, q_ref.shape[0] * MLA_HEADS, kqv, tq, o_ref, m_ref, acc_ref, slot_ref)


def _mla_attention(qm, km, vmt, B, S, tq):
    nqt = S // tq
    kern = functools.partial(_mla_kernel, tq=tq)
    qi, ki = _causal_pairs(nqt)
    return pl.pallas_call(
        kern,
        grid_spec=pltpu.PrefetchScalarGridSpec(
            num_scalar_prefetch=2,
            grid=(qi.shape[0],),
            in_specs=[pl.BlockSpec((B, MLA_HEADS, tq, LANES), lambda p, qi, ki: (0, 0, qi[p], 0)),
                      pl.BlockSpec((B, MLA_HEADS, tq, LANES), lambda p, qi, ki: (0, 0, ki[p], 0)),
                      pl.BlockSpec((B, MLA_HEADS, MLA_V + SUM_ROWS, tq), lambda p, qi, ki: (0, 0, 0, ki[p]))],
            out_specs=pl.BlockSpec((B, tq, MLA_HEADS * MLA_V), lambda p, qi, ki: (0, qi[p], 0)),
            scratch_shapes=_flash_scratch(B * MLA_HEADS, MLA_V, tq, slots=2) + [pltpu.SMEM((1,), jnp.int32)]),
        out_shape=jax.ShapeDtypeStruct((B, S, MLA_HEADS * MLA_V), BF16),
        compiler_params=_cparams(("arbitrary",)),
        name="mla_attn",
    )(qi, ki, qm, km, vmt).reshape(B * S, MLA_HEADS * MLA_V)


def _slc_kernel(qi_ref, ki_ref, q_ref, bias_ref, k_ref, vt_ref, o_ref, qa_ref, m_ref, acc_ref,
                slot_ref, *, tq):
    i, j = qi_ref[pl.program_id(0)], ki_ref[pl.program_id(0)]
    nb = q_ref.shape[0]
    nblk = bias_ref.shape[3]

    @pl.when(j == 0)
    def _():
        if nblk < LANES:
            qa_ref[...] = jnp.zeros(qa_ref.shape, BF16)
        for bb in range(nb):
            for hd in range(NSA_HEADS):
                qa_ref[bb * NSA_HEADS + hd, :, 0:nblk] = bias_ref[bb, hd // NSA_HPG]
                qa_ref[bb * NSA_HEADS + hd, :, LANES:2 * LANES] = q_ref[bb, hd]

    def kqv(v):
        bb, hd = _split_virtual(v, NSA_HEADS)
        g = hd // NSA_HPG if isinstance(hd, int) else lax.div(hd, NSA_HPG)
        return k_ref[bb, g], qa_ref[v], vt_ref[bb, g]

    _causal_flash_step(i, j, nb * NSA_HEADS, kqv, tq, o_ref, m_ref, acc_ref, slot_ref)


def _slc_attention(nq, bias, ksa, vst, B, S, tq):
    nqt = S // tq
    nblk = bias.shape[3]
    kern = functools.partial(_slc_kernel, tq=tq)
    qi, ki = _causal_pairs(nqt)
    qmap = lambda p, qi, ki: (0, 0, qi[p], 0)
    return pl.pallas_call(
        kern,
        grid_spec=pltpu.PrefetchScalarGridSpec(
            num_scalar_prefetch=2,
            grid=(qi.shape[0],),
            in_specs=[pl.BlockSpec((B, NSA_HEADS, tq, LANES), qmap),
                      pl.BlockSpec((B, NSA_GROUPS, tq, nblk), qmap),
                      pl.BlockSpec((B, NSA_GROUPS, tq, 2 * LANES), lambda p, qi, ki: (0, 0, ki[p], 0)),
                      pl.BlockSpec((B, NSA_GROUPS, NSA_DIM + SUM_ROWS, tq), lambda p, qi, ki: (0, 0, 0, ki[p]))],
            out_specs=pl.BlockSpec((B, tq, NSA_HEADS * NSA_DIM), lambda p, qi, ki: (0, qi[p], 0)),
            scratch_shapes=[pltpu.VMEM((B * NSA_HEADS, tq, 2 * LANES), BF16)]
                           + _flash_scratch(B * NSA_HEADS, NSA_DIM, tq, slots=2)
                           + [pltpu.SMEM((1,), jnp.int32)]),
        out_shape=jax.ShapeDtypeStruct((B, S, NSA_HEADS * NSA_DIM), BF16),
        compiler_params=_cparams(("arbitrary",)),
        name="slc_attn",
    )(qi, ki, nq, bias, ksa, vst).reshape(B * S, NSA_HEADS * NSA_DIM)


def _win_kernel(q_ref, k_ref, vt_ref, o_ref, m_ref, acc_ref, *, tq):
    i, j = pl.program_id(1), pl.program_id(2)

    @pl.when(j == 0)
    def _():
        _flash_init(m_ref, acc_ref)

    def step(prev):
        key, qry = _tile_iota(tq)
        mask = (key > qry) if prev else (key <= qry)
        _flash_heads(NSA_HEADS, lambda hd: (k_ref[hd // NSA_HPG], q_ref[0, hd][:, :NSA_DIM],
                                            vt_ref[hd // NSA_HPG]), mask, (m_ref, acc_ref), (m_ref, acc_ref))

    @pl.when((j == 0) & (i > 0))
    def _():
        step(True)

    @pl.when(j == 1)
    def _():
        step(False)
        _flash_finalize(o_ref, acc_ref)


def _win_attention(nq, kw, vwt, B, S, tq):
    T = B * S
    nqt = S // tq
    kern = functools.partial(_win_kernel, tq=tq)
    kidx = lambda b, i, j: b * nqt + jnp.maximum(i - 1 + j, 0)
    return pl.pallas_call(
        kern,
        grid=(B, nqt, 2),
        in_specs=[pl.BlockSpec((1, NSA_HEADS, tq, LANES), lambda b, i, j: (b, 0, i, 0)),
                  pl.BlockSpec((NSA_GROUPS, tq, NSA_DIM), lambda b, i, j: (0, kidx(b, i, j), 0)),
                  pl.BlockSpec((NSA_GROUPS, NSA_DIM + SUM_ROWS, tq), lambda b, i, j: (0, 0, kidx(b, i, j)))],
        out_specs=pl.BlockSpec((tq, NSA_HEADS * NSA_DIM), lambda b, i, j: (b * nqt + i, 0)),
        out_shape=jax.ShapeDtypeStruct((T, NSA_HEADS * NSA_DIM), BF16),
        scratch_shapes=_flash_scratch(NSA_HEADS, NSA_DIM, tq),
        compiler_params=_cparams(("parallel", "parallel", "arbitrary")),
        name="win_attn",
    )(nq, kw, vwt)


def _causal_conv(u, carry, w_ref):
    row = lax.broadcasted_iota(jnp.int32, u.shape, 0)
    c1 = carry[SUBLANES - 1:SUBLANES]
    c2 = carry[SUBLANES - 2:SUBLANES - 1]
    u1 = jnp.where(row == 0, c1, pltpu.roll(u, 1, axis=0))
    u2 = jnp.where(row == 0, c2, jnp.where(row == 1, c1, pltpu.roll(u, 2, axis=0)))
    return w_ref[0:1] * u2 + w_ref[1:2] * u1 + w_ref[2:3] * u


def _merge_kernel(x_ref, g_ref, wgc_ref, cw_ref, ym_ref, oc_ref, os_ref, ow_ref, gl_ref, ex_ref,
                  wbc_ref, wbm_ref, wbn_ref, wo_ref, gp_ref, out_ref, carry_ref, *, tiles_per_seq):
    D = x_ref.shape[1]
    DC = wbc_ref.shape[0]

    @pl.when(pl.program_id(0) % tiles_per_seq == 0)
    def _():
        carry_ref[...] = jnp.zeros(carry_ref.shape, F32)

    x = x_ref[...]
    h = _rms(x, g_ref[...]).astype(BF16)
    p = _dot(h, wgc_ref[...])
    u = p[:, 3 * D + DC:3 * D + 2 * DC] * p[:, 3 * D + 2 * DC:3 * D + 3 * DC]
    y_conv = p[:, 3 * D:3 * D + DC] * _causal_conv(u, carry_ref[...], cw_ref)
    carry_ref[...] = u[u.shape[0] - SUBLANES:]
    hi, lo = _split_bf16(gl_ref[...], 2)
    gate = jax.nn.sigmoid(_dot(hi, ex_ref[...]) + _dot(lo, ex_ref[...]))
    HD = oc_ref.shape[1]
    y_nsa = (gate[:, 0:HD] * oc_ref[...].astype(F32) + gate[:, HD:2 * HD] * os_ref[...].astype(F32)
             + gate[:, 2 * HD:3 * HD] * ow_ref[...].astype(F32))
    merged = (jax.nn.sigmoid(p[:, 0:D]) * _dot(y_conv.astype(BF16), wbc_ref[...])
              + jax.nn.sigmoid(p[:, D:2 * D]) * _dot(ym_ref[...], wbm_ref[...])
              + jax.nn.sigmoid(p[:, 2 * D:3 * D]) * _dot(y_nsa.astype(BF16), wbn_ref[...]))
    out_ref[...] = x + _rms(_dot(merged.astype(BF16), wo_ref[...]), gp_ref[...])


def _merge(x2, g_pre, w_gc, conv_w, y_mla, o_cmp, o_slc, o_win, gl, ex, wbc, wbm, wbn, wo, g_post,
           S, tm):
    T, D = x2.shape
    HD = y_mla.shape[1]
    row = lambda w: pl.BlockSpec((tm, w), lambda i: (i, 0))
    kern = functools.partial(_merge_kernel, tiles_per_seq=S // tm)
    return pl.pallas_call(
        kern,
        grid=(T // tm,),
        in_specs=[row(D), _const_spec((1, D)), _const_spec(w_gc.shape), _const_spec(conv_w.shape),
                  row(HD), row(HD), row(HD), row(HD), row(LANES), _const_spec(ex.shape),
                  _const_spec(wbc.shape), _const_spec(wbm.shape), _const_spec(wbn.shape),
                  _const_spec(wo.shape), _const_spec((1, D))],
        out_specs=row(D),
        out_shape=jax.ShapeDtypeStruct((T, D), F32),
        scratch_shapes=[pltpu.VMEM((SUBLANES, wbc.shape[0]), F32)],
        compiler_params=_cparams(("arbitrary",)),
        name="merge",
    )(x2, g_pre, w_gc, conv_w, y_mla, o_cmp, o_slc, o_win, gl, ex, wbc, wbm, wbn, wo, g_post)


def _ffn_kernel(x_ref, g_ref, wup_ref, cw_ref, cb_ref, wdn_ref, gp_ref, out_ref, carry_ref,
                *, tiles_per_seq, chunk):
    F = wdn_ref.shape[0]

    @pl.when(pl.program_id(0) % tiles_per_seq == 0)
    def _():
        carry_ref[...] = jnp.zeros(carry_ref.shape, F32)

    x = x_ref[...]
    h = _rms(x, g_ref[...]).astype(BF16)
    acc = None
    for c0 in range(0, F, chunk):
        a = _dot(h, wup_ref[:, c0:c0 + chunk])
        b = _dot(h, wup_ref[:, F + c0:F + c0 + chunk])
        ac = _causal_conv(a, carry_ref[:, c0:c0 + chunk], cw_ref.at[:, c0:c0 + chunk])
        ac = ac + cb_ref[:, c0:c0 + chunk]
        carry_ref[:, c0:c0 + chunk] = a[a.shape[0] - SUBLANES:]
        d = _dot((jax.nn.gelu(ac) * b).astype(BF16), wdn_ref[c0:c0 + chunk, :])
        acc = d if acc is None else acc + d
    out_ref[...] = x + _rms(acc, gp_ref[...])


def _ffn(x2, g_pre, w_up, conv_w, conv_b, w_dn, g_post, S, tm, chunk):
    T, D = x2.shape
    F = w_dn.shape[0]
    row = pl.BlockSpec((tm, D), lambda i: (i, 0))
    kern = functools.partial(_ffn_kernel, tiles_per_seq=S // tm, chunk=chunk)
    return pl.pallas_call(
        kern,
        grid=(T // tm,),
        in_specs=[row, _const_spec((1, D)), _const_spec(w_up.shape), _const_spec(conv_w.shape),
                  _const_spec((1, F)), _const_spec(w_dn.shape), _const_spec((1, D))],
        out_specs=row,
        out_shape=jax.ShapeDtypeStruct((T, D), F32),
        scratch_shapes=[pltpu.VMEM((SUBLANES, F), F32)],
        compiler_params=_cparams(("arbitrary",)),
        name="ffn",
    )(x2, g_pre, w_up, conv_w, conv_b, w_dn, g_post)


def _overlap_matrix_t(S):
    nch = S // CMP_STRIDE
    n_cmp = (S - CMP_LEN) // CMP_STRIDE + 1
    cs = np.arange(nch)[None, :] * CMP_STRIDE
    ss = np.arange(S // SEL_LEN)[:, None] * SEL_LEN
    ov = (cs <= ss + SEL_LEN - 1) & (cs + CMP_LEN - 1 >= ss) & (np.arange(nch)[None, :] < n_cmp)
    return ov.astype(np.float32)


def _gate_expand():
    ex = np.zeros((LANES, 3 * NSA_HEADS * NSA_DIM), np.float32)
    for hd in range(NSA_HEADS):
        for c in range(3):
            ex[hd * 3 + c, c * NSA_HEADS * NSA_DIM + hd * NSA_DIM:
               c * NSA_HEADS * NSA_DIM + (hd + 1) * NSA_DIM] = 1.0
    return ex


def _pad_rows(w, rows):
    return jnp.concatenate([w, jnp.zeros((rows - w.shape[0],) + w.shape[1:], w.dtype)], axis=0)


def kernel(x, positions, norm_mix_pre, norm_mix_post, w_in, conv_w, mla_q_norm, mla_w_uq, mla_kv_norm, mla_w_ukv, nsa_cmp_pos_k, nsa_cmp_pos_v, nsa_cmp_w_k, nsa_cmp_w_v, w_branch_conv, w_branch_mla, w_branch_nsa, w_out, norm_ffn_pre, norm_ffn_post, ffn_w_up, ffn_conv_w, ffn_conv_b, ffn_w_down):
    B, S, D = x.shape
    T = B * S
    depth = w_in.shape[0]
    DC = conv_w.shape[2]
    F = ffn_w_down.shape[1]
    QL = mla_q_norm.shape[1]
    KL = mla_kv_norm.shape[1]
    tq = 512
    assert S % tq == 0 and WINDOW == tq and N_SEL <= S // SEL_LEN <= LANES
    tm_proj, tm_merge, tm_ffn, ffn_chunk = 512, 256, 256, 1408
    assert F % ffn_chunk == 0 and ffn_chunk % LANES == 0

    x2 = x.reshape(T, D)
    pos2 = positions.reshape(T, 1)
    rc = jnp.asarray(_rope_consts())
    ovt = jnp.asarray(_overlap_matrix_t(S), BF16)
    ex = jnp.asarray(_gate_expand(), BF16)
    half = CMP_LEN // 2 * NSA_DIM
    o_att = 3 * D + 3 * DC

    for l in range(depth):
        w = w_in[l]
        o_kr = o_att + QL + KL
        z = lambda n: jnp.zeros((D, n), F32)
        w_att = jnp.concatenate(
            [w[:, o_att:o_kr], z(MLA_NOPE), w[:, o_kr:o_kr + MLA_ROPE], z(LANES - MLA_NOPE - MLA_ROPE),
             w[:, o_kr + MLA_ROPE:o_kr + MLA_ROPE + NSA_HEADS * NSA_DIM + 6 * NSA_GROUPS * NSA_DIM],
             w[:, w.shape[1] - 3 * NSA_HEADS:], z(LANES - 3 * NSA_HEADS)], axis=1).astype(BF16)
        dq = MLA_NOPE + MLA_ROPE
        w_uq = jnp.pad(mla_w_uq[l].reshape(QL, MLA_HEADS, dq),
                       ((0, 0), (0, 0), (0, LANES - dq))).reshape(QL, MLA_HEADS * LANES).astype(BF16)
        (qm, km, vmt, nq, kc_in, vc_in, ksa, vst, kw, vwt, gl) = _project(
            x2, pos2, norm_mix_pre[l][None], w_att, mla_q_norm[l][None], w_uq,
            mla_kv_norm[l][None], mla_w_ukv[l].astype(BF16), rc, S, tm_proj)

        chunks = lambda a: a.reshape(NSA_GROUPS, T // CMP_STRIDE, CMP_STRIDE * NSA_DIM)
        wv = nsa_cmp_w_v[l].reshape(2, half, NSA_DIM)
        zv = jnp.zeros_like(wv)
        wv_pad = jnp.stack([jnp.concatenate([wv, zv], axis=2), jnp.concatenate([zv, wv], axis=2)])
        kc, vct = _compress(chunks(kc_in), chunks(vc_in),
                            nsa_cmp_pos_k[l].reshape(2, half), nsa_cmp_pos_v[l].reshape(2, half),
                            nsa_cmp_w_k[l].reshape(2, half, NSA_DIM).astype(BF16),
                            wv_pad.astype(BF16), B)

        y_mla = _mla_attention(qm, km, vmt, B, S, tq)
        o_cmp, bias = _cmp_select(nq, kc, vct, ovt, B, S, tq)
        o_slc = _slc_attention(nq, bias, ksa, vst, B, S, tq)
        o_win = _win_attention(nq, kw, vwt, B, S, tq)

        x2 = _merge(x2, norm_mix_pre[l][None], w[:, :o_att].astype(BF16),
                    _pad_rows(conv_w[l], SUBLANES), y_mla, o_cmp, o_slc, o_win, gl, ex,
                    w_branch_conv[l].astype(BF16), w_branch_mla[l].astype(BF16),
                    w_branch_nsa[l].astype(BF16), w_out[l].astype(BF16), norm_mix_post[l][None],
                    S, tm_merge)
        x2 = _ffn(x2, norm_ffn_pre[l][None], ffn_w_up[l].astype(BF16),
                  _pad_rows(ffn_conv_w[l], SUBLANES), ffn_conv_b[l][None],
                  ffn_w_down[l].astype(BF16), norm_ffn_post[l][None], S, tm_ffn, ffn_chunk)
    return x2.reshape(B, S, D)
```

```python
import functools
import math

import numpy as np
import jax
import jax.numpy as jnp
from jax import lax
from jax.experimental import pallas as pl
from jax.experimental.pallas import tpu as pltpu

F32 = jnp.float32
BF16 = jnp.bfloat16

ROPE_THETA = 500000.0
RMS_EPS = 1e-6
CONV_WIDTH = 3
MLA_HEADS = 8
MLA_NOPE = 64
MLA_ROPE = 32
MLA_V = 64
NSA_HEADS = 8
NSA_GROUPS = 2
NSA_HPG = NSA_HEADS // NSA_GROUPS
NSA_DIM = 64
NSA_ROT = NSA_DIM // 4
CMP_LEN = 32
CMP_STRIDE = 16
SEL_LEN = 64
N_SEL = 16
WINDOW = 512

LANES = 128
SUBLANES = 8
VMEM_LIMIT = 56 * 1024 * 1024
MASKED = -1e30
M_FLOOR = -1e20
SEL_BIAS = -2.0 ** 100
LAZY_SPAN = 64.0
LOG2E = math.log2(math.e)
SUM_ROWS = 16


def _cparams(sem):
    return pltpu.CompilerParams(dimension_semantics=sem, vmem_limit_bytes=VMEM_LIMIT)


def _const_spec(shape):
    nd = len(shape)
    return pl.BlockSpec(shape, lambda *_: (0,) * nd, pipeline_mode=pl.Buffered(1))


def _rms(x, g):
    return x * lax.rsqrt(jnp.mean(x * x, axis=-1, keepdims=True) + RMS_EPS) * g


def _dot(a, b):
    return jnp.dot(a, b, preferred_element_type=F32)


def _split_bf16(a, terms):
    pieces, rem = [], a
    for _ in range(terms):
        piece = rem.astype(BF16)
        rem = rem - piece.astype(F32)
        pieces.append(piece)
    return pieces


def _rope(x, cos, sin_up, sin_dn, half):
    return (x * cos + pltpu.roll(x, half, axis=1) * sin_up
            + pltpu.roll(x, LANES - half, axis=1) * sin_dn)


def _proj_kernel(x_ref, pos_ref, g_ref, watt_ref, qg_ref, wuq_ref, kvg_ref, wukv_ref, rc_ref,
                 qm_ref, km_ref, vmt_ref, nq_ref, kc_ref, vc_ref, ksa_ref, vst_ref, kw_ref, vwt_ref,
                 gl_ref, *, mla_scale, nsa_scale, tiles_per_seq):
    tm = x_ref.shape[0]
    x = x_ref[...]
    h = _rms(x, g_ref[...]).astype(BF16)
    p = _dot(h, watt_ref[...])
    pos = pos_ref[...].astype(F32)
    rc = rc_ref[...]
    ang_m = pos * rc[0:1]
    cos_m, sin_m = jnp.cos(ang_m), jnp.sin(ang_m)
    sup_m, sdn_m = sin_m * rc[1:2], sin_m * rc[2:3]
    ang_n = pos * rc[3:4]
    cos_n, sin_n = jnp.cos(ang_n), jnp.sin(ang_n)
    sup_n, sdn_n = sin_n * rc[4:5], sin_n * rc[5:6]
    lane = lax.broadcasted_iota(jnp.int32, (tm, LANES), 1)
    low = lane < NSA_DIM
    ones_rows = jnp.where(lax.broadcasted_iota(jnp.int32, (SUM_ROWS, tm), 0) == 0, 1.0, 0.0)

    def with_sum_rows(vt):
        return jnp.concatenate([vt, ones_rows], axis=0).astype(BF16)

    def halves(slab):
        return (jnp.where(low, slab, 0.0), jnp.where(low, pltpu.roll(slab, NSA_DIM, axis=1), 0.0))

    cqn = _rms(p[:, 0:384], qg_ref[...]).astype(BF16)
    qf = _dot(cqn, wuq_ref[...])
    for hd in range(MLA_HEADS):
        slab = qf[:, hd * LANES:(hd + 1) * LANES]
        slab = _rope(slab, cos_m, sup_m, sdn_m, MLA_ROPE // 2) * mla_scale
        qm_ref[0, hd] = slab.T.astype(BF16)
    ckvn = _rms(p[:, 384:640], kvg_ref[...]).astype(BF16)
    kv = _dot(ckvn, wukv_ref[...])
    kr = _rope(p[:, 640:768], cos_m, sup_m, sdn_m, MLA_ROPE // 2)
    for hd in range(MLA_HEADS):
        slab = kv[:, hd * LANES:(hd + 1) * LANES]
        km_ref[0, hd] = jnp.where(lane < MLA_NOPE, slab, kr).astype(BF16)
        vmt_ref[0, hd] = with_sum_rows(slab.T[MLA_NOPE:])
    for pr in range(NSA_HEADS // 2):
        slab = p[:, 768 + pr * LANES:768 + (pr + 1) * LANES]
        slab = _rope(slab, cos_n, sup_n, sdn_n, NSA_ROT // 2) * nsa_scale
        q0, q1 = halves(slab)
        nq_ref[0, 2 * pr] = q0.T.astype(BF16)
        nq_ref[0, 2 * pr + 1] = q1.T.astype(BF16)
    nsa = lambda idx: p[:, 1280 + idx * LANES:1280 + (idx + 1) * LANES]
    kcs = _rope(nsa(0), cos_n, sup_n, sdn_n, NSA_ROT // 2)
    kc_ref[0] = kcs[:, :NSA_DIM].astype(BF16)
    kc_ref[1] = pltpu.roll(kcs, NSA_DIM, axis=1)[:, :NSA_DIM].astype(BF16)
    vcs = nsa(1)
    vc_ref[0] = vcs[:, :NSA_DIM].astype(BF16)
    vc_ref[1] = pltpu.roll(vcs, NSA_DIM, axis=1)[:, :NSA_DIM].astype(BF16)
    t_seq = (pl.program_id(0) % tiles_per_seq) * tm + lax.broadcasted_iota(jnp.int32, (tm, LANES), 0)
    onehot = jnp.where(lane == lax.shift_right_logical(t_seq, int(math.log2(SEL_LEN))), 1.0, 0.0)
    k0, k1 = halves(_rope(nsa(2), cos_n, sup_n, sdn_n, NSA_ROT // 2))
    for g, kg in enumerate((k0, k1)):
        ksa_ref[0, g, :, 0:LANES] = onehot.astype(BF16)
        ksa_ref[0, g, :, LANES:2 * LANES] = kg.astype(BF16)
    vst = nsa(3).T
    vst_ref[0, 0] = with_sum_rows(vst[:NSA_DIM])
    vst_ref[0, 1] = with_sum_rows(vst[NSA_DIM:])
    kws = _rope(nsa(4), cos_n, sup_n, sdn_n, NSA_ROT // 2)
    kw_ref[0] = kws[:, :NSA_DIM].astype(BF16)
    kw_ref[1] = pltpu.roll(kws, NSA_DIM, axis=1)[:, :NSA_DIM].astype(BF16)
    vwt = nsa(5).T
    vwt_ref[0] = with_sum_rows(vwt[:NSA_DIM])
    vwt_ref[1] = with_sum_rows(vwt[NSA_DIM:])
    gl_ref[...] = p[:, 2048:2176]


def _rope_consts():
    rc = np.zeros((SUBLANES, LANES), np.float32)
    hm = MLA_ROPE // 2
    fm = ROPE_THETA ** (-np.arange(hm, dtype=np.float32) / hm)
    rc[0, MLA_NOPE:MLA_NOPE + hm] = fm
    rc[0, MLA_NOPE + hm:MLA_NOPE + 2 * hm] = fm
    rc[1, MLA_NOPE + hm:MLA_NOPE + 2 * hm] = 1.0
    rc[2, MLA_NOPE:MLA_NOPE + hm] = -1.0
    hn = NSA_ROT // 2
    fn = ROPE_THETA ** (-np.arange(hn, dtype=np.float32) / hn)
    for base in (0, NSA_DIM):
        rc[3, base:base + hn] = fn
        rc[3, base + hn:base + 2 * hn] = fn
        rc[4, base + hn:base + 2 * hn] = 1.0
        rc[5, base:base + hn] = -1.0
    return rc


def _project(x2, pos2, g_pre, w_att, q_g, w_uq, kv_g, w_ukv, rc, S, tm):
    T, D = x2.shape
    n_att = w_att.shape[1]
    tps = S // tm
    B = T // S
    rows = lambda n, d: (jax.ShapeDtypeStruct((n, T, d), BF16), pl.BlockSpec((n, tm, d), lambda i: (0, i, 0)))
    cols = lambda n, d: (jax.ShapeDtypeStruct((n, d, T), BF16), pl.BlockSpec((n, d, tm), lambda i: (0, 0, i)))
    brows = lambda n, d: (jax.ShapeDtypeStruct((B, n, S, d), BF16),
                          pl.BlockSpec((1, n, tm, d), lambda i: (i // tps, 0, i % tps, 0)))
    bcols = lambda n, d: (jax.ShapeDtypeStruct((B, n, d, S), BF16),
                          pl.BlockSpec((1, n, d, tm), lambda i: (i // tps, 0, 0, i % tps)))
    outs = [bcols(MLA_HEADS, LANES), brows(MLA_HEADS, LANES), bcols(MLA_HEADS, MLA_V + SUM_ROWS),
            bcols(NSA_HEADS, LANES), rows(NSA_GROUPS, NSA_DIM), rows(NSA_GROUPS, NSA_DIM),
            brows(NSA_GROUPS, 2 * LANES), bcols(NSA_GROUPS, NSA_DIM + SUM_ROWS), rows(NSA_GROUPS, NSA_DIM),
            cols(NSA_GROUPS, NSA_DIM + SUM_ROWS),
            (jax.ShapeDtypeStruct((T, LANES), F32), pl.BlockSpec((tm, LANES), lambda i: (i, 0)))]
    kern = functools.partial(_proj_kernel,
                             mla_scale=float((MLA_NOPE + MLA_ROPE) ** -0.5 * LOG2E),
                             nsa_scale=float(NSA_DIM ** -0.5 * LOG2E),
                             tiles_per_seq=S // tm)
    return pl.pallas_call(
        kern,
        grid=(T // tm,),
        in_specs=[
            pl.BlockSpec((tm, D), lambda i: (i, 0)),
            pl.BlockSpec((tm, 1), lambda i: (i, 0)),
            _const_spec((1, D)),
            _const_spec((D, n_att)),
            _const_spec((1, q_g.shape[1])),
            _const_spec(w_uq.shape),
            _const_spec((1, kv_g.shape[1])),
            _const_spec(w_ukv.shape),
            _const_spec((SUBLANES, LANES)),
        ],
        out_specs=[o[1] for o in outs],
        out_shape=[o[0] for o in outs],
        compiler_params=_cparams(("parallel",)),
        name="proj",
    )(x2, pos2, g_pre, w_att, q_g, w_uq, kv_g, w_ukv, rc)


def _compress_kernel(xk_ref, xv_ref, pk_ref, pv_ref, wk_ref, wv_ref, ok_ref, ovt_ref):
    n = xk_ref.shape[1]
    vboth = None
    for g in range(NSA_GROUPS):
        xk = xk_ref[g].astype(F32)
        lo = _dot((xk + pk_ref[0:1]).astype(BF16), wk_ref[0])
        hi = _dot((xk + pk_ref[1:2]).astype(BF16), wk_ref[1])
        ok_ref[g] = (lo + pltpu.roll(hi, n - 1, axis=0)).astype(BF16)
        xv = xv_ref[g].astype(F32)
        lo = _dot((xv + pv_ref[0:1]).astype(BF16), wv_ref[g, 0])
        hi = _dot((xv + pv_ref[1:2]).astype(BF16), wv_ref[g, 1])
        vg = lo + pltpu.roll(hi, n - 1, axis=0)
        vboth = vg if vboth is None else vboth + vg
    ovt_ref[0] = vboth.T.astype(BF16)


def _compress(xk, xv, pk, pv, wk, wv, B):
    G, TC, CW = xk.shape
    nch = TC // B
    xspec = pl.BlockSpec((G, nch, CW), lambda b: (0, b, 0))
    return pl.pallas_call(
        _compress_kernel,
        grid=(B,),
        in_specs=[xspec, xspec, _const_spec(pk.shape), _const_spec(pv.shape),
                  _const_spec(wk.shape), _const_spec(wv.shape)],
        out_specs=[pl.BlockSpec((G, nch, NSA_DIM), lambda b: (0, b, 0)),
                   pl.BlockSpec((1, G * NSA_DIM, nch), lambda b: (b, 0, 0))],
        out_shape=[jax.ShapeDtypeStruct((G, TC, NSA_DIM), BF16),
                   jax.ShapeDtypeStruct((B, G * NSA_DIM, nch), BF16)],
        compiler_params=_cparams(("parallel",)),
        name="compress",
    )(xk, xv, pk, pv, wk, wv)


def _cmp_select_kernel(q_ref, kc_ref, vct_ref, ovt_ref, o_ref, bias_ref, *, tq, n_pick, nqt, n_var):
    i = pl.program_id(1)
    q0 = i * tq
    nch_all = kc_ref.shape[1]
    nblk_all = ovt_ref.shape[0]
    tiles_per_var = nqt // n_var

    def variant(nch, nblk):
        t_c = q0 + lax.broadcasted_iota(jnp.int32, (nch, tq), 1)
        n_c = lax.broadcasted_iota(jnp.int32, (nch, tq), 0)
        cmask = (n_c * CMP_STRIDE + (CMP_LEN - 1)) <= t_c
        t_b = q0 + lax.broadcasted_iota(jnp.int32, (nblk, tq), 1)
        blk = lax.broadcasted_iota(jnp.int32, (nblk, tq), 0)
        blk_f = blk.astype(F32)
        cur = lax.shift_right_logical(t_b, int(math.log2(SEL_LEN)))
        forced = (blk == 0) | (blk == cur) | (blk == cur - 1)
        causal = blk * SEL_LEN <= t_b
        outs = []
        for g in range(NSA_GROUPS):
            psum = None
            for j in range(NSA_HPG):
                qt = q_ref[0, g * NSA_HPG + j][:NSA_DIM, :]
                s = jnp.where(cmask, _dot(kc_ref[g, :nch, :], qt), MASKED)
                m = jnp.maximum(jnp.max(s, axis=0, keepdims=True), M_FLOOR)
                p = jnp.exp2(s - m)
                l = jnp.sum(p, axis=0, keepdims=True)
                pn = p * jnp.where(l > 0.0, 1.0 / l, 0.0)
                outs.append(_dot(vct_ref[0, g * NSA_DIM:(g + 1) * NSA_DIM, :nch], pn.astype(BF16)))
                psum = pn if psum is None else psum + pn
            imp = None
            for piece in _split_bf16(psum, 3):
                d = _dot(ovt_ref[:nblk, :nch], piece)
                imp = d if imp is None else imp + d
            v0 = jnp.where(forced | jnp.logical_not(causal), -1.0, imp)
            v = v0
            for _ in range(n_pick):
                mx = jnp.max(v, axis=0, keepdims=True)
                first = jnp.min(jnp.where(v == mx, blk_f, float(nblk)), axis=0, keepdims=True)
                v = jnp.where(blk_f == first, -2.0, v)
            sel = forced | ((v == -2.0) & (v0 >= 0.0))
            bias = jnp.where(sel, 0.0, SEL_BIAS)
            if nblk < nblk_all:
                bias = jnp.concatenate([bias, jnp.full((nblk_all - nblk, tq), SEL_BIAS, F32)], axis=0)
            bias_ref[0, g] = bias.astype(BF16)
        o_ref[...] = jnp.concatenate(outs, axis=0).T.astype(BF16)

    for k in range(n_var):
        @pl.when((i >= k * tiles_per_var) & (i < (k + 1) * tiles_per_var))
        def _():
            variant(nch_all * (k + 1) // n_var, nblk_all * (k + 1) // n_var)


def _cmp_select(nq, kc, vct, ovt, B, S, tq):
    T = B * S
    nch = kc.shape[1] // B
    nblk = ovt.shape[0]
    nqt = S // tq
    n_var = 4 if nqt % 4 == 0 and (nch // 4) % LANES == 0 and (nblk // 4) % SUBLANES == 0 else 1
    kern = functools.partial(_cmp_select_kernel, tq=tq, n_pick=N_SEL - 3, nqt=nqt, n_var=n_var)
    return pl.pallas_call(
        kern,
        grid=(B, nqt),
        in_specs=[
            pl.BlockSpec((1, NSA_HEADS, LANES, tq), lambda b, i: (b, 0, 0, i)),
            pl.BlockSpec((NSA_GROUPS, nch, NSA_DIM), lambda b, i: (0, b, 0)),
            pl.BlockSpec((1, NSA_GROUPS * NSA_DIM, nch), lambda b, i: (b, 0, 0)),
            _const_spec(ovt.shape),
        ],
        out_specs=[pl.BlockSpec((tq, NSA_HEADS * NSA_DIM), lambda b, i: (b * nqt + i, 0)),
                   pl.BlockSpec((1, NSA_GROUPS, nblk, tq), lambda b, i: (b, 0, 0, i))],
        out_shape=[jax.ShapeDtypeStruct((T, NSA_HEADS * NSA_DIM), BF16),
                   jax.ShapeDtypeStruct((B, NSA_GROUPS, nblk, S), BF16)],
        compiler_params=_cparams(("parallel", "parallel")),
        name="cmp_select",
    )(nq, kc, vct, ovt)


def _flash_heads(nheads, kqv, mask, src, dst, lazy=False, unroll=True):
    def scores(hd):
        k, qt, _ = kqv(hd)
        return _dot(k, qt)

    def update(hd, s):
        if mask is not None:
            s = jnp.where(mask, s, MASKED)
        tmax = jnp.max(s, axis=0, keepdims=True)
        vt = kqv(hd)[2]
        rise = None
        if src is None:
            m_new = jnp.maximum(tmax, M_FLOOR)
            acc = _dot(vt, jnp.exp2(s - m_new).astype(BF16))
        else:
            m_old = src[0][hd]
            m_new = jnp.maximum(m_old, tmax)
            alpha = jnp.exp2(m_old - m_new)
            if lazy:
                acc = (src[1][hd] + _dot(vt, jnp.exp2(s - m_old).astype(BF16))) * alpha
                rise = tmax - m_old
            else:
                acc = alpha * src[1][hd] + _dot(vt, jnp.exp2(s - m_new).astype(BF16))
        dst[0][hd] = m_new
        dst[1][hd] = acc
        return rise

    if not unroll:
        def body(hd, carry):
            update(hd, scores(hd))
            return carry
        lax.fori_loop(0, nheads, body, 0)
        return None
    ahead = 2
    pending = [scores(hd) for hd in range(min(ahead, nheads))]
    worst = None
    for hd in range(nheads):
        s = pending.pop(0)
        if hd + ahead < nheads:
            pending.append(scores(hd + ahead))
        rise = update(hd, s)
        if rise is not None:
            worst = rise if worst is None else jnp.maximum(worst, rise)
    return worst


def _flash_init(m_ref, acc_ref):
    m_ref[...] = jnp.full(m_ref.shape, M_FLOOR, F32)
    acc_ref[...] = jnp.zeros(acc_ref.shape, F32)


def _flash_finalize(o_ref, acc_ref):
    nv, rows, tq = acc_ref.shape
    dv = rows - SUM_ROWS
    groups = o_ref.shape[0] if len(o_ref.shape) == 3 else 1
    nh = nv // groups
    for gb in range(groups):
        sl = slice(gb * nh, (gb + 1) * nh)
        o = acc_ref[sl, :dv, :] * (1.0 / acc_ref[sl, dv:dv + 1, :])
        o = o.reshape(nh * dv, tq).T.astype(BF16)
        if len(o_ref.shape) == 3:
            o_ref[gb] = o
        else:
            o_ref[...] = o


def _flash_scratch(nh, dv, tq, slots=None):
    lead = (nh,) if slots is None else (slots, nh)
    return [pltpu.VMEM(lead + (1, tq), F32), pltpu.VMEM(lead + (dv + SUM_ROWS, tq), F32)]


def _causal_pairs(nqt):
    pairs = [(i, j) for i in range(nqt) for j in range(i + 1)]
    return (jnp.asarray([p[0] for p in pairs], jnp.int32), jnp.asarray([p[1] for p in pairs], jnp.int32))


def _tile_iota(tq):
    return (lax.broadcasted_iota(jnp.int32, (tq, tq), 0), lax.broadcasted_iota(jnp.int32, (tq, tq), 1))


def _causal_flash_step(i, j, nheads, kqv, tq, o_ref, m_ref, acc_ref, slot_ref):
    key, qry = _tile_iota(tq)

    @pl.when(j == 0)
    def _():
        mask = key <= qry + jnp.where(i > 0, tq, 0)
        _flash_heads(nheads, kqv, mask, None, (m_ref.at[0], acc_ref.at[0]))
        slot_ref[0] = 0

    def lazy_step(mask):
        cur = slot_ref[0]
        src = (m_ref.at[cur], acc_ref.at[cur])
        dst = (m_ref.at[1 - cur], acc_ref.at[1 - cur])
        worst = _flash_heads(nheads, kqv, mask, src, dst, lazy=True)

        @pl.when(jnp.max(worst) > LAZY_SPAN)
        def _():
            _flash_heads(nheads, kqv, mask, src, dst, unroll=False)

        slot_ref[0] = 1 - cur

    @pl.when((j > 0) & (j < i))
    def _():
        lazy_step(None)

    @pl.when((j > 0) & (j == i))
    def _():
        lazy_step(key <= qry)

    @pl.when(j == i)
    def _():
        _flash_finalize(o_ref, acc_ref.at[slot_ref[0]])


def _split_virtual(v, nheads):
    if isinstance(v, int):
        return v // nheads, v % nheads
    return lax.div(v, nheads), lax.rem(v, nheads)


def _mla_kernel(qi_ref, ki_ref, q_ref, k_ref, vt_ref, o_ref, m_ref, acc_ref, slot_ref, *, tq):
    i, j = qi_ref[pl.program_id(0)], ki_ref[pl.program_id(0)]

    def kqv(v):
        bb, hd = _split_virtual(v, MLA_HEADS)
        return k_ref[bb, hd], q_ref[bb, hd], vt_ref[bb, hd]

    _causal_flash_step(i, j, q_ref.shape[0] * MLA_HEADS, kqv, tq, o_ref, m_ref, acc_ref, slot_ref)


def _mla_attention(qm, km, vmt, B, S, tq):
    nqt = S // tq
    kern = functools.partial(_mla_kernel, tq=tq)
    qi, ki = _causal_pairs(nqt)
    return pl.pallas_call(
        kern,
        grid_spec=pltpu.PrefetchScalarGridSpec(
            num_scalar_prefetch=2,
            grid=(qi.shape[0],),
            in_specs=[pl.BlockSpec((B, MLA_HEADS, LANES, tq), lambda p, qi, ki: (0, 0, 0, qi[p])),
                      pl.BlockSpec((B, MLA_HEADS, tq, LANES), lambda p, qi, ki: (0, 0, ki[p], 0)),
                      pl.BlockSpec((B, MLA_HEADS, MLA_V + SUM_ROWS, tq), lambda p, qi, ki: (0, 0, 0, ki[p]))],
            out_specs=pl.BlockSpec((B, tq, MLA_HEADS * MLA_V), lambda p, qi, ki: (0, qi[p], 0)),
            scratch_shapes=_flash_scratch(B * MLA_HEADS, MLA_V, tq, slots=2) + [pltpu.SMEM((1,), jnp.int32)]),
        out_shape=jax.ShapeDtypeStruct((B, S, MLA_HEADS * MLA_V), BF16),
        compiler_params=_cparams(("arbitrary",)),
        name="mla_attn",
    )(qi, ki, qm, km, vmt).reshape(B * S, MLA_HEADS * MLA_V)


def _slc_kernel(qi_ref, ki_ref, q_ref, bias_ref, k_ref, vt_ref, o_ref, qa_ref, m_ref, acc_ref,
                slot_ref, *, tq):
    i, j = qi_ref[pl.program_id(0)], ki_ref[pl.program_id(0)]
    nb = q_ref.shape[0]
    nblk = bias_ref.shape[2]

    @pl.when(j == 0)
    def _():
        if nblk < LANES:
            qa_ref[...] = jnp.zeros(qa_ref.shape, BF16)
        for bb in range(nb):
            for hd in range(NSA_HEADS):
                qa_ref[bb * NSA_HEADS + hd, 0:nblk, :] = bias_ref[bb, hd // NSA_HPG]
                qa_ref[bb * NSA_HEADS + hd, LANES:2 * LANES, :] = q_ref[bb, hd]

    def kqv(v):
        bb, hd = _split_virtual(v, NSA_HEADS)
        g = hd // NSA_HPG if isinstance(hd, int) else lax.div(hd, NSA_HPG)
        return k_ref[bb, g], qa_ref[v], vt_ref[bb, g]

    _causal_flash_step(i, j, nb * NSA_HEADS, kqv, tq, o_ref, m_ref, acc_ref, slot_ref)


def _slc_attention(nq, bias, ksa, vst, B, S, tq):
    nqt = S // tq
    nblk = bias.shape[2]
    kern = functools.partial(_slc_kernel, tq=tq)
    qi, ki = _causal_pairs(nqt)
    qmap = lambda p, qi, ki: (0, 0, 0, qi[p])
    return pl.pallas_call(
        kern,
        grid_spec=pltpu.PrefetchScalarGridSpec(
            num_scalar_prefetch=2,
            grid=(qi.shape[0],),
            in_specs=[pl.BlockSpec((B, NSA_HEADS, LANES, tq), qmap),
                      pl.BlockSpec((B, NSA_GROUPS, nblk, tq), qmap),
                      pl.BlockSpec((B, NSA_GROUPS, tq, 2 * LANES), lambda p, qi, ki: (0, 0, ki[p], 0)),
                      pl.BlockSpec((B, NSA_GROUPS, NSA_DIM + SUM_ROWS, tq), lambda p, qi, ki: (0, 0, 0, ki[p]))],
            out_specs=pl.BlockSpec((B, tq, NSA_HEADS * NSA_DIM), lambda p, qi, ki: (0, qi[p], 0)),
            scratch_shapes=[pltpu.VMEM((B * NSA_HEADS, 2 * LANES, tq), BF16)]
                           + _flash_scratch(B * NSA_HEADS, NSA_DIM, tq, slots=2)
                           + [pltpu.SMEM((1,), jnp.int32)]),
        out_shape=jax.ShapeDtypeStruct((B, S, NSA_HEADS * NSA_DIM), BF16),
        compiler_params=_cparams(("arbitrary",)),
        name="slc_attn",
    )(qi, ki, nq, bias, ksa, vst).reshape(B * S, NSA_HEADS * NSA_DIM)


def _win_kernel(q_ref, k_ref, vt_ref, o_ref, m_ref, acc_ref, *, tq):
    i, j = pl.program_id(1), pl.program_id(2)

    @pl.when(j == 0)
    def _():
        _flash_init(m_ref, acc_ref)

    def step(prev):
        key, qry = _tile_iota(tq)
        mask = (key > qry) if prev else (key <= qry)
        _flash_heads(NSA_HEADS, lambda hd: (k_ref[hd // NSA_HPG], q_ref[0, hd][:NSA_DIM, :],
                                            vt_ref[hd // NSA_HPG]), mask, (m_ref, acc_ref), (m_ref, acc_ref))

    @pl.when((j == 0) & (i > 0))
    def _():
        step(True)

    @pl.when(j == 1)
    def _():
        step(False)
        _flash_finalize(o_ref, acc_ref)


def _win_attention(nq, kw, vwt, B, S, tq):
    T = B * S
    nqt = S // tq
    kern = functools.partial(_win_kernel, tq=tq)
    kidx = lambda b, i, j: b * nqt + jnp.maximum(i - 1 + j, 0)
    return pl.pallas_call(
        kern,
        grid=(B, nqt, 2),
        in_specs=[pl.BlockSpec((1, NSA_HEADS, LANES, tq), lambda b, i, j: (b, 0, 0, i)),
                  pl.BlockSpec((NSA_GROUPS, tq, NSA_DIM), lambda b, i, j: (0, kidx(b, i, j), 0)),
                  pl.BlockSpec((NSA_GROUPS, NSA_DIM + SUM_ROWS, tq), lambda b, i, j: (0, 0, kidx(b, i, j)))],
        out_specs=pl.BlockSpec((tq, NSA_HEADS * NSA_DIM), lambda b, i, j: (b * nqt + i, 0)),
        out_shape=jax.ShapeDtypeStruct((T, NSA_HEADS * NSA_DIM), BF16),
        scratch_shapes=_flash_scratch(NSA_HEADS, NSA_DIM, tq),
        compiler_params=_cparams(("parallel", "parallel", "arbitrary")),
        name="win_attn",
    )(nq, kw, vwt)


def _causal_conv(u, carry, w_ref):
    row = lax.broadcasted_iota(jnp.int32, u.shape, 0)
    c1 = carry[SUBLANES - 1:SUBLANES]
    c2 = carry[SUBLANES - 2:SUBLANES - 1]
    u1 = jnp.where(row == 0, c1, pltpu.roll(u, 1, axis=0))
    u2 = jnp.where(row == 0, c2, jnp.where(row == 1, c1, pltpu.roll(u, 2, axis=0)))
    return w_ref[0:1] * u2 + w_ref[1:2] * u1 + w_ref[2:3] * u


def _merge_kernel(x_ref, g_ref, wgc_ref, cw_ref, ym_ref, oc_ref, os_ref, ow_ref, gl_ref, ex_ref,
                  wbc_ref, wbm_ref, wbn_ref, wo_ref, gp_ref, out_ref, carry_ref, *, tiles_per_seq):
    D = x_ref.shape[1]
    DC = wbc_ref.shape[0]

    @pl.when(pl.program_id(0) % tiles_per_seq == 0)
    def _():
        carry_ref[...] = jnp.zeros(carry_ref.shape, F32)

    x = x_ref[...]
    h = _rms(x, g_ref[...]).astype(BF16)
    p = _dot(h, wgc_ref[...])
    u = p[:, 3 * D + DC:3 * D + 2 * DC] * p[:, 3 * D + 2 * DC:3 * D + 3 * DC]
    y_conv = p[:, 3 * D:3 * D + DC] * _causal_conv(u, carry_ref[...], cw_ref)
    carry_ref[...] = u[u.shape[0] - SUBLANES:]
    hi, lo = _split_bf16(gl_ref[...], 2)
    gate = jax.nn.sigmoid(_dot(hi, ex_ref[...]) + _dot(lo, ex_ref[...]))
    HD = oc_ref.shape[1]
    y_nsa = (gate[:, 0:HD] * oc_ref[...].astype(F32) + gate[:, HD:2 * HD] * os_ref[...].astype(F32)
             + gate[:, 2 * HD:3 * HD] * ow_ref[...].astype(F32))
    merged = (jax.nn.sigmoid(p[:, 0:D]) * _dot(y_conv.astype(BF16), wbc_ref[...])
              + jax.nn.sigmoid(p[:, D:2 * D]) * _dot(ym_ref[...], wbm_ref[...])
              + jax.nn.sigmoid(p[:, 2 * D:3 * D]) * _dot(y_nsa.astype(BF16), wbn_ref[...]))
    out_ref[...] = x + _rms(_dot(merged.astype(BF16), wo_ref[...]), gp_ref[...])


def _merge(x2, g_pre, w_gc, conv_w, y_mla, o_cmp, o_slc, o_win, gl, ex, wbc, wbm, wbn, wo, g_post,
           S, tm):
    T, D = x2.shape
    HD = y_mla.shape[1]
    row = lambda w: pl.BlockSpec((tm, w), lambda i: (i, 0))
    kern = functools.partial(_merge_kernel, tiles_per_seq=S // tm)
    return pl.pallas_call(
        kern,
        grid=(T // tm,),
        in_specs=[row(D), _const_spec((1, D)), _const_spec(w_gc.shape), _const_spec(conv_w.shape),
                  row(HD), row(HD), row(HD), row(HD), row(LANES), _const_spec(ex.shape),
                  _const_spec(wbc.shape), _const_spec(wbm.shape), _const_spec(wbn.shape),
                  _const_spec(wo.shape), _const_spec((1, D))],
        out_specs=row(D),
        out_shape=jax.ShapeDtypeStruct((T, D), F32),
        scratch_shapes=[pltpu.VMEM((SUBLANES, wbc.shape[0]), F32)],
        compiler_params=_cparams(("arbitrary",)),
        name="merge",
    )(x2, g_pre, w_gc, conv_w, y_mla, o_cmp, o_slc, o_win, gl, ex, wbc, wbm, wbn, wo, g_post)


def _ffn_kernel(x_ref, g_ref, wup_ref, cw_ref, cb_ref, wdn_ref, gp_ref, out_ref, carry_ref,
                *, tiles_per_seq, chunk):
    F = wdn_ref.shape[0]

    @pl.when(pl.program_id(0) % tiles_per_seq == 0)
    def _():
        carry_ref[...] = jnp.zeros(carry_ref.shape, F32)

    x = x_ref[...]
    h = _rms(x, g_ref[...]).astype(BF16)
    acc = None
    for c0 in range(0, F, chunk):
        a = _dot(h, wup_ref[:, c0:c0 + chunk])
        b = _dot(h, wup_ref[:, F + c0:F + c0 + chunk])
        ac = _causal_conv(a, carry_ref[:, c0:c0 + chunk], cw_ref.at[:, c0:c0 + chunk])
        ac = ac + cb_ref[:, c0:c0 + chunk]
        carry_ref[:, c0:c0 + chunk] = a[a.shape[0] - SUBLANES:]
        d = _dot((jax.nn.gelu(ac) * b).astype(BF16), wdn_ref[c0:c0 + chunk, :])
        acc = d if acc is None else acc + d
    out_ref[...] = x + _rms(acc, gp_ref[...])


def _ffn(x2, g_pre, w_up, conv_w, conv_b, w_dn, g_post, S, tm, chunk):
    T, D = x2.shape
    F = w_dn.shape[0]
    row = pl.BlockSpec((tm, D), lambda i: (i, 0))
    kern = functools.partial(_ffn_kernel, tiles_per_seq=S // tm, chunk=chunk)
    return pl.pallas_call(
        kern,
        grid=(T // tm,),
        in_specs=[row, _const_spec((1, D)), _const_spec(w_up.shape), _const_spec(conv_w.shape),
                  _const_spec((1, F)), _const_spec(w_dn.shape), _const_spec((1, D))],
        out_specs=row,
        out_shape=jax.ShapeDtypeStruct((T, D), F32),
        scratch_shapes=[pltpu.VMEM((SUBLANES, F), F32)],
        compiler_params=_cparams(("arbitrary",)),
        name="ffn",
    )(x2, g_pre, w_up, conv_w, conv_b, w_dn, g_post)


def _overlap_matrix_t(S):
    nch = S // CMP_STRIDE
    n_cmp = (S - CMP_LEN) // CMP_STRIDE + 1
    cs = np.arange(nch)[None, :] * CMP_STRIDE
    ss = np.arange(S // SEL_LEN)[:, None] * SEL_LEN
    ov = (cs <= ss + SEL_LEN - 1) & (cs + CMP_LEN - 1 >= ss) & (np.arange(nch)[None, :] < n_cmp)
    return ov.astype(np.float32)


def _gate_expand():
    ex = np.zeros((LANES, 3 * NSA_HEADS * NSA_DIM), np.float32)
    for hd in range(NSA_HEADS):
        for c in range(3):
            ex[hd * 3 + c, c * NSA_HEADS * NSA_DIM + hd * NSA_DIM:
               c * NSA_HEADS * NSA_DIM + (hd + 1) * NSA_DIM] = 1.0
    return ex


def _pad_rows(w, rows):
    return jnp.concatenate([w, jnp.zeros((rows - w.shape[0],) + w.shape[1:], w.dtype)], axis=0)


def kernel(x, positions, norm_mix_pre, norm_mix_post, w_in, conv_w, mla_q_norm, mla_w_uq, mla_kv_norm, mla_w_ukv, nsa_cmp_pos_k, nsa_cmp_pos_v, nsa_cmp_w_k, nsa_cmp_w_v, w_branch_conv, w_branch_mla, w_branch_nsa, w_out, norm_ffn_pre, norm_ffn_post, ffn_w_up, ffn_conv_w, ffn_conv_b, ffn_w_down):
    B, S, D = x.shape
    T = B * S
    depth = w_in.shape[0]
    DC = conv_w.shape[2]
    F = ffn_w_down.shape[1]
    QL = mla_q_norm.shape[1]
    KL = mla_kv_norm.shape[1]
    tq = 512
    assert S % tq == 0 and WINDOW == tq and N_SEL <= S // SEL_LEN <= LANES
    tm_proj, tm_merge, tm_ffn, ffn_chunk = 512, 256, 256, 1408
    assert F % ffn_chunk == 0 and ffn_chunk % LANES == 0

    x2 = x.reshape(T, D)
    pos2 = positions.reshape(T, 1)
    rc = jnp.asarray(_rope_consts())
    ovt = jnp.asarray(_overlap_matrix_t(S), BF16)
    ex = jnp.asarray(_gate_expand(), BF16)
    half = CMP_LEN // 2 * NSA_DIM
    o_att = 3 * D + 3 * DC

    for l in range(depth):
        w = w_in[l]
        o_kr = o_att + QL + KL
        z = lambda n: jnp.zeros((D, n), F32)
        w_att = jnp.concatenate(
            [w[:, o_att:o_kr], z(MLA_NOPE), w[:, o_kr:o_kr + MLA_ROPE], z(LANES - MLA_NOPE - MLA_ROPE),
             w[:, o_kr + MLA_ROPE:o_kr + MLA_ROPE + NSA_HEADS * NSA_DIM + 6 * NSA_GROUPS * NSA_DIM],
             w[:, w.shape[1] - 3 * NSA_HEADS:], z(LANES - 3 * NSA_HEADS)], axis=1).astype(BF16)
        dq = MLA_NOPE + MLA_ROPE
        w_uq = jnp.pad(mla_w_uq[l].reshape(QL, MLA_HEADS, dq),
                       ((0, 0), (0, 0), (0, LANES - dq))).reshape(QL, MLA_HEADS * LANES).astype(BF16)
        (qm, km, vmt, nq, kc_in, vc_in, ksa, vst, kw, vwt, gl) = _project(
            x2, pos2, norm_mix_pre[l][None], w_att, mla_q_norm[l][None], w_uq,
            mla_kv_norm[l][None], mla_w_ukv[l].astype(BF16), rc, S, tm_proj)

        chunks = lambda a: a.reshape(NSA_GROUPS, T // CMP_STRIDE, CMP_STRIDE * NSA_DIM)
        wv = nsa_cmp_w_v[l].reshape(2, half, NSA_DIM)
        zv = jnp.zeros_like(wv)
        wv_pad = jnp.stack([jnp.concatenate([wv, zv], axis=2), jnp.concatenate([zv, wv], axis=2)])
        kc, vct = _compress(chunks(kc_in), chunks(vc_in),
                            nsa_cmp_pos_k[l].reshape(2, half), nsa_cmp_pos_v[l].reshape(2, half),
                            nsa_cmp_w_k[l].reshape(2, half, NSA_DIM).astype(BF16),
                            wv_pad.astype(BF16), B)

        y_mla = _mla_attention(qm, km, vmt, B, S, tq)
        o_cmp, bias = _cmp_select(nq, kc, vct, ovt, B, S, tq)
        o_slc = _slc_attention(nq, bias, ksa, vst, B, S, tq)
        o_win = _win_attention(nq, kw, vwt, B, S, tq)

        x2 = _merge(x2, norm_mix_pre[l][None], w[:, :o_att].astype(BF16),
                    _pad_rows(conv_w[l], SUBLANES), y_mla, o_cmp, o_slc, o_win, gl, ex,
                    w_branch_conv[l].astype(BF16), w_branch_mla[l].astype(BF16),
                    w_branch_nsa[l].astype(BF16), w_out[l].astype(BF16), norm_mix_post[l][None],
                    S, tm_merge)
        x2 = _ffn(x2, norm_ffn_pre[l][None], ffn_w_up[l].astype(BF16),
                  _pad_rows(ffn_conv_w[l], SUBLANES), ffn_conv_b[l][None],
                  ffn_w_down[l].astype(BF16), norm_ffn_post[l][None], S, tm_ffn, ffn_chunk)
    return x2.reshape(B, S, D)
```

```python
import functools
import math

import numpy as np
import jax
import jax.numpy as jnp
from jax import lax
from jax.experimental import pallas as pl
from jax.experimental.pallas import tpu as pltpu

F32 = jnp.float32
BF16 = jnp.bfloat16

ROPE_THETA = 500000.0
RMS_EPS = 1e-6
CONV_WIDTH = 3
MLA_HEADS = 8
MLA_NOPE = 64
MLA_ROPE = 32
MLA_V = 64
NSA_HEADS = 8
NSA_GROUPS = 2
NSA_HPG = NSA_HEADS // NSA_GROUPS
NSA_DIM = 64
NSA_ROT = NSA_DIM // 4
CMP_LEN = 32
CMP_STRIDE = 16
SEL_LEN = 64
N_SEL = 16
WINDOW = 512

LANES = 128
SUBLANES = 8
VMEM_LIMIT = 56 * 1024 * 1024
MASKED = -1e30
M_FLOOR = -1e20
SEL_BIAS = -2.0 ** 100
LAZY_SPAN = 64.0
LOG2E = math.log2(math.e)
ROPE_ROWS = 32
SUM_ROWS = 16


def _cparams(sem):
    return pltpu.CompilerParams(dimension_semantics=sem, vmem_limit_bytes=VMEM_LIMIT)


def _const_spec(shape):
    nd = len(shape)
    return pl.BlockSpec(shape, lambda *_: (0,) * nd, pipeline_mode=pl.Buffered(1))


def _rms(x, g):
    return x * lax.rsqrt(jnp.mean(x * x, axis=-1, keepdims=True) + RMS_EPS) * g


def _dot(a, b):
    return jnp.dot(a, b, preferred_element_type=F32)


def _split_bf16(a, terms):
    pieces, rem = [], a
    for _ in range(terms):
        piece = rem.astype(BF16)
        rem = rem - piece.astype(F32)
        pieces.append(piece)
    return pieces


def _rope(x, cos, sin_up, sin_dn, half):
    return (x * cos + pltpu.roll(x, half, axis=1) * sin_up
            + pltpu.roll(x, LANES - half, axis=1) * sin_dn)


def _lanes(table, lane, base, moves):
    out = base
    for dst, width, src in moves:
        rolled = pltpu.roll(table, (dst - src) % LANES, axis=1)
        out = jnp.where((lane >= dst) & (lane < dst + width), rolled, out)
    return out


def _proj_kernel(x_ref, pos_ref, g_ref, watt_ref, qg_ref, wuq_ref, kvg_ref, wukv_ref, rc_ref,
                 qm_ref, km_ref, vmt_ref, nq_ref, kc_ref, vc_ref, ksa_ref, vst_ref, kw_ref, vwt_ref,
                 gl_ref, *, mla_scale, nsa_scale, tiles_per_seq):
    tm = x_ref.shape[0]
    x = x_ref[...]
    h = _rms(x, g_ref[...]).astype(BF16)
    p = _dot(h, watt_ref[...])
    lane = lax.broadcasted_iota(jnp.int32, (tm, LANES), 1)
    low = lane < NSA_DIM
    ang_t = rc_ref[:, 0:1] * pos_ref[...].astype(F32)
    cs = jnp.concatenate([jnp.cos(ang_t), jnp.sin(ang_t), jnp.zeros((LANES - 2 * ROPE_ROWS, tm), F32)],
                         axis=0).T
    hm, hn = MLA_ROPE // 2, NSA_ROT // 2
    cos_m = _lanes(cs, lane, 1.0, [(MLA_NOPE, hm, 0), (MLA_NOPE + hm, hm, 0)])
    sup_m = _lanes(cs, lane, 0.0, [(MLA_NOPE + hm, hm, ROPE_ROWS)])
    sdn_m = -_lanes(cs, lane, 0.0, [(MLA_NOPE, hm, ROPE_ROWS)])
    cos_n = _lanes(cs, lane, 1.0, [(0, hn, hm), (hn, hn, hm), (NSA_DIM, hn, hm), (NSA_DIM + hn, hn, hm)])
    sup_n = _lanes(cs, lane, 0.0, [(hn, hn, ROPE_ROWS + hm), (NSA_DIM + hn, hn, ROPE_ROWS + hm)])
    sdn_n = -_lanes(cs, lane, 0.0, [(0, hn, ROPE_ROWS + hm), (NSA_DIM, hn, ROPE_ROWS + hm)])
    ones_rows = jnp.where(lax.broadcasted_iota(jnp.int32, (SUM_ROWS, tm), 0) == 0, 1.0, 0.0)

    def with_sum_rows(vt):
        return jnp.concatenate([vt, ones_rows], axis=0).astype(BF16)

    def halves(slab):
        return (jnp.where(low, slab, 0.0), jnp.where(low, pltpu.roll(slab, NSA_DIM, axis=1), 0.0))

    cqn = _rms(p[:, 0:384], qg_ref[...]).astype(BF16)
    qf = _dot(cqn, wuq_ref[...])
    for hd in range(MLA_HEADS):
        slab = qf[:, hd * LANES:(hd + 1) * LANES]
        slab = _rope(slab, cos_m, sup_m, sdn_m, MLA_ROPE // 2) * mla_scale
        qm_ref[0, hd] = slab.T.astype(BF16)
    ckvn = _rms(p[:, 384:640], kvg_ref[...]).astype(BF16)
    kv = _dot(ckvn, wukv_ref[...])
    kr = _rope(p[:, 640:768], cos_m, sup_m, sdn_m, MLA_ROPE // 2)
    for hd in range(MLA_HEADS):
        slab = kv[:, hd * LANES:(hd + 1) * LANES]
        km_ref[0, hd] = jnp.where(lane < MLA_NOPE, slab, kr).astype(BF16)
        vmt_ref[0, hd] = with_sum_rows(slab.T[MLA_NOPE:])
    for pr in range(NSA_HEADS // 2):
        slab = p[:, 768 + pr * LANES:768 + (pr + 1) * LANES]
        slab = _rope(slab, cos_n, sup_n, sdn_n, NSA_ROT // 2) * nsa_scale
        q0, q1 = halves(slab)
        nq_ref[0, 2 * pr] = q0.T.astype(BF16)
        nq_ref[0, 2 * pr + 1] = q1.T.astype(BF16)
    nsa = lambda idx: p[:, 1280 + idx * LANES:1280 + (idx + 1) * LANES]
    kcs = _rope(nsa(0), cos_n, sup_n, sdn_n, NSA_ROT // 2)
    kc_ref[0] = kcs[:, :NSA_DIM].astype(BF16)
    kc_ref[1] = pltpu.roll(kcs, NSA_DIM, axis=1)[:, :NSA_DIM].astype(BF16)
    vcs = nsa(1)
    vc_ref[0] = vcs[:, :NSA_DIM].astype(BF16)
    vc_ref[1] = pltpu.roll(vcs, NSA_DIM, axis=1)[:, :NSA_DIM].astype(BF16)
    t_seq = (pl.program_id(0) % tiles_per_seq) * tm + lax.broadcasted_iota(jnp.int32, (tm, LANES), 0)
    onehot = jnp.where(lane == lax.shift_right_logical(t_seq, int(math.log2(SEL_LEN))), 1.0, 0.0)
    k0, k1 = halves(_rope(nsa(2), cos_n, sup_n, sdn_n, NSA_ROT // 2))
    for g, kg in enumerate((k0, k1)):
        ksa_ref[0, g, :, 0:LANES] = onehot.astype(BF16)
        ksa_ref[0, g, :, LANES:2 * LANES] = kg.astype(BF16)
    vst = nsa(3).T
    vst_ref[0, 0] = with_sum_rows(vst[:NSA_DIM])
    vst_ref[0, 1] = with_sum_rows(vst[NSA_DIM:])
    kws = _rope(nsa(4), cos_n, sup_n, sdn_n, NSA_ROT // 2)
    kw_ref[0] = kws[:, :NSA_DIM].astype(BF16)
    kw_ref[1] = pltpu.roll(kws, NSA_DIM, axis=1)[:, :NSA_DIM].astype(BF16)
    vwt = nsa(5).T
    vwt_ref[0] = with_sum_rows(vwt[:NSA_DIM])
    vwt_ref[1] = with_sum_rows(vwt[NSA_DIM:])
    gl_ref[...] = p[:, 2048:2176]


def _rope_consts():
    rc = np.zeros((ROPE_ROWS, LANES), np.float32)
    hm = MLA_ROPE // 2
    rc[:hm] = (ROPE_THETA ** (-np.arange(hm, dtype=np.float32) / hm))[:, None]
    hn = NSA_ROT // 2
    rc[hm:hm + hn] = (ROPE_THETA ** (-np.arange(hn, dtype=np.float32) / hn))[:, None]
    return rc


def _project(x2, pos2, g_pre, w_att, q_g, w_uq, kv_g, w_ukv, rc, S, tm):
    T, D = x2.shape
    n_att = w_att.shape[1]
    tps = S // tm
    B = T // S
    rows = lambda n, d: (jax.ShapeDtypeStruct((n, T, d), BF16), pl.BlockSpec((n, tm, d), lambda i: (0, i, 0)))
    cols = lambda n, d: (jax.ShapeDtypeStruct((n, d, T), BF16), pl.BlockSpec((n, d, tm), lambda i: (0, 0, i)))
    brows = lambda n, d: (jax.ShapeDtypeStruct((B, n, S, d), BF16),
                          pl.BlockSpec((1, n, tm, d), lambda i: (i // tps, 0, i % tps, 0)))
    bcols = lambda n, d: (jax.ShapeDtypeStruct((B, n, d, S), BF16),
                          pl.BlockSpec((1, n, d, tm), lambda i: (i // tps, 0, 0, i % tps)))
    outs = [bcols(MLA_HEADS, LANES), brows(MLA_HEADS, LANES), bcols(MLA_HEADS, MLA_V + SUM_ROWS),
            bcols(NSA_HEADS, LANES), rows(NSA_GROUPS, NSA_DIM), rows(NSA_GROUPS, NSA_DIM),
            brows(NSA_GROUPS, 2 * LANES), bcols(NSA_GROUPS, NSA_DIM + SUM_ROWS), rows(NSA_GROUPS, NSA_DIM),
            cols(NSA_GROUPS, NSA_DIM + SUM_ROWS),
            (jax.ShapeDtypeStruct((T, LANES), F32), pl.BlockSpec((tm, LANES), lambda i: (i, 0)))]
    kern = functools.partial(_proj_kernel,
                             mla_scale=float((MLA_NOPE + MLA_ROPE) ** -0.5 * LOG2E),
                             nsa_scale=float(NSA_DIM ** -0.5 * LOG2E),
                             tiles_per_seq=S // tm)
    return pl.pallas_call(
        kern,
        grid=(T // tm,),
        in_specs=[
            pl.BlockSpec((tm, D), lambda i: (i, 0)),
            pl.BlockSpec((1, tm), lambda i: (0, i)),
            _const_spec((1, D)),
            _const_spec((D, n_att)),
            _const_spec((1, q_g.shape[1])),
            _const_spec(w_uq.shape),
            _const_spec((1, kv_g.shape[1])),
            _const_spec(w_ukv.shape),
            _const_spec((ROPE_ROWS, LANES)),
        ],
        out_specs=[o[1] for o in outs],
        out_shape=[o[0] for o in outs],
        compiler_params=_cparams(("parallel",)),
        name="proj",
    )(x2, pos2, g_pre, w_att, q_g, w_uq, kv_g, w_ukv, rc)


def _compress_kernel(xk_ref, xv_ref, pk_ref, pv_ref, wk_ref, wv_ref, ok_ref, ovt_ref):
    n = xk_ref.shape[1]
    vboth = None
    for g in range(NSA_GROUPS):
        xk = xk_ref[g].astype(F32)
        lo = _dot((xk + pk_ref[0:1]).astype(BF16), wk_ref[0])
        hi = _dot((xk + pk_ref[1:2]).astype(BF16), wk_ref[1])
        ok_ref[g] = (lo + pltpu.roll(hi, n - 1, axis=0)).astype(BF16)
        xv = xv_ref[g].astype(F32)
        lo = _dot((xv + pv_ref[0:1]).astype(BF16), wv_ref[g, 0])
        hi = _dot((xv + pv_ref[1:2]).astype(BF16), wv_ref[g, 1])
        vg = lo + pltpu.roll(hi, n - 1, axis=0)
        vboth = vg if vboth is None else vboth + vg
    ovt_ref[0] = vboth.T.astype(BF16)


def _compress(xk, xv, pk, pv, wk, wv, B):
    G, TC, CW = xk.shape
    nch = TC // B
    xspec = pl.BlockSpec((G, nch, CW), lambda b: (0, b, 0))
    return pl.pallas_call(
        _compress_kernel,
        grid=(B,),
        in_specs=[xspec, xspec, _const_spec(pk.shape), _const_spec(pv.shape),
                  _const_spec(wk.shape), _const_spec(wv.shape)],
        out_specs=[pl.BlockSpec((G, nch, NSA_DIM), lambda b: (0, b, 0)),
                   pl.BlockSpec((1, G * NSA_DIM, nch), lambda b: (b, 0, 0))],
        out_shape=[jax.ShapeDtypeStruct((G, TC, NSA_DIM), BF16),
                   jax.ShapeDtypeStruct((B, G * NSA_DIM, nch), BF16)],
        compiler_params=_cparams(("parallel",)),
        name="compress",
    )(xk, xv, pk, pv, wk, wv)


def _cmp_select_kernel(q_ref, kc_ref, vct_ref, ovt_ref, o_ref, bias_ref, *, tq, n_pick, nqt, n_var):
    i = pl.program_id(1)
    q0 = i * tq
    nch_all = kc_ref.shape[1]
    nblk_all = ovt_ref.shape[0]
    tiles_per_var = nqt // n_var

    def variant(nch, nblk):
        t_c = q0 + lax.broadcasted_iota(jnp.int32, (nch, tq), 1)
        n_c = lax.broadcasted_iota(jnp.int32, (nch, tq), 0)
        cmask = (n_c * CMP_STRIDE + (CMP_LEN - 1)) <= t_c
        t_b = q0 + lax.broadcasted_iota(jnp.int32, (nblk, tq), 1)
        blk = lax.broadcasted_iota(jnp.int32, (nblk, tq), 0)
        blk_f = blk.astype(F32)
        cur = lax.shift_right_logical(t_b, int(math.log2(SEL_LEN)))
        forced = (blk == 0) | (blk == cur) | (blk == cur - 1)
        causal = blk * SEL_LEN <= t_b
        outs = []
        for g in range(NSA_GROUPS):
            psum = None
            for j in range(NSA_HPG):
                qt = q_ref[0, g * NSA_HPG + j][:NSA_DIM, :]
                s = jnp.where(cmask, _dot(kc_ref[g, :nch, :], qt), MASKED)
                m = jnp.maximum(jnp.max(s, axis=0, keepdims=True), M_FLOOR)
                p = jnp.exp2(s - m)
                l = jnp.sum(p, axis=0, keepdims=True)
                pn = p * jnp.where(l > 0.0, 1.0 / l, 0.0)
                outs.append(_dot(vct_ref[0, g * NSA_DIM:(g + 1) * NSA_DIM, :nch], pn.astype(BF16)))
                psum = pn if psum is None else psum + pn
            imp = None
            for piece in _split_bf16(psum, 3):
                d = _dot(ovt_ref[:nblk, :nch], piece)
                imp = d if imp is None else imp + d
            v0 = jnp.where(forced | jnp.logical_not(causal), -1.0, imp)
            v = v0
            for _ in range(n_pick):
                mx = jnp.max(v, axis=0, keepdims=True)
                first = jnp.min(jnp.where(v == mx, blk_f, float(nblk)), axis=0, keepdims=True)
                v = jnp.where(blk_f == first, -2.0, v)
            sel = forced | ((v == -2.0) & (v0 >= 0.0))
            bias = jnp.where(sel, 0.0, SEL_BIAS)
            if nblk < nblk_all:
                bias = jnp.concatenate([bias, jnp.full((nblk_all - nblk, tq), SEL_BIAS, F32)], axis=0)
            bias_ref[0, g] = bias.astype(BF16)
        o_ref[...] = jnp.concatenate(outs, axis=0).T.astype(BF16)

    for k in range(n_var):
        @pl.when((i >= k * tiles_per_var) & (i < (k + 1) * tiles_per_var))
        def _():
            variant(nch_all * (k + 1) // n_var, nblk_all * (k + 1) // n_var)


def _cmp_select(nq, kc, vct, ovt, B, S, tq):
    T = B * S
    nch = kc.shape[1] // B
    nblk = ovt.shape[0]
    nqt = S // tq
    n_var = 4 if nqt % 4 == 0 and (nch // 4) % LANES == 0 and (nblk // 4) % SUBLANES == 0 else 1
    kern = functools.partial(_cmp_select_kernel, tq=tq, n_pick=N_SEL - 3, nqt=nqt, n_var=n_var)
    return pl.pallas_call(
        kern,
        grid=(B, nqt),
        in_specs=[
            pl.BlockSpec((1, NSA_HEADS, LANES, tq), lambda b, i: (b, 0, 0, i)),
            pl.BlockSpec((NSA_GROUPS, nch, NSA_DIM), lambda b, i: (0, b, 0)),
            pl.BlockSpec((1, NSA_GROUPS * NSA_DIM, nch), lambda b, i: (b, 0, 0)),
            _const_spec(ovt.shape),
        ],
        out_specs=[pl.BlockSpec((tq, NSA_HEADS * NSA_DIM), lambda b, i: (b * nqt + i, 0)),
                   pl.BlockSpec((1, NSA_GROUPS, nblk, tq), lambda b, i: (b, 0, 0, i))],
        out_shape=[jax.ShapeDtypeStruct((T, NSA_HEADS * NSA_DIM), BF16),
                   jax.ShapeDtypeStruct((B, NSA_GROUPS, nblk, S), BF16)],
        compiler_params=_cparams(("parallel", "parallel")),
        name="cmp_select",
    )(nq, kc, vct, ovt)


def _flash_heads(nheads, kqv, mask, src, dst, lazy=False, unroll=True):
    def scores(hd):
        k, qt, _ = kqv(hd)
        return _dot(k, qt)

    def update(hd, s):
        if mask is not None:
            s = jnp.where(mask, s, MASKED)
        tmax = jnp.max(s, axis=0, keepdims=True)
        vt = kqv(hd)[2]
        rise = None
        if src is None:
            m_new = jnp.maximum(tmax, M_FLOOR)
            acc = _dot(vt, jnp.exp2(s - m_new).astype(BF16))
        else:
            m_old = src[0][hd]
            m_new = jnp.maximum(m_old, tmax)
            alpha = jnp.exp2(m_old - m_new)
            if lazy:
                acc = (src[1][hd] + _dot(vt, jnp.exp2(s - m_old).astype(BF16))) * alpha
                rise = tmax - m_old
            else:
                acc = alpha * src[1][hd] + _dot(vt, jnp.exp2(s - m_new).astype(BF16))
        dst[0][hd] = m_new
        dst[1][hd] = acc
        return rise

    if not unroll:
        def body(hd, carry):
            update(hd, scores(hd))
            return carry
        lax.fori_loop(0, nheads, body, 0)
        return None
    ahead = 2
    pending = [scores(hd) for hd in range(min(ahead, nheads))]
    worst = None
    for hd in range(nheads):
        s = pending.pop(0)
        if hd + ahead < nheads:
            pending.append(scores(hd + ahead))
        rise = update(hd, s)
        if rise is not None:
            worst = rise if worst is None else jnp.maximum(worst, rise)
    return worst


def _flash_init(m_ref, acc_ref):
    m_ref[...] = jnp.full(m_ref.shape, M_FLOOR, F32)
    acc_ref[...] = jnp.zeros(acc_ref.shape, F32)


def _flash_finalize(o_ref, acc_ref):
    nv, rows, tq = acc_ref.shape
    dv = rows - SUM_ROWS
    groups = o_ref.shape[0] if len(o_ref.shape) == 3 else 1
    nh = nv // groups
    for gb in range(groups):
        sl = slice(gb * nh, (gb + 1) * nh)
        o = acc_ref[sl, :dv, :] * (1.0 / acc_ref[sl, dv:dv + 1, :])
        o = o.reshape(nh * dv, tq).T.astype(BF16)
        if len(o_ref.shape) == 3:
            o_ref[gb] = o
        else:
            o_ref[...] = o


def _flash_scratch(nh, dv, tq, slots=None):
    lead = (nh,) if slots is None else (slots, nh)
    return [pltpu.VMEM(lead + (1, tq), F32), pltpu.VMEM(lead + (dv + SUM_ROWS, tq), F32)]


def _causal_pairs(nqt):
    pairs = [(i, j) for i in range(nqt) for j in range(i + 1)]
    return (jnp.asarray([p[0] for p in pairs], jnp.int32), jnp.asarray([p[1] for p in pairs], jnp.int32))


def _tile_iota(tq):
    return (lax.broadcasted_iota(jnp.int32, (tq, tq), 0), lax.broadcasted_iota(jnp.int32, (tq, tq), 1))


def _causal_flash_step(i, j, nheads, kqv, tq, o_ref, m_ref, acc_ref, slot_ref):
    key, qry = _tile_iota(tq)

    @pl.when(j == 0)
    def _():
        mask = key <= qry + jnp.where(i > 0, tq, 0)
        _flash_heads(nheads, kqv, mask, None, (m_ref.at[0], acc_ref.at[0]))
        slot_ref[0] = 0

    def lazy_step(mask):
        cur = slot_ref[0]
        src = (m_ref.at[cur], acc_ref.at[cur])
        dst = (m_ref.at[1 - cur], acc_ref.at[1 - cur])
        worst = _flash_heads(nheads, kqv, mask, src, dst, lazy=True)

        @pl.when(jnp.max(worst) > LAZY_SPAN)
        def _():
            _flash_heads(nheads, kqv, mask, src, dst, unroll=False)

        slot_ref[0] = 1 - cur

    @pl.when((j > 0) & (j < i))
    def _():
        lazy_step(None)

    @pl.when((j > 0) & (j == i))
    def _():
        lazy_step(key <= qry)

    @pl.when(j == i)
    def _():
        _flash_finalize(o_ref, acc_ref.at[slot_ref[0]])


def _split_virtual(v, nheads):
    if isinstance(v, int):
        return v // nheads, v % nheads
    return lax.div(v, nheads), lax.rem(v, nheads)


def _mla_kernel(qi_ref, ki_ref, q_ref, k_ref, vt_ref, o_ref, m_ref, acc_ref, slot_ref, *, tq):
    i, j = qi_ref[pl.program_id(0)], ki_ref[pl.program_id(0)]

    def kqv(v):
        bb, hd = _split_virtual(v, MLA_HEADS)
        return k_ref[bb, hd], q_ref[bb, hd], vt_ref[bb, hd]

    _causal_flash_step(i, j, q_ref.shape[0] * MLA_HEADS, kqv, tq, o_ref, m_ref, acc_ref, slot_ref)


def _mla_attention(qm, km, vmt, B, S, tq):
    nqt = S // tq
    kern = functools.partial(_mla_kernel, tq=tq)
    qi, ki = _causal_pairs(nqt)
    return pl.pallas_call(
        kern,
        grid_spec=pltpu.PrefetchScalarGridSpec(
            num_scalar_prefetch=2,
            grid=(qi.shape[0],),
            in_specs=[pl.BlockSpec((B, MLA_HEADS, LANES, tq), lambda p, qi, ki: (0, 0, 0, qi[p])),
                      pl.BlockSpec((B, MLA_HEADS, tq, LANES), lambda p, qi, ki: (0, 0, ki[p], 0)),
                      pl.BlockSpec((B, MLA_HEADS, MLA_V + SUM_ROWS, tq), lambda p, qi, ki: (0, 0, 0, ki[p]))],
            out_specs=pl.BlockSpec((B, tq, MLA_HEADS * MLA_V), lambda p, qi, ki: (0, qi[p], 0)),
            scratch_shapes=_flash_scratch(B * MLA_HEADS, MLA_V, tq, slots=2) + [pltpu.SMEM((1,), jnp.int32)]),
        out_shape=jax.ShapeDtypeStruct((B, S, MLA_HEADS * MLA_V), BF16),
        compiler_params=_cparams(("arbitrary",)),
        name="mla_attn",
    )(qi, ki, qm, km, vmt).reshape(B * S, MLA_HEADS * MLA_V)


def _slc_kernel(qi_ref, ki_ref, q_ref, bias_ref, k_ref, vt_ref, o_ref, qa_ref, m_ref, acc_ref,
                slot_ref, *, tq):
    i, j = qi_ref[pl.program_id(0)], ki_ref[pl.program_id(0)]
    nb = q_ref.shape[0]
    nblk = bias_ref.shape[2]

    @pl.when(j == 0)
    def _():
        if nblk < LANES:
            qa_ref[...] = jnp.zeros(qa_ref.shape, BF16)
        for bb in range(nb):
            for hd in range(NSA_HEADS):
                qa_ref[bb * NSA_HEADS + hd, 0:nblk, :] = bias_ref[bb, hd // NSA_HPG]
                qa_ref[bb * NSA_HEADS + hd, LANES:2 * LANES, :] = q_ref[bb, hd]

    def kqv(v):
        bb, hd = _split_virtual(v, NSA_HEADS)
        g = hd // NSA_HPG if isinstance(hd, int) else lax.div(hd, NSA_HPG)
        return k_ref[bb, g], qa_ref[v], vt_ref[bb, g]

    _causal_flash_step(i, j, nb * NSA_HEADS, kqv, tq, o_ref, m_ref, acc_ref, slot_ref)


def _slc_attention(nq, bias, ksa, vst, B, S, tq):
    nqt = S // tq
    nblk = bias.shape[2]
    kern = functools.partial(_slc_kernel, tq=tq)
    qi, ki = _causal_pairs(nqt)
    qmap = lambda p, qi, ki: (0, 0, 0, qi[p])
    return pl.pallas_call(
        kern,
        grid_spec=pltpu.PrefetchScalarGridSpec(
            num_scalar_prefetch=2,
            grid=(qi.shape[0],),
            in_specs=[pl.BlockSpec((B, NSA_HEADS, LANES, tq), qmap),
                      pl.BlockSpec((B, NSA_GROUPS, nblk, tq), qmap),
                      pl.BlockSpec((B, NSA_GROUPS, tq, 2 * LANES), lambda p, qi, ki: (0, 0, ki[p], 0)),
                      pl.BlockSpec((B, NSA_GROUPS, NSA_DIM + SUM_ROWS, tq), lambda p, qi, ki: (0, 0, 0, ki[p]))],
            out_specs=pl.BlockSpec((B, tq, NSA_HEADS * NSA_DIM), lambda p, qi, ki: (0, qi[p], 0)),
            scratch_shapes=[pltpu.VMEM((B * NSA_HEADS, 2 * LANES, tq), BF16)]
                           + _flash_scratch(B * NSA_HEADS, NSA_DIM, tq, slots=2)
                           + [pltpu.SMEM((1,), jnp.int32)]),
        out_shape=jax.ShapeDtypeStruct((B, S, NSA_HEADS * NSA_DIM), BF16),
        compiler_params=_cparams(("arbitrary",)),
        name="slc_attn",
    )(qi, ki, nq, bias, ksa, vst).reshape(B * S, NSA_HEADS * NSA_DIM)


def _win_kernel(q_ref, k_ref, vt_ref, o_ref, m_ref, acc_ref, *, tq):
    i, j = pl.program_id(1), pl.program_id(2)

    @pl.when(j == 0)
    def _():
        _flash_init(m_ref, acc_ref)

    def step(prev):
        key, qry = _tile_iota(tq)
        mask = (key > qry) if prev else (key <= qry)
        _flash_heads(NSA_HEADS, lambda hd: (k_ref[hd // NSA_HPG], q_ref[0, hd][:NSA_DIM, :],
                                            vt_ref[hd // NSA_HPG]), mask, (m_ref, acc_ref), (m_ref, acc_ref))

    @pl.when((j == 0) & (i > 0))
    def _():
        step(True)

    @pl.when(j == 1)
    def _():
        step(False)
        _flash_finalize(o_ref, acc_ref)


def _win_attention(nq, kw, vwt, B, S, tq):
    T = B * S
    nqt = S // tq
    kern = functools.partial(_win_kernel, tq=tq)
    kidx = lambda b, i, j: b * nqt + jnp.maximum(i - 1 + j, 0)
    return pl.pallas_call(
        kern,
        grid=(B, nqt, 2),
        in_specs=[pl.BlockSpec((1, NSA_HEADS, LANES, tq), lambda b, i, j: (b, 0, 0, i)),
                  pl.BlockSpec((NSA_GROUPS, tq, NSA_DIM), lambda b, i, j: (0, kidx(b, i, j), 0)),
                  pl.BlockSpec((NSA_GROUPS, NSA_DIM + SUM_ROWS, tq), lambda b, i, j: (0, 0, kidx(b, i, j)))],
        out_specs=pl.BlockSpec((tq, NSA_HEADS * NSA_DIM), lambda b, i, j: (b * nqt + i, 0)),
        out_shape=jax.ShapeDtypeStruct((T, NSA_HEADS * NSA_DIM), BF16),
        scratch_shapes=_flash_scratch(NSA_HEADS, NSA_DIM, tq),
        compiler_params=_cparams(("parallel", "parallel", "arbitrary")),
        name="win_attn",
    )(nq, kw, vwt)


def _causal_conv(u, carry, w_ref):
    row = lax.broadcasted_iota(jnp.int32, u.shape, 0)
    c1 = carry[SUBLANES - 1:SUBLANES]
    c2 = carry[SUBLANES - 2:SUBLANES - 1]
    u1 = jnp.where(row == 0, c1, pltpu.roll(u, 1, axis=0))
    u2 = jnp.where(row == 0, c2, jnp.where(row == 1, c1, pltpu.roll(u, 2, axis=0)))
    return w_ref[0:1] * u2 + w_ref[1:2] * u1 + w_ref[2:3] * u


def _merge_kernel(x_ref, g_ref, wgc_ref, cw_ref, ym_ref, oc_ref, os_ref, ow_ref, gl_ref, ex_ref,
                  wbc_ref, wbm_ref, wbn_ref, wo_ref, gp_ref, out_ref, carry_ref, *, tiles_per_seq):
    D = x_ref.shape[1]
    DC = wbc_ref.shape[0]

    @pl.when(pl.program_id(0) % tiles_per_seq == 0)
    def _():
        carry_ref[...] = jnp.zeros(carry_ref.shape, F32)

    x = x_ref[...]
    h = _rms(x, g_ref[...]).astype(BF16)
    p = _dot(h, wgc_ref[...])
    u = p[:, 3 * D + DC:3 * D + 2 * DC] * p[:, 3 * D + 2 * DC:3 * D + 3 * DC]
    y_conv = p[:, 3 * D:3 * D + DC] * _causal_conv(u, carry_ref[...], cw_ref)
    carry_ref[...] = u[u.shape[0] - SUBLANES:]
    hi, lo = _split_bf16(gl_ref[...], 2)
    gate = jax.nn.sigmoid(_dot(hi, ex_ref[...]) + _dot(lo, ex_ref[...]))
    HD = oc_ref.shape[1]
    y_nsa = (gate[:, 0:HD] * oc_ref[...].astype(F32) + gate[:, HD:2 * HD] * os_ref[...].astype(F32)
             + gate[:, 2 * HD:3 * HD] * ow_ref[...].astype(F32))
    merged = (jax.nn.sigmoid(p[:, 0:D]) * _dot(y_conv.astype(BF16), wbc_ref[...])
              + jax.nn.sigmoid(p[:, D:2 * D]) * _dot(ym_ref[...], wbm_ref[...])
              + jax.nn.sigmoid(p[:, 2 * D:3 * D]) * _dot(y_nsa.astype(BF16), wbn_ref[...]))
    out_ref[...] = x + _rms(_dot(merged.astype(BF16), wo_ref[...]), gp_ref[...])


def _merge(x2, g_pre, w_gc, conv_w, y_mla, o_cmp, o_slc, o_win, gl, ex, wbc, wbm, wbn, wo, g_post,
           S, tm):
    T, D = x2.shape
    HD = y_mla.shape[1]
    row = lambda w: pl.BlockSpec((tm, w), lambda i: (i, 0))
    kern = functools.partial(_merge_kernel, tiles_per_seq=S // tm)
    return pl.pallas_call(
        kern,
        grid=(T // tm,),
        in_specs=[row(D), _const_spec((1, D)), _const_spec(w_gc.shape), _const_spec(conv_w.shape),
                  row(HD), row(HD), row(HD), row(HD), row(LANES), _const_spec(ex.shape),
                  _const_spec(wbc.shape), _const_spec(wbm.shape), _const_spec(wbn.shape),
                  _const_spec(wo.shape), _const_spec((1, D))],
        out_specs=row(D),
        out_shape=jax.ShapeDtypeStruct((T, D), F32),
        scratch_shapes=[pltpu.VMEM((SUBLANES, wbc.shape[0]), F32)],
        compiler_params=_cparams(("arbitrary",)),
        name="merge",
    )(x2, g_pre, w_gc, conv_w, y_mla, o_cmp, o_slc, o_win, gl, ex, wbc, wbm, wbn, wo, g_post)


def _ffn_kernel(x_ref, g_ref, wup_ref, cw_ref, cb_ref, wdn_ref, gp_ref, out_ref, carry_ref,
                *, tiles_per_seq, chunk):
    F = wdn_ref.shape[0]

    @pl.when(pl.program_id(0) % tiles_per_seq == 0)
    def _():
        carry_ref[...] = jnp.zeros(carry_ref.shape, F32)

    x = x_ref[...]
    h = _rms(x, g_ref[...]).astype(BF16)
    acc = None
    for c0 in range(0, F, chunk):
        a = _dot(h, wup_ref[:, c0:c0 + chunk])
        b = _dot(h, wup_ref[:, F + c0:F + c0 + chunk])
        ac = _causal_conv(a, carry_ref[:, c0:c0 + chunk], cw_ref.at[:, c0:c0 + chunk])
        ac = ac + cb_ref[:, c0:c0 + chunk]
        carry_ref[:, c0:c0 + chunk] = a[a.shape[0] - SUBLANES:]
        d = _dot((jax.nn.gelu(ac) * b).astype(BF16), wdn_ref[c0:c0 + chunk, :])
        acc = d if acc is None else acc + d
    out_ref[...] = x + _rms(acc, gp_ref[...])


def _ffn(x2, g_pre, w_up, conv_w, conv_b, w_dn, g_post, S, tm, chunk):
    T, D = x2.shape
    F = w_dn.shape[0]
    row = pl.BlockSpec((tm, D), lambda i: (i, 0))
    kern = functools.partial(_ffn_kernel, tiles_per_seq=S // tm, chunk=chunk)
    return pl.pallas_call(
        kern,
        grid=(T // tm,),
        in_specs=[row, _const_spec((1, D)), _const_spec(w_up.shape), _const_spec(conv_w.shape),
                  _const_spec((1, F)), _const_spec(w_dn.shape), _const_spec((1, D))],
        out_specs=row,
        out_shape=jax.ShapeDtypeStruct((T, D), F32),
        scratch_shapes=[pltpu.VMEM((SUBLANES, F), F32)],
        compiler_params=_cparams(("arbitrary",)),
        name="ffn",
    )(x2, g_pre, w_up, conv_w, conv_b, w_dn, g_post)


def _overlap_matrix_t(S):
    nch = S // CMP_STRIDE
    n_cmp = (S - CMP_LEN) // CMP_STRIDE + 1
    cs = np.arange(nch)[None, :] * CMP_STRIDE
    ss = np.arange(S // SEL_LEN)[:, None] * SEL_LEN
    ov = (cs <= ss + SEL_LEN - 1) & (cs + CMP_LEN - 1 >= ss) & (np.arange(nch)[None, :] < n_cmp)
    return ov.astype(np.float32)


def _gate_expand():
    ex = np.zeros((LANES, 3 * NSA_HEADS * NSA_DIM), np.float32)
    for hd in range(NSA_HEADS):
        for c in range(3):
            ex[hd * 3 + c, c * NSA_HEADS * NSA_DIM + hd * NSA_DIM:
               c * NSA_HEADS * NSA_DIM + (hd + 1) * NSA_DIM] = 1.0
    return ex


def _pad_rows(w, rows):
    return jnp.concatenate([w, jnp.zeros((rows - w.shape[0],) + w.shape[1:], w.dtype)], axis=0)


def kernel(x, positions, norm_mix_pre, norm_mix_post, w_in, conv_w, mla_q_norm, mla_w_uq, mla_kv_norm, mla_w_ukv, nsa_cmp_pos_k, nsa_cmp_pos_v, nsa_cmp_w_k, nsa_cmp_w_v, w_branch_conv, w_branch_mla, w_branch_nsa, w_out, norm_ffn_pre, norm_ffn_post, ffn_w_up, ffn_conv_w, ffn_conv_b, ffn_w_down):
    B, S, D = x.shape
    T = B * S
    depth = w_in.shape[0]
    DC = conv_w.shape[2]
    F = ffn_w_down.shape[1]
    QL = mla_q_norm.shape[1]
    KL = mla_kv_norm.shape[1]
    tq = 512
    assert S % tq == 0 and WINDOW == tq and N_SEL <= S // SEL_LEN <= LANES
    tm_proj, tm_merge, tm_ffn, ffn_chunk = 512, 512, 512, 1408
    assert F % ffn_chunk == 0 and ffn_chunk % LANES == 0

    x2 = x.reshape(T, D)
    pos2 = positions.reshape(1, T)
    rc = jnp.asarray(_rope_consts())
    ovt = jnp.asarray(_overlap_matrix_t(S), BF16)
    ex = jnp.asarray(_gate_expand(), BF16)
    half = CMP_LEN // 2 * NSA_DIM
    o_att = 3 * D + 3 * DC

    for l in range(depth):
        w = w_in[l]
        o_kr = o_att + QL + KL
        z = lambda n: jnp.zeros((D, n), F32)
        w_att = jnp.concatenate(
            [w[:, o_att:o_kr], z(MLA_NOPE), w[:, o_kr:o_kr + MLA_ROPE], z(LANES - MLA_NOPE - MLA_ROPE),
             w[:, o_kr + MLA_ROPE:o_kr + MLA_ROPE + NSA_HEADS * NSA_DIM + 6 * NSA_GROUPS * NSA_DIM],
             w[:, w.shape[1] - 3 * NSA_HEADS:], z(LANES - 3 * NSA_HEADS)], axis=1).astype(BF16)
        dq = MLA_NOPE + MLA_ROPE
        w_uq = jnp.pad(mla_w_uq[l].reshape(QL, MLA_HEADS, dq),
                       ((0, 0), (0, 0), (0, LANES - dq))).reshape(QL, MLA_HEADS * LANES).astype(BF16)
        (qm, km, vmt, nq, kc_in, vc_in, ksa, vst, kw, vwt, gl) = _project(
            x2, pos2, norm_mix_pre[l][None], w_att, mla_q_norm[l][None], w_uq,
            mla_kv_norm[l][None], mla_w_ukv[l].astype(BF16), rc, S, tm_proj)

        chunks = lambda a: a.reshape(NSA_GROUPS, T // CMP_STRIDE, CMP_STRIDE * NSA_DIM)
        wv = nsa_cmp_w_v[l].reshape(2, half, NSA_DIM)
        zv = jnp.zeros_like(wv)
        wv_pad = jnp.stack([jnp.concatenate([wv, zv], axis=2), jnp.concatenate([zv, wv], axis=2)])
        kc, vct = _compress(chunks(kc_in), chunks(vc_in),
                            nsa_cmp_pos_k[l].reshape(2, half), nsa_cmp_pos_v[l].reshape(2, half),
                            nsa_cmp_w_k[l].reshape(2, half, NSA_DIM).astype(BF16),
                            wv_pad.astype(BF16), B)

        y_mla = _mla_attention(qm, km, vmt, B, S, tq)
        o_cmp, bias = _cmp_select(nq, kc, vct, ovt, B, S, tq)
        o_slc = _slc_attention(nq, bias, ksa, vst, B, S, tq)
        o_win = _win_attention(nq, kw, vwt, B, S, tq)

        x2 = _merge(x2, norm_mix_pre[l][None], w[:, :o_att].astype(BF16),
                    _pad_rows(conv_w[l], SUBLANES), y_mla, o_cmp, o_slc, o_win, gl, ex,
                    w_branch_conv[l].astype(BF16), w_branch_mla[l].astype(BF16),
                    w_branch_nsa[l].astype(BF16), w_out[l].astype(BF16), norm_mix_post[l][None],
                    S, tm_merge)
        x2 = _ffn(x2, norm_ffn_pre[l][None], ffn_w_up[l].astype(BF16),
                  _pad_rows(ffn_conv_w[l], SUBLANES), ffn_conv_b[l][None],
                  ffn_w_down[l].astype(BF16), norm_ffn_post[l][None], S, tm_ffn, ffn_chunk)
    return x2.reshape(B, S, D)
```

```python
import functools
import math

import numpy as np
import jax
import jax.numpy as jnp
from jax import lax
from jax.experimental import pallas as pl
from jax.experimental.pallas import tpu as pltpu

F32 = jnp.float32
BF16 = jnp.bfloat16

ROPE_THETA = 500000.0
RMS_EPS = 1e-6
CONV_WIDTH = 3
MLA_HEADS = 8
MLA_NOPE = 64
MLA_ROPE = 32
MLA_V = 64
NSA_HEADS = 8
NSA_GROUPS = 2
NSA_HPG = NSA_HEADS // NSA_GROUPS
NSA_DIM = 64
NSA_ROT = NSA_DIM // 4
CMP_LEN = 32
CMP_STRIDE = 16
SEL_LEN = 64
N_SEL = 16
WINDOW = 512

LANES = 128
SUBLANES = 8
VMEM_LIMIT = 56 * 1024 * 1024
MASKED = -1e30
M_FLOOR = -1e20
SEL_BIAS = -2.0 ** 100
LAZY_SPAN = 64.0
LOG2E = math.log2(math.e)
SEL_SLOTS = 16
ROPE_ROWS = 32
SUM_ROWS = 16


def _cparams(sem):
    return pltpu.CompilerParams(dimension_semantics=sem, vmem_limit_bytes=VMEM_LIMIT)


def _const_spec(shape):
    nd = len(shape)
    return pl.BlockSpec(shape, lambda *_: (0,) * nd, pipeline_mode=pl.Buffered(1))


def _rms(x, g):
    return x * lax.rsqrt(jnp.mean(x * x, axis=-1, keepdims=True) + RMS_EPS) * g


def _dot(a, b):
    return jnp.dot(a, b, preferred_element_type=F32)


def _split_bf16(a, terms):
    pieces, rem = [], a
    for _ in range(terms):
        piece = rem.astype(BF16)
        rem = rem - piece.astype(F32)
        pieces.append(piece)
    return pieces


def _rope(x, cos, sin_up, sin_dn, half):
    return (x * cos + pltpu.roll(x, half, axis=1) * sin_up
            + pltpu.roll(x, LANES - half, axis=1) * sin_dn)


def _lanes(table, lane, base, moves):
    out = base
    for dst, width, src in moves:
        rolled = pltpu.roll(table, (dst - src) % LANES, axis=1)
        out = jnp.where((lane >= dst) & (lane < dst + width), rolled, out)
    return out


def _proj_kernel(x_ref, pos_ref, g_ref, watt_ref, qg_ref, wuqt_ref, kvg_ref, wukv_ref, wvt_ref, rc_ref,
                 qm_ref, km_ref, vmt_ref, nq_ref, kc_ref, vc_ref, ksa_ref, vst_ref, kw_ref, vwt_ref,
                 gl_ref, *, mla_scale, nsa_scale, tiles_per_seq):
    tm = x_ref.shape[0]
    hm, hn = MLA_ROPE // 2, NSA_ROT // 2
    x = x_ref[...]
    h = _rms(x, g_ref[...]).astype(BF16)
    p = _dot(h, watt_ref[...])
    lane = lax.broadcasted_iota(jnp.int32, (tm, LANES), 1)
    low = lane < NSA_DIM
    ang_t = rc_ref[:, 0:1] * pos_ref[...].astype(F32)
    cos_t, sin_t = jnp.cos(ang_t), jnp.sin(ang_t)
    cs = jnp.concatenate([cos_t, sin_t, jnp.zeros((LANES - 2 * ROPE_ROWS, tm), F32)], axis=0).T
    cos_m = _lanes(cs, lane, 1.0, [(MLA_NOPE, hm, 0), (MLA_NOPE + hm, hm, 0)])
    sup_m = _lanes(cs, lane, 0.0, [(MLA_NOPE + hm, hm, ROPE_ROWS)])
    sdn_m = -_lanes(cs, lane, 0.0, [(MLA_NOPE, hm, ROPE_ROWS)])
    cos_n = _lanes(cs, lane, 1.0, [(0, hn, hm), (hn, hn, hm), (NSA_DIM, hn, hm), (NSA_DIM + hn, hn, hm)])
    sup_n = _lanes(cs, lane, 0.0, [(hn, hn, ROPE_ROWS + hm), (NSA_DIM + hn, hn, ROPE_ROWS + hm)])
    sdn_n = -_lanes(cs, lane, 0.0, [(0, hn, ROPE_ROWS + hm), (NSA_DIM, hn, ROPE_ROWS + hm)])
    ones_rows = jnp.where(lax.broadcasted_iota(jnp.int32, (SUM_ROWS, tm), 0) == 0, 1.0, 0.0)

    def with_sum_rows(vt):
        return jnp.concatenate([vt, ones_rows], axis=0).astype(BF16)

    def halves(slab):
        return (jnp.where(low, slab, 0.0), jnp.where(low, pltpu.roll(slab, NSA_DIM, axis=1), 0.0))

    def rope_rows(x1, x2, c, s_):
        return x1 * c - x2 * s_, x2 * c + x1 * s_

    cqn_t = _rms(p[:, 0:384], qg_ref[...]).T.astype(BF16)
    qt = _dot(wuqt_ref[...], cqn_t)
    for hd in range(MLA_HEADS):
        r0 = hd * LANES
        y1, y2 = rope_rows(qt[r0 + MLA_NOPE:r0 + MLA_NOPE + hm], qt[r0 + MLA_NOPE + hm:r0 + MLA_NOPE + 2 * hm],
                           cos_t[0:hm], sin_t[0:hm])
        slab_t = jnp.concatenate([qt[r0:r0 + MLA_NOPE], y1, y2, qt[r0 + MLA_NOPE + 2 * hm:r0 + LANES]], axis=0)
        qm_ref[0, hd] = (slab_t * mla_scale).astype(BF16)
    ckvn = _rms(p[:, 384:640], kvg_ref[...])
    kv = _dot(ckvn.astype(BF16), wukv_ref[...])
    kr = _rope(p[:, 640:768], cos_m, sup_m, sdn_m, hm)
    vt = _dot(wvt_ref[...], ckvn.T.astype(BF16))
    for hd in range(MLA_HEADS):
        km_ref[0, hd] = jnp.where(lane < MLA_NOPE, kv[:, hd * LANES:(hd + 1) * LANES], kr).astype(BF16)
        vmt_ref[0, hd] = with_sum_rows(vt[hd * MLA_V:(hd + 1) * MLA_V])
    zpad = jnp.zeros((LANES - NSA_DIM, tm), BF16)
    for pr in range(NSA_HEADS // 2):
        slab_t = p[:, 768 + pr * LANES:768 + (pr + 1) * LANES].T
        for half in range(2):
            r0 = half * NSA_DIM
            y1, y2 = rope_rows(slab_t[r0:r0 + hn], slab_t[r0 + hn:r0 + 2 * hn],
                               cos_t[hm:hm + hn], sin_t[hm:hm + hn])
            head_t = jnp.concatenate([y1, y2, slab_t[r0 + 2 * hn:r0 + NSA_DIM]], axis=0) * nsa_scale
            nq_ref[0, 2 * pr + half, 0:NSA_DIM, :] = head_t.astype(BF16)
            nq_ref[0, 2 * pr + half, NSA_DIM:LANES, :] = zpad
    nsa = lambda idx: p[:, 1280 + idx * LANES:1280 + (idx + 1) * LANES]
    kcs = _rope(nsa(0), cos_n, sup_n, sdn_n, hn)
    kc_ref[0] = kcs[:, :NSA_DIM].astype(BF16)
    kc_ref[1] = pltpu.roll(kcs, NSA_DIM, axis=1)[:, :NSA_DIM].astype(BF16)
    vcs = nsa(1)
    vc_ref[0] = vcs[:, :NSA_DIM].astype(BF16)
    vc_ref[1] = pltpu.roll(vcs, NSA_DIM, axis=1)[:, :NSA_DIM].astype(BF16)
    t_seq = (pl.program_id(0) % tiles_per_seq) * tm + lax.broadcasted_iota(jnp.int32, (tm, LANES), 0)
    slot = NSA_DIM + (lax.shift_right_logical(t_seq, int(math.log2(SEL_LEN))) & (SEL_SLOTS // 2 - 1))
    onehot = jnp.where(lane == slot, 1.0, 0.0)
    for g, kg in enumerate(halves(_rope(nsa(2), cos_n, sup_n, sdn_n, hn))):
        ksa_ref[0, g] = (kg + onehot).astype(BF16)
    vst = nsa(3).T
    vst_ref[0, 0] = with_sum_rows(vst[:NSA_DIM])
    vst_ref[0, 1] = with_sum_rows(vst[NSA_DIM:])
    kws = _rope(nsa(4), cos_n, sup_n, sdn_n, hn)
    kw_ref[0] = kws[:, :NSA_DIM].astype(BF16)
    kw_ref[1] = pltpu.roll(kws, NSA_DIM, axis=1)[:, :NSA_DIM].astype(BF16)
    vwt = nsa(5).T
    vwt_ref[0] = with_sum_rows(vwt[:NSA_DIM])
    vwt_ref[1] = with_sum_rows(vwt[NSA_DIM:])
    gl_ref[...] = p[:, 2048:2176]


def _rope_consts():
    rc = np.zeros((ROPE_ROWS, LANES), np.float32)
    hm = MLA_ROPE // 2
    rc[:hm] = (ROPE_THETA ** (-np.arange(hm, dtype=np.float32) / hm))[:, None]
    hn = NSA_ROT // 2
    rc[hm:hm + hn] = (ROPE_THETA ** (-np.arange(hn, dtype=np.float32) / hn))[:, None]
    return rc


def _project(x2, pos2, g_pre, w_att, q_g, w_uq_t, kv_g, w_ukv, w_v_t, rc, S, tm):
    T, D = x2.shape
    n_att = w_att.shape[1]
    tps = S // tm
    B = T // S
    rows = lambda n, d: (jax.ShapeDtypeStruct((n, T, d), BF16), pl.BlockSpec((n, tm, d), lambda i: (0, i, 0)))
    cols = lambda n, d: (jax.ShapeDtypeStruct((n, d, T), BF16), pl.BlockSpec((n, d, tm), lambda i: (0, 0, i)))
    brows = lambda n, d: (jax.ShapeDtypeStruct((B, n, S, d), BF16),
                          pl.BlockSpec((1, n, tm, d), lambda i: (i // tps, 0, i % tps, 0)))
    bcols = lambda n, d: (jax.ShapeDtypeStruct((B, n, d, S), BF16),
                          pl.BlockSpec((1, n, d, tm), lambda i: (i // tps, 0, 0, i % tps)))
    outs = [bcols(MLA_HEADS, LANES), brows(MLA_HEADS, LANES), bcols(MLA_HEADS, MLA_V + SUM_ROWS),
            bcols(NSA_HEADS, LANES), rows(NSA_GROUPS, NSA_DIM), rows(NSA_GROUPS, NSA_DIM),
            brows(NSA_GROUPS, LANES), bcols(NSA_GROUPS, NSA_DIM + SUM_ROWS), rows(NSA_GROUPS, NSA_DIM),
            cols(NSA_GROUPS, NSA_DIM + SUM_ROWS),
            (jax.ShapeDtypeStruct((T, LANES), F32), pl.BlockSpec((tm, LANES), lambda i: (i, 0)))]
    kern = functools.partial(_proj_kernel,
                             mla_scale=float((MLA_NOPE + MLA_ROPE) ** -0.5 * LOG2E),
                             nsa_scale=float(NSA_DIM ** -0.5 * LOG2E),
                             tiles_per_seq=S // tm)
    return pl.pallas_call(
        kern,
        grid=(T // tm,),
        in_specs=[
            pl.BlockSpec((tm, D), lambda i: (i, 0)),
            pl.BlockSpec((1, tm), lambda i: (0, i)),
            _const_spec((1, D)),
            _const_spec((D, n_att)),
            _const_spec((1, q_g.shape[1])),
            _const_spec(w_uq_t.shape),
            _const_spec((1, kv_g.shape[1])),
            _const_spec(w_ukv.shape),
            _const_spec(w_v_t.shape),
            _const_spec((ROPE_ROWS, LANES)),
        ],
        out_specs=[o[1] for o in outs],
        out_shape=[o[0] for o in outs],
        compiler_params=_cparams(("parallel",)),
        name="proj",
    )(x2, pos2, g_pre, w_att, q_g, w_uq_t, kv_g, w_ukv, w_v_t, rc)


def _compress_kernel(xk_ref, xv_ref, pk_ref, pv_ref, wk_ref, wv_ref, ok_ref, ovt_ref):
    n = xk_ref.shape[1]
    vboth = None
    for g in range(NSA_GROUPS):
        xk = xk_ref[g].astype(F32)
        lo = _dot((xk + pk_ref[0:1]).astype(BF16), wk_ref[0])
        hi = _dot((xk + pk_ref[1:2]).astype(BF16), wk_ref[1])
        ok_ref[g] = (lo + pltpu.roll(hi, n - 1, axis=0)).astype(BF16)
        xv = xv_ref[g].astype(F32)
        lo = _dot((xv + pv_ref[0:1]).astype(BF16), wv_ref[g, 0])
        hi = _dot((xv + pv_ref[1:2]).astype(BF16), wv_ref[g, 1])
        vg = lo + pltpu.roll(hi, n - 1, axis=0)
        vboth = vg if vboth is None else vboth + vg
    ovt_ref[0] = vboth.T.astype(BF16)


def _compress(xk, xv, pk, pv, wk, wv, B):
    G, TC, CW = xk.shape
    nch = TC // B
    xspec = pl.BlockSpec((G, nch, CW), lambda b: (0, b, 0))
    return pl.pallas_call(
        _compress_kernel,
        grid=(B,),
        in_specs=[xspec, xspec, _const_spec(pk.shape), _const_spec(pv.shape),
                  _const_spec(wk.shape), _const_spec(wv.shape)],
        out_specs=[pl.BlockSpec((G, nch, NSA_DIM), lambda b: (0, b, 0)),
                   pl.BlockSpec((1, G * NSA_DIM, nch), lambda b: (b, 0, 0))],
        out_shape=[jax.ShapeDtypeStruct((G, TC, NSA_DIM), BF16),
                   jax.ShapeDtypeStruct((B, G * NSA_DIM, nch), BF16)],
        compiler_params=_cparams(("parallel",)),
        name="compress",
    )(xk, xv, pk, pv, wk, wv)


def _cmp_select_kernel(q_ref, kc_ref, vct_ref, ovt_ref, o_ref, bias_ref, *, tq, n_pick, nqt, n_var):
    i = pl.program_id(1)
    q0 = i * tq
    nch_all = kc_ref.shape[1]
    nblk_all = ovt_ref.shape[0]
    tiles_per_var = nqt // n_var

    def variant(nch, nblk):
        t_c = q0 + lax.broadcasted_iota(jnp.int32, (nch, tq), 1)
        n_c = lax.broadcasted_iota(jnp.int32, (nch, tq), 0)
        cmask = (n_c * CMP_STRIDE + (CMP_LEN - 1)) <= t_c
        t_b = q0 + lax.broadcasted_iota(jnp.int32, (nblk, tq), 1)
        blk = lax.broadcasted_iota(jnp.int32, (nblk, tq), 0)
        blk_f = blk.astype(F32)
        cur = lax.shift_right_logical(t_b, int(math.log2(SEL_LEN)))
        forced = (blk == 0) | (blk == cur) | (blk == cur - 1)
        causal = blk * SEL_LEN <= t_b
        outs = []
        for g in range(NSA_GROUPS):
            psum = None
            for j in range(NSA_HPG):
                qt = q_ref[0, g * NSA_HPG + j][:NSA_DIM, :]
                s = jnp.where(cmask, _dot(kc_ref[g, :nch, :], qt), MASKED)
                m = jnp.maximum(jnp.max(s, axis=0, keepdims=True), M_FLOOR)
                p = jnp.exp2(s - m)
                l = jnp.sum(p, axis=0, keepdims=True)
                pn = p * jnp.where(l > 0.0, 1.0 / l, 0.0)
                outs.append(_dot(vct_ref[0, g * NSA_DIM:(g + 1) * NSA_DIM, :nch], pn.astype(BF16)))
                psum = pn if psum is None else psum + pn
            imp = None
            for piece in _split_bf16(psum, 3):
                d = _dot(ovt_ref[:nblk, :nch], piece)
                imp = d if imp is None else imp + d
            v0 = jnp.where(forced | jnp.logical_not(causal), -1.0, imp)
            v = v0
            for _ in range(n_pick):
                mx = jnp.max(v, axis=0, keepdims=True)
                first = jnp.min(jnp.where(v == mx, blk_f, float(nblk)), axis=0, keepdims=True)
                v = jnp.where(blk_f == first, -2.0, v)
            sel = forced | ((v == -2.0) & (v0 >= 0.0))
            bias = jnp.where(sel, 0.0, SEL_BIAS)
            per_tile = SEL_SLOTS // 2
            pad = jnp.zeros((SEL_SLOTS - per_tile, tq), F32)
            pieces = []
            for kt in range(nblk_all // per_tile):
                if kt * per_tile < nblk:
                    pieces += [bias[kt * per_tile:(kt + 1) * per_tile], pad]
                else:
                    pieces += [jnp.full((per_tile, tq), SEL_BIAS, F32), pad]
            bias_ref[0, g] = jnp.concatenate(pieces, axis=0).astype(BF16)
        o_ref[...] = jnp.concatenate(outs, axis=0).T.astype(BF16)

    for k in range(n_var):
        @pl.when((i >= k * tiles_per_var) & (i < (k + 1) * tiles_per_var))
        def _():
            variant(nch_all * (k + 1) // n_var, nblk_all * (k + 1) // n_var)


def _cmp_select(nq, kc, vct, ovt, B, S, tq):
    T = B * S
    nch = kc.shape[1] // B
    nblk = ovt.shape[0]
    nqt = S // tq
    n_var = 4 if nqt % 4 == 0 and (nch // 4) % LANES == 0 and (nblk // 4) % SUBLANES == 0 else 1
    kern = functools.partial(_cmp_select_kernel, tq=tq, n_pick=N_SEL - 3, nqt=nqt, n_var=n_var)
    return pl.pallas_call(
        kern,
        grid=(B, nqt),
        in_specs=[
            pl.BlockSpec((1, NSA_HEADS, LANES, tq), lambda b, i: (b, 0, 0, i)),
            pl.BlockSpec((NSA_GROUPS, nch, NSA_DIM), lambda b, i: (0, b, 0)),
            pl.BlockSpec((1, NSA_GROUPS * NSA_DIM, nch), lambda b, i: (b, 0, 0)),
            _const_spec(ovt.shape),
        ],
        out_specs=[pl.BlockSpec((tq, NSA_HEADS * NSA_DIM), lambda b, i: (b * nqt + i, 0)),
                   pl.BlockSpec((1, NSA_GROUPS, 2 * nblk, tq), lambda b, i: (b, 0, 0, i))],
        out_shape=[jax.ShapeDtypeStruct((T, NSA_HEADS * NSA_DIM), BF16),
                   jax.ShapeDtypeStruct((B, NSA_GROUPS, 2 * nblk, S), BF16)],
        compiler_params=_cparams(("parallel", "parallel")),
        name="cmp_select",
    )(nq, kc, vct, ovt)


def _flash_heads(nheads, kqv, mask, src, dst, lazy=False, unroll=True):
    def scores(hd):
        k, qt, _ = kqv(hd)
        return _dot(k, qt)

    def update(hd, s):
        if mask is not None:
            s = jnp.where(mask, s, MASKED)
        tmax = jnp.max(s, axis=0, keepdims=True)
        vt = kqv(hd)[2]
        rise = None
        if src is None:
            m_new = jnp.maximum(tmax, M_FLOOR)
            acc = _dot(vt, jnp.exp2(s - m_new).astype(BF16))
        else:
            m_old = src[0][hd]
            m_new = jnp.maximum(m_old, tmax)
            alpha = jnp.exp2(m_old - m_new)
            if lazy:
                acc = (src[1][hd] + _dot(vt, jnp.exp2(s - m_old).astype(BF16))) * alpha
                rise = tmax - m_old
            else:
                acc = alpha * src[1][hd] + _dot(vt, jnp.exp2(s - m_new).astype(BF16))
        dst[0][hd] = m_new
        dst[1][hd] = acc
        return rise

    if not unroll:
        def body(hd, carry):
            update(hd, scores(hd))
            return carry
        lax.fori_loop(0, nheads, body, 0)
        return None
    ahead = 2
    pending = [scores(hd) for hd in range(min(ahead, nheads))]
    worst = None
    for hd in range(nheads):
        s = pending.pop(0)
        if hd + ahead < nheads:
            pending.append(scores(hd + ahead))
        rise = update(hd, s)
        if rise is not None:
            worst = rise if worst is None else jnp.maximum(worst, rise)
    return worst


def _flash_init(m_ref, acc_ref):
    m_ref[...] = jnp.full(m_ref.shape, M_FLOOR, F32)
    acc_ref[...] = jnp.zeros(acc_ref.shape, F32)


def _flash_finalize(o_ref, acc_ref):
    nv, rows, tq = acc_ref.shape
    dv = rows - SUM_ROWS
    groups = o_ref.shape[0] if len(o_ref.shape) == 3 else 1
    nh = nv // groups
    for gb in range(groups):
        sl = slice(gb * nh, (gb + 1) * nh)
        o = acc_ref[sl, :dv, :] * (1.0 / acc_ref[sl, dv:dv + 1, :])
        o = o.reshape(nh * dv, tq).T.astype(BF16)
        if len(o_ref.shape) == 3:
            o_ref[gb] = o
        else:
            o_ref[...] = o


def _flash_scratch(nh, dv, tq, slots=None):
    lead = (nh,) if slots is None else (slots, nh)
    return [pltpu.VMEM(lead + (1, tq), F32), pltpu.VMEM(lead + (dv + SUM_ROWS, tq), F32)]


def _causal_pairs(nqt):
    pairs = [(i, j) for i in range(nqt) for j in range(i + 1)]
    return (jnp.asarray([p[0] for p in pairs], jnp.int32), jnp.asarray([p[1] for p in pairs], jnp.int32))


def _tile_iota(tq):
    return (lax.broadcasted_iota(jnp.int32, (tq, tq), 0), lax.broadcasted_iota(jnp.int32, (tq, tq), 1))


def _causal_flash_step(i, j, nheads, kqv, tq, o_ref, m_ref, acc_ref, slot_ref):
    key, qry = _tile_iota(tq)

    @pl.when(j == 0)
    def _():
        mask = key <= qry + jnp.where(i > 0, tq, 0)
        _flash_heads(nheads, kqv, mask, None, (m_ref.at[0], acc_ref.at[0]))
        slot_ref[0] = 0

    def lazy_step(mask):
        cur = slot_ref[0]
        src = (m_ref.at[cur], acc_ref.at[cur])
        dst = (m_ref.at[1 - cur], acc_ref.at[1 - cur])
        worst = _flash_heads(nheads, kqv, mask, src, dst, lazy=True)

        @pl.when(jnp.max(worst) > LAZY_SPAN)
        def _():
            _flash_heads(nheads, kqv, mask, src, dst, unroll=False)

        slot_ref[0] = 1 - cur

    @pl.when((j > 0) & (j < i))
    def _():
        lazy_step(None)

    @pl.when((j > 0) & (j == i))
    def _():
        lazy_step(key <= qry)

    @pl.when(j == i)
    def _():
        _flash_finalize(o_ref, acc_ref.at[slot_ref[0]])


def _split_virtual(v, nheads):
    if isinstance(v, int):
        return v // nheads, v % nheads
    return lax.div(v, nheads), lax.rem(v, nheads)


def _mla_kernel(qi_ref, ki_ref, q_ref, k_ref, vt_ref, o_ref, m_ref, acc_ref, slot_ref, *, tq):
    i, j = qi_ref[pl.program_id(0)], ki_ref[pl.program_id(0)]

    def kqv(v):
        bb, hd = _split_virtual(v, MLA_HEADS)
        return k_ref[bb, hd], q_ref[bb, hd], vt_ref[bb, hd]

    _causal_flash_step(i, j, q_ref.shape[0] * MLA_HEADS, kqv, tq, o_ref, m_ref, acc_ref, slot_ref)


def _mla_attention(qm, km, vmt, B, S, tq):
    nqt = S // tq
    kern = functools.partial(_mla_kernel, tq=tq)
    qi, ki = _causal_pairs(nqt)
    return pl.pallas_call(
        kern,
        grid_spec=pltpu.PrefetchScalarGridSpec(
            num_scalar_prefetch=2,
            grid=(qi.shape[0],),
            in_specs=[pl.BlockSpec((B, MLA_HEADS, LANES, tq), lambda p, qi, ki: (0, 0, 0, qi[p])),
                      pl.BlockSpec((B, MLA_HEADS, tq, LANES), lambda p, qi, ki: (0, 0, ki[p], 0)),
                      pl.BlockSpec((B, MLA_HEADS, MLA_V + SUM_ROWS, tq), lambda p, qi, ki: (0, 0, 0, ki[p]))],
            out_specs=pl.BlockSpec((B, tq, MLA_HEADS * MLA_V), lambda p, qi, ki: (0, qi[p], 0)),
            scratch_shapes=_flash_scratch(B * MLA_HEADS, MLA_V, tq, slots=2) + [pltpu.SMEM((1,), jnp.int32)]),
        out_shape=jax.ShapeDtypeStruct((B, S, MLA_HEADS * MLA_V), BF16),
        compiler_params=_cparams(("arbitrary",)),
        name="mla_attn",
    )(qi, ki, qm, km, vmt).reshape(B * S, MLA_HEADS * MLA_V)


def _slc_kernel(qi_ref, ki_ref, q_ref, bias_ref, k_ref, vt_ref, o_ref, qa_ref, m_ref, acc_ref,
                slot_ref, *, tq):
    i, j = qi_ref[pl.program_id(0)], ki_ref[pl.program_id(0)]
    nb = q_ref.shape[0]

    @pl.when(j == 0)
    def _():
        for bb in range(nb):
            for hd in range(NSA_HEADS):
                qa_ref[bb * NSA_HEADS + hd] = q_ref[bb, hd]

    rows = pl.ds(pl.multiple_of(j * SEL_SLOTS, SEL_SLOTS), SEL_SLOTS)
    for bb in range(nb):
        for g in range(NSA_GROUPS):
            tile_bias = bias_ref[bb, g, rows, :]
            for hd in range(g * NSA_HPG, (g + 1) * NSA_HPG):
                qa_ref[bb * NSA_HEADS + hd, NSA_DIM:NSA_DIM + SEL_SLOTS, :] = tile_bias

    def kqv(v):
        bb, hd = _split_virtual(v, NSA_HEADS)
        g = hd // NSA_HPG if isinstance(hd, int) else lax.div(hd, NSA_HPG)
        return k_ref[bb, g], qa_ref[v], vt_ref[bb, g]

    _causal_flash_step(i, j, nb * NSA_HEADS, kqv, tq, o_ref, m_ref, acc_ref, slot_ref)


def _slc_attention(nq, bias, ksa, vst, B, S, tq):
    nqt = S // tq
    nrow = bias.shape[2]
    assert tq // SEL_LEN == SEL_SLOTS // 2
    kern = functools.partial(_slc_kernel, tq=tq)
    qi, ki = _causal_pairs(nqt)
    qmap = lambda p, qi, ki: (0, 0, 0, qi[p])
    return pl.pallas_call(
        kern,
        grid_spec=pltpu.PrefetchScalarGridSpec(
            num_scalar_prefetch=2,
            grid=(qi.shape[0],),
            in_specs=[pl.BlockSpec((B, NSA_HEADS, LANES, tq), qmap),
                      pl.BlockSpec((B, NSA_GROUPS, nrow, tq), qmap),
                      pl.BlockSpec((B, NSA_GROUPS, tq, LANES), lambda p, qi, ki: (0, 0, ki[p], 0)),
                      pl.BlockSpec((B, NSA_GROUPS, NSA_DIM + SUM_ROWS, tq), lambda p, qi, ki: (0, 0, 0, ki[p]))],
            out_specs=pl.BlockSpec((B, tq, NSA_HEADS * NSA_DIM), lambda p, qi, ki: (0, qi[p], 0)),
            scratch_shapes=[pltpu.VMEM((B * NSA_HEADS, LANES, tq), BF16)]
                           + _flash_scratch(B * NSA_HEADS, NSA_DIM, tq, slots=2)
                           + [pltpu.SMEM((1,), jnp.int32)]),
        out_shape=jax.ShapeDtypeStruct((B, S, NSA_HEADS * NSA_DIM), BF16),
        compiler_params=_cparams(("arbitrary",)),
        name="slc_attn",
    )(qi, ki, nq, bias, ksa, vst).reshape(B * S, NSA_HEADS * NSA_DIM)


def _win_kernel(q_ref, k_ref, vt_ref, o_ref, m_ref, acc_ref, *, tq):
    i, j = pl.program_id(1), pl.program_id(2)

    @pl.when(j == 0)
    def _():
        _flash_init(m_ref, acc_ref)

    def step(prev):
        key, qry = _tile_iota(tq)
        mask = (key > qry) if prev else (key <= qry)
        _flash_heads(NSA_HEADS, lambda hd: (k_ref[hd // NSA_HPG], q_ref[0, hd][:NSA_DIM, :],
                                            vt_ref[hd // NSA_HPG]), mask, (m_ref, acc_ref), (m_ref, acc_ref))

    @pl.when((j == 0) & (i > 0))
    def _():
        step(True)

    @pl.when(j == 1)
    def _():
        step(False)
        _flash_finalize(o_ref, acc_ref)


def _win_attention(nq, kw, vwt, B, S, tq):
    T = B * S
    nqt = S // tq
    kern = functools.partial(_win_kernel, tq=tq)
    kidx = lambda b, i, j: b * nqt + jnp.maximum(i - 1 + j, 0)
    return pl.pallas_call(
        kern,
        grid=(B, nqt, 2),
        in_specs=[pl.BlockSpec((1, NSA_HEADS, LANES, tq), lambda b, i, j: (b, 0, 0, i)),
                  pl.BlockSpec((NSA_GROUPS, tq, NSA_DIM), lambda b, i, j: (0, kidx(b, i, j), 0)),
                  pl.BlockSpec((NSA_GROUPS, NSA_DIM + SUM_ROWS, tq), lambda b, i, j: (0, 0, kidx(b, i, j)))],
        out_specs=pl.BlockSpec((tq, NSA_HEADS * NSA_DIM), lambda b, i, j: (b * nqt + i, 0)),
        out_shape=jax.ShapeDtypeStruct((T, NSA_HEADS * NSA_DIM), BF16),
        scratch_shapes=_flash_scratch(NSA_HEADS, NSA_DIM, tq),
        compiler_params=_cparams(("parallel", "parallel", "arbitrary")),
        name="win_attn",
    )(nq, kw, vwt)


def _causal_conv(u, carry, w_ref):
    row = lax.broadcasted_iota(jnp.int32, u.shape, 0)
    c1 = carry[SUBLANES - 1:SUBLANES]
    c2 = carry[SUBLANES - 2:SUBLANES - 1]
    u1 = jnp.where(row == 0, c1, pltpu.roll(u, 1, axis=0))
    u2 = jnp.where(row == 0, c2, jnp.where(row == 1, c1, pltpu.roll(u, 2, axis=0)))
    return w_ref[0:1] * u2 + w_ref[1:2] * u1 + w_ref[2:3] * u


def _merge_kernel(x_ref, g_ref, wgc_ref, cw_ref, ym_ref, oc_ref, os_ref, ow_ref, gl_ref, ex_ref,
                  wbc_ref, wbm_ref, wbn_ref, wo_ref, gp_ref, out_ref, carry_ref, *, tiles_per_seq):
    D = x_ref.shape[1]
    DC = wbc_ref.shape[0]

    @pl.when(pl.program_id(0) % tiles_per_seq == 0)
    def _():
        carry_ref[...] = jnp.zeros(carry_ref.shape, F32)

    x = x_ref[...]
    h = _rms(x, g_ref[...]).astype(BF16)
    p = _dot(h, wgc_ref[...])
    u = p[:, 3 * D + DC:3 * D + 2 * DC] * p[:, 3 * D + 2 * DC:3 * D + 3 * DC]
    y_conv = p[:, 3 * D:3 * D + DC] * _causal_conv(u, carry_ref[...], cw_ref)
    carry_ref[...] = u[u.shape[0] - SUBLANES:]
    hi, lo = _split_bf16(gl_ref[...], 2)
    gate = jax.nn.sigmoid(_dot(hi, ex_ref[...]) + _dot(lo, ex_ref[...]))
    HD = oc_ref.shape[1]
    y_nsa = (gate[:, 0:HD] * oc_ref[...].astype(F32) + gate[:, HD:2 * HD] * os_ref[...].astype(F32)
             + gate[:, 2 * HD:3 * HD] * ow_ref[...].astype(F32))
    merged = (jax.nn.sigmoid(p[:, 0:D]) * _dot(y_conv.astype(BF16), wbc_ref[...])
              + jax.nn.sigmoid(p[:, D:2 * D]) * _dot(ym_ref[...], wbm_ref[...])
              + jax.nn.sigmoid(p[:, 2 * D:3 * D]) * _dot(y_nsa.astype(BF16), wbn_ref[...]))
    out_ref[...] = x + _rms(_dot(merged.astype(BF16), wo_ref[...]), gp_ref[...])


def _merge(x2, g_pre, w_gc, conv_w, y_mla, o_cmp, o_slc, o_win, gl, ex, wbc, wbm, wbn, wo, g_post,
           S, tm):
    T, D = x2.shape
    HD = y_mla.shape[1]
    row = lambda w: pl.BlockSpec((tm, w), lambda i: (i, 0))
    kern = functools.partial(_merge_kernel, tiles_per_seq=S // tm)
    return pl.pallas_call(
        kern,
        grid=(T // tm,),
        in_specs=[row(D), _const_spec((1, D)), _const_spec(w_gc.shape), _const_spec(conv_w.shape),
                  row(HD), row(HD), row(HD), row(HD), row(LANES), _const_spec(ex.shape),
                  _const_spec(wbc.shape), _const_spec(wbm.shape), _const_spec(wbn.shape),
                  _const_spec(wo.shape), _const_spec((1, D))],
        out_specs=row(D),
        out_shape=jax.ShapeDtypeStruct((T, D), F32),
        scratch_shapes=[pltpu.VMEM((SUBLANES, wbc.shape[0]), F32)],
        compiler_params=_cparams(("arbitrary",)),
        name="merge",
    )(x2, g_pre, w_gc, conv_w, y_mla, o_cmp, o_slc, o_win, gl, ex, wbc, wbm, wbn, wo, g_post)


def _ffn_kernel(x_ref, g_ref, wup_ref, cw_ref, cb_ref, wdn_ref, gp_ref, out_ref, carry_ref,
                *, tiles_per_seq, chunk):
    F = wdn_ref.shape[0]

    @pl.when(pl.program_id(0) % tiles_per_seq == 0)
    def _():
        carry_ref[...] = jnp.zeros(carry_ref.shape, F32)

    x = x_ref[...]
    h = _rms(x, g_ref[...]).astype(BF16)
    acc = None
    for c0 in range(0, F, chunk):
        a = _dot(h, wup_ref[:, c0:c0 + chunk])
        b = _dot(h, wup_ref[:, F + c0:F + c0 + chunk])
        ac = _causal_conv(a, carry_ref[:, c0:c0 + chunk], cw_ref.at[:, c0:c0 + chunk])
        ac = ac + cb_ref[:, c0:c0 + chunk]
        carry_ref[:, c0:c0 + chunk] = a[a.shape[0] - SUBLANES:]
        d = _dot((jax.nn.gelu(ac) * b).astype(BF16), wdn_ref[c0:c0 + chunk, :])
        acc = d if acc is None else acc + d
    out_ref[...] = x + _rms(acc, gp_ref[...])


def _ffn(x2, g_pre, w_up, conv_w, conv_b, w_dn, g_post, S, tm, chunk):
    T, D = x2.shape
    F = w_dn.shape[0]
    row = pl.BlockSpec((tm, D), lambda i: (i, 0))
    kern = functools.partial(_ffn_kernel, tiles_per_seq=S // tm, chunk=chunk)
    return pl.pallas_call(
        kern,
        grid=(T // tm,),
        in_specs=[row, _const_spec((1, D)), _const_spec(w_up.shape), _const_spec(conv_w.shape),
                  _const_spec((1, F)), _const_spec(w_dn.shape), _const_spec((1, D))],
        out_specs=row,
        out_shape=jax.ShapeDtypeStruct((T, D), F32),
        scratch_shapes=[pltpu.VMEM((SUBLANES, F), F32)],
        compiler_params=_cparams(("arbitrary",)),
        name="ffn",
    )(x2, g_pre, w_up, conv_w, conv_b, w_dn, g_post)


def _overlap_matrix_t(S):
    nch = S // CMP_STRIDE
    n_cmp = (S - CMP_LEN) // CMP_STRIDE + 1
    cs = np.arange(nch)[None, :] * CMP_STRIDE
    ss = np.arange(S // SEL_LEN)[:, None] * SEL_LEN
    ov = (cs <= ss + SEL_LEN - 1) & (cs + CMP_LEN - 1 >= ss) & (np.arange(nch)[None, :] < n_cmp)
    return ov.astype(np.float32)


def _gate_expand():
    ex = np.zeros((LANES, 3 * NSA_HEADS * NSA_DIM), np.float32)
    for hd in range(NSA_HEADS):
        for c in range(3):
            ex[hd * 3 + c, c * NSA_HEADS * NSA_DIM + hd * NSA_DIM:
               c * NSA_HEADS * NSA_DIM + (hd + 1) * NSA_DIM] = 1.0
    return ex


def _pad_rows(w, rows):
    return jnp.concatenate([w, jnp.zeros((rows - w.shape[0],) + w.shape[1:], w.dtype)], axis=0)


def kernel(x, positions, norm_mix_pre, norm_mix_post, w_in, conv_w, mla_q_norm, mla_w_uq, mla_kv_norm, mla_w_ukv, nsa_cmp_pos_k, nsa_cmp_pos_v, nsa_cmp_w_k, nsa_cmp_w_v, w_branch_conv, w_branch_mla, w_branch_nsa, w_out, norm_ffn_pre, norm_ffn_post, ffn_w_up, ffn_conv_w, ffn_conv_b, ffn_w_down):
    B, S, D = x.shape
    T = B * S
    depth = w_in.shape[0]
    DC = conv_w.shape[2]
    F = ffn_w_down.shape[1]
    QL = mla_q_norm.shape[1]
    KL = mla_kv_norm.shape[1]
    tq = 512
    assert S % tq == 0 and WINDOW == tq and N_SEL <= S // SEL_LEN <= LANES
    tm_proj, tm_merge, tm_ffn, ffn_chunk = 512, 512, 512, 1408
    assert F % ffn_chunk == 0 and ffn_chunk % LANES == 0

    x2 = x.reshape(T, D)
    pos2 = positions.reshape(1, T)
    rc = jnp.asarray(_rope_consts())
    ovt = jnp.asarray(_overlap_matrix_t(S), BF16)
    ex = jnp.asarray(_gate_expand(), BF16)
    half = CMP_LEN // 2 * NSA_DIM
    o_att = 3 * D + 3 * DC

    for l in range(depth):
        w = w_in[l]
        o_kr = o_att + QL + KL
        z = lambda n: jnp.zeros((D, n), F32)
        w_att = jnp.concatenate(
            [w[:, o_att:o_kr], z(MLA_NOPE), w[:, o_kr:o_kr + MLA_ROPE], z(LANES - MLA_NOPE - MLA_ROPE),
             w[:, o_kr + MLA_ROPE:o_kr + MLA_ROPE + NSA_HEADS * NSA_DIM + 6 * NSA_GROUPS * NSA_DIM],
             w[:, w.shape[1] - 3 * NSA_HEADS:], z(LANES - 3 * NSA_HEADS)], axis=1).astype(BF16)
        dq = MLA_NOPE + MLA_ROPE
        w_uq = jnp.pad(mla_w_uq[l].reshape(QL, MLA_HEADS, dq),
                       ((0, 0), (0, 0), (0, LANES - dq))).reshape(QL, MLA_HEADS * LANES).astype(BF16)
        w_v_t = mla_w_ukv[l].reshape(KL, MLA_HEADS, MLA_NOPE + MLA_V)[:, :, MLA_NOPE:].reshape(
            KL, MLA_HEADS * MLA_V).T.astype(BF16)
        (qm, km, vmt, nq, kc_in, vc_in, ksa, vst, kw, vwt, gl) = _project(
            x2, pos2, norm_mix_pre[l][None], w_att, mla_q_norm[l][None], w_uq.T,
            mla_kv_norm[l][None], mla_w_ukv[l].astype(BF16), w_v_t, rc, S, tm_proj)

        chunks = lambda a: a.reshape(NSA_GROUPS, T // CMP_STRIDE, CMP_STRIDE * NSA_DIM)
        wv = nsa_cmp_w_v[l].reshape(2, half, NSA_DIM)
        zv = jnp.zeros_like(wv)
        wv_pad = jnp.stack([jnp.concatenate([wv, zv], axis=2), jnp.concatenate([zv, wv], axis=2)])
        kc, vct = _compress(chunks(kc_in), chunks(vc_in),
                            nsa_cmp_pos_k[l].reshape(2, half), nsa_cmp_pos_v[l].reshape(2, half),
                            nsa_cmp_w_k[l].reshape(2, half, NSA_DIM).astype(BF16),
                            wv_pad.astype(BF16), B)

        y_mla = _mla_attention(qm, km, vmt, B, S, tq)
        o_cmp, bias = _cmp_select(nq, kc, vct, ovt, B, S, tq)
        o_slc = _slc_attention(nq, bias, ksa, vst, B, S, tq)
        o_win = _win_attention(nq, kw, vwt, B, S, tq)

        x2 = _merge(x2, norm_mix_pre[l][None], w[:, :o_att].astype(BF16),
                    _pad_rows(conv_w[l], SUBLANES), y_mla, o_cmp, o_slc, o_win, gl, ex,
                    w_branch_conv[l].astype(BF16), w_branch_mla[l].astype(BF16),
                    w_branch_nsa[l].astype(BF16), w_out[l].astype(BF16), norm_mix_post[l][None],
                    S, tm_merge)
        x2 = _ffn(x2, norm_ffn_pre[l][None], ffn_w_up[l].astype(BF16),
                  _pad_rows(ffn_conv_w[l], SUBLANES), ffn_conv_b[l][None],
                  ffn_w_down[l].astype(BF16), norm_ffn_post[l][None], S, tm_ffn, ffn_chunk)
    return x2.reshape(B, S, D)
```

```python
import functools
import math

import numpy as np
import jax
import jax.numpy as jnp
from jax import lax
from jax.experimental import pallas as pl
from jax.experimental.pallas import tpu as pltpu

F32 = jnp.float32
BF16 = jnp.bfloat16

ROPE_THETA = 500000.0
RMS_EPS = 1e-6
CONV_WIDTH = 3
MLA_HEADS = 8
MLA_NOPE = 64
MLA_ROPE = 32
MLA_V = 64
NSA_HEADS = 8
NSA_GROUPS = 2
NSA_HPG = NSA_HEADS // NSA_GROUPS
NSA_DIM = 64
NSA_ROT = NSA_DIM // 4
CMP_LEN = 32
CMP_STRIDE = 16
SEL_LEN = 64
N_SEL = 16
WINDOW = 512

LANES = 128
SUBLANES = 8
VMEM_LIMIT = 56 * 1024 * 1024
MASKED = -1e30
M_FLOOR = -1e20
SEL_BIAS = -2.0 ** 100
LAZY_SPAN = 64.0
LOG2E = math.log2(math.e)
SEL_SLOTS = 16
ROPE_ROWS = 32
SUM_ROWS = 16


def _cparams(sem):
    return pltpu.CompilerParams(dimension_semantics=sem, vmem_limit_bytes=VMEM_LIMIT)


def _const_spec(shape):
    nd = len(shape)
    return pl.BlockSpec(shape, lambda *_: (0,) * nd, pipeline_mode=pl.Buffered(1))


def _rms(x, g):
    return x * lax.rsqrt(jnp.mean(x * x, axis=-1, keepdims=True) + RMS_EPS) * g


def _dot(a, b):
    return jnp.dot(a, b, preferred_element_type=F32)


def _split_bf16(a, terms):
    pieces, rem = [], a
    for _ in range(terms):
        piece = rem.astype(BF16)
        rem = rem - piece.astype(F32)
        pieces.append(piece)
    return pieces


def _rope(x, cos, sin_up, sin_dn, half):
    return (x * cos + pltpu.roll(x, half, axis=1) * sin_up
            + pltpu.roll(x, LANES - half, axis=1) * sin_dn)


def _lanes(table, lane, base, moves):
    out = base
    for dst, width, src in moves:
        rolled = pltpu.roll(table, (dst - src) % LANES, axis=1)
        out = jnp.where((lane >= dst) & (lane < dst + width), rolled, out)
    return out


def _proj_kernel(x_ref, pos_ref, g_ref, watt_ref, qg_ref, wuqt_ref, kvg_ref, wukv_ref, wvt_ref, rc_ref,
                 qm_ref, km_ref, vmt_ref, nq_ref, kc_ref, vc_ref, ksa_ref, vst_ref, kw_ref, vwt_ref,
                 gl_ref, *, mla_scale, nsa_scale, tiles_per_seq):
    tm = x_ref.shape[0]
    hm, hn = MLA_ROPE // 2, NSA_ROT // 2
    x = x_ref[...]
    h = _rms(x, g_ref[...]).astype(BF16)
    p = _dot(h, watt_ref[...])
    lane = lax.broadcasted_iota(jnp.int32, (tm, LANES), 1)
    low = lane < NSA_DIM
    ang_t = rc_ref[:, 0:1] * pos_ref[...].astype(F32)
    cos_t, sin_t = jnp.cos(ang_t), jnp.sin(ang_t)
    cs = jnp.concatenate([cos_t, sin_t, jnp.zeros((LANES - 2 * ROPE_ROWS, tm), F32)], axis=0).T
    cos_m = _lanes(cs, lane, 1.0, [(MLA_NOPE, hm, 0), (MLA_NOPE + hm, hm, 0)])
    sup_m = _lanes(cs, lane, 0.0, [(MLA_NOPE + hm, hm, ROPE_ROWS)])
    sdn_m = -_lanes(cs, lane, 0.0, [(MLA_NOPE, hm, ROPE_ROWS)])
    cos_n = _lanes(cs, lane, 1.0, [(0, hn, hm), (hn, hn, hm), (NSA_DIM, hn, hm), (NSA_DIM + hn, hn, hm)])
    sup_n = _lanes(cs, lane, 0.0, [(hn, hn, ROPE_ROWS + hm), (NSA_DIM + hn, hn, ROPE_ROWS + hm)])
    sdn_n = -_lanes(cs, lane, 0.0, [(0, hn, ROPE_ROWS + hm), (NSA_DIM, hn, ROPE_ROWS + hm)])
    ones_rows = jnp.where(lax.broadcasted_iota(jnp.int32, (SUM_ROWS, tm), 0) == 0, 1.0, 0.0)

    def with_sum_rows(vt):
        return jnp.concatenate([vt, ones_rows], axis=0).astype(BF16)

    def halves(slab):
        return (jnp.where(low, slab, 0.0), jnp.where(low, pltpu.roll(slab, NSA_DIM, axis=1), 0.0))

    def rope_rows(x1, x2, c, s_):
        return x1 * c - x2 * s_, x2 * c + x1 * s_

    cqn_t = _rms(p[:, 0:384], qg_ref[...]).T.astype(BF16)
    qt = _dot(wuqt_ref[...], cqn_t)
    for hd in range(MLA_HEADS):
        r0 = hd * LANES
        y1, y2 = rope_rows(qt[r0 + MLA_NOPE:r0 + MLA_NOPE + hm], qt[r0 + MLA_NOPE + hm:r0 + MLA_NOPE + 2 * hm],
                           cos_t[0:hm], sin_t[0:hm])
        slab_t = jnp.concatenate([qt[r0:r0 + MLA_NOPE], y1, y2, qt[r0 + MLA_NOPE + 2 * hm:r0 + LANES]], axis=0)
        qm_ref[0, hd] = (slab_t * mla_scale).astype(BF16)
    ckvn = _rms(p[:, 384:640], kvg_ref[...])
    kv = _dot(ckvn.astype(BF16), wukv_ref[...])
    kr = _rope(p[:, 640:768], cos_m, sup_m, sdn_m, hm)
    vt = _dot(wvt_ref[...], ckvn.T.astype(BF16))
    for hd in range(MLA_HEADS):
        km_ref[0, hd] = jnp.where(lane < MLA_NOPE, kv[:, hd * LANES:(hd + 1) * LANES], kr).astype(BF16)
        vmt_ref[0, hd] = with_sum_rows(vt[hd * MLA_V:(hd + 1) * MLA_V])
    zpad = jnp.zeros((LANES - NSA_DIM, tm), BF16)
    for pr in range(NSA_HEADS // 2):
        slab_t = p[:, 768 + pr * LANES:768 + (pr + 1) * LANES].T
        for half in range(2):
            r0 = half * NSA_DIM
            y1, y2 = rope_rows(slab_t[r0:r0 + hn], slab_t[r0 + hn:r0 + 2 * hn],
                               cos_t[hm:hm + hn], sin_t[hm:hm + hn])
            head_t = jnp.concatenate([y1, y2, slab_t[r0 + 2 * hn:r0 + NSA_DIM]], axis=0) * nsa_scale
            nq_ref[0, 2 * pr + half, 0:NSA_DIM, :] = head_t.astype(BF16)
            nq_ref[0, 2 * pr + half, NSA_DIM:LANES, :] = zpad
    nsa = lambda idx: p[:, 1280 + idx * LANES:1280 + (idx + 1) * LANES]
    kcs = _rope(nsa(0), cos_n, sup_n, sdn_n, hn)
    kc_ref[0] = kcs[:, :NSA_DIM].astype(BF16)
    kc_ref[1] = pltpu.roll(kcs, NSA_DIM, axis=1)[:, :NSA_DIM].astype(BF16)
    vcs = nsa(1)
    vc_ref[0] = vcs[:, :NSA_DIM].astype(BF16)
    vc_ref[1] = pltpu.roll(vcs, NSA_DIM, axis=1)[:, :NSA_DIM].astype(BF16)
    t_seq = (pl.program_id(0) % tiles_per_seq) * tm + lax.broadcasted_iota(jnp.int32, (tm, LANES), 0)
    slot = NSA_DIM + (lax.shift_right_logical(t_seq, int(math.log2(SEL_LEN))) & (SEL_SLOTS // 2 - 1))
    onehot = jnp.where(lane == slot, 1.0, 0.0)
    for g, kg in enumerate(halves(_rope(nsa(2), cos_n, sup_n, sdn_n, hn))):
        ksa_ref[0, g] = (kg + onehot).astype(BF16)
    vst = nsa(3).T
    vst_ref[0, 0] = with_sum_rows(vst[:NSA_DIM])
    vst_ref[0, 1] = with_sum_rows(vst[NSA_DIM:])
    kws = _rope(nsa(4), cos_n, sup_n, sdn_n, hn)
    kw_ref[0] = kws[:, :NSA_DIM].astype(BF16)
    kw_ref[1] = pltpu.roll(kws, NSA_DIM, axis=1)[:, :NSA_DIM].astype(BF16)
    vwt = nsa(5).T
    vwt_ref[0] = with_sum_rows(vwt[:NSA_DIM])
    vwt_ref[1] = with_sum_rows(vwt[NSA_DIM:])
    gl_ref[...] = p[:, 2048:2176]


def _rope_consts():
    rc = np.zeros((ROPE_ROWS, LANES), np.float32)
    hm = MLA_ROPE // 2
    rc[:hm] = (ROPE_THETA ** (-np.arange(hm, dtype=np.float32) / hm))[:, None]
    hn = NSA_ROT // 2
    rc[hm:hm + hn] = (ROPE_THETA ** (-np.arange(hn, dtype=np.float32) / hn))[:, None]
    return rc


def _project(x2, pos2, g_pre, w_att, q_g, w_uq_t, kv_g, w_ukv, w_v_t, rc, S, tm):
    T, D = x2.shape
    n_att = w_att.shape[1]
    tps = S // tm
    B = T // S
    rows = lambda n, d: (jax.ShapeDtypeStruct((n, T, d), BF16), pl.BlockSpec((n, tm, d), lambda i: (0, i, 0)))
    cols = lambda n, d: (jax.ShapeDtypeStruct((n, d, T), BF16), pl.BlockSpec((n, d, tm), lambda i: (0, 0, i)))
    brows = lambda n, d: (jax.ShapeDtypeStruct((B, n, S, d), BF16),
                          pl.BlockSpec((1, n, tm, d), lambda i: (i // tps, 0, i % tps, 0)))
    bcols = lambda n, d: (jax.ShapeDtypeStruct((B, n, d, S), BF16),
                          pl.BlockSpec((1, n, d, tm), lambda i: (i // tps, 0, 0, i % tps)))
    outs = [bcols(MLA_HEADS, LANES), brows(MLA_HEADS, LANES), bcols(MLA_HEADS, MLA_V + SUM_ROWS),
            bcols(NSA_HEADS, LANES), rows(NSA_GROUPS, NSA_DIM), rows(NSA_GROUPS, NSA_DIM),
            brows(NSA_GROUPS, LANES), bcols(NSA_GROUPS, NSA_DIM + SUM_ROWS), rows(NSA_GROUPS, NSA_DIM),
            cols(NSA_GROUPS, NSA_DIM + SUM_ROWS),
            (jax.ShapeDtypeStruct((T, LANES), F32), pl.BlockSpec((tm, LANES), lambda i: (i, 0)))]
    kern = functools.partial(_proj_kernel,
                             mla_scale=float((MLA_NOPE + MLA_ROPE) ** -0.5 * LOG2E),
                             nsa_scale=float(NSA_DIM ** -0.5 * LOG2E),
                             tiles_per_seq=S // tm)
    return pl.pallas_call(
        kern,
        grid=(T // tm,),
        in_specs=[
            pl.BlockSpec((tm, D), lambda i: (i, 0)),
            pl.BlockSpec((1, tm), lambda i: (0, i)),
            _const_spec((1, D)),
            _const_spec((D, n_att)),
            _const_spec((1, q_g.shape[1])),
            _const_spec(w_uq_t.shape),
            _const_spec((1, kv_g.shape[1])),
            _const_spec(w_ukv.shape),
            _const_spec(w_v_t.shape),
            _const_spec((ROPE_ROWS, LANES)),
        ],
        out_specs=[o[1] for o in outs],
        out_shape=[o[0] for o in outs],
        compiler_params=_cparams(("parallel",)),
        name="proj",
    )(x2, pos2, g_pre, w_att, q_g, w_uq_t, kv_g, w_ukv, w_v_t, rc)


def _compress_kernel(xk_ref, xv_ref, pk_ref, pv_ref, wk_ref, wv_ref, ok_ref, ovt_ref):
    n = xk_ref.shape[1]
    vboth = None
    for g in range(NSA_GROUPS):
        xk = xk_ref[g].astype(F32)
        lo = _dot((xk + pk_ref[0:1]).astype(BF16), wk_ref[0])
        hi = _dot((xk + pk_ref[1:2]).astype(BF16), wk_ref[1])
        ok_ref[g] = (lo + pltpu.roll(hi, n - 1, axis=0)).astype(BF16)
        xv = xv_ref[g].astype(F32)
        lo = _dot((xv + pv_ref[0:1]).astype(BF16), wv_ref[g, 0])
        hi = _dot((xv + pv_ref[1:2]).astype(BF16), wv_ref[g, 1])
        vg = lo + pltpu.roll(hi, n - 1, axis=0)
        vboth = vg if vboth is None else vboth + vg
    ovt_ref[0] = vboth.T.astype(BF16)


def _compress(xk, xv, pk, pv, wk, wv, B):
    G, TC, CW = xk.shape
    nch = TC // B
    xspec = pl.BlockSpec((G, nch, CW), lambda b: (0, b, 0))
    return pl.pallas_call(
        _compress_kernel,
        grid=(B,),
        in_specs=[xspec, xspec, _const_spec(pk.shape), _const_spec(pv.shape),
                  _const_spec(wk.shape), _const_spec(wv.shape)],
        out_specs=[pl.BlockSpec((G, nch, NSA_DIM), lambda b: (0, b, 0)),
                   pl.BlockSpec((1, G * NSA_DIM, nch), lambda b: (b, 0, 0))],
        out_shape=[jax.ShapeDtypeStruct((G, TC, NSA_DIM), BF16),
                   jax.ShapeDtypeStruct((B, G * NSA_DIM, nch), BF16)],
        compiler_params=_cparams(("parallel",)),
        name="compress",
    )(xk, xv, pk, pv, wk, wv)


def _cmp_select_kernel(q_ref, kc_ref, vct_ref, ovt_ref, o_ref, bias_ref, *, tq, n_pick, nqt, n_var):
    i = pl.program_id(1)
    q0 = i * tq
    nch_all = kc_ref.shape[1]
    nblk_all = ovt_ref.shape[0]
    tiles_per_var = nqt // n_var

    def variant(nch, nblk):
        t_c = q0 + lax.broadcasted_iota(jnp.int32, (nch, tq), 1)
        n_c = lax.broadcasted_iota(jnp.int32, (nch, tq), 0)
        cmask = (n_c * CMP_STRIDE + (CMP_LEN - 1)) <= t_c
        t_b = q0 + lax.broadcasted_iota(jnp.int32, (nblk, tq), 1)
        blk = lax.broadcasted_iota(jnp.int32, (nblk, tq), 0)
        blk_f = blk.astype(F32)
        cur = lax.shift_right_logical(t_b, int(math.log2(SEL_LEN)))
        forced = (blk == 0) | (blk == cur) | (blk == cur - 1)
        causal = blk * SEL_LEN <= t_b
        outs = []
        for g in range(NSA_GROUPS):
            psum = None
            for j in range(NSA_HPG):
                qt = q_ref[0, g * NSA_HPG + j][:NSA_DIM, :]
                s = jnp.where(cmask, _dot(kc_ref[g, :nch, :], qt), MASKED)
                m = jnp.maximum(jnp.max(s, axis=0, keepdims=True), M_FLOOR)
                p = jnp.exp2(s - m)
                l = jnp.sum(p, axis=0, keepdims=True)
                pn = p * jnp.where(l > 0.0, 1.0 / l, 0.0)
                outs.append(_dot(vct_ref[0, g * NSA_DIM:(g + 1) * NSA_DIM, :nch], pn.astype(BF16)))
                psum = pn if psum is None else psum + pn
            imp = None
            for piece in _split_bf16(psum, 3):
                d = _dot(ovt_ref[:nblk, :nch], piece)
                imp = d if imp is None else imp + d
            v0 = jnp.where(forced | jnp.logical_not(causal), -1.0, imp)
            v = v0
            for _ in range(n_pick):
                mx = jnp.max(v, axis=0, keepdims=True)
                first = jnp.min(jnp.where(v == mx, blk_f, float(nblk)), axis=0, keepdims=True)
                v = jnp.where(blk_f == first, -2.0, v)
            sel = forced | ((v == -2.0) & (v0 >= 0.0))
            bias = jnp.where(sel, 0.0, SEL_BIAS)
            per_tile = SEL_SLOTS // 2
            pad = jnp.zeros((SEL_SLOTS - per_tile, tq), F32)
            pieces = []
            for kt in range(nblk_all // per_tile):
                if kt * per_tile < nblk:
                    pieces += [bias[kt * per_tile:(kt + 1) * per_tile], pad]
                else:
                    pieces += [jnp.full((per_tile, tq), SEL_BIAS, F32), pad]
            bias_ref[0, g] = jnp.concatenate(pieces, axis=0).astype(BF16)
        o_ref[...] = jnp.concatenate(outs, axis=0).T.astype(BF16)

    for k in range(n_var):
        @pl.when((i >= k * tiles_per_var) & (i < (k + 1) * tiles_per_var))
        def _():
            variant(nch_all * (k + 1) // n_var, nblk_all * (k + 1) // n_var)


def _cmp_select(nq, kc, vct, ovt, B, S, tq):
    T = B * S
    nch = kc.shape[1] // B
    nblk = ovt.shape[0]
    nqt = S // tq
    n_var = 4 if nqt % 4 == 0 and (nch // 4) % LANES == 0 and (nblk // 4) % SUBLANES == 0 else 1
    kern = functools.partial(_cmp_select_kernel, tq=tq, n_pick=N_SEL - 3, nqt=nqt, n_var=n_var)
    return pl.pallas_call(
        kern,
        grid=(B, nqt),
        in_specs=[
            pl.BlockSpec((1, NSA_HEADS, LANES, tq), lambda b, i: (b, 0, 0, i)),
            pl.BlockSpec((NSA_GROUPS, nch, NSA_DIM), lambda b, i: (0, b, 0)),
            pl.BlockSpec((1, NSA_GROUPS * NSA_DIM, nch), lambda b, i: (b, 0, 0)),
            _const_spec(ovt.shape),
        ],
        out_specs=[pl.BlockSpec((tq, NSA_HEADS * NSA_DIM), lambda b, i: (b * nqt + i, 0)),
                   pl.BlockSpec((1, NSA_GROUPS, 2 * nblk, tq), lambda b, i: (b, 0, 0, i))],
        out_shape=[jax.ShapeDtypeStruct((T, NSA_HEADS * NSA_DIM), BF16),
                   jax.ShapeDtypeStruct((B, NSA_GROUPS, 2 * nblk, S), BF16)],
        compiler_params=_cparams(("parallel", "parallel")),
        name="cmp_select",
    )(nq, kc, vct, ovt)


def _tile_parts(kind, tq):
    if kind == "full":
        return [(slice(0, tq), slice(0, tq), None)]
    hq = tq // 2

    def mask(rows, cols, below):
        key = rows.start + lax.broadcasted_iota(jnp.int32, (rows.stop - rows.start, hq), 0)
        qry = cols.start + lax.broadcasted_iota(jnp.int32, (rows.stop - rows.start, hq), 1)
        return (key <= qry) if below else (key > qry)

    lo, hi, all_ = slice(0, hq), slice(hq, tq), slice(0, tq)
    if kind == "diag":
        return [(lo, lo, mask(lo, lo, True)), (all_, hi, mask(all_, hi, True))]
    return [(all_, lo, mask(all_, lo, False)), (hi, hi, mask(hi, hi, False))]


def _flash_heads(nheads, kqv, parts, src, dst, lazy=False, unroll=True):
    def scores(hd, part):
        rows, cols, _ = part
        k, qt, _ = kqv(hd)
        return _dot(k[rows], qt[:, cols])

    def update(hd, part, s):
        rows, cols, mask = part
        if mask is not None:
            s = jnp.where(mask, s, MASKED)
        tmax = jnp.max(s, axis=0, keepdims=True)
        vt = kqv(hd)[2][:, rows]
        rise = None
        if src is None:
            m_new = jnp.maximum(tmax, M_FLOOR)
            acc = _dot(vt, jnp.exp2(s - m_new).astype(BF16))
        else:
            m_old = src[0][hd, :, cols]
            m_new = jnp.maximum(m_old, tmax)
            alpha = jnp.exp2(m_old - m_new)
            if lazy:
                acc = (src[1][hd, :, cols] + _dot(vt, jnp.exp2(s - m_old).astype(BF16))) * alpha
                rise = tmax - m_old
            else:
                acc = alpha * src[1][hd, :, cols] + _dot(vt, jnp.exp2(s - m_new).astype(BF16))
        dst[0][hd, :, cols] = m_new
        dst[1][hd, :, cols] = acc
        return rise

    if not unroll:
        def body(hd, carry):
            for part in parts:
                update(hd, part, scores(hd, part))
            return carry
        lax.fori_loop(0, nheads, body, 0)
        return None
    todo = [(hd, part) for hd in range(nheads) for part in parts]
    ahead = 2
    pending = [scores(*t) for t in todo[:ahead]]
    worst = {}
    for n, (hd, part) in enumerate(todo):
        s = pending.pop(0)
        if n + ahead < len(todo):
            pending.append(scores(*todo[n + ahead]))
        rise = update(hd, part, s)
        if rise is not None:
            key = rise.shape
            worst[key] = rise if key not in worst else jnp.maximum(worst[key], rise)
    if not worst:
        return None
    tops = [jnp.max(w, axis=1, keepdims=True) for w in worst.values()]
    return functools.reduce(jnp.maximum, tops)


def _flash_finalize(o_ref, acc_ref):
    nv, rows, tq = acc_ref.shape
    dv = rows - SUM_ROWS
    groups = o_ref.shape[0] if len(o_ref.shape) == 3 else 1
    nh = nv // groups
    for gb in range(groups):
        sl = slice(gb * nh, (gb + 1) * nh)
        o = acc_ref[sl, :dv, :] * (1.0 / acc_ref[sl, dv:dv + 1, :])
        o = o.reshape(nh * dv, tq).T.astype(BF16)
        if len(o_ref.shape) == 3:
            o_ref[gb] = o
        else:
            o_ref[...] = o


def _flash_scratch(nh, dv, tq, slots=None):
    lead = (nh,) if slots is None else (slots, nh)
    return [pltpu.VMEM(lead + (1, tq), F32), pltpu.VMEM(lead + (dv + SUM_ROWS, tq), F32)]


def _causal_pairs(nqt):
    pairs = [(i, j) for i in range(nqt) for j in [i] + list(range(i))]
    return (jnp.asarray([p[0] for p in pairs], jnp.int32), jnp.asarray([p[1] for p in pairs], jnp.int32))


def _lazy_flash_step(nheads, kqv, parts, m_ref, acc_ref, cur):
    src = (m_ref.at[cur], acc_ref.at[cur])
    dst = (m_ref.at[1 - cur], acc_ref.at[1 - cur])
    worst = _flash_heads(nheads, kqv, parts, src, dst, lazy=True)

    @pl.when(jnp.max(worst) > LAZY_SPAN)
    def _():
        _flash_heads(nheads, kqv, parts, src, dst, unroll=False)


def _causal_flash_step(i, j, nheads, kqv, tq, o_ref, m_ref, acc_ref, slot_ref):
    @pl.when(j == i)
    def _():
        _flash_heads(nheads, kqv, _tile_parts("diag", tq), None, (m_ref.at[0], acc_ref.at[0]))
        slot_ref[0] = 0

    @pl.when(j < i)
    def _():
        cur = slot_ref[0]
        _lazy_flash_step(nheads, kqv, _tile_parts("full", tq), m_ref, acc_ref, cur)
        slot_ref[0] = 1 - cur

    @pl.when((j == i - 1) | (i == 0))
    def _():
        _flash_finalize(o_ref, acc_ref.at[slot_ref[0]])


def _split_virtual(v, nheads):
    if isinstance(v, int):
        return v // nheads, v % nheads
    return lax.div(v, nheads), lax.rem(v, nheads)


def _mla_kernel(qi_ref, ki_ref, q_ref, k_ref, vt_ref, o_ref, m_ref, acc_ref, slot_ref, *, tq):
    i, j = qi_ref[pl.program_id(0)], ki_ref[pl.program_id(0)]

    def kqv(v):
        bb, hd = _split_virtual(v, MLA_HEADS)
        return k_ref[bb, hd], q_ref[bb, hd], vt_ref[bb, hd]

    _causal_flash_step(i, j, q_ref.shape[0] * MLA_HEADS, kqv, tq, o_ref, m_ref, acc_ref, slot_ref)


def _mla_attention(qm, km, vmt, B, S, tq):
    nqt = S // tq
    kern = functools.partial(_mla_kernel, tq=tq)
    qi, ki = _causal_pairs(nqt)
    return pl.pallas_call(
        kern,
        grid_spec=pltpu.PrefetchScalarGridSpec(
            num_scalar_prefetch=2,
            grid=(qi.shape[0],),
            in_specs=[pl.BlockSpec((B, MLA_HEADS, LANES, tq), lambda p, qi, ki: (0, 0, 0, qi[p])),
                      pl.BlockSpec((B, MLA_HEADS, tq, LANES), lambda p, qi, ki: (0, 0, ki[p], 0)),
                      pl.BlockSpec((B, MLA_HEADS, MLA_V + SUM_ROWS, tq), lambda p, qi, ki: (0, 0, 0, ki[p]))],
            out_specs=pl.BlockSpec((B, tq, MLA_HEADS * MLA_V), lambda p, qi, ki: (0, qi[p], 0)),
            scratch_shapes=_flash_scratch(B * MLA_HEADS, MLA_V, tq, slots=2) + [pltpu.SMEM((1,), jnp.int32)]),
        out_shape=jax.ShapeDtypeStruct((B, S, MLA_HEADS * MLA_V), BF16),
        compiler_params=_cparams(("arbitrary",)),
        name="mla_attn",
    )(qi, ki, qm, km, vmt).reshape(B * S, MLA_HEADS * MLA_V)


def _slc_kernel(qi_ref, ki_ref, q_ref, bias_ref, k_ref, vt_ref, o_ref, qa_ref, m_ref, acc_ref,
                slot_ref, *, tq):
    i, j = qi_ref[pl.program_id(0)], ki_ref[pl.program_id(0)]
    nb = q_ref.shape[0]

    @pl.when(j == i)
    def _():
        for bb in range(nb):
            for hd in range(NSA_HEADS):
                qa_ref[bb * NSA_HEADS + hd] = q_ref[bb, hd]

    rows = pl.ds(pl.multiple_of(j * SEL_SLOTS, SEL_SLOTS), SEL_SLOTS)
    for bb in range(nb):
        for g in range(NSA_GROUPS):
            tile_bias = bias_ref[bb, g, rows, :]
            for hd in range(g * NSA_HPG, (g + 1) * NSA_HPG):
                qa_ref[bb * NSA_HEADS + hd, NSA_DIM:NSA_DIM + SEL_SLOTS, :] = tile_bias

    def kqv(v):
        bb, hd = _split_virtual(v, NSA_HEADS)
        g = hd // NSA_HPG if isinstance(hd, int) else lax.div(hd, NSA_HPG)
        return k_ref[bb, g], qa_ref[v], vt_ref[bb, g]

    _causal_flash_step(i, j, nb * NSA_HEADS, kqv, tq, o_ref, m_ref, acc_ref, slot_ref)


def _slc_attention(nq, bias, ksa, vst, B, S, tq):
    nqt = S // tq
    nrow = bias.shape[2]
    assert tq // SEL_LEN == SEL_SLOTS // 2
    kern = functools.partial(_slc_kernel, tq=tq)
    qi, ki = _causal_pairs(nqt)
    qmap = lambda p, qi, ki: (0, 0, 0, qi[p])
    return pl.pallas_call(
        kern,
        grid_spec=pltpu.PrefetchScalarGridSpec(
            num_scalar_prefetch=2,
            grid=(qi.shape[0],),
            in_specs=[pl.BlockSpec((B, NSA_HEADS, LANES, tq), qmap),
                      pl.BlockSpec((B, NSA_GROUPS, nrow, tq), qmap),
                      pl.BlockSpec((B, NSA_GROUPS, tq, LANES), lambda p, qi, ki: (0, 0, ki[p], 0)),
                      pl.BlockSpec((B, NSA_GROUPS, NSA_DIM + SUM_ROWS, tq), lambda p, qi, ki: (0, 0, 0, ki[p]))],
            out_specs=pl.BlockSpec((B, tq, NSA_HEADS * NSA_DIM), lambda p, qi, ki: (0, qi[p], 0)),
            scratch_shapes=[pltpu.VMEM((B * NSA_HEADS, LANES, tq), BF16)]
                           + _flash_scratch(B * NSA_HEADS, NSA_DIM, tq, slots=2)
                           + [pltpu.SMEM((1,), jnp.int32)]),
        out_shape=jax.ShapeDtypeStruct((B, S, NSA_HEADS * NSA_DIM), BF16),
        compiler_params=_cparams(("arbitrary",)),
        name="slc_attn",
    )(qi, ki, nq, bias, ksa, vst).reshape(B * S, NSA_HEADS * NSA_DIM)


def _win_kernel(q_ref, k_ref, vt_ref, o_ref, m_ref, acc_ref, *, tq):
    i, j = pl.program_id(1), pl.program_id(2)

    def kqv(hd):
        g = hd // NSA_HPG if isinstance(hd, int) else lax.div(hd, NSA_HPG)
        return k_ref[g], q_ref[0, hd][:NSA_DIM, :], vt_ref[g]

    @pl.when(j == 0)
    def _():
        _flash_heads(NSA_HEADS, kqv, _tile_parts("diag", tq), None, (m_ref.at[0], acc_ref.at[0]))

    @pl.when((j == 1) & (i > 0))
    def _():
        _lazy_flash_step(NSA_HEADS, kqv, _tile_parts("prev", tq), m_ref, acc_ref, 0)

    @pl.when(j == 1)
    def _():
        _flash_finalize(o_ref, acc_ref.at[jnp.where(i > 0, 1, 0)])


def _win_attention(nq, kw, vwt, B, S, tq):
    T = B * S
    nqt = S // tq
    kern = functools.partial(_win_kernel, tq=tq)
    kidx = lambda b, i, j: b * nqt + jnp.maximum(i - j, 0)
    return pl.pallas_call(
        kern,
        grid=(B, nqt, 2),
        in_specs=[pl.BlockSpec((1, NSA_HEADS, LANES, tq), lambda b, i, j: (b, 0, 0, i)),
                  pl.BlockSpec((NSA_GROUPS, tq, NSA_DIM), lambda b, i, j: (0, kidx(b, i, j), 0)),
                  pl.BlockSpec((NSA_GROUPS, NSA_DIM + SUM_ROWS, tq), lambda b, i, j: (0, 0, kidx(b, i, j)))],
        out_specs=pl.BlockSpec((tq, NSA_HEADS * NSA_DIM), lambda b, i, j: (b * nqt + i, 0)),
        out_shape=jax.ShapeDtypeStruct((T, NSA_HEADS * NSA_DIM), BF16),
        scratch_shapes=_flash_scratch(NSA_HEADS, NSA_DIM, tq, slots=2),
        compiler_params=_cparams(("parallel", "parallel", "arbitrary")),
        name="win_attn",
    )(nq, kw, vwt)


def _causal_conv(u, carry, w_ref):
    row = lax.broadcasted_iota(jnp.int32, u.shape, 0)
    c1 = carry[SUBLANES - 1:SUBLANES]
    c2 = carry[SUBLANES - 2:SUBLANES - 1]
    u1 = jnp.where(row == 0, c1, pltpu.roll(u, 1, axis=0))
    u2 = jnp.where(row == 0, c2, jnp.where(row == 1, c1, pltpu.roll(u, 2, axis=0)))
    return w_ref[0:1] * u2 + w_ref[1:2] * u1 + w_ref[2:3] * u


def _merge_kernel(x_ref, g_ref, wgc_ref, cw_ref, ym_ref, oc_ref, os_ref, ow_ref, gl_ref, ex_ref,
                  wbc_ref, wbm_ref, wbn_ref, wo_ref, gp_ref, out_ref, carry_ref, *, tiles_per_seq):
    D = x_ref.shape[1]
    DC = wbc_ref.shape[0]

    @pl.when(pl.program_id(0) % tiles_per_seq == 0)
    def _():
        carry_ref[...] = jnp.zeros(carry_ref.shape, F32)

    x = x_ref[...]
    h = _rms(x, g_ref[...]).astype(BF16)
    p = _dot(h, wgc_ref[...])
    u = p[:, 3 * D + DC:3 * D + 2 * DC] * p[:, 3 * D + 2 * DC:3 * D + 3 * DC]
    y_conv = p[:, 3 * D:3 * D + DC] * _causal_conv(u, carry_ref[...], cw_ref)
    carry_ref[...] = u[u.shape[0] - SUBLANES:]
    hi, lo = _split_bf16(gl_ref[...], 2)
    gate = jax.nn.sigmoid(_dot(hi, ex_ref[...]) + _dot(lo, ex_ref[...]))
    HD = oc_ref.shape[1]
    y_nsa = (gate[:, 0:HD] * oc_ref[...].astype(F32) + gate[:, HD:2 * HD] * os_ref[...].astype(F32)
             + gate[:, 2 * HD:3 * HD] * ow_ref[...].astype(F32))
    merged = (jax.nn.sigmoid(p[:, 0:D]) * _dot(y_conv.astype(BF16), wbc_ref[...])
              + jax.nn.sigmoid(p[:, D:2 * D]) * _dot(ym_ref[...], wbm_ref[...])
              + jax.nn.sigmoid(p[:, 2 * D:3 * D]) * _dot(y_nsa.astype(BF16), wbn_ref[...]))
    out_ref[...] = x + _rms(_dot(merged.astype(BF16), wo_ref[...]), gp_ref[...])


def _merge(x2, g_pre, w_gc, conv_w, y_mla, o_cmp, o_slc, o_win, gl, ex, wbc, wbm, wbn, wo, g_post,
           S, tm):
    T, D = x2.shape
    HD = y_mla.shape[1]
    row = lambda w: pl.BlockSpec((tm, w), lambda i: (i, 0))
    kern = functools.partial(_merge_kernel, tiles_per_seq=S // tm)
    return pl.pallas_call(
        kern,
        grid=(T // tm,),
        in_specs=[row(D), _const_spec((1, D)), _const_spec(w_gc.shape), _const_spec(conv_w.shape),
                  row(HD), row(HD), row(HD), row(HD), row(LANES), _const_spec(ex.shape),
                  _const_spec(wbc.shape), _const_spec(wbm.shape), _const_spec(wbn.shape),
                  _const_spec(wo.shape), _const_spec((1, D))],
        out_specs=row(D),
        out_shape=jax.ShapeDtypeStruct((T, D), F32),
        scratch_shapes=[pltpu.VMEM((SUBLANES, wbc.shape[0]), F32)],
        compiler_params=_cparams(("arbitrary",)),
        name="merge",
    )(x2, g_pre, w_gc, conv_w, y_mla, o_cmp, o_slc, o_win, gl, ex, wbc, wbm, wbn, wo, g_post)


def _ffn_kernel(x_ref, g_ref, wup_ref, cw_ref, cb_ref, wdn_ref, gp_ref, out_ref, carry_ref,
                *, tiles_per_seq, chunk):
    F = wdn_ref.shape[0]

    @pl.when(pl.program_id(0) % tiles_per_seq == 0)
    def _():
        carry_ref[...] = jnp.zeros(carry_ref.shape, F32)

    x = x_ref[...]
    h = _rms(x, g_ref[...]).astype(BF16)
    acc = None
    for c0 in range(0, F, chunk):
        a = _dot(h, wup_ref[:, c0:c0 + chunk])
        b = _dot(h, wup_ref[:, F + c0:F + c0 + chunk])
        ac = _causal_conv(a, carry_ref[:, c0:c0 + chunk], cw_ref.at[:, c0:c0 + chunk])
        ac = ac + cb_ref[:, c0:c0 + chunk]
        carry_ref[:, c0:c0 + chunk] = a[a.shape[0] - SUBLANES:]
        d = _dot((jax.nn.gelu(ac) * b).astype(BF16), wdn_ref[c0:c0 + chunk, :])
        acc = d if acc is None else acc + d
    out_ref[...] = x + _rms(acc, gp_ref[...])


def _ffn(x2, g_pre, w_up, conv_w, conv_b, w_dn, g_post, S, tm, chunk):
    T, D = x2.shape
    F = w_dn.shape[0]
    row = pl.BlockSpec((tm, D), lambda i: (i, 0))
    kern = functools.partial(_ffn_kernel, tiles_per_seq=S // tm, chunk=chunk)
    return pl.pallas_call(
        kern,
        grid=(T // tm,),
        in_specs=[row, _const_spec((1, D)), _const_spec(w_up.shape), _const_spec(conv_w.shape),
                  _const_spec((1, F)), _const_spec(w_dn.shape), _const_spec((1, D))],
        out_specs=row,
        out_shape=jax.ShapeDtypeStruct((T, D), F32),
        scratch_shapes=[pltpu.VMEM((SUBLANES, F), F32)],
        compiler_params=_cparams(("arbitrary",)),
        name="ffn",
    )(x2, g_pre, w_up, conv_w, conv_b, w_dn, g_post)


def _overlap_matrix_t(S):
    nch = S // CMP_STRIDE
    n_cmp = (S - CMP_LEN) // CMP_STRIDE + 1
    cs = np.arange(nch)[None, :] * CMP_STRIDE
    ss = np.arange(S // SEL_LEN)[:, None] * SEL_LEN
    ov = (cs <= ss + SEL_LEN - 1) & (cs + CMP_LEN - 1 >= ss) & (np.arange(nch)[None, :] < n_cmp)
    return ov.astype(np.float32)


def _gate_expand():
    ex = np.zeros((LANES, 3 * NSA_HEADS * NSA_DIM), np.float32)
    for hd in range(NSA_HEADS):
        for c in range(3):
            ex[hd * 3 + c, c * NSA_HEADS * NSA_DIM + hd * NSA_DIM:
               c * NSA_HEADS * NSA_DIM + (hd + 1) * NSA_DIM] = 1.0
    return ex


def _pad_rows(w, rows):
    return jnp.concatenate([w, jnp.zeros((rows - w.shape[0],) + w.shape[1:], w.dtype)], axis=0)


def kernel(x, positions, norm_mix_pre, norm_mix_post, w_in, conv_w, mla_q_norm, mla_w_uq, mla_kv_norm, mla_w_ukv, nsa_cmp_pos_k, nsa_cmp_pos_v, nsa_cmp_w_k, nsa_cmp_w_v, w_branch_conv, w_branch_mla, w_branch_nsa, w_out, norm_ffn_pre, norm_ffn_post, ffn_w_up, ffn_conv_w, ffn_conv_b, ffn_w_down):
    B, S, D = x.shape
    T = B * S
    depth = w_in.shape[0]
    DC = conv_w.shape[2]
    F = ffn_w_down.shape[1]
    QL = mla_q_norm.shape[1]
    KL = mla_kv_norm.shape[1]
    tq = 512
    assert S % tq == 0 and WINDOW == tq and N_SEL <= S // SEL_LEN <= LANES
    tm_proj, tm_merge, tm_ffn, ffn_chunk = 512, 512, 512, 1408
    assert F % ffn_chunk == 0 and ffn_chunk % LANES == 0

    x2 = x.reshape(T, D)
    pos2 = positions.reshape(1, T)
    rc = jnp.asarray(_rope_consts())
    ovt = jnp.asarray(_overlap_matrix_t(S), BF16)
    ex = jnp.asarray(_gate_expand(), BF16)
    half = CMP_LEN // 2 * NSA_DIM
    o_att = 3 * D + 3 * DC

    for l in range(depth):
        w = w_in[l]
        o_kr = o_att + QL + KL
        z = lambda n: jnp.zeros((D, n), F32)
        w_att = jnp.concatenate(
            [w[:, o_att:o_kr], z(MLA_NOPE), w[:, o_kr:o_kr + MLA_ROPE], z(LANES - MLA_NOPE - MLA_ROPE),
             w[:, o_kr + MLA_ROPE:o_kr + MLA_ROPE + NSA_HEADS * NSA_DIM + 6 * NSA_GROUPS * NSA_DIM],
             w[:, w.shape[1] - 3 * NSA_HEADS:], z(LANES - 3 * NSA_HEADS)], axis=1).astype(BF16)
        dq = MLA_NOPE + MLA_ROPE
        w_uq = jnp.pad(mla_w_uq[l].reshape(QL, MLA_HEADS, dq),
                       ((0, 0), (0, 0), (0, LANES - dq))).reshape(QL, MLA_HEADS * LANES).astype(BF16)
        w_v_t = mla_w_ukv[l].reshape(KL, MLA_HEADS, MLA_NOPE + MLA_V)[:, :, MLA_NOPE:].reshape(
            KL, MLA_HEADS * MLA_V).T.astype(BF16)
        (qm, km, vmt, nq, kc_in, vc_in, ksa, vst, kw, vwt, gl) = _project(
            x2, pos2, norm_mix_pre[l][None], w_att, mla_q_norm[l][None], w_uq.T,
            mla_kv_norm[l][None], mla_w_ukv[l].astype(BF16), w_v_t, rc, S, tm_proj)

        chunks = lambda a: a.reshape(NSA_GROUPS, T // CMP_STRIDE, CMP_STRIDE * NSA_DIM)
        wv = nsa_cmp_w_v[l].reshape(2, half, NSA_DIM)
        zv = jnp.zeros_like(wv)
        wv_pad = jnp.stack([jnp.concatenate([wv, zv], axis=2), jnp.concatenate([zv, wv], axis=2)])
        kc, vct = _compress(chunks(kc_in), chunks(vc_in),
                            nsa_cmp_pos_k[l].reshape(2, half), nsa_cmp_pos_v[l].reshape(2, half),
                            nsa_cmp_w_k[l].reshape(2, half, NSA_DIM).astype(BF16),
                            wv_pad.astype(BF16), B)

        y_mla = _mla_attention(qm, km, vmt, B, S, tq)
        o_cmp, bias = _cmp_select(nq, kc, vct, ovt, B, S, tq)
        o_slc = _slc_attention(nq, bias, ksa, vst, B, S, tq)
        o_win = _win_attention(nq, kw, vwt, B, S, tq)

        x2 = _merge(x2, norm_mix_pre[l][None], w[:, :o_att].astype(BF16),
                    _pad_rows(conv_w[l], SUBLANES), y_mla, o_cmp, o_slc, o_win, gl, ex,
                    w_branch_conv[l].astype(BF16), w_branch_mla[l].astype(BF16),
                    w_branch_nsa[l].astype(BF16), w_out[l].astype(BF16), norm_mix_post[l][None],
                    S, tm_merge)
        x2 = _ffn(x2, norm_ffn_pre[l][None], ffn_w_up[l].astype(BF16),
                  _pad_rows(ffn_conv_w[l], SUBLANES), ffn_conv_b[l][None],
                  ffn_w_down[l].astype(BF16), norm_ffn_post[l][None], S, tm_ffn, ffn_chunk)
    return x2.reshape(B, S, D)
```

```python
import functools
import math

import numpy as np
import jax
import jax.numpy as jnp
from jax import lax
from jax.experimental import pallas as pl
from jax.experimental.pallas import tpu as pltpu

F32 = jnp.float32
BF16 = jnp.bfloat16

ROPE_THETA = 500000.0
RMS_EPS = 1e-6
CONV_WIDTH = 3
MLA_HEADS = 8
MLA_NOPE = 64
MLA_ROPE = 32
MLA_V = 64
NSA_HEADS = 8
NSA_GROUPS = 2
NSA_HPG = NSA_HEADS // NSA_GROUPS
NSA_DIM = 64
NSA_ROT = NSA_DIM // 4
CMP_LEN = 32
CMP_STRIDE = 16
SEL_LEN = 64
N_SEL = 16
WINDOW = 512

LANES = 128
SUBLANES = 8
VMEM_LIMIT = 56 * 1024 * 1024
MASKED = -1e30
M_FLOOR = -1e20
SEL_BIAS = -2.0 ** 100
LAZY_SPAN = 64.0
LOG2E = math.log2(math.e)
SEL_SLOTS = 16
ROPE_ROWS = 32
SUM_ROWS = 16


def _cparams(sem):
    return pltpu.CompilerParams(dimension_semantics=sem, vmem_limit_bytes=VMEM_LIMIT)


def _const_spec(shape):
    nd = len(shape)
    return pl.BlockSpec(shape, lambda *_: (0,) * nd, pipeline_mode=pl.Buffered(1))


def _rms(x, g):
    return x * lax.rsqrt(jnp.mean(x * x, axis=-1, keepdims=True) + RMS_EPS) * g


def _dot(a, b):
    return jnp.dot(a, b, preferred_element_type=F32)


def _split_bf16(a, terms):
    pieces, rem = [], a
    for _ in range(terms):
        piece = rem.astype(BF16)
        rem = rem - piece.astype(F32)
        pieces.append(piece)
    return pieces


def _rope(x, cos, sin_up, sin_dn, half):
    return (x * cos + pltpu.roll(x, half, axis=1) * sin_up
            + pltpu.roll(x, LANES - half, axis=1) * sin_dn)


def _lanes(table, lane, base, moves):
    out = base
    for dst, width, src in moves:
        rolled = pltpu.roll(table, (dst - src) % LANES, axis=1)
        out = jnp.where((lane >= dst) & (lane < dst + width), rolled, out)
    return out


def _proj_kernel(x_ref, pos_ref, g_ref, watt_ref, qg_ref, wuqt_ref, kvg_ref, wukv_ref, wvt_ref, rc_ref,
                 qm_ref, km_ref, vmt_ref, nq_ref, kc_ref, vc_ref, ksa_ref, vst_ref, kw_ref, vwt_ref,
                 gl_ref, kstage_ref, vstage_ref, *, mla_scale, nsa_scale, tiles_per_seq):
    tm = x_ref.shape[0]
    hm, hn = MLA_ROPE // 2, NSA_ROT // 2
    x = x_ref[...]
    h = _rms(x, g_ref[...]).astype(BF16)
    p = _dot(h, watt_ref[...])
    lane = lax.broadcasted_iota(jnp.int32, (tm, LANES), 1)
    low = lane < NSA_DIM
    ang_t = rc_ref[:, 0:1] * pos_ref[...].astype(F32)
    cos_t, sin_t = jnp.cos(ang_t), jnp.sin(ang_t)
    cs = jnp.concatenate([cos_t, sin_t, jnp.zeros((LANES - 2 * ROPE_ROWS, tm), F32)], axis=0).T
    cos_m = _lanes(cs, lane, 1.0, [(MLA_NOPE, hm, 0), (MLA_NOPE + hm, hm, 0)])
    sup_m = _lanes(cs, lane, 0.0, [(MLA_NOPE + hm, hm, ROPE_ROWS)])
    sdn_m = -_lanes(cs, lane, 0.0, [(MLA_NOPE, hm, ROPE_ROWS)])
    cos_n = _lanes(cs, lane, 1.0, [(0, hn, hm), (hn, hn, hm), (NSA_DIM, hn, hm), (NSA_DIM + hn, hn, hm)])
    sup_n = _lanes(cs, lane, 0.0, [(hn, hn, ROPE_ROWS + hm), (NSA_DIM + hn, hn, ROPE_ROWS + hm)])
    sdn_n = -_lanes(cs, lane, 0.0, [(0, hn, ROPE_ROWS + hm), (NSA_DIM, hn, ROPE_ROWS + hm)])
    ones_rows = jnp.where(lax.broadcasted_iota(jnp.int32, (SUM_ROWS, tm), 0) == 0, 1.0, 0.0)

    def with_sum_rows(vt):
        return jnp.concatenate([vt, ones_rows], axis=0).astype(BF16)

    def halves(slab):
        return (jnp.where(low, slab, 0.0), jnp.where(low, pltpu.roll(slab, NSA_DIM, axis=1), 0.0))

    def rope_rows(x1, x2, c, s_):
        return x1 * c - x2 * s_, x2 * c + x1 * s_

    cqn_t = _rms(p[:, 0:384], qg_ref[...]).T.astype(BF16)
    qt = _dot(wuqt_ref[...], cqn_t)
    for hd in range(MLA_HEADS):
        r0 = hd * LANES
        y1, y2 = rope_rows(qt[r0 + MLA_NOPE:r0 + MLA_NOPE + hm], qt[r0 + MLA_NOPE + hm:r0 + MLA_NOPE + 2 * hm],
                           cos_t[0:hm], sin_t[0:hm])
        slab_t = jnp.concatenate([qt[r0:r0 + MLA_NOPE], y1, y2, qt[r0 + MLA_NOPE + 2 * hm:r0 + LANES]], axis=0)
        qm_ref[0, hd] = (slab_t * mla_scale).astype(BF16)
    ckvn = _rms(p[:, 384:640], kvg_ref[...])
    kv = _dot(ckvn.astype(BF16), wukv_ref[...])
    kr = _rope(p[:, 640:768], cos_m, sup_m, sdn_m, hm)
    vt = _dot(wvt_ref[...], ckvn.T.astype(BF16))
    for hd in range(MLA_HEADS):
        km_ref[0, hd] = jnp.where(lane < MLA_NOPE, kv[:, hd * LANES:(hd + 1) * LANES], kr).astype(BF16)
        vmt_ref[0, hd] = with_sum_rows(vt[hd * MLA_V:(hd + 1) * MLA_V])
    zpad = jnp.zeros((LANES - NSA_DIM, tm), BF16)
    for pr in range(NSA_HEADS // 2):
        slab_t = p[:, 768 + pr * LANES:768 + (pr + 1) * LANES].T
        for half in range(2):
            r0 = half * NSA_DIM
            y1, y2 = rope_rows(slab_t[r0:r0 + hn], slab_t[r0 + hn:r0 + 2 * hn],
                               cos_t[hm:hm + hn], sin_t[hm:hm + hn])
            head_t = jnp.concatenate([y1, y2, slab_t[r0 + 2 * hn:r0 + NSA_DIM]], axis=0) * nsa_scale
            nq_ref[0, 2 * pr + half, 0:NSA_DIM, :] = head_t.astype(BF16)
            nq_ref[0, 2 * pr + half, NSA_DIM:LANES, :] = zpad
    nsa = lambda idx: p[:, 1280 + idx * LANES:1280 + (idx + 1) * LANES]
    def store_chunks(slab, stage_ref, out_ref):
        stage_ref[...] = slab
        nrow = tm // CMP_STRIDE
        first = lax.broadcasted_iota(jnp.int32, (nrow, LANES), 1) < NSA_DIM
        for r in range(0, CMP_STRIDE, 2):
            a = stage_ref[pl.ds(r, nrow, stride=CMP_STRIDE), :]
            b = stage_ref[pl.ds(r + 1, nrow, stride=CMP_STRIDE), :]
            cols = slice(r * NSA_DIM, (r + 2) * NSA_DIM)
            out_ref[0, :, cols] = jnp.where(first, a, pltpu.roll(b, NSA_DIM, axis=1)).astype(BF16)
            out_ref[1, :, cols] = jnp.where(first, pltpu.roll(a, NSA_DIM, axis=1), b).astype(BF16)

    store_chunks(_rope(nsa(0), cos_n, sup_n, sdn_n, hn), kstage_ref, kc_ref)
    store_chunks(nsa(1), vstage_ref, vc_ref)
    t_seq = (pl.program_id(0) % tiles_per_seq) * tm + lax.broadcasted_iota(jnp.int32, (tm, LANES), 0)
    slot = NSA_DIM + (lax.shift_right_logical(t_seq, int(math.log2(SEL_LEN))) & (SEL_SLOTS // 2 - 1))
    onehot = jnp.where(lane == slot, 1.0, 0.0)
    for g, kg in enumerate(halves(_rope(nsa(2), cos_n, sup_n, sdn_n, hn))):
        ksa_ref[0, g] = (kg + onehot).astype(BF16)
    vst = nsa(3).T
    vst_ref[0, 0] = with_sum_rows(vst[:NSA_DIM])
    vst_ref[0, 1] = with_sum_rows(vst[NSA_DIM:])
    kws = _rope(nsa(4), cos_n, sup_n, sdn_n, hn)
    kw_ref[0] = kws[:, :NSA_DIM].astype(BF16)
    kw_ref[1] = pltpu.roll(kws, NSA_DIM, axis=1)[:, :NSA_DIM].astype(BF16)
    vwt = nsa(5).T
    vwt_ref[0] = with_sum_rows(vwt[:NSA_DIM])
    vwt_ref[1] = with_sum_rows(vwt[NSA_DIM:])
    gl_ref[...] = p[:, 2048:2176]


def _rope_consts():
    rc = np.zeros((ROPE_ROWS, LANES), np.float32)
    hm = MLA_ROPE // 2
    rc[:hm] = (ROPE_THETA ** (-np.arange(hm, dtype=np.float32) / hm))[:, None]
    hn = NSA_ROT // 2
    rc[hm:hm + hn] = (ROPE_THETA ** (-np.arange(hn, dtype=np.float32) / hn))[:, None]
    return rc


def _project(x2, pos2, g_pre, w_att, q_g, w_uq_t, kv_g, w_ukv, w_v_t, rc, S, tm):
    T, D = x2.shape
    n_att = w_att.shape[1]
    tps = S // tm
    B = T // S
    rows = lambda n, d: (jax.ShapeDtypeStruct((n, T, d), BF16), pl.BlockSpec((n, tm, d), lambda i: (0, i, 0)))
    cols = lambda n, d: (jax.ShapeDtypeStruct((n, d, T), BF16), pl.BlockSpec((n, d, tm), lambda i: (0, 0, i)))
    brows = lambda n, d: (jax.ShapeDtypeStruct((B, n, S, d), BF16),
                          pl.BlockSpec((1, n, tm, d), lambda i: (i // tps, 0, i % tps, 0)))
    bcols = lambda n, d: (jax.ShapeDtypeStruct((B, n, d, S), BF16),
                          pl.BlockSpec((1, n, d, tm), lambda i: (i // tps, 0, 0, i % tps)))
    chunked = (jax.ShapeDtypeStruct((NSA_GROUPS, T // CMP_STRIDE, CMP_STRIDE * NSA_DIM), BF16),
               pl.BlockSpec((NSA_GROUPS, tm // CMP_STRIDE, CMP_STRIDE * NSA_DIM), lambda i: (0, i, 0)))
    outs = [bcols(MLA_HEADS, LANES), brows(MLA_HEADS, LANES), bcols(MLA_HEADS, MLA_V + SUM_ROWS),
            bcols(NSA_HEADS, LANES), chunked, chunked,
            brows(NSA_GROUPS, LANES), bcols(NSA_GROUPS, NSA_DIM + SUM_ROWS), rows(NSA_GROUPS, NSA_DIM),
            cols(NSA_GROUPS, NSA_DIM + SUM_ROWS),
            (jax.ShapeDtypeStruct((T, LANES), F32), pl.BlockSpec((tm, LANES), lambda i: (i, 0)))]
    kern = functools.partial(_proj_kernel,
                             mla_scale=float((MLA_NOPE + MLA_ROPE) ** -0.5 * LOG2E),
                             nsa_scale=float(NSA_DIM ** -0.5 * LOG2E),
                             tiles_per_seq=S // tm)
    return pl.pallas_call(
        kern,
        grid=(T // tm,),
        in_specs=[
            pl.BlockSpec((tm, D), lambda i: (i, 0)),
            pl.BlockSpec((1, tm), lambda i: (0, i)),
            _const_spec((1, D)),
            _const_spec((D, n_att)),
            _const_spec((1, q_g.shape[1])),
            _const_spec(w_uq_t.shape),
            _const_spec((1, kv_g.shape[1])),
            _const_spec(w_ukv.shape),
            _const_spec(w_v_t.shape),
            _const_spec((ROPE_ROWS, LANES)),
        ],
        out_specs=[o[1] for o in outs],
        out_shape=[o[0] for o in outs],
        scratch_shapes=[pltpu.VMEM((tm, LANES), F32), pltpu.VMEM((tm, LANES), F32)],
        compiler_params=_cparams(("parallel",)),
        name="proj",
    )(x2, pos2, g_pre, w_att, q_g, w_uq_t, kv_g, w_ukv, w_v_t, rc)


def _compress_kernel(xk_ref, xv_ref, pk_ref, pv_ref, wk_ref, wv_ref, ok_ref, ovt_ref):
    n = xk_ref.shape[1]
    vboth = None
    for g in range(NSA_GROUPS):
        xk = xk_ref[g].astype(F32)
        lo = _dot((xk + pk_ref[0:1]).astype(BF16), wk_ref[0])
        hi = _dot((xk + pk_ref[1:2]).astype(BF16), wk_ref[1])
        ok_ref[g] = (lo + pltpu.roll(hi, n - 1, axis=0)).astype(BF16)
        xv = xv_ref[g].astype(F32)
        lo = _dot((xv + pv_ref[0:1]).astype(BF16), wv_ref[g, 0])
        hi = _dot((xv + pv_ref[1:2]).astype(BF16), wv_ref[g, 1])
        vg = lo + pltpu.roll(hi, n - 1, axis=0)
        vboth = vg if vboth is None else vboth + vg
    ovt_ref[0] = vboth.T.astype(BF16)


def _compress(xk, xv, pk, pv, wk, wv, B):
    G, TC, CW = xk.shape
    nch = TC // B
    xspec = pl.BlockSpec((G, nch, CW), lambda b: (0, b, 0))
    return pl.pallas_call(
        _compress_kernel,
        grid=(B,),
        in_specs=[xspec, xspec, _const_spec(pk.shape), _const_spec(pv.shape),
                  _const_spec(wk.shape), _const_spec(wv.shape)],
        out_specs=[pl.BlockSpec((G, nch, NSA_DIM), lambda b: (0, b, 0)),
                   pl.BlockSpec((1, G * NSA_DIM, nch), lambda b: (b, 0, 0))],
        out_shape=[jax.ShapeDtypeStruct((G, TC, NSA_DIM), BF16),
                   jax.ShapeDtypeStruct((B, G * NSA_DIM, nch), BF16)],
        compiler_params=_cparams(("parallel",)),
        name="compress",
    )(xk, xv, pk, pv, wk, wv)


def _cmp_select_kernel(q_ref, kc_ref, vct_ref, ovt_ref, o_ref, bias_ref, *, tq, n_pick, nqt, n_var):
    i = pl.program_id(1)
    q0 = i * tq
    nch_all = kc_ref.shape[1]
    nblk_all = ovt_ref.shape[0]
    tiles_per_var = nqt // n_var

    def variant(nch, nblk):
        t_c = q0 + lax.broadcasted_iota(jnp.int32, (nch, tq), 1)
        n_c = lax.broadcasted_iota(jnp.int32, (nch, tq), 0)
        cmask = (n_c * CMP_STRIDE + (CMP_LEN - 1)) <= t_c
        t_b = q0 + lax.broadcasted_iota(jnp.int32, (nblk, tq), 1)
        blk = lax.broadcasted_iota(jnp.int32, (nblk, tq), 0)
        blk_f = blk.astype(F32)
        cur = lax.shift_right_logical(t_b, int(math.log2(SEL_LEN)))
        forced = (blk == 0) | (blk == cur) | (blk == cur - 1)
        causal = blk * SEL_LEN <= t_b
        outs = []
        for g in range(NSA_GROUPS):
            psum = None
            for j in range(NSA_HPG):
                qt = q_ref[0, g * NSA_HPG + j][:NSA_DIM, :]
                s = jnp.where(cmask, _dot(kc_ref[g, :nch, :], qt), MASKED)
                m = jnp.maximum(jnp.max(s, axis=0, keepdims=True), M_FLOOR)
                p = jnp.exp2(s - m)
                l = jnp.sum(p, axis=0, keepdims=True)
                pn = p * jnp.where(l > 0.0, 1.0 / l, 0.0)
                outs.append(_dot(vct_ref[0, g * NSA_DIM:(g + 1) * NSA_DIM, :nch], pn.astype(BF16)))
                psum = pn if psum is None else psum + pn
            imp = None
            for piece in _split_bf16(psum, 3):
                d = _dot(ovt_ref[:nblk, :nch], piece)
                imp = d if imp is None else imp + d
            v0 = jnp.where(forced | jnp.logical_not(causal), -1.0, imp)
            v = v0
            for _ in range(n_pick):
                mx = jnp.max(v, axis=0, keepdims=True)
                first = jnp.min(jnp.where(v == mx, blk_f, float(nblk)), axis=0, keepdims=True)
                v = jnp.where(blk_f == first, -2.0, v)
            sel = forced | ((v == -2.0) & (v0 >= 0.0))
            bias = jnp.where(sel, 0.0, SEL_BIAS)
            per_tile = SEL_SLOTS // 2
            pad = jnp.zeros((SEL_SLOTS - per_tile, tq), F32)
            pieces = []
            for kt in range(nblk_all // per_tile):
                if kt * per_tile < nblk:
                    pieces += [bias[kt * per_tile:(kt + 1) * per_tile], pad]
                else:
                    pieces += [jnp.full((per_tile, tq), SEL_BIAS, F32), pad]
            bias_ref[0, g] = jnp.concatenate(pieces, axis=0).astype(BF16)
        o_ref[...] = jnp.concatenate(outs, axis=0).T.astype(BF16)

    for k in range(n_var):
        @pl.when((i >= k * tiles_per_var) & (i < (k + 1) * tiles_per_var))
        def _():
            variant(nch_all * (k + 1) // n_var, nblk_all * (k + 1) // n_var)


def _cmp_select(nq, kc, vct, ovt, B, S, tq):
    T = B * S
    nch = kc.shape[1] // B
    nblk = ovt.shape[0]
    nqt = S // tq
    n_var = 4 if nqt % 4 == 0 and (nch // 4) % LANES == 0 and (nblk // 4) % SUBLANES == 0 else 1
    kern = functools.partial(_cmp_select_kernel, tq=tq, n_pick=N_SEL - 3, nqt=nqt, n_var=n_var)
    return pl.pallas_call(
        kern,
        grid=(B, nqt),
        in_specs=[
            pl.BlockSpec((1, NSA_HEADS, LANES, tq), lambda b, i: (b, 0, 0, i)),
            pl.BlockSpec((NSA_GROUPS, nch, NSA_DIM), lambda b, i: (0, b, 0)),
            pl.BlockSpec((1, NSA_GROUPS * NSA_DIM, nch), lambda b, i: (b, 0, 0)),
            _const_spec(ovt.shape),
        ],
        out_specs=[pl.BlockSpec((tq, NSA_HEADS * NSA_DIM), lambda b, i: (b * nqt + i, 0)),
                   pl.BlockSpec((1, NSA_GROUPS, 2 * nblk, tq), lambda b, i: (b, 0, 0, i))],
        out_shape=[jax.ShapeDtypeStruct((T, NSA_HEADS * NSA_DIM), BF16),
                   jax.ShapeDtypeStruct((B, NSA_GROUPS, 2 * nblk, S), BF16)],
        compiler_params=_cparams(("parallel", "parallel")),
        name="cmp_select",
    )(nq, kc, vct, ovt)


def _tile_parts(kind, tq):
    if kind == "full":
        return [(slice(0, tq), slice(0, tq), None)]
    hq = tq // 2

    def mask(rows, cols, below):
        key = rows.start + lax.broadcasted_iota(jnp.int32, (rows.stop - rows.start, hq), 0)
        qry = cols.start + lax.broadcasted_iota(jnp.int32, (rows.stop - rows.start, hq), 1)
        return (key <= qry) if below else (key > qry)

    lo, hi, all_ = slice(0, hq), slice(hq, tq), slice(0, tq)
    if kind == "diag":
        return [(lo, lo, mask(lo, lo, True)), (all_, hi, mask(all_, hi, True))]
    return [(all_, lo, mask(all_, lo, False)), (hi, hi, mask(hi, hi, False))]


def _flash_heads(nheads, kqv, parts, src, dst, lazy=False, unroll=True):
    def scores(hd, part):
        rows, cols, _ = part
        k, qt, _ = kqv(hd)
        return _dot(k[rows], qt[:, cols])

    def update(hd, part, s):
        rows, cols, mask = part
        if mask is not None:
            s = jnp.where(mask, s, MASKED)
        tmax = jnp.max(s, axis=0, keepdims=True)
        vt = kqv(hd)[2][:, rows]
        rise = None
        if src is None:
            m_new = jnp.maximum(tmax, M_FLOOR)
            acc = _dot(vt, jnp.exp2(s - m_new).astype(BF16))
        else:
            m_old = src[0][hd, :, cols]
            m_new = jnp.maximum(m_old, tmax)
            alpha = jnp.exp2(m_old - m_new)
            if lazy:
                acc = (src[1][hd, :, cols] + _dot(vt, jnp.exp2(s - m_old).astype(BF16))) * alpha
                rise = tmax - m_old
            else:
                acc = alpha * src[1][hd, :, cols] + _dot(vt, jnp.exp2(s - m_new).astype(BF16))
        dst[0][hd, :, cols] = m_new
        dst[1][hd, :, cols] = acc
        return rise

    if not unroll:
        def body(hd, carry):
            for part in parts:
                update(hd, part, scores(hd, part))
            return carry
        lax.fori_loop(0, nheads, body, 0)
        return None
    todo = [(hd, part) for hd in range(nheads) for part in parts]
    ahead = 2
    pending = [scores(*t) for t in todo[:ahead]]
    worst = {}
    for n, (hd, part) in enumerate(todo):
        s = pending.pop(0)
        if n + ahead < len(todo):
            pending.append(scores(*todo[n + ahead]))
        rise = update(hd, part, s)
        if rise is not None:
            key = rise.shape
            worst[key] = rise if key not in worst else jnp.maximum(worst[key], rise)
    if not worst:
        return None
    tops = [jnp.max(w, axis=1, keepdims=True) for w in worst.values()]
    return functools.reduce(jnp.maximum, tops)


def _flash_finalize(o_ref, acc_ref):
    nv, rows, tq = acc_ref.shape
    dv = rows - SUM_ROWS
    groups = o_ref.shape[0] if len(o_ref.shape) == 3 else 1
    nh = nv // groups
    for gb in range(groups):
        sl = slice(gb * nh, (gb + 1) * nh)
        o = acc_ref[sl, :dv, :] * (1.0 / acc_ref[sl, dv:dv + 1, :])
        o = o.reshape(nh * dv, tq).T.astype(BF16)
        if len(o_ref.shape) == 3:
            o_ref[gb] = o
        else:
            o_ref[...] = o


def _flash_scratch(nh, dv, tq, slots=None):
    lead = (nh,) if slots is None else (slots, nh)
    return [pltpu.VMEM(lead + (1, tq), F32), pltpu.VMEM(lead + (dv + SUM_ROWS, tq), F32)]


def _causal_pairs(nqt):
    pairs = [(i, j) for i in range(nqt) for j in [i] + list(range(i))]
    return (jnp.asarray([p[0] for p in pairs], jnp.int32), jnp.asarray([p[1] for p in pairs], jnp.int32))


def _lazy_flash_step(nheads, kqv, parts, m_ref, acc_ref, cur):
    src = (m_ref.at[cur], acc_ref.at[cur])
    dst = (m_ref.at[1 - cur], acc_ref.at[1 - cur])
    worst = _flash_heads(nheads, kqv, parts, src, dst, lazy=True)

    @pl.when(jnp.max(worst) > LAZY_SPAN)
    def _():
        _flash_heads(nheads, kqv, parts, src, dst, unroll=False)


def _causal_flash_step(i, j, nheads, kqv, tq, o_ref, m_ref, acc_ref, slot_ref):
    @pl.when(j == i)
    def _():
        _flash_heads(nheads, kqv, _tile_parts("diag", tq), None, (m_ref.at[0], acc_ref.at[0]))
        slot_ref[0] = 0

    @pl.when(j < i)
    def _():
        cur = slot_ref[0]
        _lazy_flash_step(nheads, kqv, _tile_parts("full", tq), m_ref, acc_ref, cur)
        slot_ref[0] = 1 - cur

    @pl.when((j == i - 1) | (i == 0))
    def _():
        _flash_finalize(o_ref, acc_ref.at[slot_ref[0]])


def _split_virtual(v, nheads):
    if isinstance(v, int):
        return v // nheads, v % nheads
    return lax.div(v, nheads), lax.rem(v, nheads)


def _mla_kernel(qi_ref, ki_ref, q_ref, k_ref, vt_ref, o_ref, m_ref, acc_ref, slot_ref, *, tq):
    i, j = qi_ref[pl.program_id(0)], ki_ref[pl.program_id(0)]

    def kqv(v):
        bb, hd = _split_virtual(v, MLA_HEADS)
        return k_ref[bb, hd], q_ref[bb, hd], vt_ref[bb, hd]

    _causal_flash_step(i, j, q_ref.shape[0] * MLA_HEADS, kqv, tq, o_ref, m_ref, acc_ref, slot_ref)


def _mla_attention(qm, km, vmt, B, S, tq):
    nqt = S // tq
    kern = functools.partial(_mla_kernel, tq=tq)
    qi, ki = _causal_pairs(nqt)
    return pl.pallas_call(
        kern,
        grid_spec=pltpu.PrefetchScalarGridSpec(
            num_scalar_prefetch=2,
            grid=(qi.shape[0],),
            in_specs=[pl.BlockSpec((B, MLA_HEADS, LANES, tq), lambda p, qi, ki: (0, 0, 0, qi[p])),
                      pl.BlockSpec((B, MLA_HEADS, tq, LANES), lambda p, qi, ki: (0, 0, ki[p], 0)),
                      pl.BlockSpec((B, MLA_HEADS, MLA_V + SUM_ROWS, tq), lambda p, qi, ki: (0, 0, 0, ki[p]))],
            out_specs=pl.BlockSpec((B, tq, MLA_HEADS * MLA_V), lambda p, qi, ki: (0, qi[p], 0)),
            scratch_shapes=_flash_scratch(B * MLA_HEADS, MLA_V, tq, slots=2) + [pltpu.SMEM((1,), jnp.int32)]),
        out_shape=jax.ShapeDtypeStruct((B, S, MLA_HEADS * MLA_V), BF16),
        compiler_params=_cparams(("arbitrary",)),
        name="mla_attn",
    )(qi, ki, qm, km, vmt).reshape(B * S, MLA_HEADS * MLA_V)


def _slc_kernel(qi_ref, ki_ref, q_ref, bias_ref, k_ref, vt_ref, o_ref, qa_ref, m_ref, acc_ref,
                slot_ref, *, tq):
    i, j = qi_ref[pl.program_id(0)], ki_ref[pl.program_id(0)]
    nb = q_ref.shape[0]

    @pl.when(j == i)
    def _():
        for bb in range(nb):
            for hd in range(NSA_HEADS):
                qa_ref[bb * NSA_HEADS + hd] = q_ref[bb, hd]

    rows = pl.ds(pl.multiple_of(j * SEL_SLOTS, SEL_SLOTS), SEL_SLOTS)
    for bb in range(nb):
        for g in range(NSA_GROUPS):
            tile_bias = bias_ref[bb, g, rows, :]
            for hd in range(g * NSA_HPG, (g + 1) * NSA_HPG):
                qa_ref[bb * NSA_HEADS + hd, NSA_DIM:NSA_DIM + SEL_SLOTS, :] = tile_bias

    def kqv(v):
        bb, hd = _split_virtual(v, NSA_HEADS)
        g = hd // NSA_HPG if isinstance(hd, int) else lax.div(hd, NSA_HPG)
        return k_ref[bb, g], qa_ref[v], vt_ref[bb, g]

    _causal_flash_step(i, j, nb * NSA_HEADS, kqv, tq, o_ref, m_ref, acc_ref, slot_ref)


def _slc_attention(nq, bias, ksa, vst, B, S, tq):
    nqt = S // tq
    nrow = bias.shape[2]
    assert tq // SEL_LEN == SEL_SLOTS // 2
    kern = functools.partial(_slc_kernel, tq=tq)
    qi, ki = _causal_pairs(nqt)
    qmap = lambda p, qi, ki: (0, 0, 0, qi[p])
    return pl.pallas_call(
        kern,
        grid_spec=pltpu.PrefetchScalarGridSpec(
            num_scalar_prefetch=2,
            grid=(qi.shape[0],),
            in_specs=[pl.BlockSpec((B, NSA_HEADS, LANES, tq), qmap),
                      pl.BlockSpec((B, NSA_GROUPS, nrow, tq), qmap),
                      pl.BlockSpec((B, NSA_GROUPS, tq, LANES), lambda p, qi, ki: (0, 0, ki[p], 0)),
                      pl.BlockSpec((B, NSA_GROUPS, NSA_DIM + SUM_ROWS, tq), lambda p, qi, ki: (0, 0, 0, ki[p]))],
            out_specs=pl.BlockSpec((B, tq, NSA_HEADS * NSA_DIM), lambda p, qi, ki: (0, qi[p], 0)),
            scratch_shapes=[pltpu.VMEM((B * NSA_HEADS, LANES, tq), BF16)]
                           + _flash_scratch(B * NSA_HEADS, NSA_DIM, tq, slots=2)
                           + [pltpu.SMEM((1,), jnp.int32)]),
        out_shape=jax.ShapeDtypeStruct((B, S, NSA_HEADS * NSA_DIM), BF16),
        compiler_params=_cparams(("arbitrary",)),
        name="slc_attn",
    )(qi, ki, nq, bias, ksa, vst).reshape(B * S, NSA_HEADS * NSA_DIM)


def _win_kernel(q_ref, k_ref, vt_ref, o_ref, m_ref, acc_ref, *, tq):
    i, j = pl.program_id(1), pl.program_id(2)

    def kqv(hd):
        g = hd // NSA_HPG if isinstance(hd, int) else lax.div(hd, NSA_HPG)
        return k_ref[g], q_ref[0, hd][:NSA_DIM, :], vt_ref[g]

    @pl.when(j == 0)
    def _():
        _flash_heads(NSA_HEADS, kqv, _tile_parts("diag", tq), None, (m_ref.at[0], acc_ref.at[0]))

    @pl.when((j == 1) & (i > 0))
    def _():
        _lazy_flash_step(NSA_HEADS, kqv, _tile_parts("prev", tq), m_ref, acc_ref, 0)

    @pl.when(j == 1)
    def _():
        _flash_finalize(o_ref, acc_ref.at[jnp.where(i > 0, 1, 0)])


def _win_attention(nq, kw, vwt, B, S, tq):
    T = B * S
    nqt = S // tq
    kern = functools.partial(_win_kernel, tq=tq)
    kidx = lambda b, i, j: b * nqt + jnp.maximum(i - j, 0)
    return pl.pallas_call(
        kern,
        grid=(B, nqt, 2),
        in_specs=[pl.BlockSpec((1, NSA_HEADS, LANES, tq), lambda b, i, j: (b, 0, 0, i)),
                  pl.BlockSpec((NSA_GROUPS, tq, NSA_DIM), lambda b, i, j: (0, kidx(b, i, j), 0)),
                  pl.BlockSpec((NSA_GROUPS, NSA_DIM + SUM_ROWS, tq), lambda b, i, j: (0, 0, kidx(b, i, j)))],
        out_specs=pl.BlockSpec((tq, NSA_HEADS * NSA_DIM), lambda b, i, j: (b * nqt + i, 0)),
        out_shape=jax.ShapeDtypeStruct((T, NSA_HEADS * NSA_DIM), BF16),
        scratch_shapes=_flash_scratch(NSA_HEADS, NSA_DIM, tq, slots=2),
        compiler_params=_cparams(("parallel", "parallel", "arbitrary")),
        name="win_attn",
    )(nq, kw, vwt)


def _causal_conv(u, carry, w_ref):
    row = lax.broadcasted_iota(jnp.int32, u.shape, 0)
    c1 = carry[SUBLANES - 1:SUBLANES]
    c2 = carry[SUBLANES - 2:SUBLANES - 1]
    u1 = jnp.where(row == 0, c1, pltpu.roll(u, 1, axis=0))
    u2 = jnp.where(row == 0, c2, jnp.where(row == 1, c1, pltpu.roll(u, 2, axis=0)))
    return w_ref[0:1] * u2 + w_ref[1:2] * u1 + w_ref[2:3] * u


def _merge_kernel(x_ref, g_ref, wgc_ref, cw_ref, ym_ref, oc_ref, os_ref, ow_ref, gl_ref, ex_ref,
                  wbc_ref, wbm_ref, wbn_ref, wo_ref, gp_ref, out_ref, carry_ref, *, tiles_per_seq):
    D = x_ref.shape[1]
    DC = wbc_ref.shape[0]

    @pl.when(pl.program_id(0) % tiles_per_seq == 0)
    def _():
        carry_ref[...] = jnp.zeros(carry_ref.shape, F32)

    x = x_ref[...]
    h = _rms(x, g_ref[...]).astype(BF16)
    p = _dot(h, wgc_ref[...])
    u = p[:, 3 * D + DC:3 * D + 2 * DC] * p[:, 3 * D + 2 * DC:3 * D + 3 * DC]
    y_conv = p[:, 3 * D:3 * D + DC] * _causal_conv(u, carry_ref[...], cw_ref)
    carry_ref[...] = u[u.shape[0] - SUBLANES:]
    hi, lo = _split_bf16(gl_ref[...], 2)
    gate = jax.nn.sigmoid(_dot(hi, ex_ref[...]) + _dot(lo, ex_ref[...]))
    HD = oc_ref.shape[1]
    y_nsa = (gate[:, 0:HD] * oc_ref[...].astype(F32) + gate[:, HD:2 * HD] * os_ref[...].astype(F32)
             + gate[:, 2 * HD:3 * HD] * ow_ref[...].astype(F32))
    merged = (jax.nn.sigmoid(p[:, 0:D]) * _dot(y_conv.astype(BF16), wbc_ref[...])
              + jax.nn.sigmoid(p[:, D:2 * D]) * _dot(ym_ref[...], wbm_ref[...])
              + jax.nn.sigmoid(p[:, 2 * D:3 * D]) * _dot(y_nsa.astype(BF16), wbn_ref[...]))
    out_ref[...] = x + _rms(_dot(merged.astype(BF16), wo_ref[...]), gp_ref[...])


def _merge(x2, g_pre, w_gc, conv_w, y_mla, o_cmp, o_slc, o_win, gl, ex, wbc, wbm, wbn, wo, g_post,
           S, tm):
    T, D = x2.shape
    HD = y_mla.shape[1]
    row = lambda w: pl.BlockSpec((tm, w), lambda i: (i, 0))
    kern = functools.partial(_merge_kernel, tiles_per_seq=S // tm)
    return pl.pallas_call(
        kern,
        grid=(T // tm,),
        in_specs=[row(D), _const_spec((1, D)), _const_spec(w_gc.shape), _const_spec(conv_w.shape),
                  row(HD), row(HD), row(HD), row(HD), row(LANES), _const_spec(ex.shape),
                  _const_spec(wbc.shape), _const_spec(wbm.shape), _const_spec(wbn.shape),
                  _const_spec(wo.shape), _const_spec((1, D))],
        out_specs=row(D),
        out_shape=jax.ShapeDtypeStruct((T, D), F32),
        scratch_shapes=[pltpu.VMEM((SUBLANES, wbc.shape[0]), F32)],
        compiler_params=_cparams(("arbitrary",)),
        name="merge",
    )(x2, g_pre, w_gc, conv_w, y_mla, o_cmp, o_slc, o_win, gl, ex, wbc, wbm, wbn, wo, g_post)


def _ffn_kernel(x_ref, g_ref, wup_ref, cw_ref, cb_ref, wdn_ref, gp_ref, out_ref, carry_ref,
                *, tiles_per_seq, chunk):
    F = wdn_ref.shape[0]

    @pl.when(pl.program_id(0) % tiles_per_seq == 0)
    def _():
        carry_ref[...] = jnp.zeros(carry_ref.shape, F32)

    x = x_ref[...]
    h = _rms(x, g_ref[...]).astype(BF16)
    acc = None
    for c0 in range(0, F, chunk):
        a = _dot(h, wup_ref[:, c0:c0 + chunk])
        b = _dot(h, wup_ref[:, F + c0:F + c0 + chunk])
        ac = _causal_conv(a, carry_ref[:, c0:c0 + chunk], cw_ref.at[:, c0:c0 + chunk])
        ac = ac + cb_ref[:, c0:c0 + chunk]
        carry_ref[:, c0:c0 + chunk] = a[a.shape[0] - SUBLANES:]
        d = _dot((jax.nn.gelu(ac) * b).astype(BF16), wdn_ref[c0:c0 + chunk, :])
        acc = d if acc is None else acc + d
    out_ref[...] = x + _rms(acc, gp_ref[...])


def _ffn(x2, g_pre, w_up, conv_w, conv_b, w_dn, g_post, S, tm, chunk):
    T, D = x2.shape
    F = w_dn.shape[0]
    row = pl.BlockSpec((tm, D), lambda i: (i, 0))
    kern = functools.partial(_ffn_kernel, tiles_per_seq=S // tm, chunk=chunk)
    return pl.pallas_call(
        kern,
        grid=(T // tm,),
        in_specs=[row, _const_spec((1, D)), _const_spec(w_up.shape), _const_spec(conv_w.shape),
                  _const_spec((1, F)), _const_spec(w_dn.shape), _const_spec((1, D))],
        out_specs=row,
        out_shape=jax.ShapeDtypeStruct((T, D), F32),
        scratch_shapes=[pltpu.VMEM((SUBLANES, F), F32)],
        compiler_params=_cparams(("arbitrary",)),
        name="ffn",
    )(x2, g_pre, w_up, conv_w, conv_b, w_dn, g_post)


def _overlap_matrix_t(S):
    nch = S // CMP_STRIDE
    n_cmp = (S - CMP_LEN) // CMP_STRIDE + 1
    cs = np.arange(nch)[None, :] * CMP_STRIDE
    ss = np.arange(S // SEL_LEN)[:, None] * SEL_LEN
    ov = (cs <= ss + SEL_LEN - 1) & (cs + CMP_LEN - 1 >= ss) & (np.arange(nch)[None, :] < n_cmp)
    return ov.astype(np.float32)


def _gate_expand():
    ex = np.zeros((LANES, 3 * NSA_HEADS * NSA_DIM), np.float32)
    for hd in range(NSA_HEADS):
        for c in range(3):
            ex[hd * 3 + c, c * NSA_HEADS * NSA_DIM + hd * NSA_DIM:
               c * NSA_HEADS * NSA_DIM + (hd + 1) * NSA_DIM] = 1.0
    return ex


def _pad_rows(w, rows):
    return jnp.concatenate([w, jnp.zeros((rows - w.shape[0],) + w.shape[1:], w.dtype)], axis=0)


def kernel(x, positions, norm_mix_pre, norm_mix_post, w_in, conv_w, mla_q_norm, mla_w_uq, mla_kv_norm, mla_w_ukv, nsa_cmp_pos_k, nsa_cmp_pos_v, nsa_cmp_w_k, nsa_cmp_w_v, w_branch_conv, w_branch_mla, w_branch_nsa, w_out, norm_ffn_pre, norm_ffn_post, ffn_w_up, ffn_conv_w, ffn_conv_b, ffn_w_down):
    B, S, D = x.shape
    T = B * S
    depth = w_in.shape[0]
    DC = conv_w.shape[2]
    F = ffn_w_down.shape[1]
    QL = mla_q_norm.shape[1]
    KL = mla_kv_norm.shape[1]
    tq = 512
    assert S % tq == 0 and WINDOW == tq and N_SEL <= S // SEL_LEN <= LANES
    tm_proj, tm_merge, tm_ffn, ffn_chunk = 512, 512, 512, 1408
    assert F % ffn_chunk == 0 and ffn_chunk % LANES == 0

    x2 = x.reshape(T, D)
    pos2 = positions.reshape(1, T)
    rc = jnp.asarray(_rope_consts())
    ovt = jnp.asarray(_overlap_matrix_t(S), BF16)
    ex = jnp.asarray(_gate_expand(), BF16)
    half = CMP_LEN // 2 * NSA_DIM
    o_att = 3 * D + 3 * DC

    for l in range(depth):
        w = w_in[l]
        o_kr = o_att + QL + KL
        z = lambda n: jnp.zeros((D, n), F32)
        w_att = jnp.concatenate(
            [w[:, o_att:o_kr], z(MLA_NOPE), w[:, o_kr:o_kr + MLA_ROPE], z(LANES - MLA_NOPE - MLA_ROPE),
             w[:, o_kr + MLA_ROPE:o_kr + MLA_ROPE + NSA_HEADS * NSA_DIM + 6 * NSA_GROUPS * NSA_DIM],
             w[:, w.shape[1] - 3 * NSA_HEADS:], z(LANES - 3 * NSA_HEADS)], axis=1).astype(BF16)
        dq = MLA_NOPE + MLA_ROPE
        w_uq = jnp.pad(mla_w_uq[l].reshape(QL, MLA_HEADS, dq),
                       ((0, 0), (0, 0), (0, LANES - dq))).reshape(QL, MLA_HEADS * LANES).astype(BF16)
        w_v_t = mla_w_ukv[l].reshape(KL, MLA_HEADS, MLA_NOPE + MLA_V)[:, :, MLA_NOPE:].reshape(
            KL, MLA_HEADS * MLA_V).T.astype(BF16)
        (qm, km, vmt, nq, kc_in, vc_in, ksa, vst, kw, vwt, gl) = _project(
            x2, pos2, norm_mix_pre[l][None], w_att, mla_q_norm[l][None], w_uq.T,
            mla_kv_norm[l][None], mla_w_ukv[l].astype(BF16), w_v_t, rc, S, tm_proj)

        wv = nsa_cmp_w_v[l].reshape(2, half, NSA_DIM)
        zv = jnp.zeros_like(wv)
        wv_pad = jnp.stack([jnp.concatenate([wv, zv], axis=2), jnp.concatenate([zv, wv], axis=2)])
        kc, vct = _compress(kc_in, vc_in,
                            nsa_cmp_pos_k[l].reshape(2, half), nsa_cmp_pos_v[l].reshape(2, half),
                            nsa_cmp_w_k[l].reshape(2, half, NSA_DIM).astype(BF16),
                            wv_pad.astype(BF16), B)

        y_mla = _mla_attention(qm, km, vmt, B, S, tq)
        o_cmp, bias = _cmp_select(nq, kc, vct, ovt, B, S, tq)
        o_slc = _slc_attention(nq, bias, ksa, vst, B, S, tq)
        o_win = _win_attention(nq, kw, vwt, B, S, tq)

        x2 = _merge(x2, norm_mix_pre[l][None], w[:, :o_att].astype(BF16),
                    _pad_rows(conv_w[l], SUBLANES), y_mla, o_cmp, o_slc, o_win, gl, ex,
                    w_branch_conv[l].astype(BF16), w_branch_mla[l].astype(BF16),
                    w_branch_nsa[l].astype(BF16), w_out[l].astype(BF16), norm_mix_post[l][None],
                    S, tm_merge)
        x2 = _ffn(x2, norm_ffn_pre[l][None], ffn_w_up[l].astype(BF16),
                  _pad_rows(ffn_conv_w[l], SUBLANES), ffn_conv_b[l][None],
                  ffn_w_down[l].astype(BF16), norm_ffn_post[l][None], S, tm_ffn, ffn_chunk)
    return x2.reshape(B, S, D)
```

```python
import functools
import math

import numpy as np
import jax
import jax.numpy as jnp
from jax import lax
from jax.experimental import pallas as pl
from jax.experimental.pallas import tpu as pltpu

F32 = jnp.float32
BF16 = jnp.bfloat16

ROPE_THETA = 500000.0
RMS_EPS = 1e-6
CONV_WIDTH = 3
MLA_HEADS = 8
MLA_NOPE = 64
MLA_ROPE = 32
MLA_V = 64
NSA_HEADS = 8
NSA_GROUPS = 2
NSA_HPG = NSA_HEADS // NSA_GROUPS
NSA_DIM = 64
NSA_ROT = NSA_DIM // 4
CMP_LEN = 32
CMP_STRIDE = 16
SEL_LEN = 64
N_SEL = 16
WINDOW = 512

LANES = 128
SUBLANES = 8
VMEM_LIMIT = 56 * 1024 * 1024
MASKED = -1e30
M_FLOOR = -1e20
SEL_BIAS = -2.0 ** 100
LAZY_SPAN = 64.0
LOG2E = math.log2(math.e)
SEL_SLOTS = 16
ROPE_ROWS = 32
SUM_ROWS = 16


def _cparams(sem):
    return pltpu.CompilerParams(dimension_semantics=sem, vmem_limit_bytes=VMEM_LIMIT)


def _const_spec(shape):
    nd = len(shape)
    return pl.BlockSpec(shape, lambda *_: (0,) * nd, pipeline_mode=pl.Buffered(1))


def _rms(x, g):
    return x * lax.rsqrt(jnp.mean(x * x, axis=-1, keepdims=True) + RMS_EPS) * g


def _dot(a, b):
    return jnp.dot(a, b, preferred_element_type=F32)


def _split_bf16(a, terms):
    pieces, rem = [], a
    for _ in range(terms):
        piece = rem.astype(BF16)
        rem = rem - piece.astype(F32)
        pieces.append(piece)
    return pieces


def _rope(x, cos, sin_up, sin_dn, half):
    return (x * cos + pltpu.roll(x, half, axis=1) * sin_up
            + pltpu.roll(x, LANES - half, axis=1) * sin_dn)


def _lanes(table, lane, base, moves):
    out = base
    for dst, width, src in moves:
        rolled = pltpu.roll(table, (dst - src) % LANES, axis=1)
        out = jnp.where((lane >= dst) & (lane < dst + width), rolled, out)
    return out


def _proj_kernel(x_ref, pos_ref, g_ref, watt_ref, qg_ref, wuqt_ref, kvg_ref, wukv_ref, wvt_ref, rc_ref,
                 qm_ref, km_ref, vmt_ref, nq_ref, kc_ref, vc_ref, ksa_ref, vst_ref, kw_ref, vwt_ref,
                 gl_ref, kstage_ref, vstage_ref, *, mla_scale, nsa_scale, tiles_per_seq):
    tm = x_ref.shape[0]
    hm, hn = MLA_ROPE // 2, NSA_ROT // 2
    x = x_ref[...]
    h = _rms(x, g_ref[...]).astype(BF16)
    p = _dot(h, watt_ref[...])
    ql, kl = qg_ref.shape[1], kvg_ref.shape[1]
    c_kr, c_nq, c_kv, c_gl = _att_columns(ql, kl)
    lane = lax.broadcasted_iota(jnp.int32, (tm, LANES), 1)
    low = lane < NSA_DIM
    ang_t = rc_ref[:, 0:1] * pos_ref[...].astype(F32)
    cos_t, sin_t = jnp.cos(ang_t), jnp.sin(ang_t)
    cs = jnp.concatenate([cos_t, sin_t, jnp.zeros((LANES - 2 * ROPE_ROWS, tm), F32)], axis=0).T
    cos_m = _lanes(cs, lane, 1.0, [(MLA_NOPE, hm, 0), (MLA_NOPE + hm, hm, 0)])
    sup_m = _lanes(cs, lane, 0.0, [(MLA_NOPE + hm, hm, ROPE_ROWS)])
    sdn_m = -_lanes(cs, lane, 0.0, [(MLA_NOPE, hm, ROPE_ROWS)])
    cos_n = _lanes(cs, lane, 1.0, [(0, hn, hm), (hn, hn, hm), (NSA_DIM, hn, hm), (NSA_DIM + hn, hn, hm)])
    sup_n = _lanes(cs, lane, 0.0, [(hn, hn, ROPE_ROWS + hm), (NSA_DIM + hn, hn, ROPE_ROWS + hm)])
    sdn_n = -_lanes(cs, lane, 0.0, [(0, hn, ROPE_ROWS + hm), (NSA_DIM, hn, ROPE_ROWS + hm)])
    ones_rows = jnp.where(lax.broadcasted_iota(jnp.int32, (SUM_ROWS, tm), 0) == 0, 1.0, 0.0)

    def with_sum_rows(vt):
        return jnp.concatenate([vt, ones_rows], axis=0).astype(BF16)

    def halves(slab):
        return (jnp.where(low, slab, 0.0), jnp.where(low, pltpu.roll(slab, NSA_DIM, axis=1), 0.0))

    def rope_rows(x1, x2, c, s_):
        return x1 * c - x2 * s_, x2 * c + x1 * s_

    cqn_t = _rms(p[:, 0:ql], qg_ref[...]).T.astype(BF16)
    qt = _dot(wuqt_ref[...], cqn_t)
    for hd in range(MLA_HEADS):
        r0 = hd * LANES
        y1, y2 = rope_rows(qt[r0 + MLA_NOPE:r0 + MLA_NOPE + hm], qt[r0 + MLA_NOPE + hm:r0 + MLA_NOPE + 2 * hm],
                           cos_t[0:hm], sin_t[0:hm])
        slab_t = jnp.concatenate([qt[r0:r0 + MLA_NOPE], y1, y2, qt[r0 + MLA_NOPE + 2 * hm:r0 + LANES]], axis=0)
        qm_ref[0, hd] = (slab_t * mla_scale).astype(BF16)
    ckvn = _rms(p[:, ql:ql + kl], kvg_ref[...])
    kv = _dot(ckvn.astype(BF16), wukv_ref[...])
    kr = _rope(p[:, c_kr:c_kr + LANES], cos_m, sup_m, sdn_m, hm)
    vt = _dot(wvt_ref[...], ckvn.T.astype(BF16))
    for hd in range(MLA_HEADS):
        km_ref[0, hd] = jnp.where(lane < MLA_NOPE, kv[:, hd * LANES:(hd + 1) * LANES], kr).astype(BF16)
        vmt_ref[0, hd] = with_sum_rows(vt[hd * MLA_V:(hd + 1) * MLA_V])
    zpad = jnp.zeros((LANES - NSA_DIM, tm), BF16)
    for pr in range(NSA_HEADS // 2):
        slab_t = p[:, c_nq + pr * LANES:c_nq + (pr + 1) * LANES].T
        for half in range(2):
            r0 = half * NSA_DIM
            y1, y2 = rope_rows(slab_t[r0:r0 + hn], slab_t[r0 + hn:r0 + 2 * hn],
                               cos_t[hm:hm + hn], sin_t[hm:hm + hn])
            head_t = jnp.concatenate([y1, y2, slab_t[r0 + 2 * hn:r0 + NSA_DIM]], axis=0) * nsa_scale
            nq_ref[0, 2 * pr + half, 0:NSA_DIM, :] = head_t.astype(BF16)
            nq_ref[0, 2 * pr + half, NSA_DIM:LANES, :] = zpad
    nsa = lambda idx: p[:, c_kv + idx * LANES:c_kv + (idx + 1) * LANES]
    def store_chunks(slab, stage_ref, out_ref):
        stage_ref[...] = slab
        nrow = tm // CMP_STRIDE
        first = lax.broadcasted_iota(jnp.int32, (nrow, LANES), 1) < NSA_DIM
        for r in range(0, CMP_STRIDE, 2):
            a = stage_ref[pl.ds(r, nrow, stride=CMP_STRIDE), :]
            b = stage_ref[pl.ds(r + 1, nrow, stride=CMP_STRIDE), :]
            cols = slice(r * NSA_DIM, (r + 2) * NSA_DIM)
            out_ref[0, :, cols] = jnp.where(first, a, pltpu.roll(b, NSA_DIM, axis=1)).astype(BF16)
            out_ref[1, :, cols] = jnp.where(first, pltpu.roll(a, NSA_DIM, axis=1), b).astype(BF16)

    store_chunks(_rope(nsa(0), cos_n, sup_n, sdn_n, hn), kstage_ref, kc_ref)
    store_chunks(nsa(1), vstage_ref, vc_ref)
    t_seq = (pl.program_id(0) % tiles_per_seq) * tm + lax.broadcasted_iota(jnp.int32, (tm, LANES), 0)
    slot = NSA_DIM + (lax.shift_right_logical(t_seq, int(math.log2(SEL_LEN))) & (SEL_SLOTS // 2 - 1))
    onehot = jnp.where(lane == slot, 1.0, 0.0)
    for g, kg in enumerate(halves(_rope(nsa(2), cos_n, sup_n, sdn_n, hn))):
        ksa_ref[0, g] = (kg + onehot).astype(BF16)
    vst = nsa(3).T
    vst_ref[0, 0] = with_sum_rows(vst[:NSA_DIM])
    vst_ref[0, 1] = with_sum_rows(vst[NSA_DIM:])
    kws = _rope(nsa(4), cos_n, sup_n, sdn_n, hn)
    kw_ref[0, 0] = kws[:, :NSA_DIM].astype(BF16)
    kw_ref[0, 1] = pltpu.roll(kws, NSA_DIM, axis=1)[:, :NSA_DIM].astype(BF16)
    vwt = nsa(5).T
    vwt_ref[0, 0] = with_sum_rows(vwt[:NSA_DIM])
    vwt_ref[0, 1] = with_sum_rows(vwt[NSA_DIM:])
    gl_ref[...] = p[:, c_gl:c_gl + LANES]


def _att_columns(ql, kl):
    c_kr = ql + kl
    c_nq = c_kr + LANES
    c_kv = c_nq + NSA_HEADS * NSA_DIM
    c_gl = c_kv + 6 * NSA_GROUPS * NSA_DIM
    return c_kr, c_nq, c_kv, c_gl


def _rope_consts():
    rc = np.zeros((ROPE_ROWS, LANES), np.float32)
    hm = MLA_ROPE // 2
    rc[:hm] = (ROPE_THETA ** (-np.arange(hm, dtype=np.float32) / hm))[:, None]
    hn = NSA_ROT // 2
    rc[hm:hm + hn] = (ROPE_THETA ** (-np.arange(hn, dtype=np.float32) / hn))[:, None]
    return rc


def _project(x2, pos2, g_pre, w_att, q_g, w_uq_t, kv_g, w_ukv, w_v_t, rc, S, tm):
    T, D = x2.shape
    n_att = w_att.shape[1]
    tps = S // tm
    B = T // S
    brows = lambda n, d: (jax.ShapeDtypeStruct((B, n, S, d), BF16),
                          pl.BlockSpec((1, n, tm, d), lambda i: (i // tps, 0, i % tps, 0)))
    bcols = lambda n, d: (jax.ShapeDtypeStruct((B, n, d, S), BF16),
                          pl.BlockSpec((1, n, d, tm), lambda i: (i // tps, 0, 0, i % tps)))
    chunked = (jax.ShapeDtypeStruct((NSA_GROUPS, T // CMP_STRIDE, CMP_STRIDE * NSA_DIM), BF16),
               pl.BlockSpec((NSA_GROUPS, tm // CMP_STRIDE, CMP_STRIDE * NSA_DIM), lambda i: (0, i, 0)))
    outs = [bcols(MLA_HEADS, LANES), brows(MLA_HEADS, LANES), bcols(MLA_HEADS, MLA_V + SUM_ROWS),
            bcols(NSA_HEADS, LANES), chunked, chunked,
            brows(NSA_GROUPS, LANES), bcols(NSA_GROUPS, NSA_DIM + SUM_ROWS), brows(NSA_GROUPS, NSA_DIM),
            bcols(NSA_GROUPS, NSA_DIM + SUM_ROWS),
            (jax.ShapeDtypeStruct((T, LANES), F32), pl.BlockSpec((tm, LANES), lambda i: (i, 0)))]
    kern = functools.partial(_proj_kernel,
                             mla_scale=float((MLA_NOPE + MLA_ROPE) ** -0.5 * LOG2E),
                             nsa_scale=float(NSA_DIM ** -0.5 * LOG2E),
                             tiles_per_seq=S // tm)
    return pl.pallas_call(
        kern,
        grid=(T // tm,),
        in_specs=[
            pl.BlockSpec((tm, D), lambda i: (i, 0)),
            pl.BlockSpec((1, tm), lambda i: (0, i)),
            _const_spec((1, D)),
            _const_spec((D, n_att)),
            _const_spec((1, q_g.shape[1])),
            _const_spec(w_uq_t.shape),
            _const_spec((1, kv_g.shape[1])),
            _const_spec(w_ukv.shape),
            _const_spec(w_v_t.shape),
            _const_spec((ROPE_ROWS, LANES)),
        ],
        out_specs=[o[1] for o in outs],
        out_shape=[o[0] for o in outs],
        scratch_shapes=[pltpu.VMEM((tm, LANES), F32), pltpu.VMEM((tm, LANES), F32)],
        compiler_params=_cparams(("parallel",)),
        name="proj",
    )(x2, pos2, g_pre, w_att, q_g, w_uq_t, kv_g, w_ukv, w_v_t, rc)


def _compress_kernel(xk_ref, xv_ref, pk_ref, pv_ref, wk_ref, wv_ref, ok_ref, ovt_ref):
    n = xk_ref.shape[1]
    vboth = None
    for g in range(NSA_GROUPS):
        xk = xk_ref[g].astype(F32)
        lo = _dot((xk + pk_ref[0:1]).astype(BF16), wk_ref[0])
        hi = _dot((xk + pk_ref[1:2]).astype(BF16), wk_ref[1])
        ok_ref[g] = (lo + pltpu.roll(hi, n - 1, axis=0)).astype(BF16)
        xv = xv_ref[g].astype(F32)
        lo = _dot((xv + pv_ref[0:1]).astype(BF16), wv_ref[g, 0])
        hi = _dot((xv + pv_ref[1:2]).astype(BF16), wv_ref[g, 1])
        vg = lo + pltpu.roll(hi, n - 1, axis=0)
        vboth = vg if vboth is None else vboth + vg
    ovt_ref[0] = vboth.T.astype(BF16)


def _compress(xk, xv, pk, pv, wk, wv, B):
    G, TC, CW = xk.shape
    nch = TC // B
    xspec = pl.BlockSpec((G, nch, CW), lambda b: (0, b, 0))
    return pl.pallas_call(
        _compress_kernel,
        grid=(B,),
        in_specs=[xspec, xspec, _const_spec(pk.shape), _const_spec(pv.shape),
                  _const_spec(wk.shape), _const_spec(wv.shape)],
        out_specs=[pl.BlockSpec((G, nch, NSA_DIM), lambda b: (0, b, 0)),
                   pl.BlockSpec((1, G * NSA_DIM, nch), lambda b: (b, 0, 0))],
        out_shape=[jax.ShapeDtypeStruct((G, TC, NSA_DIM), BF16),
                   jax.ShapeDtypeStruct((B, G * NSA_DIM, nch), BF16)],
        compiler_params=_cparams(("parallel",)),
        name="compress",
    )(xk, xv, pk, pv, wk, wv)


def _cmp_select_kernel(q_ref, kc_ref, vct_ref, ovt_ref, o_ref, bias_ref, *, tq, n_pick, nqt, n_var):
    i = pl.program_id(1)
    q0 = i * tq
    nch_all = kc_ref.shape[1]
    nblk_all = ovt_ref.shape[0]
    tiles_per_var = nqt // n_var

    def variant(nch, nblk):
        t_c = q0 + lax.broadcasted_iota(jnp.int32, (nch, tq), 1)
        n_c = lax.broadcasted_iota(jnp.int32, (nch, tq), 0)
        cmask = (n_c * CMP_STRIDE + (CMP_LEN - 1)) <= t_c
        t_b = q0 + lax.broadcasted_iota(jnp.int32, (nblk, tq), 1)
        blk = lax.broadcasted_iota(jnp.int32, (nblk, tq), 0)
        blk_f = blk.astype(F32)
        cur = lax.shift_right_logical(t_b, int(math.log2(SEL_LEN)))
        forced = (blk == 0) | (blk == cur) | (blk == cur - 1)
        causal = blk * SEL_LEN <= t_b
        outs = []
        for g in range(NSA_GROUPS):
            psum = None
            for j in range(NSA_HPG):
                qt = q_ref[0, g * NSA_HPG + j][:NSA_DIM, :]
                s = jnp.where(cmask, _dot(kc_ref[g, :nch, :], qt), MASKED)
                m = jnp.maximum(jnp.max(s, axis=0, keepdims=True), M_FLOOR)
                p = jnp.exp2(s - m)
                l = jnp.sum(p, axis=0, keepdims=True)
                pn = p * jnp.where(l > 0.0, 1.0 / l, 0.0)
                outs.append(_dot(vct_ref[0, g * NSA_DIM:(g + 1) * NSA_DIM, :nch], pn.astype(BF16)))
                psum = pn if psum is None else psum + pn
            imp = None
            for piece in _split_bf16(psum, 3):
                d = _dot(ovt_ref[:nblk, :nch], piece)
                imp = d if imp is None else imp + d
            v0 = jnp.where(forced | jnp.logical_not(causal), -1.0, imp)
            v = v0
            for _ in range(n_pick):
                mx = jnp.max(v, axis=0, keepdims=True)
                first = jnp.min(jnp.where(v == mx, blk_f, float(nblk)), axis=0, keepdims=True)
                v = jnp.where(blk_f == first, -2.0, v)
            sel = forced | ((v == -2.0) & (v0 >= 0.0))
            bias = jnp.where(sel, 0.0, SEL_BIAS)
            per_tile = SEL_SLOTS // 2
            pad = jnp.zeros((SEL_SLOTS - per_tile, tq), F32)
            pieces = []
            for kt in range(nblk_all // per_tile):
                if kt * per_tile < nblk:
                    pieces += [bias[kt * per_tile:(kt + 1) * per_tile], pad]
                else:
                    pieces += [jnp.full((per_tile, tq), SEL_BIAS, F32), pad]
            bias_ref[0, g] = jnp.concatenate(pieces, axis=0).astype(BF16)
        o_ref[...] = jnp.concatenate(outs, axis=0).T.astype(BF16)

    for k in range(n_var):
        @pl.when((i >= k * tiles_per_var) & (i < (k + 1) * tiles_per_var))
        def _():
            variant(nch_all * (k + 1) // n_var, nblk_all * (k + 1) // n_var)


def _cmp_select(nq, kc, vct, ovt, B, S, tq):
    T = B * S
    nch = kc.shape[1] // B
    nblk = ovt.shape[0]
    nqt = S // tq
    n_var = 4 if nqt % 4 == 0 and (nch // 4) % LANES == 0 and (nblk // 4) % SUBLANES == 0 else 1
    kern = functools.partial(_cmp_select_kernel, tq=tq, n_pick=N_SEL - 3, nqt=nqt, n_var=n_var)
    return pl.pallas_call(
        kern,
        grid=(B, nqt),
        in_specs=[
            pl.BlockSpec((1, NSA_HEADS, LANES, tq), lambda b, i: (b, 0, 0, i)),
            pl.BlockSpec((NSA_GROUPS, nch, NSA_DIM), lambda b, i: (0, b, 0)),
            pl.BlockSpec((1, NSA_GROUPS * NSA_DIM, nch), lambda b, i: (b, 0, 0)),
            _const_spec(ovt.shape),
        ],
        out_specs=[pl.BlockSpec((tq, NSA_HEADS * NSA_DIM), lambda b, i: (b * nqt + i, 0)),
                   pl.BlockSpec((1, NSA_GROUPS, 2 * nblk, tq), lambda b, i: (b, 0, 0, i))],
        out_shape=[jax.ShapeDtypeStruct((T, NSA_HEADS * NSA_DIM), BF16),
                   jax.ShapeDtypeStruct((B, NSA_GROUPS, 2 * nblk, S), BF16)],
        compiler_params=_cparams(("parallel", "parallel")),
        name="cmp_select",
    )(nq, kc, vct, ovt)


def _tile_parts(kind, tq):
    if kind == "full":
        return [(slice(0, tq), slice(0, tq), None)]
    hq = tq // 2

    def mask(rows, cols, below):
        key = rows.start + lax.broadcasted_iota(jnp.int32, (rows.stop - rows.start, hq), 0)
        qry = cols.start + lax.broadcasted_iota(jnp.int32, (rows.stop - rows.start, hq), 1)
        return (key <= qry) if below else (key > qry)

    lo, hi, all_ = slice(0, hq), slice(hq, tq), slice(0, tq)
    if kind == "diag":
        return [(lo, lo, mask(lo, lo, True)), (all_, hi, mask(all_, hi, True))]
    return [(all_, lo, mask(all_, lo, False)), (hi, hi, mask(hi, hi, False))]


def _flash_heads(nheads, kqv, parts, src, dst, lazy=False, unroll=True):
    def scores(hd, part):
        rows, cols, _ = part
        k, qt, _ = kqv(hd)
        return _dot(k[rows], qt[:, cols])

    def update(hd, part, s):
        rows, cols, mask = part
        if mask is not None:
            s = jnp.where(mask, s, MASKED)
        tmax = jnp.max(s, axis=0, keepdims=True)
        vt = kqv(hd)[2][:, rows]
        rise = None
        if src is None:
            m_new = jnp.maximum(tmax, M_FLOOR)
            acc = _dot(vt, jnp.exp2(s - m_new).astype(BF16))
        else:
            m_old = src[0][hd, :, cols]
            m_new = jnp.maximum(m_old, tmax)
            alpha = jnp.exp2(m_old - m_new)
            if lazy:
                acc = (src[1][hd, :, cols] + _dot(vt, jnp.exp2(s - m_old).astype(BF16))) * alpha
                rise = tmax - m_old
            else:
                acc = alpha * src[1][hd, :, cols] + _dot(vt, jnp.exp2(s - m_new).astype(BF16))
        dst[0][hd, :, cols] = m_new
        dst[1][hd, :, cols] = acc
        return rise

    if not unroll:
        def body(hd, carry):
            for part in parts:
                update(hd, part, scores(hd, part))
            return carry
        lax.fori_loop(0, nheads, body, 0)
        return None
    todo = [(hd, part) for hd in range(nheads) for part in parts]
    ahead = 2
    pending = [scores(*t) for t in todo[:ahead]]
    worst = {}
    for n, (hd, part) in enumerate(todo):
        s = pending.pop(0)
        if n + ahead < len(todo):
            pending.append(scores(*todo[n + ahead]))
        rise = update(hd, part, s)
        if rise is not None:
            key = rise.shape
            worst[key] = rise if key not in worst else jnp.maximum(worst[key], rise)
    if not worst:
        return None
    tops = [jnp.max(w, axis=1, keepdims=True) for w in worst.values()]
    return functools.reduce(jnp.maximum, tops)


def _flash_finalize(o_ref, acc_ref):
    nv, rows, tq = acc_ref.shape
    dv = rows - SUM_ROWS
    groups = o_ref.shape[0] if len(o_ref.shape) == 3 else 1
    nh = nv // groups
    for gb in range(groups):
        sl = slice(gb * nh, (gb + 1) * nh)
        o = acc_ref[sl, :dv, :] * (1.0 / acc_ref[sl, dv:dv + 1, :])
        o = o.reshape(nh * dv, tq).T.astype(BF16)
        if len(o_ref.shape) == 3:
            o_ref[gb] = o
        else:
            o_ref[...] = o


def _flash_scratch(nh, dv, tq, slots=None):
    lead = (nh,) if slots is None else (slots, nh)
    return [pltpu.VMEM(lead + (1, tq), F32), pltpu.VMEM(lead + (dv + SUM_ROWS, tq), F32)]


def _causal_pairs(nqt):
    pairs = [(i, j) for i in range(nqt) for j in [i] + list(range(i))]
    return (jnp.asarray([p[0] for p in pairs], jnp.int32), jnp.asarray([p[1] for p in pairs], jnp.int32))


def _lazy_flash_step(nheads, kqv, parts, m_ref, acc_ref, cur):
    src = (m_ref.at[cur], acc_ref.at[cur])
    dst = (m_ref.at[1 - cur], acc_ref.at[1 - cur])
    worst = _flash_heads(nheads, kqv, parts, src, dst, lazy=True)

    @pl.when(jnp.max(worst) > LAZY_SPAN)
    def _():
        _flash_heads(nheads, kqv, parts, src, dst, unroll=False)


def _causal_flash_step(i, j, nheads, kqv, tq, o_ref, m_ref, acc_ref, slot_ref):
    @pl.when(j == i)
    def _():
        _flash_heads(nheads, kqv, _tile_parts("diag", tq), None, (m_ref.at[0], acc_ref.at[0]))
        slot_ref[0] = 0

    @pl.when(j < i)
    def _():
        cur = slot_ref[0]
        _lazy_flash_step(nheads, kqv, _tile_parts("full", tq), m_ref, acc_ref, cur)
        slot_ref[0] = 1 - cur

    @pl.when((j == i - 1) | (i == 0))
    def _():
        _flash_finalize(o_ref, acc_ref.at[slot_ref[0]])


def _split_virtual(v, nheads):
    if isinstance(v, int):
        return v // nheads, v % nheads
    return lax.div(v, nheads), lax.rem(v, nheads)


def _mla_kernel(qi_ref, ki_ref, q_ref, k_ref, vt_ref, o_ref, m_ref, acc_ref, slot_ref, *, tq):
    i, j = qi_ref[pl.program_id(0)], ki_ref[pl.program_id(0)]

    def kqv(v):
        bb, hd = _split_virtual(v, MLA_HEADS)
        return k_ref[bb, hd], q_ref[bb, hd], vt_ref[bb, hd]

    _causal_flash_step(i, j, q_ref.shape[0] * MLA_HEADS, kqv, tq, o_ref, m_ref, acc_ref, slot_ref)


def _mla_attention(qm, km, vmt, B, S, tq):
    nqt = S // tq
    kern = functools.partial(_mla_kernel, tq=tq)
    qi, ki = _causal_pairs(nqt)
    return pl.pallas_call(
        kern,
        grid_spec=pltpu.PrefetchScalarGridSpec(
            num_scalar_prefetch=2,
            grid=(qi.shape[0],),
            in_specs=[pl.BlockSpec((B, MLA_HEADS, LANES, tq), lambda p, qi, ki: (0, 0, 0, qi[p])),
                      pl.BlockSpec((B, MLA_HEADS, tq, LANES), lambda p, qi, ki: (0, 0, ki[p], 0)),
                      pl.BlockSpec((B, MLA_HEADS, MLA_V + SUM_ROWS, tq), lambda p, qi, ki: (0, 0, 0, ki[p]))],
            out_specs=pl.BlockSpec((B, tq, MLA_HEADS * MLA_V), lambda p, qi, ki: (0, qi[p], 0)),
            scratch_shapes=_flash_scratch(B * MLA_HEADS, MLA_V, tq, slots=2) + [pltpu.SMEM((1,), jnp.int32)]),
        out_shape=jax.ShapeDtypeStruct((B, S, MLA_HEADS * MLA_V), BF16),
        compiler_params=_cparams(("arbitrary",)),
        name="mla_attn",
    )(qi, ki, qm, km, vmt).reshape(B * S, MLA_HEADS * MLA_V)


def _slc_kernel(qi_ref, ki_ref, q_ref, bias_ref, k_ref, vt_ref, o_ref, qa_ref, m_ref, acc_ref,
                slot_ref, *, tq):
    i, j = qi_ref[pl.program_id(0)], ki_ref[pl.program_id(0)]
    nb = q_ref.shape[0]

    @pl.when(j == i)
    def _():
        for bb in range(nb):
            for hd in range(NSA_HEADS):
                qa_ref[bb * NSA_HEADS + hd] = q_ref[bb, hd]

    rows = pl.ds(pl.multiple_of(j * SEL_SLOTS, SEL_SLOTS), SEL_SLOTS)
    for bb in range(nb):
        for g in range(NSA_GROUPS):
            tile_bias = bias_ref[bb, g, rows, :]
            for hd in range(g * NSA_HPG, (g + 1) * NSA_HPG):
                qa_ref[bb * NSA_HEADS + hd, NSA_DIM:NSA_DIM + SEL_SLOTS, :] = tile_bias

    def kqv(v):
        bb, hd = _split_virtual(v, NSA_HEADS)
        g = hd // NSA_HPG if isinstance(hd, int) else lax.div(hd, NSA_HPG)
        return k_ref[bb, g], qa_ref[v], vt_ref[bb, g]

    _causal_flash_step(i, j, nb * NSA_HEADS, kqv, tq, o_ref, m_ref, acc_ref, slot_ref)


def _slc_attention(nq, bias, ksa, vst, B, S, tq):
    nqt = S // tq
    nrow = bias.shape[2]
    assert tq // SEL_LEN == SEL_SLOTS // 2
    kern = functools.partial(_slc_kernel, tq=tq)
    qi, ki = _causal_pairs(nqt)
    qmap = lambda p, qi, ki: (0, 0, 0, qi[p])
    return pl.pallas_call(
        kern,
        grid_spec=pltpu.PrefetchScalarGridSpec(
            num_scalar_prefetch=2,
            grid=(qi.shape[0],),
            in_specs=[pl.BlockSpec((B, NSA_HEADS, LANES, tq), qmap),
                      pl.BlockSpec((B, NSA_GROUPS, nrow, tq), qmap),
                      pl.BlockSpec((B, NSA_GROUPS, tq, LANES), lambda p, qi, ki: (0, 0, ki[p], 0)),
                      pl.BlockSpec((B, NSA_GROUPS, NSA_DIM + SUM_ROWS, tq), lambda p, qi, ki: (0, 0, 0, ki[p]))],
            out_specs=pl.BlockSpec((B, tq, NSA_HEADS * NSA_DIM), lambda p, qi, ki: (0, qi[p], 0)),
            scratch_shapes=[pltpu.VMEM((B * NSA_HEADS, LANES, tq), BF16)]
                           + _flash_scratch(B * NSA_HEADS, NSA_DIM, tq, slots=2)
                           + [pltpu.SMEM((1,), jnp.int32)]),
        out_shape=jax.ShapeDtypeStruct((B, S, NSA_HEADS * NSA_DIM), BF16),
        compiler_params=_cparams(("arbitrary",)),
        name="slc_attn",
    )(qi, ki, nq, bias, ksa, vst).reshape(B * S, NSA_HEADS * NSA_DIM)


def _win_kernel(q_ref, k_ref, vt_ref, o_ref, m_ref, acc_ref, *, tq):
    i, j = pl.program_id(0), pl.program_id(1)

    def kqv(v):
        bb, hd = _split_virtual(v, NSA_HEADS)
        g = hd // NSA_HPG if isinstance(hd, int) else lax.div(hd, NSA_HPG)
        return k_ref[bb, g], q_ref[bb, hd][:NSA_DIM, :], vt_ref[bb, g]

    nv = q_ref.shape[0] * NSA_HEADS

    @pl.when(j == 0)
    def _():
        _flash_heads(nv, kqv, _tile_parts("diag", tq), None, (m_ref.at[0], acc_ref.at[0]))

    @pl.when((j == 1) & (i > 0))
    def _():
        _lazy_flash_step(nv, kqv, _tile_parts("prev", tq), m_ref, acc_ref, 0)

    @pl.when(j == 1)
    def _():
        _flash_finalize(o_ref, acc_ref.at[jnp.where(i > 0, 1, 0)])


def _win_attention(nq, kw, vwt, B, S, tq):
    nqt = S // tq
    kern = functools.partial(_win_kernel, tq=tq)
    kidx = lambda i, j: jnp.maximum(i - j, 0)
    return pl.pallas_call(
        kern,
        grid=(nqt, 2),
        in_specs=[pl.BlockSpec((B, NSA_HEADS, LANES, tq), lambda i, j: (0, 0, 0, i)),
                  pl.BlockSpec((B, NSA_GROUPS, tq, NSA_DIM), lambda i, j: (0, 0, kidx(i, j), 0)),
                  pl.BlockSpec((B, NSA_GROUPS, NSA_DIM + SUM_ROWS, tq), lambda i, j: (0, 0, 0, kidx(i, j)))],
        out_specs=pl.BlockSpec((B, tq, NSA_HEADS * NSA_DIM), lambda i, j: (0, i, 0)),
        out_shape=jax.ShapeDtypeStruct((B, S, NSA_HEADS * NSA_DIM), BF16),
        scratch_shapes=_flash_scratch(B * NSA_HEADS, NSA_DIM, tq, slots=2),
        compiler_params=_cparams(("parallel", "arbitrary")),
        name="win_attn",
    )(nq, kw, vwt).reshape(B * S, NSA_HEADS * NSA_DIM)


def _causal_conv(u, carry, w_ref):
    row = lax.broadcasted_iota(jnp.int32, u.shape, 0)
    c1 = carry[SUBLANES - 1:SUBLANES]
    c2 = carry[SUBLANES - 2:SUBLANES - 1]
    u1 = jnp.where(row == 0, c1, pltpu.roll(u, 1, axis=0))
    u2 = jnp.where(row == 0, c2, jnp.where(row == 1, c1, pltpu.roll(u, 2, axis=0)))
    return w_ref[0:1] * u2 + w_ref[1:2] * u1 + w_ref[2:3] * u


def _merge_kernel(x_ref, g_ref, wgc_ref, cw_ref, ym_ref, oc_ref, os_ref, ow_ref, gl_ref, ex_ref,
                  wbc_ref, wbm_ref, wbn_ref, wo_ref, gp_ref, out_ref, carry_ref, *, tiles_per_seq):
    D = x_ref.shape[1]
    DC = wbc_ref.shape[0]

    @pl.when(pl.program_id(0) % tiles_per_seq == 0)
    def _():
        carry_ref[...] = jnp.zeros(carry_ref.shape, F32)

    x = x_ref[...]
    h = _rms(x, g_ref[...]).astype(BF16)
    p = _dot(h, wgc_ref[...])
    u = p[:, 3 * D + DC:3 * D + 2 * DC] * p[:, 3 * D + 2 * DC:3 * D + 3 * DC]
    y_conv = p[:, 3 * D:3 * D + DC] * _causal_conv(u, carry_ref[...], cw_ref)
    carry_ref[...] = u[u.shape[0] - SUBLANES:]
    hi, lo = _split_bf16(gl_ref[...], 2)
    gate = jax.nn.sigmoid(_dot(hi, ex_ref[...]) + _dot(lo, ex_ref[...]))
    HD = oc_ref.shape[1]
    y_nsa = (gate[:, 0:HD] * oc_ref[...].astype(F32) + gate[:, HD:2 * HD] * os_ref[...].astype(F32)
             + gate[:, 2 * HD:3 * HD] * ow_ref[...].astype(F32))
    merged = (jax.nn.sigmoid(p[:, 0:D]) * _dot(y_conv.astype(BF16), wbc_ref[...])
              + jax.nn.sigmoid(p[:, D:2 * D]) * _dot(ym_ref[...], wbm_ref[...])
              + jax.nn.sigmoid(p[:, 2 * D:3 * D]) * _dot(y_nsa.astype(BF16), wbn_ref[...]))
    out_ref[...] = x + _rms(_dot(merged.astype(BF16), wo_ref[...]), gp_ref[...])


def _merge(x2, g_pre, w_gc, conv_w, y_mla, o_cmp, o_slc, o_win, gl, ex, wbc, wbm, wbn, wo, g_post,
           S, tm):
    T, D = x2.shape
    HD = y_mla.shape[1]
    row = lambda w: pl.BlockSpec((tm, w), lambda i: (i, 0))
    kern = functools.partial(_merge_kernel, tiles_per_seq=S // tm)
    return pl.pallas_call(
        kern,
        grid=(T // tm,),
        in_specs=[row(D), _const_spec((1, D)), _const_spec(w_gc.shape), _const_spec(conv_w.shape),
                  row(HD), row(HD), row(HD), row(HD), row(LANES), _const_spec(ex.shape),
                  _const_spec(wbc.shape), _const_spec(wbm.shape), _const_spec(wbn.shape),
                  _const_spec(wo.shape), _const_spec((1, D))],
        out_specs=row(D),
        out_shape=jax.ShapeDtypeStruct((T, D), F32),
        scratch_shapes=[pltpu.VMEM((SUBLANES, wbc.shape[0]), F32)],
        compiler_params=_cparams(("arbitrary",)),
        name="merge",
    )(x2, g_pre, w_gc, conv_w, y_mla, o_cmp, o_slc, o_win, gl, ex, wbc, wbm, wbn, wo, g_post)


def _ffn_kernel(x_ref, g_ref, wup_ref, cw_ref, cb_ref, wdn_ref, gp_ref, out_ref, carry_ref,
                *, tiles_per_seq, chunk):
    F = wdn_ref.shape[0]

    @pl.when(pl.program_id(0) % tiles_per_seq == 0)
    def _():
        carry_ref[...] = jnp.zeros(carry_ref.shape, F32)

    x = x_ref[...]
    h = _rms(x, g_ref[...]).astype(BF16)
    acc = None
    for c0 in range(0, F, chunk):
        a = _dot(h, wup_ref[:, c0:c0 + chunk])
        b = _dot(h, wup_ref[:, F + c0:F + c0 + chunk])
        ac = _causal_conv(a, carry_ref[:, c0:c0 + chunk], cw_ref.at[:, c0:c0 + chunk])
        ac = ac + cb_ref[:, c0:c0 + chunk]
        carry_ref[:, c0:c0 + chunk] = a[a.shape[0] - SUBLANES:]
        d = _dot((jax.nn.gelu(ac) * b).astype(BF16), wdn_ref[c0:c0 + chunk, :])
        acc = d if acc is None else acc + d
    out_ref[...] = x + _rms(acc, gp_ref[...])


def _ffn(x2, g_pre, w_up, conv_w, conv_b, w_dn, g_post, S, tm, chunk):
    T, D = x2.shape
    F = w_dn.shape[0]
    row = pl.BlockSpec((tm, D), lambda i: (i, 0))
    kern = functools.partial(_ffn_kernel, tiles_per_seq=S // tm, chunk=chunk)
    return pl.pallas_call(
        kern,
        grid=(T // tm,),
        in_specs=[row, _const_spec((1, D)), _const_spec(w_up.shape), _const_spec(conv_w.shape),
                  _const_spec((1, F)), _const_spec(w_dn.shape), _const_spec((1, D))],
        out_specs=row,
        out_shape=jax.ShapeDtypeStruct((T, D), F32),
        scratch_shapes=[pltpu.VMEM((SUBLANES, F), F32)],
        compiler_params=_cparams(("arbitrary",)),
        name="ffn",
    )(x2, g_pre, w_up, conv_w, conv_b, w_dn, g_post)


def _overlap_matrix_t(S):
    nch = S // CMP_STRIDE
    n_cmp = (S - CMP_LEN) // CMP_STRIDE + 1
    cs = np.arange(nch)[None, :] * CMP_STRIDE
    ss = np.arange(S // SEL_LEN)[:, None] * SEL_LEN
    ov = (cs <= ss + SEL_LEN - 1) & (cs + CMP_LEN - 1 >= ss) & (np.arange(nch)[None, :] < n_cmp)
    return ov.astype(np.float32)


def _gate_expand():
    ex = np.zeros((LANES, 3 * NSA_HEADS * NSA_DIM), np.float32)
    for hd in range(NSA_HEADS):
        for c in range(3):
            ex[hd * 3 + c, c * NSA_HEADS * NSA_DIM + hd * NSA_DIM:
               c * NSA_HEADS * NSA_DIM + (hd + 1) * NSA_DIM] = 1.0
    return ex


def _pad_rows(w, rows):
    return jnp.concatenate([w, jnp.zeros((rows - w.shape[0],) + w.shape[1:], w.dtype)], axis=0)


def kernel(x, positions, norm_mix_pre, norm_mix_post, w_in, conv_w, mla_q_norm, mla_w_uq, mla_kv_norm, mla_w_ukv, nsa_cmp_pos_k, nsa_cmp_pos_v, nsa_cmp_w_k, nsa_cmp_w_v, w_branch_conv, w_branch_mla, w_branch_nsa, w_out, norm_ffn_pre, norm_ffn_post, ffn_w_up, ffn_conv_w, ffn_conv_b, ffn_w_down):
    B, S, D = x.shape
    T = B * S
    depth = w_in.shape[0]
    DC = conv_w.shape[2]
    F = ffn_w_down.shape[1]
    QL = mla_q_norm.shape[1]
    KL = mla_kv_norm.shape[1]
    tq = 512
    assert S % tq == 0 and WINDOW == tq and N_SEL <= S // SEL_LEN <= LANES
    tm_proj, tm_merge, tm_ffn, ffn_chunk = 512, 512, 512, 1408
    assert F % ffn_chunk == 0 and ffn_chunk % LANES == 0

    x2 = x.reshape(T, D)
    pos2 = positions.reshape(1, T)
    rc = jnp.asarray(_rope_consts())
    ovt = jnp.asarray(_overlap_matrix_t(S), BF16)
    ex = jnp.asarray(_gate_expand(), BF16)
    half = CMP_LEN // 2 * NSA_DIM
    o_att = 3 * D + 3 * DC

    for l in range(depth):
        w = w_in[l]
        o_kr = o_att + QL + KL
        z = lambda n: jnp.zeros((D, n), F32)
        w_att = jnp.concatenate(
            [w[:, o_att:o_kr], z(MLA_NOPE), w[:, o_kr:o_kr + MLA_ROPE], z(LANES - MLA_NOPE - MLA_ROPE),
             w[:, o_kr + MLA_ROPE:o_kr + MLA_ROPE + NSA_HEADS * NSA_DIM + 6 * NSA_GROUPS * NSA_DIM],
             w[:, w.shape[1] - 3 * NSA_HEADS:], z(LANES - 3 * NSA_HEADS)], axis=1).astype(BF16)
        dq = MLA_NOPE + MLA_ROPE
        w_uq = jnp.pad(mla_w_uq[l].reshape(QL, MLA_HEADS, dq),
                       ((0, 0), (0, 0), (0, LANES - dq))).reshape(QL, MLA_HEADS * LANES).astype(BF16)
        w_v_t = mla_w_ukv[l].reshape(KL, MLA_HEADS, MLA_NOPE + MLA_V)[:, :, MLA_NOPE:].reshape(
            KL, MLA_HEADS * MLA_V).T.astype(BF16)
        (qm, km, vmt, nq, kc_in, vc_in, ksa, vst, kw, vwt, gl) = _project(
            x2, pos2, norm_mix_pre[l][None], w_att, mla_q_norm[l][None], w_uq.T,
            mla_kv_norm[l][None], mla_w_ukv[l].astype(BF16), w_v_t, rc, S, tm_proj)

        wv = nsa_cmp_w_v[l].reshape(2, half, NSA_DIM)
        zv = jnp.zeros_like(wv)
        wv_pad = jnp.stack([jnp.concatenate([wv, zv], axis=2), jnp.concatenate([zv, wv], axis=2)])
        kc, vct = _compress(kc_in, vc_in,
                            nsa_cmp_pos_k[l].reshape(2, half), nsa_cmp_pos_v[l].reshape(2, half),
                            nsa_cmp_w_k[l].reshape(2, half, NSA_DIM).astype(BF16),
                            wv_pad.astype(BF16), B)

        y_mla = _mla_attention(qm, km, vmt, B, S, tq)
        o_cmp, bias = _cmp_select(nq, kc, vct, ovt, B, S, tq)
        o_slc = _slc_attention(nq, bias, ksa, vst, B, S, tq)
        o_win = _win_attention(nq, kw, vwt, B, S, tq)

        x2 = _merge(x2, norm_mix_pre[l][None], w[:, :o_att].astype(BF16),
                    _pad_rows(conv_w[l], SUBLANES), y_mla, o_cmp, o_slc, o_win, gl, ex,
                    w_branch_conv[l].astype(BF16), w_branch_mla[l].astype(BF16),
                    w_branch_nsa[l].astype(BF16), w_out[l].astype(BF16), norm_mix_post[l][None],
                    S, tm_merge)
        x2 = _ffn(x2, norm_ffn_pre[l][None], ffn_w_up[l].astype(BF16),
                  _pad_rows(ffn_conv_w[l], SUBLANES), ffn_conv_b[l][None],
                  ffn_w_down[l].astype(BF16), norm_ffn_post[l][None], S, tm_ffn, ffn_chunk)
    return x2.reshape(B, S, D)
```

```python
import functools
import math

import numpy as np
import jax
import jax.numpy as jnp
from jax import lax
from jax.experimental import pallas as pl
from jax.experimental.pallas import tpu as pltpu

F32 = jnp.float32
BF16 = jnp.bfloat16

ROPE_THETA = 500000.0
RMS_EPS = 1e-6
CONV_WIDTH = 3
MLA_HEADS = 8
MLA_NOPE = 64
MLA_ROPE = 32
MLA_V = 64
NSA_HEADS = 8
NSA_GROUPS = 2
NSA_HPG = NSA_HEADS // NSA_GROUPS
NSA_DIM = 64
NSA_ROT = NSA_DIM // 4
CMP_LEN = 32
CMP_STRIDE = 16
SEL_LEN = 64
N_SEL = 16
WINDOW = 512

LANES = 128
SUBLANES = 8
VMEM_LIMIT = 56 * 1024 * 1024
MASKED = -1e30
M_FLOOR = -1e20
SEL_BIAS = -2.0 ** 100
LAZY_SPAN = 64.0
LOG2E = math.log2(math.e)
SEL_SLOTS = 16
ROPE_ROWS = 32
SUM_ROWS = 16


def _cparams(sem):
    return pltpu.CompilerParams(dimension_semantics=sem, vmem_limit_bytes=VMEM_LIMIT)


def _const_spec(shape):
    nd = len(shape)
    return pl.BlockSpec(shape, lambda *_: (0,) * nd, pipeline_mode=pl.Buffered(1))


def _rms(x, g):
    return x * lax.rsqrt(jnp.mean(x * x, axis=-1, keepdims=True) + RMS_EPS) * g


def _dot(a, b):
    return jnp.dot(a, b, preferred_element_type=F32)


def _split_bf16(a, terms):
    pieces, rem = [], a
    for _ in range(terms):
        piece = rem.astype(BF16)
        rem = rem - piece.astype(F32)
        pieces.append(piece)
    return pieces


def _rope(x, cos, sin_up, sin_dn, half):
    return (x * cos + pltpu.roll(x, half, axis=1) * sin_up
            + pltpu.roll(x, LANES - half, axis=1) * sin_dn)


def _lanes(table, lane, base, moves):
    out = base
    for dst, width, src in moves:
        rolled = pltpu.roll(table, (dst - src) % LANES, axis=1)
        out = jnp.where((lane >= dst) & (lane < dst + width), rolled, out)
    return out


def _proj_kernel(x_ref, pos_ref, g_ref, watt_ref, qg_ref, wuqt_ref, kvg_ref, wukv_ref, wvt_ref, rc_ref,
                 qm_ref, km_ref, vmt_ref, nq_ref, kc_ref, vc_ref, ksa_ref, vst_ref, kw_ref, vwt_ref,
                 gl_ref, kstage_ref, vstage_ref, *, mla_scale, nsa_scale, tiles_per_seq):
    tm = x_ref.shape[0]
    hm, hn = MLA_ROPE // 2, NSA_ROT // 2
    x = x_ref[...]
    h = _rms(x, g_ref[...]).astype(BF16)
    p = _dot(h, watt_ref[...])
    ql, kl = qg_ref.shape[1], kvg_ref.shape[1]
    c_kr, c_nq, c_kv, c_gl = _att_columns(ql, kl)
    lane = lax.broadcasted_iota(jnp.int32, (tm, LANES), 1)
    low = lane < NSA_DIM
    ang_t = rc_ref[:, 0:1] * pos_ref[...].astype(F32)
    cos_t, sin_t = jnp.cos(ang_t), jnp.sin(ang_t)
    cs = jnp.concatenate([cos_t, sin_t, jnp.zeros((LANES - 2 * ROPE_ROWS, tm), F32)], axis=0).T
    cos_m = _lanes(cs, lane, 1.0, [(MLA_NOPE, hm, 0), (MLA_NOPE + hm, hm, 0)])
    sup_m = _lanes(cs, lane, 0.0, [(MLA_NOPE + hm, hm, ROPE_ROWS)])
    sdn_m = -_lanes(cs, lane, 0.0, [(MLA_NOPE, hm, ROPE_ROWS)])
    cos_n = _lanes(cs, lane, 1.0, [(0, hn, hm), (hn, hn, hm), (NSA_DIM, hn, hm), (NSA_DIM + hn, hn, hm)])
    sup_n = _lanes(cs, lane, 0.0, [(hn, hn, ROPE_ROWS + hm), (NSA_DIM + hn, hn, ROPE_ROWS + hm)])
    sdn_n = -_lanes(cs, lane, 0.0, [(0, hn, ROPE_ROWS + hm), (NSA_DIM, hn, ROPE_ROWS + hm)])
    ones_rows = jnp.where(lax.broadcasted_iota(jnp.int32, (SUM_ROWS, tm), 0) == 0, 1.0, 0.0)

    def with_sum_rows(vt):
        return jnp.concatenate([vt, ones_rows], axis=0).astype(BF16)

    def halves(slab):
        return (jnp.where(low, slab, 0.0), jnp.where(low, pltpu.roll(slab, NSA_DIM, axis=1), 0.0))

    def rope_rows(x1, x2, c, s_):
        return x1 * c - x2 * s_, x2 * c + x1 * s_

    cqn_t = _rms(p[:, 0:ql], qg_ref[...]).T.astype(BF16)
    qt = _dot(wuqt_ref[...], cqn_t)
    for hd in range(MLA_HEADS):
        r0 = hd * LANES
        y1, y2 = rope_rows(qt[r0 + MLA_NOPE:r0 + MLA_NOPE + hm], qt[r0 + MLA_NOPE + hm:r0 + MLA_NOPE + 2 * hm],
                           cos_t[0:hm], sin_t[0:hm])
        slab_t = jnp.concatenate([qt[r0:r0 + MLA_NOPE], y1, y2, qt[r0 + MLA_NOPE + 2 * hm:r0 + LANES]], axis=0)
        qm_ref[0, hd] = (slab_t * mla_scale).astype(BF16)
    ckvn = _rms(p[:, ql:ql + kl], kvg_ref[...])
    kv = _dot(ckvn.astype(BF16), wukv_ref[...])
    kr = _rope(p[:, c_kr:c_kr + LANES], cos_m, sup_m, sdn_m, hm)
    vt = _dot(wvt_ref[...], ckvn.T.astype(BF16))
    for hd in range(MLA_HEADS):
        km_ref[0, hd] = jnp.where(lane < MLA_NOPE, kv[:, hd * LANES:(hd + 1) * LANES], kr).astype(BF16)
        vmt_ref[0, hd] = with_sum_rows(vt[hd * MLA_V:(hd + 1) * MLA_V])
    zpad = jnp.zeros((LANES - NSA_DIM, tm), BF16)
    for pr in range(NSA_HEADS // 2):
        slab_t = p[:, c_nq + pr * LANES:c_nq + (pr + 1) * LANES].T
        for half in range(2):
            r0 = half * NSA_DIM
            y1, y2 = rope_rows(slab_t[r0:r0 + hn], slab_t[r0 + hn:r0 + 2 * hn],
                               cos_t[hm:hm + hn], sin_t[hm:hm + hn])
            head_t = jnp.concatenate([y1, y2, slab_t[r0 + 2 * hn:r0 + NSA_DIM]], axis=0) * nsa_scale
            nq_ref[0, 2 * pr + half, 0:NSA_DIM, :] = head_t.astype(BF16)
            nq_ref[0, 2 * pr + half, NSA_DIM:LANES, :] = zpad
    nsa = lambda idx: p[:, c_kv + idx * LANES:c_kv + (idx + 1) * LANES]
    def store_chunks(slab, stage_ref, out_ref):
        stage_ref[...] = slab
        nrow = tm // CMP_STRIDE
        first = lax.broadcasted_iota(jnp.int32, (nrow, LANES), 1) < NSA_DIM
        for r in range(0, CMP_STRIDE, 2):
            a = stage_ref[pl.ds(r, nrow, stride=CMP_STRIDE), :]
            b = stage_ref[pl.ds(r + 1, nrow, stride=CMP_STRIDE), :]
            cols = slice(r * NSA_DIM, (r + 2) * NSA_DIM)
            out_ref[0, :, cols] = jnp.where(first, a, pltpu.roll(b, NSA_DIM, axis=1)).astype(BF16)
            out_ref[1, :, cols] = jnp.where(first, pltpu.roll(a, NSA_DIM, axis=1), b).astype(BF16)

    store_chunks(_rope(nsa(0), cos_n, sup_n, sdn_n, hn), kstage_ref, kc_ref)
    store_chunks(nsa(1), vstage_ref, vc_ref)
    t_seq = (pl.program_id(0) % tiles_per_seq) * tm + lax.broadcasted_iota(jnp.int32, (tm, LANES), 0)
    slot = NSA_DIM + (lax.shift_right_logical(t_seq, int(math.log2(SEL_LEN))) & (SEL_SLOTS // 2 - 1))
    onehot = jnp.where(lane == slot, 1.0, 0.0)
    for g, kg in enumerate(halves(_rope(nsa(2), cos_n, sup_n, sdn_n, hn))):
        ksa_ref[0, g] = (kg + onehot).astype(BF16)
    vst = nsa(3).T
    vst_ref[0, 0] = with_sum_rows(vst[:NSA_DIM])
    vst_ref[0, 1] = with_sum_rows(vst[NSA_DIM:])
    kws = _rope(nsa(4), cos_n, sup_n, sdn_n, hn)
    kw_ref[0, 0] = kws[:, :NSA_DIM].astype(BF16)
    kw_ref[0, 1] = pltpu.roll(kws, NSA_DIM, axis=1)[:, :NSA_DIM].astype(BF16)
    vwt = nsa(5).T
    vwt_ref[0, 0] = with_sum_rows(vwt[:NSA_DIM])
    vwt_ref[0, 1] = with_sum_rows(vwt[NSA_DIM:])
    gl_ref[...] = p[:, c_gl:c_gl + LANES]


def _att_columns(ql, kl):
    c_kr = ql + kl
    c_nq = c_kr + LANES
    c_kv = c_nq + NSA_HEADS * NSA_DIM
    c_gl = c_kv + 6 * NSA_GROUPS * NSA_DIM
    return c_kr, c_nq, c_kv, c_gl


def _rope_consts():
    rc = np.zeros((ROPE_ROWS, LANES), np.float32)
    hm = MLA_ROPE // 2
    rc[:hm] = (ROPE_THETA ** (-np.arange(hm, dtype=np.float32) / hm))[:, None]
    hn = NSA_ROT // 2
    rc[hm:hm + hn] = (ROPE_THETA ** (-np.arange(hn, dtype=np.float32) / hn))[:, None]
    return rc


def _project(x2, pos2, g_pre, w_att, q_g, w_uq_t, kv_g, w_ukv, w_v_t, rc, S, tm):
    T, D = x2.shape
    n_att = w_att.shape[1]
    tps = S // tm
    B = T // S
    brows = lambda n, d: (jax.ShapeDtypeStruct((B, n, S, d), BF16),
                          pl.BlockSpec((1, n, tm, d), lambda i: (i // tps, 0, i % tps, 0)))
    bcols = lambda n, d: (jax.ShapeDtypeStruct((B, n, d, S), BF16),
                          pl.BlockSpec((1, n, d, tm), lambda i: (i // tps, 0, 0, i % tps)))
    chunked = (jax.ShapeDtypeStruct((NSA_GROUPS, T // CMP_STRIDE, CMP_STRIDE * NSA_DIM), BF16),
               pl.BlockSpec((NSA_GROUPS, tm // CMP_STRIDE, CMP_STRIDE * NSA_DIM), lambda i: (0, i, 0)))
    outs = [bcols(MLA_HEADS, LANES), brows(MLA_HEADS, LANES), bcols(MLA_HEADS, MLA_V + SUM_ROWS),
            bcols(NSA_HEADS, LANES), chunked, chunked,
            brows(NSA_GROUPS, LANES), bcols(NSA_GROUPS, NSA_DIM + SUM_ROWS), brows(NSA_GROUPS, NSA_DIM),
            bcols(NSA_GROUPS, NSA_DIM + SUM_ROWS),
            (jax.ShapeDtypeStruct((T, LANES), F32), pl.BlockSpec((tm, LANES), lambda i: (i, 0)))]
    kern = functools.partial(_proj_kernel,
                             mla_scale=float((MLA_NOPE + MLA_ROPE) ** -0.5 * LOG2E),
                             nsa_scale=float(NSA_DIM ** -0.5 * LOG2E),
                             tiles_per_seq=S // tm)
    return pl.pallas_call(
        kern,
        grid=(T // tm,),
        in_specs=[
            pl.BlockSpec((tm, D), lambda i: (i, 0)),
            pl.BlockSpec((1, tm), lambda i: (0, i)),
            _const_spec((1, D)),
            _const_spec((D, n_att)),
            _const_spec((1, q_g.shape[1])),
            _const_spec(w_uq_t.shape),
            _const_spec((1, kv_g.shape[1])),
            _const_spec(w_ukv.shape),
            _const_spec(w_v_t.shape),
            _const_spec((ROPE_ROWS, LANES)),
        ],
        out_specs=[o[1] for o in outs],
        out_shape=[o[0] for o in outs],
        scratch_shapes=[pltpu.VMEM((tm, LANES), F32), pltpu.VMEM((tm, LANES), F32)],
        compiler_params=_cparams(("parallel",)),
        name="proj",
    )(x2, pos2, g_pre, w_att, q_g, w_uq_t, kv_g, w_ukv, w_v_t, rc)


def _compress_kernel(xk_ref, xv_ref, pk_ref, pv_ref, wk_ref, wv_ref, ok_ref, ovt_ref):
    n = xk_ref.shape[1]
    vboth = None
    for g in range(NSA_GROUPS):
        xk = xk_ref[g].astype(F32)
        lo = _dot((xk + pk_ref[0:1]).astype(BF16), wk_ref[0])
        hi = _dot((xk + pk_ref[1:2]).astype(BF16), wk_ref[1])
        ok_ref[g] = (lo + pltpu.roll(hi, n - 1, axis=0)).astype(BF16)
        xv = xv_ref[g].astype(F32)
        lo = _dot((xv + pv_ref[0:1]).astype(BF16), wv_ref[g, 0])
        hi = _dot((xv + pv_ref[1:2]).astype(BF16), wv_ref[g, 1])
        vg = lo + pltpu.roll(hi, n - 1, axis=0)
        vboth = vg if vboth is None else vboth + vg
    ovt_ref[0] = vboth.T.astype(BF16)


def _compress(xk, xv, pk, pv, wk, wv, B):
    G, TC, CW = xk.shape
    nch = TC // B
    xspec = pl.BlockSpec((G, nch, CW), lambda b: (0, b, 0))
    return pl.pallas_call(
        _compress_kernel,
        grid=(B,),
        in_specs=[xspec, xspec, _const_spec(pk.shape), _const_spec(pv.shape),
                  _const_spec(wk.shape), _const_spec(wv.shape)],
        out_specs=[pl.BlockSpec((G, nch, NSA_DIM), lambda b: (0, b, 0)),
                   pl.BlockSpec((1, G * NSA_DIM, nch), lambda b: (b, 0, 0))],
        out_shape=[jax.ShapeDtypeStruct((G, TC, NSA_DIM), BF16),
                   jax.ShapeDtypeStruct((B, G * NSA_DIM, nch), BF16)],
        compiler_params=_cparams(("parallel",)),
        name="compress",
    )(xk, xv, pk, pv, wk, wv)


def _cmp_select_kernel(q_ref, kc_ref, vct_ref, ovt_ref, o_ref, bias_ref, *, tq, n_pick, nqt, n_var):
    i = pl.program_id(1)
    q0 = i * tq
    nch_all = kc_ref.shape[1]
    nblk_all = ovt_ref.shape[0]
    tiles_per_var = nqt // n_var

    def variant(nch, nblk):
        t_c = q0 + lax.broadcasted_iota(jnp.int32, (nch, tq), 1)
        n_c = lax.broadcasted_iota(jnp.int32, (nch, tq), 0)
        cmask = (n_c * CMP_STRIDE + (CMP_LEN - 1)) <= t_c
        t_b = q0 + lax.broadcasted_iota(jnp.int32, (nblk, tq), 1)
        blk = lax.broadcasted_iota(jnp.int32, (nblk, tq), 0)
        blk_f = blk.astype(F32)
        cur = lax.shift_right_logical(t_b, int(math.log2(SEL_LEN)))
        forced = (blk == 0) | (blk == cur) | (blk == cur - 1)
        causal = blk * SEL_LEN <= t_b
        outs = []
        for g in range(NSA_GROUPS):
            psum = None
            scores = [_dot(kc_ref[g, :nch, :], q_ref[0, g * NSA_HPG + j][:NSA_DIM, :]) for j in range(NSA_HPG)]
            for j in range(NSA_HPG):
                s = jnp.where(cmask, scores[j], MASKED)
                m = jnp.maximum(jnp.max(s, axis=0, keepdims=True), M_FLOOR)
                p = jnp.exp2(s - m)
                l = jnp.sum(p, axis=0, keepdims=True)
                pn = p * jnp.where(l > 0.0, 1.0 / l, 0.0)
                outs.append(_dot(vct_ref[0, g * NSA_DIM:(g + 1) * NSA_DIM, :nch], pn.astype(BF16)))
                psum = pn if psum is None else psum + pn
            imp = None
            for piece in _split_bf16(psum, 3):
                d = _dot(ovt_ref[:nblk, :nch], piece)
                imp = d if imp is None else imp + d
            v0 = jnp.where(forced | jnp.logical_not(causal), -1.0, imp)
            v = v0
            for _ in range(n_pick):
                mx = jnp.max(v, axis=0, keepdims=True)
                first = jnp.min(jnp.where(v == mx, blk_f, float(nblk)), axis=0, keepdims=True)
                v = jnp.where(blk_f == first, -2.0, v)
            sel = forced | ((v == -2.0) & (v0 >= 0.0))
            bias = jnp.where(sel, 0.0, SEL_BIAS)
            per_tile = SEL_SLOTS // 2
            pad = jnp.zeros((SEL_SLOTS - per_tile, tq), F32)
            pieces = []
            for kt in range(nblk_all // per_tile):
                if kt * per_tile < nblk:
                    pieces += [bias[kt * per_tile:(kt + 1) * per_tile], pad]
                else:
                    pieces += [jnp.full((per_tile, tq), SEL_BIAS, F32), pad]
            bias_ref[0, g] = jnp.concatenate(pieces, axis=0).astype(BF16)
        o_ref[...] = jnp.concatenate(outs, axis=0).T.astype(BF16)

    for k in range(n_var):
        @pl.when((i >= k * tiles_per_var) & (i < (k + 1) * tiles_per_var))
        def _():
            variant(nch_all * (k + 1) // n_var, nblk_all * (k + 1) // n_var)


def _cmp_select(nq, kc, vct, ovt, B, S, tq):
    T = B * S
    nch = kc.shape[1] // B
    nblk = ovt.shape[0]
    nqt = S // tq
    n_var = 4 if nqt % 4 == 0 and (nch // 4) % LANES == 0 and (nblk // 4) % SUBLANES == 0 else 1
    kern = functools.partial(_cmp_select_kernel, tq=tq, n_pick=N_SEL - 3, nqt=nqt, n_var=n_var)
    return pl.pallas_call(
        kern,
        grid=(B, nqt),
        in_specs=[
            pl.BlockSpec((1, NSA_HEADS, LANES, tq), lambda b, i: (b, 0, 0, i)),
            pl.BlockSpec((NSA_GROUPS, nch, NSA_DIM), lambda b, i: (0, b, 0)),
            pl.BlockSpec((1, NSA_GROUPS * NSA_DIM, nch), lambda b, i: (b, 0, 0)),
            _const_spec(ovt.shape),
        ],
        out_specs=[pl.BlockSpec((tq, NSA_HEADS * NSA_DIM), lambda b, i: (b * nqt + i, 0)),
                   pl.BlockSpec((1, NSA_GROUPS, 2 * nblk, tq), lambda b, i: (b, 0, 0, i))],
        out_shape=[jax.ShapeDtypeStruct((T, NSA_HEADS * NSA_DIM), BF16),
                   jax.ShapeDtypeStruct((B, NSA_GROUPS, 2 * nblk, S), BF16)],
        compiler_params=_cparams(("parallel", "parallel")),
        name="cmp_select",
    )(nq, kc, vct, ovt)


def _tile_parts(kind, tq):
    hq = tq // 2
    if kind == "full":
        return [(slice(0, tq), slice(0, hq), None), (slice(0, tq), slice(hq, tq), None)]

    def mask(rows, cols, below):
        key = rows.start + lax.broadcasted_iota(jnp.int32, (rows.stop - rows.start, hq), 0)
        qry = cols.start + lax.broadcasted_iota(jnp.int32, (rows.stop - rows.start, hq), 1)
        return (key <= qry) if below else (key > qry)

    lo, hi, all_ = slice(0, hq), slice(hq, tq), slice(0, tq)
    if kind == "diag":
        return [(lo, lo, mask(lo, lo, True)), (all_, hi, mask(all_, hi, True))]
    return [(all_, lo, mask(all_, lo, False)), (hi, hi, mask(hi, hi, False))]


def _flash_heads(nheads, kqv, parts, src, dst, lazy=False, unroll=True):
    def scores(hd, part):
        rows, cols, _ = part
        k, qt, _ = kqv(hd)
        return _dot(k[rows], qt[:, cols])

    def update(hd, part, s):
        rows, cols, mask = part
        if mask is not None:
            s = jnp.where(mask, s, MASKED)
        tmax = jnp.max(s, axis=0, keepdims=True)
        vt = kqv(hd)[2][:, rows]
        rise = None
        if src is None:
            m_new = jnp.maximum(tmax, M_FLOOR)
            acc = _dot(vt, jnp.exp2(s - m_new).astype(BF16))
        else:
            m_old = src[0][hd, :, cols]
            m_new = jnp.maximum(m_old, tmax)
            alpha = jnp.exp2(m_old - m_new)
            if lazy:
                acc = (src[1][hd, :, cols] + _dot(vt, jnp.exp2(s - m_old).astype(BF16))) * alpha
                rise = tmax - m_old
            else:
                acc = alpha * src[1][hd, :, cols] + _dot(vt, jnp.exp2(s - m_new).astype(BF16))
        dst[0][hd, :, cols] = m_new
        dst[1][hd, :, cols] = acc
        return rise

    if not unroll:
        def body(hd, carry):
            for part in parts:
                update(hd, part, scores(hd, part))
            return carry
        lax.fori_loop(0, nheads, body, 0)
        return None
    todo = [(hd, part) for hd in range(nheads) for part in parts]
    ahead = 2 if lazy else 4
    pending = [scores(*t) for t in todo[:ahead]]
    worst = {}
    for n, (hd, part) in enumerate(todo):
        s = pending.pop(0)
        if n + ahead < len(todo):
            pending.append(scores(*todo[n + ahead]))
        rise = update(hd, part, s)
        if rise is not None:
            key = rise.shape
            worst[key] = rise if key not in worst else jnp.maximum(worst[key], rise)
    if not worst:
        return None
    tops = [jnp.max(w, axis=1, keepdims=True) for w in worst.values()]
    return functools.reduce(jnp.maximum, tops)


def _flash_finalize(o_ref, acc_ref):
    nv, rows, tq = acc_ref.shape
    dv = rows - SUM_ROWS
    groups = o_ref.shape[0] if len(o_ref.shape) == 3 else 1
    nh = nv // groups
    for gb in range(groups):
        sl = slice(gb * nh, (gb + 1) * nh)
        o = acc_ref[sl, :dv, :] * (1.0 / acc_ref[sl, dv:dv + 1, :])
        o = o.reshape(nh * dv, tq).T.astype(BF16)
        if len(o_ref.shape) == 3:
            o_ref[gb] = o
        else:
            o_ref[...] = o


def _flash_scratch(nh, dv, tq, slots=None):
    lead = (nh,) if slots is None else (slots, nh)
    return [pltpu.VMEM(lead + (1, tq), F32), pltpu.VMEM(lead + (dv + SUM_ROWS, tq), F32)]


def _causal_pairs(nqt):
    pairs = [(i, j) for i in range(nqt) for j in [i] + list(range(i))]
    return (jnp.asarray([p[0] for p in pairs], jnp.int32), jnp.asarray([p[1] for p in pairs], jnp.int32))


def _lazy_flash_step(nheads, kqv, parts, m_ref, acc_ref, cur):
    src = (m_ref.at[cur], acc_ref.at[cur])
    dst = (m_ref.at[1 - cur], acc_ref.at[1 - cur])
    worst = _flash_heads(nheads, kqv, parts, src, dst, lazy=True)

    @pl.when(jnp.max(worst) > LAZY_SPAN)
    def _():
        _flash_heads(nheads, kqv, parts, src, dst, unroll=False)


def _causal_flash_step(i, j, nheads, kqv, tq, o_ref, m_ref, acc_ref, slot_ref):
    @pl.when(j == i)
    def _():
        _flash_heads(nheads, kqv, _tile_parts("diag", tq), None, (m_ref.at[0], acc_ref.at[0]))
        slot_ref[0] = 0

    @pl.when(j < i)
    def _():
        cur = slot_ref[0]
        _lazy_flash_step(nheads, kqv, _tile_parts("full", tq), m_ref, acc_ref, cur)
        slot_ref[0] = 1 - cur

    @pl.when((j == i - 1) | (i == 0))
    def _():
        _flash_finalize(o_ref, acc_ref.at[slot_ref[0]])


def _split_virtual(v, nheads):
    if isinstance(v, int):
        return v // nheads, v % nheads
    return lax.div(v, nheads), lax.rem(v, nheads)


def _mla_kernel(qi_ref, ki_ref, q_ref, k_ref, vt_ref, o_ref, m_ref, acc_ref, slot_ref, *, tq):
    i, j = qi_ref[pl.program_id(0)], ki_ref[pl.program_id(0)]

    def kqv(v):
        bb, hd = _split_virtual(v, MLA_HEADS)
        return k_ref[bb, hd], q_ref[bb, hd], vt_ref[bb, hd]

    _causal_flash_step(i, j, q_ref.shape[0] * MLA_HEADS, kqv, tq, o_ref, m_ref, acc_ref, slot_ref)


def _mla_attention(qm, km, vmt, B, S, tq):
    nqt = S // tq
    kern = functools.partial(_mla_kernel, tq=tq)
    qi, ki = _causal_pairs(nqt)
    return pl.pallas_call(
        kern,
        grid_spec=pltpu.PrefetchScalarGridSpec(
            num_scalar_prefetch=2,
            grid=(qi.shape[0],),
            in_specs=[pl.BlockSpec((B, MLA_HEADS, LANES, tq), lambda p, qi, ki: (0, 0, 0, qi[p])),
                      pl.BlockSpec((B, MLA_HEADS, tq, LANES), lambda p, qi, ki: (0, 0, ki[p], 0)),
                      pl.BlockSpec((B, MLA_HEADS, MLA_V + SUM_ROWS, tq), lambda p, qi, ki: (0, 0, 0, ki[p]))],
            out_specs=pl.BlockSpec((B, tq, MLA_HEADS * MLA_V), lambda p, qi, ki: (0, qi[p], 0)),
            scratch_shapes=_flash_scratch(B * MLA_HEADS, MLA_V, tq, slots=2) + [pltpu.SMEM((1,), jnp.int32)]),
        out_shape=jax.ShapeDtypeStruct((B, S, MLA_HEADS * MLA_V), BF16),
        compiler_params=_cparams(("arbitrary",)),
        name="mla_attn",
    )(qi, ki, qm, km, vmt).reshape(B * S, MLA_HEADS * MLA_V)


def _slc_kernel(qi_ref, ki_ref, q_ref, bias_ref, k_ref, vt_ref, o_ref, qa_ref, m_ref, acc_ref,
                slot_ref, *, tq):
    i, j = qi_ref[pl.program_id(0)], ki_ref[pl.program_id(0)]
    nb = q_ref.shape[0]

    @pl.when(j == i)
    def _():
        for bb in range(nb):
            for hd in range(NSA_HEADS):
                qa_ref[bb * NSA_HEADS + hd] = q_ref[bb, hd]

    rows = pl.ds(pl.multiple_of(j * SEL_SLOTS, SEL_SLOTS), SEL_SLOTS)
    for bb in range(nb):
        for g in range(NSA_GROUPS):
            tile_bias = bias_ref[bb, g, rows, :]
            for hd in range(g * NSA_HPG, (g + 1) * NSA_HPG):
                qa_ref[bb * NSA_HEADS + hd, NSA_DIM:NSA_DIM + SEL_SLOTS, :] = tile_bias

    def kqv(v):
        bb, hd = _split_virtual(v, NSA_HEADS)
        g = hd // NSA_HPG if isinstance(hd, int) else lax.div(hd, NSA_HPG)
        return k_ref[bb, g], qa_ref[v], vt_ref[bb, g]

    _causal_flash_step(i, j, nb * NSA_HEADS, kqv, tq, o_ref, m_ref, acc_ref, slot_ref)


def _slc_attention(nq, bias, ksa, vst, B, S, tq):
    nqt = S // tq
    nrow = bias.shape[2]
    assert tq // SEL_LEN == SEL_SLOTS // 2
    kern = functools.partial(_slc_kernel, tq=tq)
    qi, ki = _causal_pairs(nqt)
    qmap = lambda p, qi, ki: (0, 0, 0, qi[p])
    return pl.pallas_call(
        kern,
        grid_spec=pltpu.PrefetchScalarGridSpec(
            num_scalar_prefetch=2,
            grid=(qi.shape[0],),
            in_specs=[pl.BlockSpec((B, NSA_HEADS, LANES, tq), qmap),
                      pl.BlockSpec((B, NSA_GROUPS, nrow, tq), qmap),
                      pl.BlockSpec((B, NSA_GROUPS, tq, LANES), lambda p, qi, ki: (0, 0, ki[p], 0)),
                      pl.BlockSpec((B, NSA_GROUPS, NSA_DIM + SUM_ROWS, tq), lambda p, qi, ki: (0, 0, 0, ki[p]))],
            out_specs=pl.BlockSpec((B, tq, NSA_HEADS * NSA_DIM), lambda p, qi, ki: (0, qi[p], 0)),
            scratch_shapes=[pltpu.VMEM((B * NSA_HEADS, LANES, tq), BF16)]
                           + _flash_scratch(B * NSA_HEADS, NSA_DIM, tq, slots=2)
                           + [pltpu.SMEM((1,), jnp.int32)]),
        out_shape=jax.ShapeDtypeStruct((B, S, NSA_HEADS * NSA_DIM), BF16),
        compiler_params=_cparams(("arbitrary",)),
        name="slc_attn",
    )(qi, ki, nq, bias, ksa, vst).reshape(B * S, NSA_HEADS * NSA_DIM)


def _win_kernel(q_ref, k_ref, vt_ref, o_ref, m_ref, acc_ref, *, tq):
    i, j = pl.program_id(0), pl.program_id(1)

    def kqv(v):
        bb, hd = _split_virtual(v, NSA_HEADS)
        g = hd // NSA_HPG if isinstance(hd, int) else lax.div(hd, NSA_HPG)
        return k_ref[bb, g], q_ref[bb, hd][:NSA_DIM, :], vt_ref[bb, g]

    nv = q_ref.shape[0] * NSA_HEADS

    @pl.when(j == 0)
    def _():
        _flash_heads(nv, kqv, _tile_parts("diag", tq), None, (m_ref.at[0], acc_ref.at[0]))

    @pl.when((j == 1) & (i > 0))
    def _():
        _lazy_flash_step(nv, kqv, _tile_parts("prev", tq), m_ref, acc_ref, 0)

    @pl.when(j == 1)
    def _():
        _flash_finalize(o_ref, acc_ref.at[jnp.where(i > 0, 1, 0)])


def _win_attention(nq, kw, vwt, B, S, tq):
    nqt = S // tq
    kern = functools.partial(_win_kernel, tq=tq)
    kidx = lambda i, j: jnp.maximum(i - j, 0)
    return pl.pallas_call(
        kern,
        grid=(nqt, 2),
        in_specs=[pl.BlockSpec((B, NSA_HEADS, LANES, tq), lambda i, j: (0, 0, 0, i)),
                  pl.BlockSpec((B, NSA_GROUPS, tq, NSA_DIM), lambda i, j: (0, 0, kidx(i, j), 0)),
                  pl.BlockSpec((B, NSA_GROUPS, NSA_DIM + SUM_ROWS, tq), lambda i, j: (0, 0, 0, kidx(i, j)))],
        out_specs=pl.BlockSpec((B, tq, NSA_HEADS * NSA_DIM), lambda i, j: (0, i, 0)),
        out_shape=jax.ShapeDtypeStruct((B, S, NSA_HEADS * NSA_DIM), BF16),
        scratch_shapes=_flash_scratch(B * NSA_HEADS, NSA_DIM, tq, slots=2),
        compiler_params=_cparams(("parallel", "arbitrary")),
        name="win_attn",
    )(nq, kw, vwt).reshape(B * S, NSA_HEADS * NSA_DIM)


def _causal_conv(u, carry, w_ref):
    row = lax.broadcasted_iota(jnp.int32, u.shape, 0)
    c1 = carry[SUBLANES - 1:SUBLANES]
    c2 = carry[SUBLANES - 2:SUBLANES - 1]
    u1 = jnp.where(row == 0, c1, pltpu.roll(u, 1, axis=0))
    u2 = jnp.where(row == 0, c2, jnp.where(row == 1, c1, pltpu.roll(u, 2, axis=0)))
    return w_ref[0:1] * u2 + w_ref[1:2] * u1 + w_ref[2:3] * u


def _merge_kernel(x_ref, g_ref, wgc_ref, cw_ref, ym_ref, oc_ref, os_ref, ow_ref, gl_ref, ex_ref,
                  wbc_ref, wbm_ref, wbn_ref, wo_ref, gp_ref, out_ref, carry_ref, *, tiles_per_seq):
    D = x_ref.shape[1]
    DC = wbc_ref.shape[0]

    @pl.when(pl.program_id(0) % tiles_per_seq == 0)
    def _():
        carry_ref[...] = jnp.zeros(carry_ref.shape, F32)

    x = x_ref[...]
    h = _rms(x, g_ref[...]).astype(BF16)
    p = _dot(h, wgc_ref[...])
    u = p[:, 3 * D + DC:3 * D + 2 * DC] * p[:, 3 * D + 2 * DC:3 * D + 3 * DC]
    y_conv = p[:, 3 * D:3 * D + DC] * _causal_conv(u, carry_ref[...], cw_ref)
    carry_ref[...] = u[u.shape[0] - SUBLANES:]
    hi, lo = _split_bf16(gl_ref[...], 2)
    gate = jax.nn.sigmoid(_dot(hi, ex_ref[...]) + _dot(lo, ex_ref[...]))
    HD = oc_ref.shape[1]
    y_nsa = (gate[:, 0:HD] * oc_ref[...].astype(F32) + gate[:, HD:2 * HD] * os_ref[...].astype(F32)
             + gate[:, 2 * HD:3 * HD] * ow_ref[...].astype(F32))
    merged = (jax.nn.sigmoid(p[:, 0:D]) * _dot(y_conv.astype(BF16), wbc_ref[...])
              + jax.nn.sigmoid(p[:, D:2 * D]) * _dot(ym_ref[...], wbm_ref[...])
              + jax.nn.sigmoid(p[:, 2 * D:3 * D]) * _dot(y_nsa.astype(BF16), wbn_ref[...]))
    out_ref[...] = x + _rms(_dot(merged.astype(BF16), wo_ref[...]), gp_ref[...])


def _merge(x2, g_pre, w_gc, conv_w, y_mla, o_cmp, o_slc, o_win, gl, ex, wbc, wbm, wbn, wo, g_post,
           S, tm):
    T, D = x2.shape
    HD = y_mla.shape[1]
    row = lambda w: pl.BlockSpec((tm, w), lambda i: (i, 0))
    kern = functools.partial(_merge_kernel, tiles_per_seq=S // tm)
    return pl.pallas_call(
        kern,
        grid=(T // tm,),
        in_specs=[row(D), _const_spec((1, D)), _const_spec(w_gc.shape), _const_spec(conv_w.shape),
                  row(HD), row(HD), row(HD), row(HD), row(LANES), _const_spec(ex.shape),
                  _const_spec(wbc.shape), _const_spec(wbm.shape), _const_spec(wbn.shape),
                  _const_spec(wo.shape), _const_spec((1, D))],
        out_specs=row(D),
        out_shape=jax.ShapeDtypeStruct((T, D), F32),
        scratch_shapes=[pltpu.VMEM((SUBLANES, wbc.shape[0]), F32)],
        compiler_params=_cparams(("arbitrary",)),
        name="merge",
    )(x2, g_pre, w_gc, conv_w, y_mla, o_cmp, o_slc, o_win, gl, ex, wbc, wbm, wbn, wo, g_post)


def _ffn_kernel(x_ref, g_ref, wup_ref, cw_ref, cb_ref, wdn_ref, gp_ref, out_ref, carry_ref,
                *, tiles_per_seq, chunk):
    F = wdn_ref.shape[0]

    @pl.when(pl.program_id(0) % tiles_per_seq == 0)
    def _():
        carry_ref[...] = jnp.zeros(carry_ref.shape, F32)

    x = x_ref[...]
    h = _rms(x, g_ref[...]).astype(BF16)
    acc = None
    for c0 in range(0, F, chunk):
        a = _dot(h, wup_ref[:, c0:c0 + chunk])
        b = _dot(h, wup_ref[:, F + c0:F + c0 + chunk])
        ac = _causal_conv(a, carry_ref[:, c0:c0 + chunk], cw_ref.at[:, c0:c0 + chunk])
        ac = ac + cb_ref[:, c0:c0 + chunk]
        carry_ref[:, c0:c0 + chunk] = a[a.shape[0] - SUBLANES:]
        d = _dot((jax.nn.gelu(ac) * b).astype(BF16), wdn_ref[c0:c0 + chunk, :])
        acc = d if acc is None else acc + d
    out_ref[...] = x + _rms(acc, gp_ref[...])


def _ffn(x2, g_pre, w_up, conv_w, conv_b, w_dn, g_post, S, tm, chunk):
    T, D = x2.shape
    F = w_dn.shape[0]
    row = pl.BlockSpec((tm, D), lambda i: (i, 0))
    kern = functools.partial(_ffn_kernel, tiles_per_seq=S // tm, chunk=chunk)
    return pl.pallas_call(
        kern,
        grid=(T // tm,),
        in_specs=[row, _const_spec((1, D)), _const_spec(w_up.shape), _const_spec(conv_w.shape),
                  _const_spec((1, F)), _const_spec(w_dn.shape), _const_spec((1, D))],
        out_specs=row,
        out_shape=jax.ShapeDtypeStruct((T, D), F32),
        scratch_shapes=[pltpu.VMEM((SUBLANES, F), F32)],
        compiler_params=_cparams(("arbitrary",)),
        name="ffn",
    )(x2, g_pre, w_up, conv_w, conv_b, w_dn, g_post)


def _overlap_matrix_t(S):
    nch = S // CMP_STRIDE
    n_cmp = (S - CMP_LEN) // CMP_STRIDE + 1
    cs = np.arange(nch)[None, :] * CMP_STRIDE
    ss = np.arange(S // SEL_LEN)[:, None] * SEL_LEN
    ov = (cs <= ss + SEL_LEN - 1) & (cs + CMP_LEN - 1 >= ss) & (np.arange(nch)[None, :] < n_cmp)
    return ov.astype(np.float32)


def _gate_expand():
    ex = np.zeros((LANES, 3 * NSA_HEADS * NSA_DIM), np.float32)
    for hd in range(NSA_HEADS):
        for c in range(3):
            ex[hd * 3 + c, c * NSA_HEADS * NSA_DIM + hd * NSA_DIM:
               c * NSA_HEADS * NSA_DIM + (hd + 1) * NSA_DIM] = 1.0
    return ex


def _pad_rows(w, rows):
    return jnp.concatenate([w, jnp.zeros((rows - w.shape[0],) + w.shape[1:], w.dtype)], axis=0)


def kernel(x, positions, norm_mix_pre, norm_mix_post, w_in, conv_w, mla_q_norm, mla_w_uq, mla_kv_norm, mla_w_ukv, nsa_cmp_pos_k, nsa_cmp_pos_v, nsa_cmp_w_k, nsa_cmp_w_v, w_branch_conv, w_branch_mla, w_branch_nsa, w_out, norm_ffn_pre, norm_ffn_post, ffn_w_up, ffn_conv_w, ffn_conv_b, ffn_w_down):
    B, S, D = x.shape
    T = B * S
    depth = w_in.shape[0]
    DC = conv_w.shape[2]
    F = ffn_w_down.shape[1]
    QL = mla_q_norm.shape[1]
    KL = mla_kv_norm.shape[1]
    tq = 512
    assert S % tq == 0 and WINDOW == tq and N_SEL <= S // SEL_LEN <= LANES
    tm_proj, tm_merge, tm_ffn, ffn_chunk = 512, 512, 512, 1408
    assert F % ffn_chunk == 0 and ffn_chunk % LANES == 0

    x2 = x.reshape(T, D)
    pos2 = positions.reshape(1, T)
    rc = jnp.asarray(_rope_consts())
    ovt = jnp.asarray(_overlap_matrix_t(S), BF16)
    ex = jnp.asarray(_gate_expand(), BF16)
    half = CMP_LEN // 2 * NSA_DIM
    o_att = 3 * D + 3 * DC

    for l in range(depth):
        w = w_in[l]
        o_kr = o_att + QL + KL
        z = lambda n: jnp.zeros((D, n), F32)
        w_att = jnp.concatenate(
            [w[:, o_att:o_kr], z(MLA_NOPE), w[:, o_kr:o_kr + MLA_ROPE], z(LANES - MLA_NOPE - MLA_ROPE),
             w[:, o_kr + MLA_ROPE:o_kr + MLA_ROPE + NSA_HEADS * NSA_DIM + 6 * NSA_GROUPS * NSA_DIM],
             w[:, w.shape[1] - 3 * NSA_HEADS:], z(LANES - 3 * NSA_HEADS)], axis=1).astype(BF16)
        dq = MLA_NOPE + MLA_ROPE
        w_uq = jnp.pad(mla_w_uq[l].reshape(QL, MLA_HEADS, dq),
                       ((0, 0), (0, 0), (0, LANES - dq))).reshape(QL, MLA_HEADS * LANES).astype(BF16)
        w_v_t = mla_w_ukv[l].reshape(KL, MLA_HEADS, MLA_NOPE + MLA_V)[:, :, MLA_NOPE:].reshape(
            KL, MLA_HEADS * MLA_V).T.astype(BF16)
        (qm, km, vmt, nq, kc_in, vc_in, ksa, vst, kw, vwt, gl) = _project(
            x2, pos2, norm_mix_pre[l][None], w_att, mla_q_norm[l][None], w_uq.T,
            mla_kv_norm[l][None], mla_w_ukv[l].astype(BF16), w_v_t, rc, S, tm_proj)

        wv = nsa_cmp_w_v[l].reshape(2, half, NSA_DIM)
        zv = jnp.zeros_like(wv)
        wv_pad = jnp.stack([jnp.concatenate([wv, zv], axis=2), jnp.concatenate([zv, wv], axis=2)])
        kc, vct = _compress(kc_in, vc_in,
                            nsa_cmp_pos_k[l].reshape(2, half), nsa_cmp_pos_v[l].reshape(2, half),
                            nsa_cmp_w_k[l].reshape(2, half, NSA_DIM).astype(BF16),
                            wv_pad.astype(BF16), B)

        y_mla = _mla_attention(qm, km, vmt, B, S, tq)
        o_cmp, bias = _cmp_select(nq, kc, vct, ovt, B, S, tq)
        o_slc = _slc_attention(nq, bias, ksa, vst, B, S, tq)
        o_win = _win_attention(nq, kw, vwt, B, S, tq)

        x2 = _merge(x2, norm_mix_pre[l][None], w[:, :o_att].astype(BF16),
                    _pad_rows(conv_w[l], SUBLANES), y_mla, o_cmp, o_slc, o_win, gl, ex,
                    w_branch_conv[l].astype(BF16), w_branch_mla[l].astype(BF16),
                    w_branch_nsa[l].astype(BF16), w_out[l].astype(BF16), norm_mix_post[l][None],
                    S, tm_merge)
        x2 = _ffn(x2, norm_ffn_pre[l][None], ffn_w_up[l].astype(BF16),
                  _pad_rows(ffn_conv_w[l], SUBLANES), ffn_conv_b[l][None],
                  ffn_w_down[l].astype(BF16), norm_ffn_post[l][None], S, tm_ffn, ffn_chunk)
    return x2.reshape(B, S, D)
```

```python
import functools
import math

import numpy as np
import jax
import jax.numpy as jnp
from jax import lax
from jax.experimental import pallas as pl
from jax.experimental.pallas import tpu as pltpu

F32 = jnp.float32
BF16 = jnp.bfloat16

ROPE_THETA = 500000.0
RMS_EPS = 1e-6
CONV_WIDTH = 3
MLA_HEADS = 8
MLA_NOPE = 64
MLA_ROPE = 32
MLA_V = 64
NSA_HEADS = 8
NSA_GROUPS = 2
NSA_HPG = NSA_HEADS // NSA_GROUPS
NSA_DIM = 64
NSA_ROT = NSA_DIM // 4
CMP_LEN = 32
CMP_STRIDE = 16
SEL_LEN = 64
N_SEL = 16
WINDOW = 512

LANES = 128
SUBLANES = 8
VMEM_LIMIT = 56 * 1024 * 1024
MASKED = -1e30
M_FLOOR = -1e20
SEL_BIAS = -2.0 ** 100
LAZY_SPAN = 64.0
LOG2E = math.log2(math.e)
SEL_SLOTS = 16
ROPE_ROWS = 32
SUM_ROWS = 16


def _cparams(sem):
    return pltpu.CompilerParams(dimension_semantics=sem, vmem_limit_bytes=VMEM_LIMIT)


def _const_spec(shape):
    nd = len(shape)
    return pl.BlockSpec(shape, lambda *_: (0,) * nd, pipeline_mode=pl.Buffered(1))


def _rms(x, g):
    return x * lax.rsqrt(jnp.mean(x * x, axis=-1, keepdims=True) + RMS_EPS) * g


def _dot(a, b):
    return jnp.dot(a, b, preferred_element_type=F32)


def _split_bf16(a, terms):
    pieces, rem = [], a
    for _ in range(terms):
        piece = rem.astype(BF16)
        rem = rem - piece.astype(F32)
        pieces.append(piece)
    return pieces


def _rope(x, cos, sin_up, sin_dn, half):
    return (x * cos + pltpu.roll(x, half, axis=1) * sin_up
            + pltpu.roll(x, LANES - half, axis=1) * sin_dn)


def _lanes(table, lane, base, moves):
    out = base
    for dst, width, src in moves:
        rolled = pltpu.roll(table, (dst - src) % LANES, axis=1)
        out = jnp.where((lane >= dst) & (lane < dst + width), rolled, out)
    return out


def _proj_kernel(x_ref, pos_ref, g_ref, watt_ref, qg_ref, wuqt_ref, kvg_ref, wukv_ref, wvt_ref, rc_ref,
                 qm_ref, km_ref, vmt_ref, nq_ref, kc_ref, vc_ref, ksa_ref, vst_ref, kw_ref, vwt_ref,
                 gl_ref, kstage_ref, vstage_ref, *, mla_scale, nsa_scale, tiles_per_seq):
    tm = x_ref.shape[0]
    hm, hn = MLA_ROPE // 2, NSA_ROT // 2
    x = x_ref[...]
    h = _rms(x, g_ref[...]).astype(BF16)
    p = _dot(h, watt_ref[...])
    ql, kl = qg_ref.shape[1], kvg_ref.shape[1]
    c_kr, c_nq, c_kv, c_gl = _att_columns(ql, kl)
    lane = lax.broadcasted_iota(jnp.int32, (tm, LANES), 1)
    low = lane < NSA_DIM
    ang_t = rc_ref[:, 0:1] * pos_ref[...].astype(F32)
    cos_t, sin_t = jnp.cos(ang_t), jnp.sin(ang_t)
    cs = jnp.concatenate([cos_t, sin_t, jnp.zeros((LANES - 2 * ROPE_ROWS, tm), F32)], axis=0).T
    cos_m = _lanes(cs, lane, 1.0, [(MLA_NOPE, hm, 0), (MLA_NOPE + hm, hm, 0)])
    sup_m = _lanes(cs, lane, 0.0, [(MLA_NOPE + hm, hm, ROPE_ROWS)])
    sdn_m = -_lanes(cs, lane, 0.0, [(MLA_NOPE, hm, ROPE_ROWS)])
    cos_n = _lanes(cs, lane, 1.0, [(0, hn, hm), (hn, hn, hm), (NSA_DIM, hn, hm), (NSA_DIM + hn, hn, hm)])
    sup_n = _lanes(cs, lane, 0.0, [(hn, hn, ROPE_ROWS + hm), (NSA_DIM + hn, hn, ROPE_ROWS + hm)])
    sdn_n = -_lanes(cs, lane, 0.0, [(0, hn, ROPE_ROWS + hm), (NSA_DIM, hn, ROPE_ROWS + hm)])
    ones_rows = jnp.where(lax.broadcasted_iota(jnp.int32, (SUM_ROWS, tm), 0) == 0, 1.0, 0.0)

    def with_sum_rows(vt):
        return jnp.concatenate([vt, ones_rows], axis=0).astype(BF16)

    def halves(slab):
        return (jnp.where(low, slab, 0.0), jnp.where(low, pltpu.roll(slab, NSA_DIM, axis=1), 0.0))

    def rope_rows(x1, x2, c, s_):
        return x1 * c - x2 * s_, x2 * c + x1 * s_

    cqn_t = _rms(p[:, 0:ql], qg_ref[...]).T.astype(BF16)
    qt = _dot(wuqt_ref[...], cqn_t)
    for hd in range(MLA_HEADS):
        r0 = hd * LANES
        y1, y2 = rope_rows(qt[r0 + MLA_NOPE:r0 + MLA_NOPE + hm], qt[r0 + MLA_NOPE + hm:r0 + MLA_NOPE + 2 * hm],
                           cos_t[0:hm], sin_t[0:hm])
        slab_t = jnp.concatenate([qt[r0:r0 + MLA_NOPE], y1, y2, qt[r0 + MLA_NOPE + 2 * hm:r0 + LANES]], axis=0)
        qm_ref[0, hd] = (slab_t * mla_scale).astype(BF16)
    ckvn = _rms(p[:, ql:ql + kl], kvg_ref[...])
    kv = _dot(ckvn.astype(BF16), wukv_ref[...])
    kr = _rope(p[:, c_kr:c_kr + LANES], cos_m, sup_m, sdn_m, hm)
    vt = _dot(wvt_ref[...], ckvn.T.astype(BF16))
    for hd in range(MLA_HEADS):
        km_ref[0, hd] = jnp.where(lane < MLA_NOPE, kv[:, hd * LANES:(hd + 1) * LANES], kr).astype(BF16)
        vmt_ref[0, hd] = with_sum_rows(vt[hd * MLA_V:(hd + 1) * MLA_V])
    zpad = jnp.zeros((LANES - NSA_DIM, tm), BF16)
    for pr in range(NSA_HEADS // 2):
        slab_t = p[:, c_nq + pr * LANES:c_nq + (pr + 1) * LANES].T
        for half in range(2):
            r0 = half * NSA_DIM
            y1, y2 = rope_rows(slab_t[r0:r0 + hn], slab_t[r0 + hn:r0 + 2 * hn],
                               cos_t[hm:hm + hn], sin_t[hm:hm + hn])
            head_t = jnp.concatenate([y1, y2, slab_t[r0 + 2 * hn:r0 + NSA_DIM]], axis=0) * nsa_scale
            nq_ref[0, 2 * pr + half, 0:NSA_DIM, :] = head_t.astype(BF16)
            nq_ref[0, 2 * pr + half, NSA_DIM:LANES, :] = zpad
    nsa = lambda idx: p[:, c_kv + idx * LANES:c_kv + (idx + 1) * LANES]
    def store_chunks(slab, stage_ref, out_ref):
        stage_ref[...] = slab
        nrow = tm // CMP_STRIDE
        first = lax.broadcasted_iota(jnp.int32, (nrow, LANES), 1) < NSA_DIM
        for r in range(0, CMP_STRIDE, 2):
            a = stage_ref[pl.ds(r, nrow, stride=CMP_STRIDE), :]
            b = stage_ref[pl.ds(r + 1, nrow, stride=CMP_STRIDE), :]
            cols = slice(r * NSA_DIM, (r + 2) * NSA_DIM)
            out_ref[0, :, cols] = jnp.where(first, a, pltpu.roll(b, NSA_DIM, axis=1)).astype(BF16)
            out_ref[1, :, cols] = jnp.where(first, pltpu.roll(a, NSA_DIM, axis=1), b).astype(BF16)

    store_chunks(_rope(nsa(0), cos_n, sup_n, sdn_n, hn), kstage_ref, kc_ref)
    store_chunks(nsa(1), vstage_ref, vc_ref)
    t_seq = (pl.program_id(0) % tiles_per_seq) * tm + lax.broadcasted_iota(jnp.int32, (tm, LANES), 0)
    slot = NSA_DIM + (lax.shift_right_logical(t_seq, int(math.log2(SEL_LEN))) & (SEL_SLOTS // 2 - 1))
    onehot = jnp.where(lane == slot, 1.0, 0.0)
    for g, kg in enumerate(halves(_rope(nsa(2), cos_n, sup_n, sdn_n, hn))):
        ksa_ref[0, g] = (kg + onehot).astype(BF16)
    vst = nsa(3).T
    vst_ref[0, 0] = with_sum_rows(vst[:NSA_DIM])
    vst_ref[0, 1] = with_sum_rows(vst[NSA_DIM:])
    kws = _rope(nsa(4), cos_n, sup_n, sdn_n, hn)
    kw_ref[0, 0] = kws[:, :NSA_DIM].astype(BF16)
    kw_ref[0, 1] = pltpu.roll(kws, NSA_DIM, axis=1)[:, :NSA_DIM].astype(BF16)
    vwt = nsa(5).T
    vwt_ref[0, 0] = with_sum_rows(vwt[:NSA_DIM])
    vwt_ref[0, 1] = with_sum_rows(vwt[NSA_DIM:])
    gl_ref[...] = p[:, c_gl:c_gl + LANES]


def _att_columns(ql, kl):
    c_kr = ql + kl
    c_nq = c_kr + LANES
    c_kv = c_nq + NSA_HEADS * NSA_DIM
    c_gl = c_kv + 6 * NSA_GROUPS * NSA_DIM
    return c_kr, c_nq, c_kv, c_gl


def _rope_consts():
    rc = np.zeros((ROPE_ROWS, LANES), np.float32)
    hm = MLA_ROPE // 2
    rc[:hm] = (ROPE_THETA ** (-np.arange(hm, dtype=np.float32) / hm))[:, None]
    hn = NSA_ROT // 2
    rc[hm:hm + hn] = (ROPE_THETA ** (-np.arange(hn, dtype=np.float32) / hn))[:, None]
    return rc


def _project(x2, pos2, g_pre, w_att, q_g, w_uq_t, kv_g, w_ukv, w_v_t, rc, S, tm):
    T, D = x2.shape
    n_att = w_att.shape[1]
    tps = S // tm
    B = T // S
    brows = lambda n, d: (jax.ShapeDtypeStruct((B, n, S, d), BF16),
                          pl.BlockSpec((1, n, tm, d), lambda i: (i // tps, 0, i % tps, 0)))
    bcols = lambda n, d: (jax.ShapeDtypeStruct((B, n, d, S), BF16),
                          pl.BlockSpec((1, n, d, tm), lambda i: (i // tps, 0, 0, i % tps)))
    chunked = (jax.ShapeDtypeStruct((NSA_GROUPS, T // CMP_STRIDE, CMP_STRIDE * NSA_DIM), BF16),
               pl.BlockSpec((NSA_GROUPS, tm // CMP_STRIDE, CMP_STRIDE * NSA_DIM), lambda i: (0, i, 0)))
    outs = [bcols(MLA_HEADS, LANES), brows(MLA_HEADS, LANES), bcols(MLA_HEADS, MLA_V + SUM_ROWS),
            bcols(NSA_HEADS, LANES), chunked, chunked,
            brows(NSA_GROUPS, LANES), bcols(NSA_GROUPS, NSA_DIM + SUM_ROWS), brows(NSA_GROUPS, NSA_DIM),
            bcols(NSA_GROUPS, NSA_DIM + SUM_ROWS),
            (jax.ShapeDtypeStruct((T, LANES), F32), pl.BlockSpec((tm, LANES), lambda i: (i, 0)))]
    kern = functools.partial(_proj_kernel,
                             mla_scale=float((MLA_NOPE + MLA_ROPE) ** -0.5 * LOG2E),
                             nsa_scale=float(NSA_DIM ** -0.5 * LOG2E),
                             tiles_per_seq=S // tm)
    return pl.pallas_call(
        kern,
        grid=(T // tm,),
        in_specs=[
            pl.BlockSpec((tm, D), lambda i: (i, 0)),
            pl.BlockSpec((1, tm), lambda i: (0, i)),
            _const_spec((1, D)),
            _const_spec((D, n_att)),
            _const_spec((1, q_g.shape[1])),
            _const_spec(w_uq_t.shape),
            _const_spec((1, kv_g.shape[1])),
            _const_spec(w_ukv.shape),
            _const_spec(w_v_t.shape),
            _const_spec((ROPE_ROWS, LANES)),
        ],
        out_specs=[o[1] for o in outs],
        out_shape=[o[0] for o in outs],
        scratch_shapes=[pltpu.VMEM((tm, LANES), F32), pltpu.VMEM((tm, LANES), F32)],
        compiler_params=_cparams(("parallel",)),
        name="proj",
    )(x2, pos2, g_pre, w_att, q_g, w_uq_t, kv_g, w_ukv, w_v_t, rc)


def _compress_kernel(xk_ref, xv_ref, pk_ref, pv_ref, wk_ref, wv_ref, ok_ref, ovt_ref):
    n = xk_ref.shape[1]
    vboth = None
    for g in range(NSA_GROUPS):
        xk = xk_ref[g].astype(F32)
        lo = _dot((xk + pk_ref[0:1]).astype(BF16), wk_ref[0])
        hi = _dot((xk + pk_ref[1:2]).astype(BF16), wk_ref[1])
        ok_ref[g] = (lo + pltpu.roll(hi, n - 1, axis=0)).astype(BF16)
        xv = xv_ref[g].astype(F32)
        lo = _dot((xv + pv_ref[0:1]).astype(BF16), wv_ref[g, 0])
        hi = _dot((xv + pv_ref[1:2]).astype(BF16), wv_ref[g, 1])
        vg = lo + pltpu.roll(hi, n - 1, axis=0)
        vboth = vg if vboth is None else vboth + vg
    ovt_ref[0] = vboth.T.astype(BF16)


def _compress(xk, xv, pk, pv, wk, wv, B):
    G, TC, CW = xk.shape
    nch = TC // B
    xspec = pl.BlockSpec((G, nch, CW), lambda b: (0, b, 0))
    return pl.pallas_call(
        _compress_kernel,
        grid=(B,),
        in_specs=[xspec, xspec, _const_spec(pk.shape), _const_spec(pv.shape),
                  _const_spec(wk.shape), _const_spec(wv.shape)],
        out_specs=[pl.BlockSpec((G, nch, NSA_DIM), lambda b: (0, b, 0)),
                   pl.BlockSpec((1, G * NSA_DIM, nch), lambda b: (b, 0, 0))],
        out_shape=[jax.ShapeDtypeStruct((G, TC, NSA_DIM), BF16),
                   jax.ShapeDtypeStruct((B, G * NSA_DIM, nch), BF16)],
        compiler_params=_cparams(("parallel",)),
        name="compress",
    )(xk, xv, pk, pv, wk, wv)


def _cmp_select_kernel(q_ref, kc_ref, vct_ref, ovt_ref, o_ref, bias_ref, *, tq, n_pick, nqt, n_var):
    i = pl.program_id(1)
    q0 = i * tq
    nch_all = kc_ref.shape[1]
    nblk_all = ovt_ref.shape[0]
    tiles_per_var = nqt // n_var

    def variant(nch, nblk):
        for c0 in range(0, tq, tq // 2):
            half(nch, nblk, c0, tq // 2)

    def half(nch, nblk, c0, w):
        cols = slice(c0, c0 + w)
        t_c = q0 + c0 + lax.broadcasted_iota(jnp.int32, (nch, w), 1)
        n_c = lax.broadcasted_iota(jnp.int32, (nch, w), 0)
        cmask = (n_c * CMP_STRIDE + (CMP_LEN - 1)) <= t_c
        t_b = q0 + c0 + lax.broadcasted_iota(jnp.int32, (nblk, w), 1)
        blk = lax.broadcasted_iota(jnp.int32, (nblk, w), 0)
        blk_f = blk.astype(F32)
        cur = lax.shift_right_logical(t_b, int(math.log2(SEL_LEN)))
        forced = (blk == 0) | (blk == cur) | (blk == cur - 1)
        causal = blk * SEL_LEN <= t_b
        outs = []
        for g in range(NSA_GROUPS):
            psum = None
            scores = [_dot(kc_ref[g, :nch, :], q_ref[0, g * NSA_HPG + j][:NSA_DIM, cols])
                      for j in range(NSA_HPG)]
            for j in range(NSA_HPG):
                s = jnp.where(cmask, scores[j], MASKED)
                m = jnp.maximum(jnp.max(s, axis=0, keepdims=True), M_FLOOR)
                p = jnp.exp2(s - m)
                l = jnp.sum(p, axis=0, keepdims=True)
                pn = p * jnp.where(l > 0.0, 1.0 / l, 0.0)
                outs.append(_dot(vct_ref[0, g * NSA_DIM:(g + 1) * NSA_DIM, :nch], pn.astype(BF16)))
                psum = pn if psum is None else psum + pn
            imp = None
            for piece in _split_bf16(psum, 3):
                d = _dot(ovt_ref[:nblk, :nch], piece)
                imp = d if imp is None else imp + d
            v0 = jnp.where(forced | jnp.logical_not(causal), -1.0, imp)
            v = v0
            for _ in range(n_pick):
                mx = jnp.max(v, axis=0, keepdims=True)
                first = jnp.min(jnp.where(v == mx, blk_f, float(nblk)), axis=0, keepdims=True)
                v = jnp.where(blk_f == first, -2.0, v)
            sel = forced | ((v == -2.0) & (v0 >= 0.0))
            bias = jnp.where(sel, 0.0, SEL_BIAS)
            per_tile = SEL_SLOTS // 2
            pad = jnp.zeros((SEL_SLOTS - per_tile, w), F32)
            pieces = []
            for kt in range(nblk_all // per_tile):
                if kt * per_tile < nblk:
                    pieces += [bias[kt * per_tile:(kt + 1) * per_tile], pad]
                else:
                    pieces += [jnp.full((per_tile, w), SEL_BIAS, F32), pad]
            bias_ref[0, g, :, cols] = jnp.concatenate(pieces, axis=0).astype(BF16)
        o_ref[cols, :] = jnp.concatenate(outs, axis=0).T.astype(BF16)

    for k in range(n_var):
        @pl.when((i >= k * tiles_per_var) & (i < (k + 1) * tiles_per_var))
        def _():
            variant(nch_all * (k + 1) // n_var, nblk_all * (k + 1) // n_var)


def _cmp_select(nq, kc, vct, ovt, B, S, tq):
    T = B * S
    nch = kc.shape[1] // B
    nblk = ovt.shape[0]
    nqt = S // tq
    n_var = 4 if nqt % 4 == 0 and (nch // 4) % LANES == 0 and (nblk // 4) % SUBLANES == 0 else 1
    kern = functools.partial(_cmp_select_kernel, tq=tq, n_pick=N_SEL - 3, nqt=nqt, n_var=n_var)
    return pl.pallas_call(
        kern,
        grid=(B, nqt),
        in_specs=[
            pl.BlockSpec((1, NSA_HEADS, LANES, tq), lambda b, i: (b, 0, 0, i)),
            pl.BlockSpec((NSA_GROUPS, nch, NSA_DIM), lambda b, i: (0, b, 0)),
            pl.BlockSpec((1, NSA_GROUPS * NSA_DIM, nch), lambda b, i: (b, 0, 0)),
            _const_spec(ovt.shape),
        ],
        out_specs=[pl.BlockSpec((tq, NSA_HEADS * NSA_DIM), lambda b, i: (b * nqt + i, 0)),
                   pl.BlockSpec((1, NSA_GROUPS, 2 * nblk, tq), lambda b, i: (b, 0, 0, i))],
        out_shape=[jax.ShapeDtypeStruct((T, NSA_HEADS * NSA_DIM), BF16),
                   jax.ShapeDtypeStruct((B, NSA_GROUPS, 2 * nblk, S), BF16)],
        compiler_params=_cparams(("parallel", "parallel")),
        name="cmp_select",
    )(nq, kc, vct, ovt)


def _tile_parts(kind, tq):
    hq = tq // 2
    if kind == "full":
        return [(slice(0, tq), slice(0, hq), None), (slice(0, tq), slice(hq, tq), None)]

    def mask(rows, cols, below):
        key = rows.start + lax.broadcasted_iota(jnp.int32, (rows.stop - rows.start, hq), 0)
        qry = cols.start + lax.broadcasted_iota(jnp.int32, (rows.stop - rows.start, hq), 1)
        return (key <= qry) if below else (key > qry)

    lo, hi, all_ = slice(0, hq), slice(hq, tq), slice(0, tq)
    if kind == "diag":
        return [(lo, lo, mask(lo, lo, True)), (all_, hi, mask(all_, hi, True))]
    return [(all_, lo, mask(all_, lo, False)), (hi, hi, mask(hi, hi, False))]


def _flash_heads(nheads, kqv, parts, src, dst, lazy=False, unroll=True):
    def scores(hd, part):
        rows, cols, _ = part
        k, qt, _ = kqv(hd)
        return _dot(k[rows], qt[:, cols])

    def update(hd, part, s):
        rows, cols, mask = part
        if mask is not None:
            s = jnp.where(mask, s, MASKED)
        tmax = jnp.max(s, axis=0, keepdims=True)
        vt = kqv(hd)[2][:, rows]
        rise = None
        if src is None:
            m_new = jnp.maximum(tmax, M_FLOOR)
            acc = _dot(vt, jnp.exp2(s - m_new).astype(BF16))
        else:
            m_old = src[0][hd, :, cols]
            m_new = jnp.maximum(m_old, tmax)
            alpha = jnp.exp2(m_old - m_new)
            if lazy:
                acc = (src[1][hd, :, cols] + _dot(vt, jnp.exp2(s - m_old).astype(BF16))) * alpha
                rise = tmax - m_old
            else:
                acc = alpha * src[1][hd, :, cols] + _dot(vt, jnp.exp2(s - m_new).astype(BF16))
        dst[0][hd, :, cols] = m_new
        dst[1][hd, :, cols] = acc
        return rise

    if not unroll:
        def body(hd, carry):
            for part in parts:
                update(hd, part, scores(hd, part))
            return carry
        lax.fori_loop(0, nheads, body, 0)
        return None
    todo = [(hd, part) for hd in range(nheads) for part in parts]
    ahead = 2 if lazy else 4
    pending = [scores(*t) for t in todo[:ahead]]
    worst = {}
    for n, (hd, part) in enumerate(todo):
        s = pending.pop(0)
        if n + ahead < len(todo):
            pending.append(scores(*todo[n + ahead]))
        rise = update(hd, part, s)
        if rise is not None:
            key = rise.shape
            worst[key] = rise if key not in worst else jnp.maximum(worst[key], rise)
    if not worst:
        return None
    tops = [jnp.max(w, axis=1, keepdims=True) for w in worst.values()]
    return functools.reduce(jnp.maximum, tops)


def _flash_finalize(o_ref, acc_ref):
    nv, rows, tq = acc_ref.shape
    dv = rows - SUM_ROWS
    groups = o_ref.shape[0] if len(o_ref.shape) == 3 else 1
    nh = nv // groups
    for gb in range(groups):
        sl = slice(gb * nh, (gb + 1) * nh)
        o = acc_ref[sl, :dv, :] * (1.0 / acc_ref[sl, dv:dv + 1, :])
        o = o.reshape(nh * dv, tq).T.astype(BF16)
        if len(o_ref.shape) == 3:
            o_ref[gb] = o
        else:
            o_ref[...] = o


def _flash_scratch(nh, dv, tq, slots=None):
    lead = (nh,) if slots is None else (slots, nh)
    return [pltpu.VMEM(lead + (1, tq), F32), pltpu.VMEM(lead + (dv + SUM_ROWS, tq), F32)]


def _causal_pairs(nqt):
    pairs = [(i, j) for i in range(nqt) for j in [i] + list(range(i))]
    return (jnp.asarray([p[0] for p in pairs], jnp.int32), jnp.asarray([p[1] for p in pairs], jnp.int32))


def _lazy_flash_step(nheads, kqv, parts, m_ref, acc_ref, cur):
    src = (m_ref.at[cur], acc_ref.at[cur])
    dst = (m_ref.at[1 - cur], acc_ref.at[1 - cur])
    worst = _flash_heads(nheads, kqv, parts, src, dst, lazy=True)

    @pl.when(jnp.max(worst) > LAZY_SPAN)
    def _():
        _flash_heads(nheads, kqv, parts, src, dst, unroll=False)


def _causal_flash_step(i, j, nheads, kqv, tq, o_ref, m_ref, acc_ref, slot_ref):
    @pl.when(j == i)
    def _():
        _flash_heads(nheads, kqv, _tile_parts("diag", tq), None, (m_ref.at[0], acc_ref.at[0]))
        slot_ref[0] = 0

    @pl.when(j < i)
    def _():
        cur = slot_ref[0]
        _lazy_flash_step(nheads, kqv, _tile_parts("full", tq), m_ref, acc_ref, cur)
        slot_ref[0] = 1 - cur

    @pl.when((j == i - 1) | (i == 0))
    def _():
        _flash_finalize(o_ref, acc_ref.at[slot_ref[0]])


def _split_virtual(v, nheads):
    if isinstance(v, int):
        return v // nheads, v % nheads
    return lax.div(v, nheads), lax.rem(v, nheads)


def _mla_kernel(qi_ref, ki_ref, q_ref, k_ref, vt_ref, o_ref, m_ref, acc_ref, slot_ref, *, tq):
    i, j = qi_ref[pl.program_id(0)], ki_ref[pl.program_id(0)]

    def kqv(v):
        bb, hd = _split_virtual(v, MLA_HEADS)
        return k_ref[bb, hd], q_ref[bb, hd], vt_ref[bb, hd]

    _causal_flash_step(i, j, q_ref.shape[0] * MLA_HEADS, kqv, tq, o_ref, m_ref, acc_ref, slot_ref)


def _mla_attention(qm, km, vmt, B, S, tq):
    nqt = S // tq
    kern = functools.partial(_mla_kernel, tq=tq)
    qi, ki = _causal_pairs(nqt)
    return pl.pallas_call(
        kern,
        grid_spec=pltpu.PrefetchScalarGridSpec(
            num_scalar_prefetch=2,
            grid=(qi.shape[0],),
            in_specs=[pl.BlockSpec((B, MLA_HEADS, LANES, tq), lambda p, qi, ki: (0, 0, 0, qi[p])),
                      pl.BlockSpec((B, MLA_HEADS, tq, LANES), lambda p, qi, ki: (0, 0, ki[p], 0)),
                      pl.BlockSpec((B, MLA_HEADS, MLA_V + SUM_ROWS, tq), lambda p, qi, ki: (0, 0, 0, ki[p]))],
            out_specs=pl.BlockSpec((B, tq, MLA_HEADS * MLA_V), lambda p, qi, ki: (0, qi[p], 0)),
            scratch_shapes=_flash_scratch(B * MLA_HEADS, MLA_V, tq, slots=2) + [pltpu.SMEM((1,), jnp.int32)]),
        out_shape=jax.ShapeDtypeStruct((B, S, MLA_HEADS * MLA_V), BF16),
        compiler_params=_cparams(("arbitrary",)),
        name="mla_attn",
    )(qi, ki, qm, km, vmt).reshape(B * S, MLA_HEADS * MLA_V)


def _slc_kernel(qi_ref, ki_ref, q_ref, bias_ref, k_ref, vt_ref, o_ref, qa_ref, m_ref, acc_ref,
                slot_ref, *, tq):
    i, j = qi_ref[pl.program_id(0)], ki_ref[pl.program_id(0)]
    nb = q_ref.shape[0]

    @pl.when(j == i)
    def _():
        for bb in range(nb):
            for hd in range(NSA_HEADS):
                qa_ref[bb * NSA_HEADS + hd] = q_ref[bb, hd]

    rows = pl.ds(pl.multiple_of(j * SEL_SLOTS, SEL_SLOTS), SEL_SLOTS)
    for bb in range(nb):
        for g in range(NSA_GROUPS):
            tile_bias = bias_ref[bb, g, rows, :]
            for hd in range(g * NSA_HPG, (g + 1) * NSA_HPG):
                qa_ref[bb * NSA_HEADS + hd, NSA_DIM:NSA_DIM + SEL_SLOTS, :] = tile_bias

    def kqv(v):
        bb, hd = _split_virtual(v, NSA_HEADS)
        g = hd // NSA_HPG if isinstance(hd, int) else lax.div(hd, NSA_HPG)
        return k_ref[bb, g], qa_ref[v], vt_ref[bb, g]

    _causal_flash_step(i, j, nb * NSA_HEADS, kqv, tq, o_ref, m_ref, acc_ref, slot_ref)


def _slc_attention(nq, bias, ksa, vst, B, S, tq):
    nqt = S // tq
    nrow = bias.shape[2]
    assert tq // SEL_LEN == SEL_SLOTS // 2
    kern = functools.partial(_slc_kernel, tq=tq)
    qi, ki = _causal_pairs(nqt)
    qmap = lambda p, qi, ki: (0, 0, 0, qi[p])
    return pl.pallas_call(
        kern,
        grid_spec=pltpu.PrefetchScalarGridSpec(
            num_scalar_prefetch=2,
            grid=(qi.shape[0],),
            in_specs=[pl.BlockSpec((B, NSA_HEADS, LANES, tq), qmap),
                      pl.BlockSpec((B, NSA_GROUPS, nrow, tq), qmap),
                      pl.BlockSpec((B, NSA_GROUPS, tq, LANES), lambda p, qi, ki: (0, 0, ki[p], 0)),
                      pl.BlockSpec((B, NSA_GROUPS, NSA_DIM + SUM_ROWS, tq), lambda p, qi, ki: (0, 0, 0, ki[p]))],
            out_specs=pl.BlockSpec((B, tq, NSA_HEADS * NSA_DIM), lambda p, qi, ki: (0, qi[p], 0)),
            scratch_shapes=[pltpu.VMEM((B * NSA_HEADS, LANES, tq), BF16)]
                           + _flash_scratch(B * NSA_HEADS, NSA_DIM, tq, slots=2)
                           + [pltpu.SMEM((1,), jnp.int32)]),
        out_shape=jax.ShapeDtypeStruct((B, S, NSA_HEADS * NSA_DIM), BF16),
        compiler_params=_cparams(("arbitrary",)),
        name="slc_attn",
    )(qi, ki, nq, bias, ksa, vst).reshape(B * S, NSA_HEADS * NSA_DIM)


def _win_kernel(q_ref, k_ref, vt_ref, o_ref, m_ref, acc_ref, *, tq):
    i, j = pl.program_id(0), pl.program_id(1)

    def kqv(v):
        bb, hd = _split_virtual(v, NSA_HEADS)
        g = hd // NSA_HPG if isinstance(hd, int) else lax.div(hd, NSA_HPG)
        return k_ref[bb, g], q_ref[bb, hd][:NSA_DIM, :], vt_ref[bb, g]

    nv = q_ref.shape[0] * NSA_HEADS

    @pl.when(j == 0)
    def _():
        _flash_heads(nv, kqv, _tile_parts("diag", tq), None, (m_ref.at[0], acc_ref.at[0]))

    @pl.when((j == 1) & (i > 0))
    def _():
        _lazy_flash_step(nv, kqv, _tile_parts("prev", tq), m_ref, acc_ref, 0)

    @pl.when(j == 1)
    def _():
        _flash_finalize(o_ref, acc_ref.at[jnp.where(i > 0, 1, 0)])


def _win_attention(nq, kw, vwt, B, S, tq):
    nqt = S // tq
    kern = functools.partial(_win_kernel, tq=tq)
    kidx = lambda i, j: jnp.maximum(i - j, 0)
    return pl.pallas_call(
        kern,
        grid=(nqt, 2),
        in_specs=[pl.BlockSpec((B, NSA_HEADS, LANES, tq), lambda i, j: (0, 0, 0, i)),
                  pl.BlockSpec((B, NSA_GROUPS, tq, NSA_DIM), lambda i, j: (0, 0, kidx(i, j), 0)),
                  pl.BlockSpec((B, NSA_GROUPS, NSA_DIM + SUM_ROWS, tq), lambda i, j: (0, 0, 0, kidx(i, j)))],
        out_specs=pl.BlockSpec((B, tq, NSA_HEADS * NSA_DIM), lambda i, j: (0, i, 0)),
        out_shape=jax.ShapeDtypeStruct((B, S, NSA_HEADS * NSA_DIM), BF16),
        scratch_shapes=_flash_scratch(B * NSA_HEADS, NSA_DIM, tq, slots=2),
        compiler_params=_cparams(("parallel", "arbitrary")),
        name="win_attn",
    )(nq, kw, vwt).reshape(B * S, NSA_HEADS * NSA_DIM)


def _causal_conv(u, carry, w_ref):
    row = lax.broadcasted_iota(jnp.int32, u.shape, 0)
    c1 = carry[SUBLANES - 1:SUBLANES]
    c2 = carry[SUBLANES - 2:SUBLANES - 1]
    u1 = jnp.where(row == 0, c1, pltpu.roll(u, 1, axis=0))
    u2 = jnp.where(row == 0, c2, jnp.where(row == 1, c1, pltpu.roll(u, 2, axis=0)))
    return w_ref[0:1] * u2 + w_ref[1:2] * u1 + w_ref[2:3] * u


def _merge_kernel(x_ref, g_ref, wgc_ref, cw_ref, ym_ref, oc_ref, os_ref, ow_ref, gl_ref, ex_ref,
                  wbc_ref, wbm_ref, wbn_ref, wo_ref, gp_ref, out_ref, carry_ref, *, tiles_per_seq):
    D = x_ref.shape[1]
    DC = wbc_ref.shape[0]

    @pl.when(pl.program_id(0) % tiles_per_seq == 0)
    def _():
        carry_ref[...] = jnp.zeros(carry_ref.shape, F32)

    x = x_ref[...]
    h = _rms(x, g_ref[...]).astype(BF16)
    p = _dot(h, wgc_ref[...])
    u = p[:, 3 * D + DC:3 * D + 2 * DC] * p[:, 3 * D + 2 * DC:3 * D + 3 * DC]
    y_conv = p[:, 3 * D:3 * D + DC] * _causal_conv(u, carry_ref[...], cw_ref)
    carry_ref[...] = u[u.shape[0] - SUBLANES:]
    hi, lo = _split_bf16(gl_ref[...], 2)
    gate = jax.nn.sigmoid(_dot(hi, ex_ref[...]) + _dot(lo, ex_ref[...]))
    HD = oc_ref.shape[1]
    y_nsa = (gate[:, 0:HD] * oc_ref[...].astype(F32) + gate[:, HD:2 * HD] * os_ref[...].astype(F32)
             + gate[:, 2 * HD:3 * HD] * ow_ref[...].astype(F32))
    merged = (jax.nn.sigmoid(p[:, 0:D]) * _dot(y_conv.astype(BF16), wbc_ref[...])
              + jax.nn.sigmoid(p[:, D:2 * D]) * _dot(ym_ref[...], wbm_ref[...])
              + jax.nn.sigmoid(p[:, 2 * D:3 * D]) * _dot(y_nsa.astype(BF16), wbn_ref[...]))
    out_ref[...] = x + _rms(_dot(merged.astype(BF16), wo_ref[...]), gp_ref[...])


def _merge(x2, g_pre, w_gc, conv_w, y_mla, o_cmp, o_slc, o_win, gl, ex, wbc, wbm, wbn, wo, g_post,
           S, tm):
    T, D = x2.shape
    HD = y_mla.shape[1]
    row = lambda w: pl.BlockSpec((tm, w), lambda i: (i, 0))
    kern = functools.partial(_merge_kernel, tiles_per_seq=S // tm)
    return pl.pallas_call(
        kern,
        grid=(T // tm,),
        in_specs=[row(D), _const_spec((1, D)), _const_spec(w_gc.shape), _const_spec(conv_w.shape),
                  row(HD), row(HD), row(HD), row(HD), row(LANES), _const_spec(ex.shape),
                  _const_spec(wbc.shape), _const_spec(wbm.shape), _const_spec(wbn.shape),
                  _const_spec(wo.shape), _const_spec((1, D))],
        out_specs=row(D),
        out_shape=jax.ShapeDtypeStruct((T, D), F32),
        scratch_shapes=[pltpu.VMEM((SUBLANES, wbc.shape[0]), F32)],
        compiler_params=_cparams(("arbitrary",)),
        name="merge",
    )(x2, g_pre, w_gc, conv_w, y_mla, o_cmp, o_slc, o_win, gl, ex, wbc, wbm, wbn, wo, g_post)


def _ffn_kernel(x_ref, g_ref, wup_ref, cw_ref, cb_ref, wdn_ref, gp_ref, out_ref, carry_ref,
                *, tiles_per_seq, chunk):
    F = wdn_ref.shape[0]

    @pl.when(pl.program_id(0) % tiles_per_seq == 0)
    def _():
        carry_ref[...] = jnp.zeros(carry_ref.shape, F32)

    x = x_ref[...]
    h = _rms(x, g_ref[...]).astype(BF16)
    acc = None
    for c0 in range(0, F, chunk):
        a = _dot(h, wup_ref[:, c0:c0 + chunk])
        b = _dot(h, wup_ref[:, F + c0:F + c0 + chunk])
        ac = _causal_conv(a, carry_ref[:, c0:c0 + chunk], cw_ref.at[:, c0:c0 + chunk])
        ac = ac + cb_ref[:, c0:c0 + chunk]
        carry_ref[:, c0:c0 + chunk] = a[a.shape[0] - SUBLANES:]
        d = _dot((jax.nn.gelu(ac) * b).astype(BF16), wdn_ref[c0:c0 + chunk, :])
        acc = d if acc is None else acc + d
    out_ref[...] = x + _rms(acc, gp_ref[...])


def _ffn(x2, g_pre, w_up, conv_w, conv_b, w_dn, g_post, S, tm, chunk):
    T, D = x2.shape
    F = w_dn.shape[0]
    row = pl.BlockSpec((tm, D), lambda i: (i, 0))
    kern = functools.partial(_ffn_kernel, tiles_per_seq=S // tm, chunk=chunk)
    return pl.pallas_call(
        kern,
        grid=(T // tm,),
        in_specs=[row, _const_spec((1, D)), _const_spec(w_up.shape), _const_spec(conv_w.shape),
                  _const_spec((1, F)), _const_spec(w_dn.shape), _const_spec((1, D))],
        out_specs=row,
        out_shape=jax.ShapeDtypeStruct((T, D), F32),
        scratch_shapes=[pltpu.VMEM((SUBLANES, F), F32)],
        compiler_params=_cparams(("arbitrary",)),
        name="ffn",
    )(x2, g_pre, w_up, conv_w, conv_b, w_dn, g_post)


def _overlap_matrix_t(S):
    nch = S // CMP_STRIDE
    n_cmp = (S - CMP_LEN) // CMP_STRIDE + 1
    cs = np.arange(nch)[None, :] * CMP_STRIDE
    ss = np.arange(S // SEL_LEN)[:, None] * SEL_LEN
    ov = (cs <= ss + SEL_LEN - 1) & (cs + CMP_LEN - 1 >= ss) & (np.arange(nch)[None, :] < n_cmp)
    return ov.astype(np.float32)


def _gate_expand():
    ex = np.zeros((LANES, 3 * NSA_HEADS * NSA_DIM), np.float32)
    for hd in range(NSA_HEADS):
        for c in range(3):
            ex[hd * 3 + c, c * NSA_HEADS * NSA_DIM + hd * NSA_DIM:
               c * NSA_HEADS * NSA_DIM + (hd + 1) * NSA_DIM] = 1.0
    return ex


def _pad_rows(w, rows):
    return jnp.concatenate([w, jnp.zeros((rows - w.shape[0],) + w.shape[1:], w.dtype)], axis=0)


def kernel(x, positions, norm_mix_pre, norm_mix_post, w_in, conv_w, mla_q_norm, mla_w_uq, mla_kv_norm, mla_w_ukv, nsa_cmp_pos_k, nsa_cmp_pos_v, nsa_cmp_w_k, nsa_cmp_w_v, w_branch_conv, w_branch_mla, w_branch_nsa, w_out, norm_ffn_pre, norm_ffn_post, ffn_w_up, ffn_conv_w, ffn_conv_b, ffn_w_down):
    B, S, D = x.shape
    T = B * S
    depth = w_in.shape[0]
    DC = conv_w.shape[2]
    F = ffn_w_down.shape[1]
    QL = mla_q_norm.shape[1]
    KL = mla_kv_norm.shape[1]
    tq = 512
    assert S % tq == 0 and WINDOW == tq and N_SEL <= S // SEL_LEN <= LANES
    tm_proj, tm_merge, tm_ffn, ffn_chunk = 512, 512, 512, 1408
    assert F % ffn_chunk == 0 and ffn_chunk % LANES == 0

    x2 = x.reshape(T, D)
    pos2 = positions.reshape(1, T)
    rc = jnp.asarray(_rope_consts())
    ovt = jnp.asarray(_overlap_matrix_t(S), BF16)
    ex = jnp.asarray(_gate_expand(), BF16)
    half = CMP_LEN // 2 * NSA_DIM
    o_att = 3 * D + 3 * DC

    for l in range(depth):
        w = w_in[l]
        o_kr = o_att + QL + KL
        z = lambda n: jnp.zeros((D, n), F32)
        w_att = jnp.concatenate(
            [w[:, o_att:o_kr], z(MLA_NOPE), w[:, o_kr:o_kr + MLA_ROPE], z(LANES - MLA_NOPE - MLA_ROPE),
             w[:, o_kr + MLA_ROPE:o_kr + MLA_ROPE + NSA_HEADS * NSA_DIM + 6 * NSA_GROUPS * NSA_DIM],
             w[:, w.shape[1] - 3 * NSA_HEADS:], z(LANES - 3 * NSA_HEADS)], axis=1).astype(BF16)
        dq = MLA_NOPE + MLA_ROPE
        w_uq = jnp.pad(mla_w_uq[l].reshape(QL, MLA_HEADS, dq),
                       ((0, 0), (0, 0), (0, LANES - dq))).reshape(QL, MLA_HEADS * LANES).astype(BF16)
        w_v_t = mla_w_ukv[l].reshape(KL, MLA_HEADS, MLA_NOPE + MLA_V)[:, :, MLA_NOPE:].reshape(
            KL, MLA_HEADS * MLA_V).T.astype(BF16)
        (qm, km, vmt, nq, kc_in, vc_in, ksa, vst, kw, vwt, gl) = _project(
            x2, pos2, norm_mix_pre[l][None], w_att, mla_q_norm[l][None], w_uq.T,
            mla_kv_norm[l][None], mla_w_ukv[l].astype(BF16), w_v_t, rc, S, tm_proj)

        wv = nsa_cmp_w_v[l].reshape(2, half, NSA_DIM)
        zv = jnp.zeros_like(wv)
        wv_pad = jnp.stack([jnp.concatenate([wv, zv], axis=2), jnp.concatenate([zv, wv], axis=2)])
        kc, vct = _compress(kc_in, vc_in,
                            nsa_cmp_pos_k[l].reshape(2, half), nsa_cmp_pos_v[l].reshape(2, half),
                            nsa_cmp_w_k[l].reshape(2, half, NSA_DIM).astype(BF16),
                            wv_pad.astype(BF16), B)

        y_mla = _mla_attention(qm, km, vmt, B, S, tq)
        o_cmp, bias = _cmp_select(nq, kc, vct, ovt, B, S, tq)
        o_slc = _slc_attention(nq, bias, ksa, vst, B, S, tq)
        o_win = _win_attention(nq, kw, vwt, B, S, tq)

        x2 = _merge(x2, norm_mix_pre[l][None], w[:, :o_att].astype(BF16),
                    _pad_rows(conv_w[l], SUBLANES), y_mla, o_cmp, o_slc, o_win, gl, ex,
                    w_branch_conv[l].astype(BF16), w_branch_mla[l].astype(BF16),
                    w_branch_nsa[l].astype(BF16), w_out[l].astype(BF16), norm_mix_post[l][None],
                    S, tm_merge)
        x2 = _ffn(x2, norm_ffn_pre[l][None], ffn_w_up[l].astype(BF16),
                  _pad_rows(ffn_conv_w[l], SUBLANES), ffn_conv_b[l][None],
                  ffn_w_down[l].astype(BF16), norm_ffn_post[l][None], S, tm_ffn, ffn_chunk)
    return x2.reshape(B, S, D)
```

```python
import functools
import math

import numpy as np
import jax
import jax.numpy as jnp
from jax import lax
from jax.experimental import pallas as pl
from jax.experimental.pallas import tpu as pltpu

F32 = jnp.float32
BF16 = jnp.bfloat16

ROPE_THETA = 500000.0
RMS_EPS = 1e-6
CONV_WIDTH = 3
MLA_HEADS = 8
MLA_NOPE = 64
MLA_ROPE = 32
MLA_V = 64
NSA_HEADS = 8
NSA_GROUPS = 2
NSA_HPG = NSA_HEADS // NSA_GROUPS
NSA_DIM = 64
NSA_ROT = NSA_DIM // 4
CMP_LEN = 32
CMP_STRIDE = 16
SEL_LEN = 64
N_SEL = 16
WINDOW = 512

LANES = 128
SUBLANES = 8
VMEM_LIMIT = 56 * 1024 * 1024
MASKED = -1e30
M_FLOOR = -1e20
SEL_BIAS = -2.0 ** 100
LAZY_SPAN = 64.0
LOG2E = math.log2(math.e)
SEL_SLOTS = 16
ROPE_ROWS = 32
SUM_ROWS = 16


def _cparams(sem):
    return pltpu.CompilerParams(dimension_semantics=sem, vmem_limit_bytes=VMEM_LIMIT)


def _const_spec(shape):
    nd = len(shape)
    return pl.BlockSpec(shape, lambda *_: (0,) * nd, pipeline_mode=pl.Buffered(1))


def _rms(x, g):
    return x * lax.rsqrt(jnp.mean(x * x, axis=-1, keepdims=True) + RMS_EPS) * g


def _dot(a, b):
    return jnp.dot(a, b, preferred_element_type=F32)


def _split_bf16(a, terms):
    pieces, rem = [], a
    for _ in range(terms):
        piece = rem.astype(BF16)
        rem = rem - piece.astype(F32)
        pieces.append(piece)
    return pieces


def _rope(x, cos, sin_up, sin_dn, half):
    return (x * cos + pltpu.roll(x, half, axis=1) * sin_up
            + pltpu.roll(x, LANES - half, axis=1) * sin_dn)


def _lanes(table, lane, base, moves):
    out = base
    for dst, width, src in moves:
        rolled = pltpu.roll(table, (dst - src) % LANES, axis=1)
        out = jnp.where((lane >= dst) & (lane < dst + width), rolled, out)
    return out


def _proj_kernel(x_ref, pos_ref, g_ref, watt_ref, qg_ref, wuqt_ref, kvg_ref, wukv_ref, wvt_ref, rc_ref,
                 qm_ref, km_ref, vmt_ref, nq_ref, kc_ref, vc_ref, ksa_ref, vst_ref, kw_ref, vwt_ref,
                 gl_ref, kstage_ref, vstage_ref, *, mla_scale, nsa_scale, tiles_per_seq):
    tm = x_ref.shape[0]
    hm, hn = MLA_ROPE // 2, NSA_ROT // 2
    x = x_ref[...]
    h = _rms(x, g_ref[...]).astype(BF16)
    p = _dot(h, watt_ref[...])
    ql, kl = qg_ref.shape[1], kvg_ref.shape[1]
    c_kr, c_nq, c_kv, c_gl = _att_columns(ql, kl)
    lane = lax.broadcasted_iota(jnp.int32, (tm, LANES), 1)
    low = lane < NSA_DIM
    ang_t = rc_ref[:, 0:1] * pos_ref[...].astype(F32)
    cos_t, sin_t = jnp.cos(ang_t), jnp.sin(ang_t)
    cs = jnp.concatenate([cos_t, sin_t, jnp.zeros((LANES - 2 * ROPE_ROWS, tm), F32)], axis=0).T
    cos_m = _lanes(cs, lane, 1.0, [(MLA_NOPE, hm, 0), (MLA_NOPE + hm, hm, 0)])
    sup_m = _lanes(cs, lane, 0.0, [(MLA_NOPE + hm, hm, ROPE_ROWS)])
    sdn_m = -_lanes(cs, lane, 0.0, [(MLA_NOPE, hm, ROPE_ROWS)])
    cos_n = _lanes(cs, lane, 1.0, [(0, hn, hm), (hn, hn, hm), (NSA_DIM, hn, hm), (NSA_DIM + hn, hn, hm)])
    sup_n = _lanes(cs, lane, 0.0, [(hn, hn, ROPE_ROWS + hm), (NSA_DIM + hn, hn, ROPE_ROWS + hm)])
    sdn_n = -_lanes(cs, lane, 0.0, [(0, hn, ROPE_ROWS + hm), (NSA_DIM, hn, ROPE_ROWS + hm)])
    ones_rows = jnp.where(lax.broadcasted_iota(jnp.int32, (SUM_ROWS, tm), 0) == 0, 1.0, 0.0)

    def with_sum_rows(vt):
        return jnp.concatenate([vt, ones_rows], axis=0).astype(BF16)

    def halves(slab):
        return (jnp.where(low, slab, 0.0), jnp.where(low, pltpu.roll(slab, NSA_DIM, axis=1), 0.0))

    def rope_rows(x1, x2, c, s_):
        return x1 * c - x2 * s_, x2 * c + x1 * s_

    cqn_t = _rms(p[:, 0:ql], qg_ref[...]).T.astype(BF16)
    qt = _dot(wuqt_ref[...], cqn_t)
    for hd in range(MLA_HEADS):
        r0 = hd * LANES
        y1, y2 = rope_rows(qt[r0 + MLA_NOPE:r0 + MLA_NOPE + hm], qt[r0 + MLA_NOPE + hm:r0 + MLA_NOPE + 2 * hm],
                           cos_t[0:hm], sin_t[0:hm])
        slab_t = jnp.concatenate([qt[r0:r0 + MLA_NOPE], y1, y2, qt[r0 + MLA_NOPE + 2 * hm:r0 + LANES]], axis=0)
        qm_ref[0, hd] = (slab_t * mla_scale).astype(BF16)
    ckvn = _rms(p[:, ql:ql + kl], kvg_ref[...])
    kv = _dot(ckvn.astype(BF16), wukv_ref[...])
    kr = _rope(p[:, c_kr:c_kr + LANES], cos_m, sup_m, sdn_m, hm)
    vt = _dot(wvt_ref[...], ckvn.T.astype(BF16))
    for hd in range(MLA_HEADS):
        km_ref[0, hd] = jnp.where(lane < MLA_NOPE, kv[:, hd * LANES:(hd + 1) * LANES], kr).astype(BF16)
        vmt_ref[0, hd] = with_sum_rows(vt[hd * MLA_V:(hd + 1) * MLA_V])
    zpad = jnp.zeros((LANES - NSA_DIM, tm), BF16)
    for pr in range(NSA_HEADS // 2):
        slab_t = p[:, c_nq + pr * LANES:c_nq + (pr + 1) * LANES].T
        for half in range(2):
            r0 = half * NSA_DIM
            y1, y2 = rope_rows(slab_t[r0:r0 + hn], slab_t[r0 + hn:r0 + 2 * hn],
                               cos_t[hm:hm + hn], sin_t[hm:hm + hn])
            head_t = jnp.concatenate([y1, y2, slab_t[r0 + 2 * hn:r0 + NSA_DIM]], axis=0) * nsa_scale
            nq_ref[0, 2 * pr + half, 0:NSA_DIM, :] = head_t.astype(BF16)
            nq_ref[0, 2 * pr + half, NSA_DIM:LANES, :] = zpad
    nsa = lambda idx: p[:, c_kv + idx * LANES:c_kv + (idx + 1) * LANES]
    def store_chunks(slab, stage_ref, out_ref):
        stage_ref[...] = slab
        nrow = tm // CMP_STRIDE
        first = lax.broadcasted_iota(jnp.int32, (nrow, LANES), 1) < NSA_DIM
        for r in range(0, CMP_STRIDE, 2):
            a = stage_ref[pl.ds(r, nrow, stride=CMP_STRIDE), :]
            b = stage_ref[pl.ds(r + 1, nrow, stride=CMP_STRIDE), :]
            cols = slice(r * NSA_DIM, (r + 2) * NSA_DIM)
            out_ref[0, :, cols] = jnp.where(first, a, pltpu.roll(b, NSA_DIM, axis=1)).astype(BF16)
            out_ref[1, :, cols] = jnp.where(first, pltpu.roll(a, NSA_DIM, axis=1), b).astype(BF16)

    store_chunks(_rope(nsa(0), cos_n, sup_n, sdn_n, hn), kstage_ref, kc_ref)
    store_chunks(nsa(1), vstage_ref, vc_ref)
    t_seq = (pl.program_id(0) % tiles_per_seq) * tm + lax.broadcasted_iota(jnp.int32, (tm, LANES), 0)
    slot = NSA_DIM + (lax.shift_right_logical(t_seq, int(math.log2(SEL_LEN))) & (SEL_SLOTS // 2 - 1))
    onehot = jnp.where(lane == slot, 1.0, 0.0)
    for g, kg in enumerate(halves(_rope(nsa(2), cos_n, sup_n, sdn_n, hn))):
        ksa_ref[0, g] = (kg + onehot).astype(BF16)
    vst = nsa(3).T
    vst_ref[0, 0] = with_sum_rows(vst[:NSA_DIM])
    vst_ref[0, 1] = with_sum_rows(vst[NSA_DIM:])
    kws = _rope(nsa(4), cos_n, sup_n, sdn_n, hn)
    kw_ref[0, 0] = kws[:, :NSA_DIM].astype(BF16)
    kw_ref[0, 1] = pltpu.roll(kws, NSA_DIM, axis=1)[:, :NSA_DIM].astype(BF16)
    vwt = nsa(5).T
    vwt_ref[0, 0] = with_sum_rows(vwt[:NSA_DIM])
    vwt_ref[0, 1] = with_sum_rows(vwt[NSA_DIM:])
    gl_ref[...] = p[:, c_gl:c_gl + LANES]


def _att_columns(ql, kl):
    c_kr = ql + kl
    c_nq = c_kr + LANES
    c_kv = c_nq + NSA_HEADS * NSA_DIM
    c_gl = c_kv + 6 * NSA_GROUPS * NSA_DIM
    return c_kr, c_nq, c_kv, c_gl


def _rope_consts():
    rc = np.zeros((ROPE_ROWS, LANES), np.float32)
    hm = MLA_ROPE // 2
    rc[:hm] = (ROPE_THETA ** (-np.arange(hm, dtype=np.float32) / hm))[:, None]
    hn = NSA_ROT // 2
    rc[hm:hm + hn] = (ROPE_THETA ** (-np.arange(hn, dtype=np.float32) / hn))[:, None]
    return rc


def _project(x2, pos2, g_pre, w_att, q_g, w_uq_t, kv_g, w_ukv, w_v_t, rc, S, tm):
    T, D = x2.shape
    n_att = w_att.shape[1]
    tps = S // tm
    B = T // S
    brows = lambda n, d: (jax.ShapeDtypeStruct((B, n, S, d), BF16),
                          pl.BlockSpec((1, n, tm, d), lambda i: (i // tps, 0, i % tps, 0)))
    bcols = lambda n, d: (jax.ShapeDtypeStruct((B, n, d, S), BF16),
                          pl.BlockSpec((1, n, d, tm), lambda i: (i // tps, 0, 0, i % tps)))
    chunked = (jax.ShapeDtypeStruct((NSA_GROUPS, T // CMP_STRIDE, CMP_STRIDE * NSA_DIM), BF16),
               pl.BlockSpec((NSA_GROUPS, tm // CMP_STRIDE, CMP_STRIDE * NSA_DIM), lambda i: (0, i, 0)))
    outs = [bcols(MLA_HEADS, LANES), brows(MLA_HEADS, LANES), bcols(MLA_HEADS, MLA_V + SUM_ROWS),
            bcols(NSA_HEADS, LANES), chunked, chunked,
            brows(NSA_GROUPS, LANES), bcols(NSA_GROUPS, NSA_DIM + SUM_ROWS), brows(NSA_GROUPS, NSA_DIM),
            bcols(NSA_GROUPS, NSA_DIM + SUM_ROWS),
            (jax.ShapeDtypeStruct((T, LANES), F32), pl.BlockSpec((tm, LANES), lambda i: (i, 0)))]
    kern = functools.partial(_proj_kernel,
                             mla_scale=float((MLA_NOPE + MLA_ROPE) ** -0.5 * LOG2E),
                             nsa_scale=float(NSA_DIM ** -0.5 * LOG2E),
                             tiles_per_seq=S // tm)
    return pl.pallas_call(
        kern,
        grid=(T // tm,),
        in_specs=[
            pl.BlockSpec((tm, D), lambda i: (i, 0)),
            pl.BlockSpec((1, tm), lambda i: (0, i)),
            _const_spec((1, D)),
            _const_spec((D, n_att)),
            _const_spec((1, q_g.shape[1])),
            _const_spec(w_uq_t.shape),
            _const_spec((1, kv_g.shape[1])),
            _const_spec(w_ukv.shape),
            _const_spec(w_v_t.shape),
            _const_spec((ROPE_ROWS, LANES)),
        ],
        out_specs=[o[1] for o in outs],
        out_shape=[o[0] for o in outs],
        scratch_shapes=[pltpu.VMEM((tm, LANES), F32), pltpu.VMEM((tm, LANES), F32)],
        compiler_params=_cparams(("parallel",)),
        name="proj",
    )(x2, pos2, g_pre, w_att, q_g, w_uq_t, kv_g, w_ukv, w_v_t, rc)


def _compress_kernel(xk_ref, xv_ref, pk_ref, pv_ref, wk_ref, wv_ref, ok_ref, ovt_ref):
    n = xk_ref.shape[1]
    vboth = None
    for g in range(NSA_GROUPS):
        xk = xk_ref[g].astype(F32)
        lo = _dot((xk + pk_ref[0:1]).astype(BF16), wk_ref[0])
        hi = _dot((xk + pk_ref[1:2]).astype(BF16), wk_ref[1])
        ok_ref[g] = (lo + pltpu.roll(hi, n - 1, axis=0)).astype(BF16)
        xv = xv_ref[g].astype(F32)
        lo = _dot((xv + pv_ref[0:1]).astype(BF16), wv_ref[g, 0])
        hi = _dot((xv + pv_ref[1:2]).astype(BF16), wv_ref[g, 1])
        vg = lo + pltpu.roll(hi, n - 1, axis=0)
        vboth = vg if vboth is None else vboth + vg
    ovt_ref[0] = vboth.T.astype(BF16)


def _compress(xk, xv, pk, pv, wk, wv, B):
    G, TC, CW = xk.shape
    nch = TC // B
    xspec = pl.BlockSpec((G, nch, CW), lambda b: (0, b, 0))
    return pl.pallas_call(
        _compress_kernel,
        grid=(B,),
        in_specs=[xspec, xspec, _const_spec(pk.shape), _const_spec(pv.shape),
                  _const_spec(wk.shape), _const_spec(wv.shape)],
        out_specs=[pl.BlockSpec((G, nch, NSA_DIM), lambda b: (0, b, 0)),
                   pl.BlockSpec((1, G * NSA_DIM, nch), lambda b: (b, 0, 0))],
        out_shape=[jax.ShapeDtypeStruct((G, TC, NSA_DIM), BF16),
                   jax.ShapeDtypeStruct((B, G * NSA_DIM, nch), BF16)],
        compiler_params=_cparams(("parallel",)),
        name="compress",
    )(xk, xv, pk, pv, wk, wv)


def _cmp_select_kernel(q_ref, kc_ref, vct_ref, ovt_ref, o_ref, bias_ref, *, tq, n_pick, nqt, n_var):
    i = pl.program_id(1)
    q0 = i * tq
    nch_all = kc_ref.shape[1]
    nblk_all = ovt_ref.shape[0]
    tiles_per_var = nqt // n_var

    def variant(nch, nblk):
        for c0 in range(0, tq, tq // 2):
            half(nch, nblk, c0, tq // 2)

    def half(nch, nblk, c0, w):
        cols = slice(c0, c0 + w)
        t_c = q0 + c0 + lax.broadcasted_iota(jnp.int32, (nch, w), 1)
        n_c = lax.broadcasted_iota(jnp.int32, (nch, w), 0)
        cmask = (n_c * CMP_STRIDE + (CMP_LEN - 1)) <= t_c
        t_b = q0 + c0 + lax.broadcasted_iota(jnp.int32, (nblk, w), 1)
        blk = lax.broadcasted_iota(jnp.int32, (nblk, w), 0)
        blk_f = blk.astype(F32)
        cur = lax.shift_right_logical(t_b, int(math.log2(SEL_LEN)))
        forced = (blk == 0) | (blk == cur) | (blk == cur - 1)
        causal = blk * SEL_LEN <= t_b
        outs = []
        for g in range(NSA_GROUPS):
            psum = None
            scores = [_dot(kc_ref[g, :nch, :], q_ref[0, g * NSA_HPG + j][:NSA_DIM, cols])
                      for j in range(NSA_HPG)]
            for j in range(NSA_HPG):
                s = jnp.where(cmask, scores[j], MASKED)
                m = jnp.maximum(jnp.max(s, axis=0, keepdims=True), M_FLOOR)
                p = jnp.exp2(s - m)
                l = jnp.sum(p, axis=0, keepdims=True)
                pn = p * jnp.where(l > 0.0, 1.0 / l, 0.0)
                outs.append(_dot(vct_ref[0, g * NSA_DIM:(g + 1) * NSA_DIM, :nch], pn.astype(BF16)))
                psum = pn if psum is None else psum + pn
            imp = None
            for piece in _split_bf16(psum, 3):
                d = _dot(ovt_ref[:nblk, :nch], piece)
                imp = d if imp is None else imp + d
            v0 = jnp.where(forced | jnp.logical_not(causal), -1.0, imp)
            v = v0
            for _ in range(n_pick):
                mx = jnp.max(v, axis=0, keepdims=True)
                first = jnp.min(jnp.where(v == mx, blk_f, float(nblk)), axis=0, keepdims=True)
                v = jnp.where(blk_f == first, -2.0, v)
            sel = forced | ((v == -2.0) & (v0 >= 0.0))
            bias = jnp.where(sel, 0.0, SEL_BIAS)
            per_tile = SEL_SLOTS // 2
            pad = jnp.zeros((SEL_SLOTS - per_tile, w), F32)
            pieces = []
            for kt in range(nblk_all // per_tile):
                if kt * per_tile < nblk:
                    pieces += [bias[kt * per_tile:(kt + 1) * per_tile], pad]
                else:
                    pieces += [jnp.full((per_tile, w), SEL_BIAS, F32), pad]
            bias_ref[0, g, :, cols] = jnp.concatenate(pieces, axis=0).astype(BF16)
        o_ref[cols, :] = jnp.concatenate(outs, axis=0).T.astype(BF16)

    for k in range(n_var):
        @pl.when((i >= k * tiles_per_var) & (i < (k + 1) * tiles_per_var))
        def _():
            variant(nch_all * (k + 1) // n_var, nblk_all * (k + 1) // n_var)


def _cmp_select(nq, kc, vct, ovt, B, S, tq):
    T = B * S
    nch = kc.shape[1] // B
    nblk = ovt.shape[0]
    nqt = S // tq
    n_var = 4 if nqt % 4 == 0 and (nch // 4) % LANES == 0 and (nblk // 4) % SUBLANES == 0 else 1
    kern = functools.partial(_cmp_select_kernel, tq=tq, n_pick=N_SEL - 3, nqt=nqt, n_var=n_var)
    return pl.pallas_call(
        kern,
        grid=(B, nqt),
        in_specs=[
            pl.BlockSpec((1, NSA_HEADS, LANES, tq), lambda b, i: (b, 0, 0, i)),
            pl.BlockSpec((NSA_GROUPS, nch, NSA_DIM), lambda b, i: (0, b, 0)),
            pl.BlockSpec((1, NSA_GROUPS * NSA_DIM, nch), lambda b, i: (b, 0, 0)),
            _const_spec(ovt.shape),
        ],
        out_specs=[pl.BlockSpec((tq, NSA_HEADS * NSA_DIM), lambda b, i: (b * nqt + i, 0)),
                   pl.BlockSpec((1, NSA_GROUPS, 2 * nblk, tq), lambda b, i: (b, 0, 0, i))],
        out_shape=[jax.ShapeDtypeStruct((T, NSA_HEADS * NSA_DIM), BF16),
                   jax.ShapeDtypeStruct((B, NSA_GROUPS, 2 * nblk, S), BF16)],
        compiler_params=_cparams(("parallel", "parallel")),
        name="cmp_select",
    )(nq, kc, vct, ovt)


def _tile_parts(kind, tq):
    hq = tq // 2
    if kind == "full":
        return [(slice(0, tq), slice(0, hq), None), (slice(0, tq), slice(hq, tq), None)]

    def mask(rows, cols, below):
        key = rows.start + lax.broadcasted_iota(jnp.int32, (rows.stop - rows.start, hq), 0)
        qry = cols.start + lax.broadcasted_iota(jnp.int32, (rows.stop - rows.start, hq), 1)
        return (key <= qry) if below else (key > qry)

    lo, hi, all_ = slice(0, hq), slice(hq, tq), slice(0, tq)
    if kind == "diag":
        return [(lo, lo, mask(lo, lo, True)), (all_, hi, mask(all_, hi, True))]
    return [(all_, lo, mask(all_, lo, False)), (hi, hi, mask(hi, hi, False))]


def _flash_heads(nheads, kqv, parts, src, dst, lazy=False, unroll=True):
    def scores(hd, part):
        rows, cols, _ = part
        k, qt, _ = kqv(hd)
        return _dot(k[rows], qt[:, cols])

    def update(hd, part, s):
        rows, cols, mask = part
        if mask is not None:
            s = jnp.where(mask, s, MASKED)
        tmax = jnp.max(s, axis=0, keepdims=True)
        vt = kqv(hd)[2][:, rows]
        rise = None
        if src is None:
            m_new = jnp.maximum(tmax, M_FLOOR)
            acc = _dot(vt, jnp.exp2(s - m_new).astype(BF16))
        else:
            m_old = src[0][hd, :, cols]
            m_new = jnp.maximum(m_old, tmax)
            alpha = jnp.exp2(m_old - m_new)
            if lazy:
                acc = (src[1][hd, :, cols] + _dot(vt, jnp.exp2(s - m_old).astype(BF16))) * alpha
                rise = tmax - m_old
            else:
                acc = alpha * src[1][hd, :, cols] + _dot(vt, jnp.exp2(s - m_new).astype(BF16))
        dst[0][hd, :, cols] = m_new
        dst[1][hd, :, cols] = acc
        return rise

    if not unroll:
        def body(hd, carry):
            for part in parts:
                update(hd, part, scores(hd, part))
            return carry
        lax.fori_loop(0, nheads, body, 0)
        return None
    todo = [(hd, part) for hd in range(nheads) for part in parts]
    ahead = 2 if lazy else 4
    pending = [scores(*t) for t in todo[:ahead]]
    worst = {}
    for n, (hd, part) in enumerate(todo):
        s = pending.pop(0)
        if n + ahead < len(todo):
            pending.append(scores(*todo[n + ahead]))
        rise = update(hd, part, s)
        if rise is not None:
            key = rise.shape
            worst[key] = rise if key not in worst else jnp.maximum(worst[key], rise)
    if not worst:
        return None
    tops = [jnp.max(w, axis=1, keepdims=True) for w in worst.values()]
    return functools.reduce(jnp.maximum, tops)


def _flash_finalize(o_ref, acc_ref):
    nv, rows, tq = acc_ref.shape
    dv = rows - SUM_ROWS
    groups = o_ref.shape[0] if len(o_ref.shape) == 3 else 1
    nh = nv // groups
    for gb in range(groups):
        sl = slice(gb * nh, (gb + 1) * nh)
        o = acc_ref[sl, :dv, :] * (1.0 / acc_ref[sl, dv:dv + 1, :])
        o = o.reshape(nh * dv, tq).T.astype(BF16)
        if len(o_ref.shape) == 3:
            o_ref[gb] = o
        else:
            o_ref[...] = o


def _flash_scratch(nh, dv, tq, slots=None):
    lead = (nh,) if slots is None else (slots, nh)
    return [pltpu.VMEM(lead + (1, tq), F32), pltpu.VMEM(lead + (dv + SUM_ROWS, tq), F32)]


def _causal_pairs(nqt):
    pairs = [(i, j) for i in range(nqt) for j in [i] + list(range(i))]
    return (jnp.asarray([p[0] for p in pairs], jnp.int32), jnp.asarray([p[1] for p in pairs], jnp.int32))


def _lazy_flash_step(nheads, kqv, parts, m_ref, acc_ref, cur):
    src = (m_ref.at[cur], acc_ref.at[cur])
    dst = (m_ref.at[1 - cur], acc_ref.at[1 - cur])
    worst = _flash_heads(nheads, kqv, parts, src, dst, lazy=True)

    @pl.when(jnp.max(worst) > LAZY_SPAN)
    def _():
        _flash_heads(nheads, kqv, parts, src, dst, unroll=False)


def _causal_flash_step(i, j, nheads, kqv, tq, o_ref, m_ref, acc_ref, slot_ref):
    @pl.when(j == i)
    def _():
        _flash_heads(nheads, kqv, _tile_parts("diag", tq), None, (m_ref.at[0], acc_ref.at[0]))
        slot_ref[0] = 0

    @pl.when(j < i)
    def _():
        cur = slot_ref[0]
        _lazy_flash_step(nheads, kqv, _tile_parts("full", tq), m_ref, acc_ref, cur)
        slot_ref[0] = 1 - cur

    @pl.when((j == i - 1) | (i == 0))
    def _():
        _flash_finalize(o_ref, acc_ref.at[slot_ref[0]])


def _split_virtual(v, nheads):
    if isinstance(v, int):
        return v // nheads, v % nheads
    return lax.div(v, nheads), lax.rem(v, nheads)


def _mla_kernel(qi_ref, ki_ref, q_ref, k_ref, vt_ref, o_ref, m_ref, acc_ref, slot_ref, *, tq):
    i, j = qi_ref[pl.program_id(0)], ki_ref[pl.program_id(0)]

    def kqv(v):
        bb, hd = _split_virtual(v, MLA_HEADS)
        return k_ref[bb, hd], q_ref[bb, hd], vt_ref[bb, hd]

    _causal_flash_step(i, j, q_ref.shape[0] * MLA_HEADS, kqv, tq, o_ref, m_ref, acc_ref, slot_ref)


def _mla_attention(qm, km, vmt, B, S, tq):
    nqt = S // tq
    kern = functools.partial(_mla_kernel, tq=tq)
    qi, ki = _causal_pairs(nqt)
    return pl.pallas_call(
        kern,
        grid_spec=pltpu.PrefetchScalarGridSpec(
            num_scalar_prefetch=2,
            grid=(qi.shape[0],),
            in_specs=[pl.BlockSpec((B, MLA_HEADS, LANES, tq), lambda p, qi, ki: (0, 0, 0, qi[p])),
                      pl.BlockSpec((B, MLA_HEADS, tq, LANES), lambda p, qi, ki: (0, 0, ki[p], 0)),
                      pl.BlockSpec((B, MLA_HEADS, MLA_V + SUM_ROWS, tq), lambda p, qi, ki: (0, 0, 0, ki[p]))],
            out_specs=pl.BlockSpec((B, tq, MLA_HEADS * MLA_V), lambda p, qi, ki: (0, qi[p], 0)),
            scratch_shapes=_flash_scratch(B * MLA_HEADS, MLA_V, tq, slots=2) + [pltpu.SMEM((1,), jnp.int32)]),
        out_shape=jax.ShapeDtypeStruct((B, S, MLA_HEADS * MLA_V), BF16),
        compiler_params=_cparams(("arbitrary",)),
        name="mla_attn",
    )(qi, ki, qm, km, vmt).reshape(B * S, MLA_HEADS * MLA_V)


def _slc_kernel(qi_ref, ki_ref, q_ref, bias_ref, k_ref, vt_ref, o_ref, qa_ref, m_ref, acc_ref,
                slot_ref, *, tq):
    i, j = qi_ref[pl.program_id(0)], ki_ref[pl.program_id(0)]
    nb = q_ref.shape[0]

    @pl.when(j == i)
    def _():
        for bb in range(nb):
            for hd in range(NSA_HEADS):
                qa_ref[bb * NSA_HEADS + hd] = q_ref[bb, hd]

    rows = pl.ds(pl.multiple_of(j * SEL_SLOTS, SEL_SLOTS), SEL_SLOTS)
    for bb in range(nb):
        for g in range(NSA_GROUPS):
            tile_bias = bias_ref[bb, g, rows, :]
            for hd in range(g * NSA_HPG, (g + 1) * NSA_HPG):
                qa_ref[bb * NSA_HEADS + hd, NSA_DIM:NSA_DIM + SEL_SLOTS, :] = tile_bias

    def kqv(v):
        bb, hd = _split_virtual(v, NSA_HEADS)
        g = hd // NSA_HPG if isinstance(hd, int) else lax.div(hd, NSA_HPG)
        return k_ref[bb, g], qa_ref[v], vt_ref[bb, g]

    _causal_flash_step(i, j, nb * NSA_HEADS, kqv, tq, o_ref, m_ref, acc_ref, slot_ref)


def _slc_attention(nq, bias, ksa, vst, B, S, tq):
    nqt = S // tq
    nrow = bias.shape[2]
    assert tq // SEL_LEN == SEL_SLOTS // 2
    kern = functools.partial(_slc_kernel, tq=tq)
    qi, ki = _causal_pairs(nqt)
    qmap = lambda p, qi, ki: (0, 0, 0, qi[p])
    return pl.pallas_call(
        kern,
        grid_spec=pltpu.PrefetchScalarGridSpec(
            num_scalar_prefetch=2,
            grid=(qi.shape[0],),
            in_specs=[pl.BlockSpec((B, NSA_HEADS, LANES, tq), qmap),
                      pl.BlockSpec((B, NSA_GROUPS, nrow, tq), qmap),
                      pl.BlockSpec((B, NSA_GROUPS, tq, LANES), lambda p, qi, ki: (0, 0, ki[p], 0)),
                      pl.BlockSpec((B, NSA_GROUPS, NSA_DIM + SUM_ROWS, tq), lambda p, qi, ki: (0, 0, 0, ki[p]))],
            out_specs=pl.BlockSpec((B, tq, NSA_HEADS * NSA_DIM), lambda p, qi, ki: (0, qi[p], 0)),
            scratch_shapes=[pltpu.VMEM((B * NSA_HEADS, LANES, tq), BF16)]
                           + _flash_scratch(B * NSA_HEADS, NSA_DIM, tq, slots=2)
                           + [pltpu.SMEM((1,), jnp.int32)]),
        out_shape=jax.ShapeDtypeStruct((B, S, NSA_HEADS * NSA_DIM), BF16),
        compiler_params=_cparams(("arbitrary",)),
        name="slc_attn",
    )(qi, ki, nq, bias, ksa, vst).reshape(B * S, NSA_HEADS * NSA_DIM)


def _win_kernel(q_ref, k_ref, vt_ref, o_ref, m_ref, acc_ref, *, tq):
    i, j = pl.program_id(0), pl.program_id(1)

    def kqv(v):
        bb, hd = _split_virtual(v, NSA_HEADS)
        g = hd // NSA_HPG if isinstance(hd, int) else lax.div(hd, NSA_HPG)
        return k_ref[bb, g], q_ref[bb, hd][:NSA_DIM, :], vt_ref[bb, g]

    nv = q_ref.shape[0] * NSA_HEADS

    @pl.when(j == 0)
    def _():
        _flash_heads(nv, kqv, _tile_parts("diag", tq), None, (m_ref.at[0], acc_ref.at[0]))

    @pl.when((j == 1) & (i > 0))
    def _():
        _lazy_flash_step(nv, kqv, _tile_parts("prev", tq), m_ref, acc_ref, 0)

    @pl.when(j == 1)
    def _():
        _flash_finalize(o_ref, acc_ref.at[jnp.where(i > 0, 1, 0)])


def _win_attention(nq, kw, vwt, B, S, tq):
    nqt = S // tq
    kern = functools.partial(_win_kernel, tq=tq)
    kidx = lambda i, j: jnp.maximum(i - j, 0)
    return pl.pallas_call(
        kern,
        grid=(nqt, 2),
        in_specs=[pl.BlockSpec((B, NSA_HEADS, LANES, tq), lambda i, j: (0, 0, 0, i)),
                  pl.BlockSpec((B, NSA_GROUPS, tq, NSA_DIM), lambda i, j: (0, 0, kidx(i, j), 0)),
                  pl.BlockSpec((B, NSA_GROUPS, NSA_DIM + SUM_ROWS, tq), lambda i, j: (0, 0, 0, kidx(i, j)))],
        out_specs=pl.BlockSpec((B, tq, NSA_HEADS * NSA_DIM), lambda i, j: (0, i, 0)),
        out_shape=jax.ShapeDtypeStruct((B, S, NSA_HEADS * NSA_DIM), BF16),
        scratch_shapes=_flash_scratch(B * NSA_HEADS, NSA_DIM, tq, slots=2),
        compiler_params=_cparams(("parallel", "arbitrary")),
        name="win_attn",
    )(nq, kw, vwt).reshape(B * S, NSA_HEADS * NSA_DIM)


def _causal_conv(u, carry, w_ref):
    row = lax.broadcasted_iota(jnp.int32, u.shape, 0)
    c1 = carry[SUBLANES - 1:SUBLANES]
    c2 = carry[SUBLANES - 2:SUBLANES - 1]
    u1 = jnp.where(row == 0, c1, pltpu.roll(u, 1, axis=0))
    u2 = jnp.where(row == 0, c2, jnp.where(row == 1, c1, pltpu.roll(u, 2, axis=0)))
    return w_ref[0:1] * u2 + w_ref[1:2] * u1 + w_ref[2:3] * u


def _merge_kernel(x_ref, g_ref, wgc_ref, cw_ref, ym_ref, oc_ref, os_ref, ow_ref, gl_ref,
                  wbc_ref, wbm_ref, wbn_ref, wo_ref, gp_ref, out_ref, carry_ref, *, tiles_per_seq):
    D = x_ref.shape[1]
    DC = wbc_ref.shape[0]

    @pl.when(pl.program_id(0) % tiles_per_seq == 0)
    def _():
        carry_ref[...] = jnp.zeros(carry_ref.shape, F32)

    x = x_ref[...]
    h = _rms(x, g_ref[...]).astype(BF16)
    p = _dot(h, wgc_ref[...])
    u = p[:, 3 * D + DC:3 * D + 2 * DC] * p[:, 3 * D + 2 * DC:3 * D + 3 * DC]
    y_conv = p[:, 3 * D:3 * D + DC] * _causal_conv(u, carry_ref[...], cw_ref)
    carry_ref[...] = u[u.shape[0] - SUBLANES:]
    gsig = jax.nn.sigmoid(gl_ref[...])
    first = lax.broadcasted_iota(jnp.int32, (x.shape[0], LANES), 1) < NSA_DIM

    def branch_gate(c):
        col = lambda hd: jnp.broadcast_to(gsig[:, 3 * hd + c:3 * hd + c + 1], (x.shape[0], LANES))
        return jnp.concatenate([jnp.where(first, col(hd), col(hd + 1)) for hd in range(0, NSA_HEADS, 2)],
                               axis=1)

    y_nsa = (branch_gate(0) * oc_ref[...].astype(F32) + branch_gate(1) * os_ref[...].astype(F32)
             + branch_gate(2) * ow_ref[...].astype(F32))
    merged = (jax.nn.sigmoid(p[:, 0:D]) * _dot(y_conv.astype(BF16), wbc_ref[...])
              + jax.nn.sigmoid(p[:, D:2 * D]) * _dot(ym_ref[...], wbm_ref[...])
              + jax.nn.sigmoid(p[:, 2 * D:3 * D]) * _dot(y_nsa.astype(BF16), wbn_ref[...]))
    out_ref[...] = x + _rms(_dot(merged.astype(BF16), wo_ref[...]), gp_ref[...])


def _merge(x2, g_pre, w_gc, conv_w, y_mla, o_cmp, o_slc, o_win, gl, wbc, wbm, wbn, wo, g_post,
           S, tm):
    T, D = x2.shape
    HD = y_mla.shape[1]
    row = lambda w: pl.BlockSpec((tm, w), lambda i: (i, 0))
    kern = functools.partial(_merge_kernel, tiles_per_seq=S // tm)
    return pl.pallas_call(
        kern,
        grid=(T // tm,),
        in_specs=[row(D), _const_spec((1, D)), _const_spec(w_gc.shape), _const_spec(conv_w.shape),
                  row(HD), row(HD), row(HD), row(HD), row(LANES),
                  _const_spec(wbc.shape), _const_spec(wbm.shape), _const_spec(wbn.shape),
                  _const_spec(wo.shape), _const_spec((1, D))],
        out_specs=row(D),
        out_shape=jax.ShapeDtypeStruct((T, D), F32),
        scratch_shapes=[pltpu.VMEM((SUBLANES, wbc.shape[0]), F32)],
        compiler_params=_cparams(("arbitrary",)),
        name="merge",
    )(x2, g_pre, w_gc, conv_w, y_mla, o_cmp, o_slc, o_win, gl, wbc, wbm, wbn, wo, g_post)


def _ffn_kernel(x_ref, g_ref, wup_ref, cw_ref, cb_ref, wdn_ref, gp_ref, out_ref, carry_ref,
                *, tiles_per_seq, chunk):
    F = wdn_ref.shape[0]

    @pl.when(pl.program_id(0) % tiles_per_seq == 0)
    def _():
        carry_ref[...] = jnp.zeros(carry_ref.shape, F32)

    x = x_ref[...]
    h = _rms(x, g_ref[...]).astype(BF16)
    acc = None
    for c0 in range(0, F, chunk):
        a = _dot(h, wup_ref[:, c0:c0 + chunk])
        b = _dot(h, wup_ref[:, F + c0:F + c0 + chunk])
        ac = _causal_conv(a, carry_ref[:, c0:c0 + chunk], cw_ref.at[:, c0:c0 + chunk])
        ac = ac + cb_ref[:, c0:c0 + chunk]
        carry_ref[:, c0:c0 + chunk] = a[a.shape[0] - SUBLANES:]
        d = _dot((jax.nn.gelu(ac) * b).astype(BF16), wdn_ref[c0:c0 + chunk, :])
        acc = d if acc is None else acc + d
    out_ref[...] = x + _rms(acc, gp_ref[...])


def _ffn(x2, g_pre, w_up, conv_w, conv_b, w_dn, g_post, S, tm, chunk):
    T, D = x2.shape
    F = w_dn.shape[0]
    row = pl.BlockSpec((tm, D), lambda i: (i, 0))
    kern = functools.partial(_ffn_kernel, tiles_per_seq=S // tm, chunk=chunk)
    return pl.pallas_call(
        kern,
        grid=(T // tm,),
        in_specs=[row, _const_spec((1, D)), _const_spec(w_up.shape), _const_spec(conv_w.shape),
                  _const_spec((1, F)), _const_spec(w_dn.shape), _const_spec((1, D))],
        out_specs=row,
        out_shape=jax.ShapeDtypeStruct((T, D), F32),
        scratch_shapes=[pltpu.VMEM((SUBLANES, F), F32)],
        compiler_params=_cparams(("arbitrary",)),
        name="ffn",
    )(x2, g_pre, w_up, conv_w, conv_b, w_dn, g_post)


def _overlap_matrix_t(S):
    nch = S // CMP_STRIDE
    n_cmp = (S - CMP_LEN) // CMP_STRIDE + 1
    cs = np.arange(nch)[None, :] * CMP_STRIDE
    ss = np.arange(S // SEL_LEN)[:, None] * SEL_LEN
    ov = (cs <= ss + SEL_LEN - 1) & (cs + CMP_LEN - 1 >= ss) & (np.arange(nch)[None, :] < n_cmp)
    return ov.astype(np.float32)


def _pad_rows(w, rows):
    return jnp.concatenate([w, jnp.zeros((rows - w.shape[0],) + w.shape[1:], w.dtype)], axis=0)


def kernel(x, positions, norm_mix_pre, norm_mix_post, w_in, conv_w, mla_q_norm, mla_w_uq, mla_kv_norm, mla_w_ukv, nsa_cmp_pos_k, nsa_cmp_pos_v, nsa_cmp_w_k, nsa_cmp_w_v, w_branch_conv, w_branch_mla, w_branch_nsa, w_out, norm_ffn_pre, norm_ffn_post, ffn_w_up, ffn_conv_w, ffn_conv_b, ffn_w_down):
    B, S, D = x.shape
    T = B * S
    depth = w_in.shape[0]
    DC = conv_w.shape[2]
    F = ffn_w_down.shape[1]
    QL = mla_q_norm.shape[1]
    KL = mla_kv_norm.shape[1]
    tq = 512
    assert S % tq == 0 and WINDOW == tq and N_SEL <= S // SEL_LEN <= LANES
    tm_proj, tm_merge, tm_ffn, ffn_chunk = 512, 512, 512, 1408
    assert F % ffn_chunk == 0 and ffn_chunk % LANES == 0

    x2 = x.reshape(T, D)
    pos2 = positions.reshape(1, T)
    rc = jnp.asarray(_rope_consts())
    ovt = jnp.asarray(_overlap_matrix_t(S), BF16)
    half = CMP_LEN // 2 * NSA_DIM
    o_att = 3 * D + 3 * DC

    for l in range(depth):
        w = w_in[l]
        o_kr = o_att + QL + KL
        z = lambda n: jnp.zeros((D, n), F32)
        w_att = jnp.concatenate(
            [w[:, o_att:o_kr], z(MLA_NOPE), w[:, o_kr:o_kr + MLA_ROPE], z(LANES - MLA_NOPE - MLA_ROPE),
             w[:, o_kr + MLA_ROPE:o_kr + MLA_ROPE + NSA_HEADS * NSA_DIM + 6 * NSA_GROUPS * NSA_DIM],
             w[:, w.shape[1] - 3 * NSA_HEADS:], z(LANES - 3 * NSA_HEADS)], axis=1).astype(BF16)
        dq = MLA_NOPE + MLA_ROPE
        w_uq = jnp.pad(mla_w_uq[l].reshape(QL, MLA_HEADS, dq),
                       ((0, 0), (0, 0), (0, LANES - dq))).reshape(QL, MLA_HEADS * LANES).astype(BF16)
        w_v_t = mla_w_ukv[l].reshape(KL, MLA_HEADS, MLA_NOPE + MLA_V)[:, :, MLA_NOPE:].reshape(
            KL, MLA_HEADS * MLA_V).T.astype(BF16)
        (qm, km, vmt, nq, kc_in, vc_in, ksa, vst, kw, vwt, gl) = _project(
            x2, pos2, norm_mix_pre[l][None], w_att, mla_q_norm[l][None], w_uq.T,
            mla_kv_norm[l][None], mla_w_ukv[l].astype(BF16), w_v_t, rc, S, tm_proj)

        wv = nsa_cmp_w_v[l].reshape(2, half, NSA_DIM)
        zv = jnp.zeros_like(wv)
        wv_pad = jnp.stack([jnp.concatenate([wv, zv], axis=2), jnp.concatenate([zv, wv], axis=2)])
        kc, vct = _compress(kc_in, vc_in,
                            nsa_cmp_pos_k[l].reshape(2, half), nsa_cmp_pos_v[l].reshape(2, half),
                            nsa_cmp_w_k[l].reshape(2, half, NSA_DIM).astype(BF16),
                            wv_pad.astype(BF16), B)

        y_mla = _mla_attention(qm, km, vmt, B, S, tq)
        o_cmp, bias = _cmp_select(nq, kc, vct, ovt, B, S, tq)
        o_slc = _slc_attention(nq, bias, ksa, vst, B, S, tq)
        o_win = _win_attention(nq, kw, vwt, B, S, tq)

        x2 = _merge(x2, norm_mix_pre[l][None], w[:, :o_att].astype(BF16),
                    _pad_rows(conv_w[l], SUBLANES), y_mla, o_cmp, o_slc, o_win, gl,
                    w_branch_conv[l].astype(BF16), w_branch_mla[l].astype(BF16),
                    w_branch_nsa[l].astype(BF16), w_out[l].astype(BF16), norm_mix_post[l][None],
                    S, tm_merge)
        x2 = _ffn(x2, norm_ffn_pre[l][None], ffn_w_up[l].astype(BF16),
                  _pad_rows(ffn_conv_w[l], SUBLANES), ffn_conv_b[l][None],
                  ffn_w_down[l].astype(BF16), norm_ffn_post[l][None], S, tm_ffn, ffn_chunk)
    return x2.reshape(B, S, D)
```

```python
import functools
import math

import numpy as np
import jax
import jax.numpy as jnp
from jax import lax
from jax.experimental import pallas as pl
from jax.experimental.pallas import tpu as pltpu

F32 = jnp.float32
BF16 = jnp.bfloat16

ROPE_THETA = 500000.0
RMS_EPS = 1e-6
CONV_WIDTH = 3
MLA_HEADS = 8
MLA_NOPE = 64
MLA_ROPE = 32
MLA_V = 64
NSA_HEADS = 8
NSA_GROUPS = 2
NSA_HPG = NSA_HEADS // NSA_GROUPS
NSA_DIM = 64
NSA_ROT = NSA_DIM // 4
CMP_LEN = 32
CMP_STRIDE = 16
SEL_LEN = 64
N_SEL = 16
WINDOW = 512

LANES = 128
SUBLANES = 8
VMEM_LIMIT = 56 * 1024 * 1024
MASKED = -1e30
M_FLOOR = -1e20
SEL_BIAS = -2.0 ** 100
LAZY_SPAN = 64.0
LOG2E = math.log2(math.e)
SEL_SLOTS = 16
ROPE_ROWS = 32
SUM_ROWS = 16


def _cparams(sem):
    return pltpu.CompilerParams(dimension_semantics=sem, vmem_limit_bytes=VMEM_LIMIT)


def _const_spec(shape):
    nd = len(shape)
    return pl.BlockSpec(shape, lambda *_: (0,) * nd, pipeline_mode=pl.Buffered(1))


def _rms(x, g):
    return x * lax.rsqrt(jnp.mean(x * x, axis=-1, keepdims=True) + RMS_EPS) * g


def _dot(a, b):
    return jnp.dot(a, b, preferred_element_type=F32)


def _split_bf16(a, terms):
    pieces, rem = [], a
    for _ in range(terms):
        piece = rem.astype(BF16)
        rem = rem - piece.astype(F32)
        pieces.append(piece)
    return pieces


def _rope(x, cos, sin_up, sin_dn, half):
    return (x * cos + pltpu.roll(x, half, axis=1) * sin_up
            + pltpu.roll(x, LANES - half, axis=1) * sin_dn)


def _lanes(table, lane, base, moves):
    out = base
    for dst, width, src in moves:
        rolled = pltpu.roll(table, (dst - src) % LANES, axis=1)
        out = jnp.where((lane >= dst) & (lane < dst + width), rolled, out)
    return out


def _proj_kernel(x_ref, pos_ref, g_ref, watt_ref, qg_ref, wuqt_ref, kvg_ref, wukv_ref, wvt_ref, rc_ref,
                 qm_ref, km_ref, vmt_ref, nq_ref, kc_ref, vc_ref, ksa_ref, vst_ref, kw_ref, vwt_ref,
                 gl_ref, kstage_ref, vstage_ref, *, mla_scale, nsa_scale, tiles_per_seq):
    tm = x_ref.shape[0]
    hm, hn = MLA_ROPE // 2, NSA_ROT // 2
    x = x_ref[...]
    h = _rms(x, g_ref[...]).astype(BF16)
    p = _dot(h, watt_ref[...])
    ql, kl = qg_ref.shape[1], kvg_ref.shape[1]
    c_kr, c_nq, c_kv, c_gl = _att_columns(ql, kl)
    lane = lax.broadcasted_iota(jnp.int32, (tm, LANES), 1)
    low = lane < NSA_DIM
    ang_t = rc_ref[:, 0:1] * pos_ref[...].astype(F32)
    cos_t, sin_t = jnp.cos(ang_t), jnp.sin(ang_t)
    cs = jnp.concatenate([cos_t, sin_t, jnp.zeros((LANES - 2 * ROPE_ROWS, tm), F32)], axis=0).T
    cos_m = _lanes(cs, lane, 1.0, [(MLA_NOPE, hm, 0), (MLA_NOPE + hm, hm, 0)])
    sup_m = _lanes(cs, lane, 0.0, [(MLA_NOPE + hm, hm, ROPE_ROWS)])
    sdn_m = -_lanes(cs, lane, 0.0, [(MLA_NOPE, hm, ROPE_ROWS)])
    cos_n = _lanes(cs, lane, 1.0, [(0, hn, hm), (hn, hn, hm), (NSA_DIM, hn, hm), (NSA_DIM + hn, hn, hm)])
    sup_n = _lanes(cs, lane, 0.0, [(hn, hn, ROPE_ROWS + hm), (NSA_DIM + hn, hn, ROPE_ROWS + hm)])
    sdn_n = -_lanes(cs, lane, 0.0, [(0, hn, ROPE_ROWS + hm), (NSA_DIM, hn, ROPE_ROWS + hm)])
    ones_rows = jnp.where(lax.broadcasted_iota(jnp.int32, (SUM_ROWS, tm), 0) == 0, 1.0, 0.0)

    def with_sum_rows(vt):
        return jnp.concatenate([vt, ones_rows], axis=0).astype(BF16)

    def halves(slab):
        return (jnp.where(low, slab, 0.0), jnp.where(low, pltpu.roll(slab, NSA_DIM, axis=1), 0.0))

    def rope_rows(x1, x2, c, s_):
        return x1 * c - x2 * s_, x2 * c + x1 * s_

    cqn_t = _rms(p[:, 0:ql], qg_ref[...]).T.astype(BF16)
    qt = _dot(wuqt_ref[...], cqn_t)
    for hd in range(MLA_HEADS):
        r0 = hd * LANES
        y1, y2 = rope_rows(qt[r0 + MLA_NOPE:r0 + MLA_NOPE + hm], qt[r0 + MLA_NOPE + hm:r0 + MLA_NOPE + 2 * hm],
                           cos_t[0:hm], sin_t[0:hm])
        slab_t = jnp.concatenate([qt[r0:r0 + MLA_NOPE], y1, y2, qt[r0 + MLA_NOPE + 2 * hm:r0 + LANES]], axis=0)
        qm_ref[0, hd] = (slab_t * mla_scale).astype(BF16)
    ckvn = _rms(p[:, ql:ql + kl], kvg_ref[...])
    kv = _dot(ckvn.astype(BF16), wukv_ref[...])
    kr = _rope(p[:, c_kr:c_kr + LANES], cos_m, sup_m, sdn_m, hm)
    vt = _dot(wvt_ref[...], ckvn.T.astype(BF16))
    for hd in range(MLA_HEADS):
        km_ref[0, hd] = jnp.where(lane < MLA_NOPE, kv[:, hd * LANES:(hd + 1) * LANES], kr).astype(BF16)
        vmt_ref[0, hd] = with_sum_rows(vt[hd * MLA_V:(hd + 1) * MLA_V])
    zpad = jnp.zeros((LANES - NSA_DIM, tm), BF16)
    for pr in range(NSA_HEADS // 2):
        slab_t = p[:, c_nq + pr * LANES:c_nq + (pr + 1) * LANES].T
        for half in range(2):
            r0 = half * NSA_DIM
            y1, y2 = rope_rows(slab_t[r0:r0 + hn], slab_t[r0 + hn:r0 + 2 * hn],
                               cos_t[hm:hm + hn], sin_t[hm:hm + hn])
            head_t = jnp.concatenate([y1, y2, slab_t[r0 + 2 * hn:r0 + NSA_DIM]], axis=0) * nsa_scale
            nq_ref[0, 2 * pr + half, 0:NSA_DIM, :] = head_t.astype(BF16)
            nq_ref[0, 2 * pr + half, NSA_DIM:LANES, :] = zpad
    nsa = lambda idx: p[:, c_kv + idx * LANES:c_kv + (idx + 1) * LANES]
    def store_chunks(slab, stage_ref, out_ref):
        stage_ref[...] = slab
        nrow = tm // CMP_STRIDE
        first = lax.broadcasted_iota(jnp.int32, (nrow, LANES), 1) < NSA_DIM
        for r in range(0, CMP_STRIDE, 2):
            a = stage_ref[pl.ds(r, nrow, stride=CMP_STRIDE), :]
            b = stage_ref[pl.ds(r + 1, nrow, stride=CMP_STRIDE), :]
            cols = slice(r * NSA_DIM, (r + 2) * NSA_DIM)
            out_ref[0, :, cols] = jnp.where(first, a, pltpu.roll(b, NSA_DIM, axis=1)).astype(BF16)
            out_ref[1, :, cols] = jnp.where(first, pltpu.roll(a, NSA_DIM, axis=1), b).astype(BF16)

    store_chunks(_rope(nsa(0), cos_n, sup_n, sdn_n, hn), kstage_ref, kc_ref)
    store_chunks(nsa(1), vstage_ref, vc_ref)
    t_seq = (pl.program_id(0) % tiles_per_seq) * tm + lax.broadcasted_iota(jnp.int32, (tm, LANES), 0)
    slot = NSA_DIM + (lax.shift_right_logical(t_seq, int(math.log2(SEL_LEN))) & (SEL_SLOTS // 2 - 1))
    onehot = jnp.where(lane == slot, 1.0, 0.0)
    for g, kg in enumerate(halves(_rope(nsa(2), cos_n, sup_n, sdn_n, hn))):
        ksa_ref[0, g] = (kg + onehot).astype(BF16)
    vst = nsa(3).T
    vst_ref[0, 0] = with_sum_rows(vst[:NSA_DIM])
    vst_ref[0, 1] = with_sum_rows(vst[NSA_DIM:])
    kws = _rope(nsa(4), cos_n, sup_n, sdn_n, hn)
    kw_ref[0, 0] = kws[:, :NSA_DIM].astype(BF16)
    kw_ref[0, 1] = pltpu.roll(kws, NSA_DIM, axis=1)[:, :NSA_DIM].astype(BF16)
    vwt = nsa(5).T
    vwt_ref[0, 0] = with_sum_rows(vwt[:NSA_DIM])
    vwt_ref[0, 1] = with_sum_rows(vwt[NSA_DIM:])
    gl_ref[...] = p[:, c_gl:c_gl + LANES]


def _att_columns(ql, kl):
    c_kr = ql + kl
    c_nq = c_kr + LANES
    c_kv = c_nq + NSA_HEADS * NSA_DIM
    c_gl = c_kv + 6 * NSA_GROUPS * NSA_DIM
    return c_kr, c_nq, c_kv, c_gl


def _rope_consts():
    rc = np.zeros((ROPE_ROWS, LANES), np.float32)
    hm = MLA_ROPE // 2
    rc[:hm] = (ROPE_THETA ** (-np.arange(hm, dtype=np.float32) / hm))[:, None]
    hn = NSA_ROT // 2
    rc[hm:hm + hn] = (ROPE_THETA ** (-np.arange(hn, dtype=np.float32) / hn))[:, None]
    return rc


def _project(x2, pos2, g_pre, w_att, q_g, w_uq_t, kv_g, w_ukv, w_v_t, rc, S, tm):
    T, D = x2.shape
    n_att = w_att.shape[1]
    tps = S // tm
    B = T // S
    brows = lambda n, d: (jax.ShapeDtypeStruct((B, n, S, d), BF16),
                          pl.BlockSpec((1, n, tm, d), lambda i: (i // tps, 0, i % tps, 0)))
    bcols = lambda n, d: (jax.ShapeDtypeStruct((B, n, d, S), BF16),
                          pl.BlockSpec((1, n, d, tm), lambda i: (i // tps, 0, 0, i % tps)))
    chunked = (jax.ShapeDtypeStruct((NSA_GROUPS, T // CMP_STRIDE, CMP_STRIDE * NSA_DIM), BF16),
               pl.BlockSpec((NSA_GROUPS, tm // CMP_STRIDE, CMP_STRIDE * NSA_DIM), lambda i: (0, i, 0)))
    outs = [bcols(MLA_HEADS, LANES), brows(MLA_HEADS, LANES), bcols(MLA_HEADS, MLA_V + SUM_ROWS),
            bcols(NSA_HEADS, LANES), chunked, chunked,
            brows(NSA_GROUPS, LANES), bcols(NSA_GROUPS, NSA_DIM + SUM_ROWS), brows(NSA_GROUPS, NSA_DIM),
            bcols(NSA_GROUPS, NSA_DIM + SUM_ROWS),
            (jax.ShapeDtypeStruct((T, LANES), F32), pl.BlockSpec((tm, LANES), lambda i: (i, 0)))]
    kern = functools.partial(_proj_kernel,
                             mla_scale=float((MLA_NOPE + MLA_ROPE) ** -0.5 * LOG2E),
                             nsa_scale=float(NSA_DIM ** -0.5 * LOG2E),
                             tiles_per_seq=S // tm)
    return pl.pallas_call(
        kern,
        grid=(T // tm,),
        in_specs=[
            pl.BlockSpec((tm, D), lambda i: (i, 0)),
            pl.BlockSpec((1, tm), lambda i: (0, i)),
            _const_spec((1, D)),
            _const_spec((D, n_att)),
            _const_spec((1, q_g.shape[1])),
            _const_spec(w_uq_t.shape),
            _const_spec((1, kv_g.shape[1])),
            _const_spec(w_ukv.shape),
            _const_spec(w_v_t.shape),
            _const_spec((ROPE_ROWS, LANES)),
        ],
        out_specs=[o[1] for o in outs],
        out_shape=[o[0] for o in outs],
        scratch_shapes=[pltpu.VMEM((tm, LANES), F32), pltpu.VMEM((tm, LANES), F32)],
        compiler_params=_cparams(("parallel",)),
        name="proj",
    )(x2, pos2, g_pre, w_att, q_g, w_uq_t, kv_g, w_ukv, w_v_t, rc)


def _compress_kernel(xk_ref, xv_ref, pk_ref, pv_ref, wk_ref, wv_ref, ok_ref, ovt_ref):
    n = xk_ref.shape[1]
    vboth = None
    for g in range(NSA_GROUPS):
        xk = xk_ref[g].astype(F32)
        lo = _dot((xk + pk_ref[0:1]).astype(BF16), wk_ref[0])
        hi = _dot((xk + pk_ref[1:2]).astype(BF16), wk_ref[1])
        ok_ref[g] = (lo + pltpu.roll(hi, n - 1, axis=0)).astype(BF16)
        xv = xv_ref[g].astype(F32)
        lo = _dot((xv + pv_ref[0:1]).astype(BF16), wv_ref[g, 0])
        hi = _dot((xv + pv_ref[1:2]).astype(BF16), wv_ref[g, 1])
        vg = lo + pltpu.roll(hi, n - 1, axis=0)
        vboth = vg if vboth is None else vboth + vg
    ovt_ref[0] = vboth.T.astype(BF16)


def _compress(xk, xv, pk, pv, wk, wv, B):
    G, TC, CW = xk.shape
    nch = TC // B
    xspec = pl.BlockSpec((G, nch, CW), lambda b: (0, b, 0))
    return pl.pallas_call(
        _compress_kernel,
        grid=(B,),
        in_specs=[xspec, xspec, _const_spec(pk.shape), _const_spec(pv.shape),
                  _const_spec(wk.shape), _const_spec(wv.shape)],
        out_specs=[pl.BlockSpec((G, nch, NSA_DIM), lambda b: (0, b, 0)),
                   pl.BlockSpec((1, G * NSA_DIM, nch), lambda b: (b, 0, 0))],
        out_shape=[jax.ShapeDtypeStruct((G, TC, NSA_DIM), BF16),
                   jax.ShapeDtypeStruct((B, G * NSA_DIM, nch), BF16)],
        compiler_params=_cparams(("parallel",)),
        name="compress",
    )(xk, xv, pk, pv, wk, wv)


def _cmp_select_kernel(q_ref, kc_ref, vct_ref, ovt_ref, o_ref, bias_ref, *, tq, n_pick, nqt, n_var):
    i = pl.program_id(1)
    q0 = i * tq
    nch_all = kc_ref.shape[1]
    nblk_all = ovt_ref.shape[0]
    tiles_per_var = nqt // n_var

    def variant(nch, nblk):
        for c0 in range(0, tq, tq // 2):
            half(nch, nblk, c0, tq // 2)

    def half(nch, nblk, c0, w):
        cols = slice(c0, c0 + w)
        t_c = q0 + c0 + lax.broadcasted_iota(jnp.int32, (nch, w), 1)
        n_c = lax.broadcasted_iota(jnp.int32, (nch, w), 0)
        cmask = (n_c * CMP_STRIDE + (CMP_LEN - 1)) <= t_c
        t_b = q0 + c0 + lax.broadcasted_iota(jnp.int32, (nblk, w), 1)
        blk = lax.broadcasted_iota(jnp.int32, (nblk, w), 0)
        blk_f = blk.astype(F32)
        cur = lax.shift_right_logical(t_b, int(math.log2(SEL_LEN)))
        forced = (blk == 0) | (blk == cur) | (blk == cur - 1)
        causal = blk * SEL_LEN <= t_b
        outs = []
        for g in range(NSA_GROUPS):
            psum = None
            scores = [_dot(kc_ref[g, :nch, :], q_ref[0, g * NSA_HPG + j][:NSA_DIM, cols])
                      for j in range(NSA_HPG)]
            for j in range(NSA_HPG):
                s = jnp.where(cmask, scores[j], MASKED)
                m = jnp.maximum(jnp.max(s, axis=0, keepdims=True), M_FLOOR)
                p = jnp.exp2(s - m)
                l = jnp.sum(p, axis=0, keepdims=True)
                pn = p * jnp.where(l > 0.0, 1.0 / l, 0.0)
                outs.append(_dot(vct_ref[0, g * NSA_DIM:(g + 1) * NSA_DIM, :nch], pn.astype(BF16)))
                psum = pn if psum is None else psum + pn
            imp = None
            for piece in _split_bf16(psum, 3):
                d = _dot(ovt_ref[:nblk, :nch], piece)
                imp = d if imp is None else imp + d
            v0 = jnp.where(forced | jnp.logical_not(causal), -1.0, imp)
            v = v0
            for _ in range(n_pick):
                mx = jnp.max(v, axis=0, keepdims=True)
                first = jnp.min(jnp.where(v == mx, blk_f, float(nblk)), axis=0, keepdims=True)
                v = jnp.where(blk_f == first, -2.0, v)
            sel = forced | ((v == -2.0) & (v0 >= 0.0))
            bias = jnp.where(sel, 0.0, SEL_BIAS)
            per_tile = SEL_SLOTS // 2
            pad = jnp.zeros((SEL_SLOTS - per_tile, w), F32)
            pieces = []
            for kt in range(nblk_all // per_tile):
                if kt * per_tile < nblk:
                    pieces += [bias[kt * per_tile:(kt + 1) * per_tile], pad]
                else:
                    pieces += [jnp.full((per_tile, w), SEL_BIAS, F32), pad]
            bias_ref[0, g, :, cols] = jnp.concatenate(pieces, axis=0).astype(BF16)
        o_ref[cols, :] = jnp.concatenate(outs, axis=0).T.astype(BF16)

    for k in range(n_var):
        @pl.when((i >= k * tiles_per_var) & (i < (k + 1) * tiles_per_var))
        def _():
            variant(nch_all * (k + 1) // n_var, nblk_all * (k + 1) // n_var)


def _cmp_select(nq, kc, vct, ovt, B, S, tq):
    T = B * S
    nch = kc.shape[1] // B
    nblk = ovt.shape[0]
    nqt = S // tq
    n_var = 4 if nqt % 4 == 0 and (nch // 4) % LANES == 0 and (nblk // 4) % SUBLANES == 0 else 1
    kern = functools.partial(_cmp_select_kernel, tq=tq, n_pick=N_SEL - 3, nqt=nqt, n_var=n_var)
    return pl.pallas_call(
        kern,
        grid=(B, nqt),
        in_specs=[
            pl.BlockSpec((1, NSA_HEADS, LANES, tq), lambda b, i: (b, 0, 0, i)),
            pl.BlockSpec((NSA_GROUPS, nch, NSA_DIM), lambda b, i: (0, b, 0)),
            pl.BlockSpec((1, NSA_GROUPS * NSA_DIM, nch), lambda b, i: (b, 0, 0)),
            _const_spec(ovt.shape),
        ],
        out_specs=[pl.BlockSpec((tq, NSA_HEADS * NSA_DIM), lambda b, i: (b * nqt + i, 0)),
                   pl.BlockSpec((1, NSA_GROUPS, 2 * nblk, tq), lambda b, i: (b, 0, 0, i))],
        out_shape=[jax.ShapeDtypeStruct((T, NSA_HEADS * NSA_DIM), BF16),
                   jax.ShapeDtypeStruct((B, NSA_GROUPS, 2 * nblk, S), BF16)],
        compiler_params=_cparams(("parallel", "parallel")),
        name="cmp_select",
    )(nq, kc, vct, ovt)


def _tile_parts(kind, tq):
    hq = tq // 2
    if kind == "full":
        return [(slice(0, tq), slice(0, hq), None), (slice(0, tq), slice(hq, tq), None)]

    def mask(rows, cols, below):
        key = rows.start + lax.broadcasted_iota(jnp.int32, (rows.stop - rows.start, hq), 0)
        qry = cols.start + lax.broadcasted_iota(jnp.int32, (rows.stop - rows.start, hq), 1)
        return (key <= qry) if below else (key > qry)

    lo, hi, all_ = slice(0, hq), slice(hq, tq), slice(0, tq)
    if kind == "diag":
        return [(lo, lo, mask(lo, lo, True)), (all_, hi, mask(all_, hi, True))]
    return [(all_, lo, mask(all_, lo, False)), (hi, hi, mask(hi, hi, False))]


def _flash_heads(nheads, kqv, parts, src, dst, lazy=False, unroll=True):
    def scores(hd, part):
        rows, cols, _ = part
        k, qt, _ = kqv(hd)
        return _dot(k[rows], qt[:, cols])

    def update(hd, part, s):
        rows, cols, mask = part
        if mask is not None:
            s = jnp.where(mask, s, MASKED)
        tmax = jnp.max(s, axis=0, keepdims=True)
        vt = kqv(hd)[2][:, rows]
        rise = None
        if src is None:
            m_new = jnp.maximum(tmax, M_FLOOR)
            acc = _dot(vt, jnp.exp2(s - m_new).astype(BF16))
        else:
            m_old = src[0][hd, :, cols]
            m_new = jnp.maximum(m_old, tmax)
            alpha = jnp.exp2(m_old - m_new)
            if lazy:
                acc = (src[1][hd, :, cols] + _dot(vt, jnp.exp2(s - m_old).astype(BF16))) * alpha
                rise = tmax - m_old
            else:
                acc = alpha * src[1][hd, :, cols] + _dot(vt, jnp.exp2(s - m_new).astype(BF16))
        dst[0][hd, :, cols] = m_new
        dst[1][hd, :, cols] = acc
        return rise

    if not unroll:
        def body(hd, carry):
            for part in parts:
                update(hd, part, scores(hd, part))
            return carry
        lax.fori_loop(0, nheads, body, 0)
        return None
    todo = [(hd, part) for hd in range(nheads) for part in parts]
    ahead = 2 if lazy else 4
    pending = [scores(*t) for t in todo[:ahead]]
    worst = {}
    for n, (hd, part) in enumerate(todo):
        s = pending.pop(0)
        if n + ahead < len(todo):
            pending.append(scores(*todo[n + ahead]))
        rise = update(hd, part, s)
        if rise is not None:
            key = rise.shape
            worst[key] = rise if key not in worst else jnp.maximum(worst[key], rise)
    if not worst:
        return None
    tops = [jnp.max(w, axis=1, keepdims=True) for w in worst.values()]
    return functools.reduce(jnp.maximum, tops)


def _flash_finalize(o_ref, acc_ref):
    nv, rows, tq = acc_ref.shape
    dv = rows - SUM_ROWS
    nh = nv // o_ref.shape[0]
    for bb in range(o_ref.shape[0]):
        sl = slice(bb * nh, (bb + 1) * nh)
        o = acc_ref[sl, :dv, :] * (1.0 / acc_ref[sl, dv:dv + 1, :])
        o_ref[bb] = o.reshape(nh * dv, tq).T.astype(BF16)


def _flash_scratch(nv, dv, tq):
    return [pltpu.VMEM((2, nv, 1, tq), F32), pltpu.VMEM((2, nv, dv + SUM_ROWS, tq), F32)]


def _causal_pairs(nqt):
    pairs = [(i, j) for i in range(nqt) for j in [i] + list(range(i))]
    return (jnp.asarray([p[0] for p in pairs], jnp.int32), jnp.asarray([p[1] for p in pairs], jnp.int32))


def _lazy_flash_step(nheads, kqv, parts, m_ref, acc_ref, cur):
    src = (m_ref.at[cur], acc_ref.at[cur])
    dst = (m_ref.at[1 - cur], acc_ref.at[1 - cur])
    worst = _flash_heads(nheads, kqv, parts, src, dst, lazy=True)

    @pl.when(jnp.max(worst) > LAZY_SPAN)
    def _():
        _flash_heads(nheads, kqv, parts, src, dst, unroll=False)


def _causal_flash_step(i, j, nheads, kqv, tq, o_ref, m_ref, acc_ref, slot_ref):
    @pl.when(j == i)
    def _():
        _flash_heads(nheads, kqv, _tile_parts("diag", tq), None, (m_ref.at[0], acc_ref.at[0]))
        slot_ref[0] = 0

    @pl.when(j < i)
    def _():
        cur = slot_ref[0]
        _lazy_flash_step(nheads, kqv, _tile_parts("full", tq), m_ref, acc_ref, cur)
        slot_ref[0] = 1 - cur

    @pl.when((j == i - 1) | (i == 0))
    def _():
        _flash_finalize(o_ref, acc_ref.at[slot_ref[0]])


def _split_virtual(v, nheads):
    if isinstance(v, int):
        return v // nheads, v % nheads
    return lax.div(v, nheads), lax.rem(v, nheads)


def _mla_kernel(qi_ref, ki_ref, q_ref, k_ref, vt_ref, o_ref, m_ref, acc_ref, slot_ref, *, tq):
    i, j = qi_ref[pl.program_id(0)], ki_ref[pl.program_id(0)]

    def kqv(v):
        bb, hd = _split_virtual(v, MLA_HEADS)
        return k_ref[bb, hd], q_ref[bb, hd], vt_ref[bb, hd]

    _causal_flash_step(i, j, q_ref.shape[0] * MLA_HEADS, kqv, tq, o_ref, m_ref, acc_ref, slot_ref)


def _mla_attention(qm, km, vmt, B, S, tq):
    nqt = S // tq
    kern = functools.partial(_mla_kernel, tq=tq)
    qi, ki = _causal_pairs(nqt)
    return pl.pallas_call(
        kern,
        grid_spec=pltpu.PrefetchScalarGridSpec(
            num_scalar_prefetch=2,
            grid=(qi.shape[0],),
            in_specs=[pl.BlockSpec((B, MLA_HEADS, LANES, tq), lambda p, qi, ki: (0, 0, 0, qi[p])),
                      pl.BlockSpec((B, MLA_HEADS, tq, LANES), lambda p, qi, ki: (0, 0, ki[p], 0)),
                      pl.BlockSpec((B, MLA_HEADS, MLA_V + SUM_ROWS, tq), lambda p, qi, ki: (0, 0, 0, ki[p]))],
            out_specs=pl.BlockSpec((B, tq, MLA_HEADS * MLA_V), lambda p, qi, ki: (0, qi[p], 0)),
            scratch_shapes=_flash_scratch(B * MLA_HEADS, MLA_V, tq) + [pltpu.SMEM((1,), jnp.int32)]),
        out_shape=jax.ShapeDtypeStruct((B, S, MLA_HEADS * MLA_V), BF16),
        compiler_params=_cparams(("arbitrary",)),
        name="mla_attn",
    )(qi, ki, qm, km, vmt).reshape(B * S, MLA_HEADS * MLA_V)


def _slc_kernel(qi_ref, ki_ref, q_ref, bias_ref, k_ref, vt_ref, o_ref, qa_ref, m_ref, acc_ref,
                slot_ref, *, tq):
    i, j = qi_ref[pl.program_id(0)], ki_ref[pl.program_id(0)]
    nb = q_ref.shape[0]

    @pl.when(j == i)
    def _():
        for bb in range(nb):
            for hd in range(NSA_HEADS):
                qa_ref[bb * NSA_HEADS + hd] = q_ref[bb, hd]

    rows = pl.ds(pl.multiple_of(j * SEL_SLOTS, SEL_SLOTS), SEL_SLOTS)
    for bb in range(nb):
        for g in range(NSA_GROUPS):
            tile_bias = bias_ref[bb, g, rows, :]
            for hd in range(g * NSA_HPG, (g + 1) * NSA_HPG):
                qa_ref[bb * NSA_HEADS + hd, NSA_DIM:NSA_DIM + SEL_SLOTS, :] = tile_bias

    def kqv(v):
        bb, hd = _split_virtual(v, NSA_HEADS)
        g = hd // NSA_HPG if isinstance(hd, int) else lax.div(hd, NSA_HPG)
        return k_ref[bb, g], qa_ref[v], vt_ref[bb, g]

    _causal_flash_step(i, j, nb * NSA_HEADS, kqv, tq, o_ref, m_ref, acc_ref, slot_ref)


def _slc_attention(nq, bias, ksa, vst, B, S, tq):
    nqt = S // tq
    nrow = bias.shape[2]
    assert tq // SEL_LEN == SEL_SLOTS // 2
    kern = functools.partial(_slc_kernel, tq=tq)
    qi, ki = _causal_pairs(nqt)
    qmap = lambda p, qi, ki: (0, 0, 0, qi[p])
    return pl.pallas_call(
        kern,
        grid_spec=pltpu.PrefetchScalarGridSpec(
            num_scalar_prefetch=2,
            grid=(qi.shape[0],),
            in_specs=[pl.BlockSpec((B, NSA_HEADS, LANES, tq), qmap),
                      pl.BlockSpec((B, NSA_GROUPS, nrow, tq), qmap),
                      pl.BlockSpec((B, NSA_GROUPS, tq, LANES), lambda p, qi, ki: (0, 0, ki[p], 0)),
                      pl.BlockSpec((B, NSA_GROUPS, NSA_DIM + SUM_ROWS, tq), lambda p, qi, ki: (0, 0, 0, ki[p]))],
            out_specs=pl.BlockSpec((B, tq, NSA_HEADS * NSA_DIM), lambda p, qi, ki: (0, qi[p], 0)),
            scratch_shapes=[pltpu.VMEM((B * NSA_HEADS, LANES, tq), BF16)]
                           + _flash_scratch(B * NSA_HEADS, NSA_DIM, tq)
                           + [pltpu.SMEM((1,), jnp.int32)]),
        out_shape=jax.ShapeDtypeStruct((B, S, NSA_HEADS * NSA_DIM), BF16),
        compiler_params=_cparams(("arbitrary",)),
        name="slc_attn",
    )(qi, ki, nq, bias, ksa, vst).reshape(B * S, NSA_HEADS * NSA_DIM)


def _win_kernel(q_ref, k_ref, vt_ref, o_ref, m_ref, acc_ref, *, tq):
    i, j = pl.program_id(0), pl.program_id(1)

    def kqv(v):
        bb, hd = _split_virtual(v, NSA_HEADS)
        g = hd // NSA_HPG if isinstance(hd, int) else lax.div(hd, NSA_HPG)
        return k_ref[bb, g], q_ref[bb, hd][:NSA_DIM, :], vt_ref[bb, g]

    nv = q_ref.shape[0] * NSA_HEADS

    @pl.when(j == 0)
    def _():
        _flash_heads(nv, kqv, _tile_parts("diag", tq), None, (m_ref.at[0], acc_ref.at[0]))

    @pl.when((j == 1) & (i > 0))
    def _():
        _lazy_flash_step(nv, kqv, _tile_parts("prev", tq), m_ref, acc_ref, 0)

    @pl.when(j == 1)
    def _():
        _flash_finalize(o_ref, acc_ref.at[jnp.where(i > 0, 1, 0)])


def _win_attention(nq, kw, vwt, B, S, tq):
    nqt = S // tq
    kern = functools.partial(_win_kernel, tq=tq)
    kidx = lambda i, j: jnp.maximum(i - j, 0)
    return pl.pallas_call(
        kern,
        grid=(nqt, 2),
        in_specs=[pl.BlockSpec((B, NSA_HEADS, LANES, tq), lambda i, j: (0, 0, 0, i)),
                  pl.BlockSpec((B, NSA_GROUPS, tq, NSA_DIM), lambda i, j: (0, 0, kidx(i, j), 0)),
                  pl.BlockSpec((B, NSA_GROUPS, NSA_DIM + SUM_ROWS, tq), lambda i, j: (0, 0, 0, kidx(i, j)))],
        out_specs=pl.BlockSpec((B, tq, NSA_HEADS * NSA_DIM), lambda i, j: (0, i, 0)),
        out_shape=jax.ShapeDtypeStruct((B, S, NSA_HEADS * NSA_DIM), BF16),
        scratch_shapes=_flash_scratch(B * NSA_HEADS, NSA_DIM, tq),
        compiler_params=_cparams(("parallel", "arbitrary")),
        name="win_attn",
    )(nq, kw, vwt).reshape(B * S, NSA_HEADS * NSA_DIM)


def _causal_conv(u, carry, w_ref):
    assert CONV_WIDTH == 3
    row =lax.broadcasted_iota(jnp.int32, u.shape, 0)
    c1 = carry[SUBLANES - 1:SUBLANES]
    c2 = carry[SUBLANES - 2:SUBLANES - 1]
    u1 = jnp.where(row == 0, c1, pltpu.roll(u, 1, axis=0))
    u2 = jnp.where(row == 0, c2, jnp.where(row == 1, c1, pltpu.roll(u, 2, axis=0)))
    return w_ref[0:1] * u2 + w_ref[1:2] * u1 + w_ref[2:3] * u


def _merge_kernel(x_ref, g_ref, wgc_ref, cw_ref, ym_ref, oc_ref, os_ref, ow_ref, gl_ref,
                  wbc_ref, wbm_ref, wbn_ref, wo_ref, gp_ref, out_ref, carry_ref, *, tiles_per_seq):
    D = x_ref.shape[1]
    DC = wbc_ref.shape[0]

    @pl.when(pl.program_id(0) % tiles_per_seq == 0)
    def _():
        carry_ref[...] = jnp.zeros(carry_ref.shape, F32)

    x = x_ref[...]
    h = _rms(x, g_ref[...]).astype(BF16)
    p = _dot(h, wgc_ref[...])
    u = p[:, 3 * D + DC:3 * D + 2 * DC] * p[:, 3 * D + 2 * DC:3 * D + 3 * DC]
    y_conv = p[:, 3 * D:3 * D + DC] * _causal_conv(u, carry_ref[...], cw_ref)
    carry_ref[...] = u[u.shape[0] - SUBLANES:]
    gsig = jax.nn.sigmoid(gl_ref[...])
    first = lax.broadcasted_iota(jnp.int32, (x.shape[0], LANES), 1) < NSA_DIM

    def branch_gate(c):
        col = lambda hd: jnp.broadcast_to(gsig[:, 3 * hd + c:3 * hd + c + 1], (x.shape[0], LANES))
        return jnp.concatenate([jnp.where(first, col(hd), col(hd + 1)) for hd in range(0, NSA_HEADS, 2)],
                               axis=1)

    y_nsa = (branch_gate(0) * oc_ref[...].astype(F32) + branch_gate(1) * os_ref[...].astype(F32)
             + branch_gate(2) * ow_ref[...].astype(F32))
    merged = (jax.nn.sigmoid(p[:, 0:D]) * _dot(y_conv.astype(BF16), wbc_ref[...])
              + jax.nn.sigmoid(p[:, D:2 * D]) * _dot(ym_ref[...], wbm_ref[...])
              + jax.nn.sigmoid(p[:, 2 * D:3 * D]) * _dot(y_nsa.astype(BF16), wbn_ref[...]))
    out_ref[...] = x + _rms(_dot(merged.astype(BF16), wo_ref[...]), gp_ref[...])


def _merge(x2, g_pre, w_gc, conv_w, y_mla, o_cmp, o_slc, o_win, gl, wbc, wbm, wbn, wo, g_post,
           S, tm):
    T, D = x2.shape
    HD = y_mla.shape[1]
    row = lambda w: pl.BlockSpec((tm, w), lambda i: (i, 0))
    kern = functools.partial(_merge_kernel, tiles_per_seq=S // tm)
    return pl.pallas_call(
        kern,
        grid=(T // tm,),
        in_specs=[row(D), _const_spec((1, D)), _const_spec(w_gc.shape), _const_spec(conv_w.shape),
                  row(HD), row(HD), row(HD), row(HD), row(LANES),
                  _const_spec(wbc.shape), _const_spec(wbm.shape), _const_spec(wbn.shape),
                  _const_spec(wo.shape), _const_spec((1, D))],
        out_specs=row(D),
        out_shape=jax.ShapeDtypeStruct((T, D), F32),
        scratch_shapes=[pltpu.VMEM((SUBLANES, wbc.shape[0]), F32)],
        compiler_params=_cparams(("arbitrary",)),
        name="merge",
    )(x2, g_pre, w_gc, conv_w, y_mla, o_cmp, o_slc, o_win, gl, wbc, wbm, wbn, wo, g_post)


def _ffn_kernel(x_ref, g_ref, wup_ref, cw_ref, cb_ref, wdn_ref, gp_ref, out_ref, carry_ref,
                *, tiles_per_seq, chunk):
    F = wdn_ref.shape[0]

    @pl.when(pl.program_id(0) % tiles_per_seq == 0)
    def _():
        carry_ref[...] = jnp.zeros(carry_ref.shape, F32)

    x = x_ref[...]
    h = _rms(x, g_ref[...]).astype(BF16)
    acc = None
    for c0 in range(0, F, chunk):
        a = _dot(h, wup_ref[:, c0:c0 + chunk])
        b = _dot(h, wup_ref[:, F + c0:F + c0 + chunk])
        ac = _causal_conv(a, carry_ref[:, c0:c0 + chunk], cw_ref.at[:, c0:c0 + chunk])
        ac = ac + cb_ref[:, c0:c0 + chunk]
        carry_ref[:, c0:c0 + chunk] = a[a.shape[0] - SUBLANES:]
        d = _dot((jax.nn.gelu(ac) * b).astype(BF16), wdn_ref[c0:c0 + chunk, :])
        acc = d if acc is None else acc + d
    out_ref[...] = x + _rms(acc, gp_ref[...])


def _ffn(x2, g_pre, w_up, conv_w, conv_b, w_dn, g_post, S, tm, chunk):
    T, D = x2.shape
    F = w_dn.shape[0]
    row = pl.BlockSpec((tm, D), lambda i: (i, 0))
    kern = functools.partial(_ffn_kernel, tiles_per_seq=S // tm, chunk=chunk)
    return pl.pallas_call(
        kern,
        grid=(T // tm,),
        in_specs=[row, _const_spec((1, D)), _const_spec(w_up.shape), _const_spec(conv_w.shape),
                  _const_spec((1, F)), _const_spec(w_dn.shape), _const_spec((1, D))],
        out_specs=row,
        out_shape=jax.ShapeDtypeStruct((T, D), F32),
        scratch_shapes=[pltpu.VMEM((SUBLANES, F), F32)],
        compiler_params=_cparams(("arbitrary",)),
        name="ffn",
    )(x2, g_pre, w_up, conv_w, conv_b, w_dn, g_post)


def _overlap_matrix_t(S):
    nch = S // CMP_STRIDE
    n_cmp = (S - CMP_LEN) // CMP_STRIDE + 1
    cs = np.arange(nch)[None, :] * CMP_STRIDE
    ss = np.arange(S // SEL_LEN)[:, None] * SEL_LEN
    ov = (cs <= ss + SEL_LEN - 1) & (cs + CMP_LEN - 1 >= ss) & (np.arange(nch)[None, :] < n_cmp)
    return ov.astype(np.float32)


def _pad_rows(w, rows):
    return jnp.concatenate([w, jnp.zeros((rows - w.shape[0],) + w.shape[1:], w.dtype)], axis=0)


def kernel(x, positions, norm_mix_pre, norm_mix_post, w_in, conv_w, mla_q_norm, mla_w_uq, mla_kv_norm, mla_w_ukv, nsa_cmp_pos_k, nsa_cmp_pos_v, nsa_cmp_w_k, nsa_cmp_w_v, w_branch_conv, w_branch_mla, w_branch_nsa, w_out, norm_ffn_pre, norm_ffn_post, ffn_w_up, ffn_conv_w, ffn_conv_b, ffn_w_down):
    B, S, D = x.shape
    T = B * S
    depth = w_in.shape[0]
    DC = conv_w.shape[2]
    F = ffn_w_down.shape[1]
    QL = mla_q_norm.shape[1]
    KL = mla_kv_norm.shape[1]
    tq = 512
    assert S % tq == 0 and WINDOW == tq and N_SEL <= S // SEL_LEN <= LANES
    tm_proj, tm_merge, tm_ffn, ffn_chunk = 512, 512, 512, 1408
    assert F % ffn_chunk == 0 and ffn_chunk % LANES == 0

    x2 = x.reshape(T, D)
    pos2 = positions.reshape(1, T)
    rc = jnp.asarray(_rope_consts())
    ovt = jnp.asarray(_overlap_matrix_t(S), BF16)
    half = CMP_LEN // 2 * NSA_DIM
    o_att = 3 * D + 3 * DC

    for l in range(depth):
        w = w_in[l]
        o_kr = o_att + QL + KL
        z = lambda n: jnp.zeros((D, n), F32)
        w_att = jnp.concatenate(
            [w[:, o_att:o_kr], z(MLA_NOPE), w[:, o_kr:o_kr + MLA_ROPE], z(LANES - MLA_NOPE - MLA_ROPE),
             w[:, o_kr + MLA_ROPE:o_kr + MLA_ROPE + NSA_HEADS * NSA_DIM + 6 * NSA_GROUPS * NSA_DIM],
             w[:, w.shape[1] - 3 * NSA_HEADS:], z(LANES - 3 * NSA_HEADS)], axis=1).astype(BF16)
        dq = MLA_NOPE + MLA_ROPE
        w_uq = jnp.pad(mla_w_uq[l].reshape(QL, MLA_HEADS, dq),
                       ((0, 0), (0, 0), (0, LANES - dq))).reshape(QL, MLA_HEADS * LANES).astype(BF16)
        w_v_t = mla_w_ukv[l].reshape(KL, MLA_HEADS, MLA_NOPE + MLA_V)[:, :, MLA_NOPE:].reshape(
            KL, MLA_HEADS * MLA_V).T.astype(BF16)
        (qm, km, vmt, nq, kc_in, vc_in, ksa, vst, kw, vwt, gl) = _project(
            x2, pos2, norm_mix_pre[l][None], w_att, mla_q_norm[l][None], w_uq.T,
            mla_kv_norm[l][None], mla_w_ukv[l].astype(BF16), w_v_t, rc, S, tm_proj)

        wv = nsa_cmp_w_v[l].reshape(2, half, NSA_DIM)
        zv = jnp.zeros_like(wv)
        wv_pad = jnp.stack([jnp.concatenate([wv, zv], axis=2), jnp.concatenate([zv, wv], axis=2)])
        kc, vct = _compress(kc_in, vc_in,
                            nsa_cmp_pos_k[l].reshape(2, half), nsa_cmp_pos_v[l].reshape(2, half),
                            nsa_cmp_w_k[l].reshape(2, half, NSA_DIM).astype(BF16),
                            wv_pad.astype(BF16), B)

        y_mla = _mla_attention(qm, km, vmt, B, S, tq)
        o_cmp, bias = _cmp_select(nq, kc, vct, ovt, B, S, tq)
        o_slc = _slc_attention(nq, bias, ksa, vst, B, S, tq)
        o_win = _win_attention(nq, kw, vwt, B, S, tq)

        x2 = _merge(x2, norm_mix_pre[l][None], w[:, :o_att].astype(BF16),
                    _pad_rows(conv_w[l], SUBLANES), y_mla, o_cmp, o_slc, o_win, gl,
                    w_branch_conv[l].astype(BF16), w_branch_mla[l].astype(BF16),
                    w_branch_nsa[l].astype(BF16), w_out[l].astype(BF16), norm_mix_post[l][None],
                    S, tm_merge)
        x2 = _ffn(x2, norm_ffn_pre[l][None], ffn_w_up[l].astype(BF16),
                  _pad_rows(ffn_conv_w[l], SUBLANES), ffn_conv_b[l][None],
                  ffn_w_down[l].astype(BF16), norm_ffn_post[l][None], S, tm_ffn, ffn_chunk)
    return x2.reshape(B, S, D)
```

```python
import functools
import math

import numpy as np
import jax
import jax.numpy as jnp
from jax import lax
from jax.experimental import pallas as pl
from jax.experimental.pallas import tpu as pltpu

F32 = jnp.float32
BF16 = jnp.bfloat16

ROPE_THETA = 500000.0
RMS_EPS = 1e-6
CONV_WIDTH = 3
MLA_HEADS = 8
MLA_NOPE = 64
MLA_ROPE = 32
MLA_V = 64
NSA_HEADS = 8
NSA_GROUPS = 2
NSA_HPG = NSA_HEADS // NSA_GROUPS
NSA_DIM = 64
NSA_ROT = NSA_DIM // 4
CMP_LEN = 32
CMP_STRIDE = 16
SEL_LEN = 64
N_SEL = 16
WINDOW = 512

LANES = 128
SUBLANES = 8
VMEM_LIMIT = 56 * 1024 * 1024
MASKED = -1e30
M_FLOOR = -1e20
SEL_BIAS = -2.0 ** 100
LAZY_SPAN = 64.0
LOG2E = math.log2(math.e)
SEL_SLOTS = 16
ROPE_ROWS = 32
SUM_ROWS = 16


def _cparams(sem):
    return pltpu.CompilerParams(dimension_semantics=sem, vmem_limit_bytes=VMEM_LIMIT)


def _const_spec(shape):
    nd = len(shape)
    return pl.BlockSpec(shape, lambda *_: (0,) * nd, pipeline_mode=pl.Buffered(1))


def _rms(x, g):
    return x * lax.rsqrt(jnp.mean(x * x, axis=-1, keepdims=True) + RMS_EPS) * g


def _dot(a, b):
    return jnp.dot(a, b, preferred_element_type=F32)


def _split_bf16(a, terms):
    pieces, rem = [], a
    for _ in range(terms):
        piece = rem.astype(BF16)
        rem = rem - piece.astype(F32)
        pieces.append(piece)
    return pieces


def _rope(x, cos, sin_up, sin_dn, half):
    return (x * cos + pltpu.roll(x, half, axis=1) * sin_up
            + pltpu.roll(x, LANES - half, axis=1) * sin_dn)


def _lanes(table, lane, base, moves):
    out = base
    for dst, width, src in moves:
        rolled = pltpu.roll(table, (dst - src) % LANES, axis=1)
        out = jnp.where((lane >= dst) & (lane < dst + width), rolled, out)
    return out


def _proj_kernel(x_ref, pos_ref, g_ref, watt_ref, qg_ref, wuqt_ref, kvg_ref, wukv_ref, wvt_ref, rc_ref,
                 qm_ref, km_ref, vmt_ref, nq_ref, kc_ref, vc_ref, ksa_ref, vst_ref, kw_ref, vwt_ref,
                 gl_ref, kstage_ref, vstage_ref, *, mla_scale, nsa_scale, tiles_per_seq):
    tm = x_ref.shape[0]
    hm, hn = MLA_ROPE // 2, NSA_ROT // 2
    x = x_ref[...]
    h = _rms(x, g_ref[...]).astype(BF16)
    p = _dot(h, watt_ref[...])
    ql, kl = qg_ref.shape[1], kvg_ref.shape[1]
    c_kr, c_nq, c_kv, c_gl = _att_columns(ql, kl)
    lane = lax.broadcasted_iota(jnp.int32, (tm, LANES), 1)
    low = lane < NSA_DIM
    ang_t = rc_ref[:, 0:1] * pos_ref[...].astype(F32)
    cos_t, sin_t = jnp.cos(ang_t), jnp.sin(ang_t)
    cs = jnp.concatenate([cos_t, sin_t, jnp.zeros((LANES - 2 * ROPE_ROWS, tm), F32)], axis=0).T
    cos_m = _lanes(cs, lane, 1.0, [(MLA_NOPE, hm, 0), (MLA_NOPE + hm, hm, 0)])
    sup_m = _lanes(cs, lane, 0.0, [(MLA_NOPE + hm, hm, ROPE_ROWS)])
    sdn_m = -_lanes(cs, lane, 0.0, [(MLA_NOPE, hm, ROPE_ROWS)])
    cos_n = _lanes(cs, lane, 1.0, [(0, hn, hm), (hn, hn, hm), (NSA_DIM, hn, hm), (NSA_DIM + hn, hn, hm)])
    sup_n = _lanes(cs, lane, 0.0, [(hn, hn, ROPE_ROWS + hm), (NSA_DIM + hn, hn, ROPE_ROWS + hm)])
    sdn_n = -_lanes(cs, lane, 0.0, [(0, hn, ROPE_ROWS + hm), (NSA_DIM, hn, ROPE_ROWS + hm)])
    ones_rows = jnp.where(lax.broadcasted_iota(jnp.int32, (SUM_ROWS, tm), 0) == 0, 1.0, 0.0)

    def with_sum_rows(vt):
        return jnp.concatenate([vt, ones_rows], axis=0).astype(BF16)

    def halves(slab):
        return (jnp.where(low, slab, 0.0), jnp.where(low, pltpu.roll(slab, NSA_DIM, axis=1), 0.0))

    def rope_rows(x1, x2, c, s_):
        return x1 * c - x2 * s_, x2 * c + x1 * s_

    cqn_t = _rms(p[:, 0:ql], qg_ref[...]).T.astype(BF16)
    qt = _dot(wuqt_ref[...], cqn_t)
    for hd in range(MLA_HEADS):
        r0 = hd * LANES
        y1, y2 = rope_rows(qt[r0 + MLA_NOPE:r0 + MLA_NOPE + hm], qt[r0 + MLA_NOPE + hm:r0 + MLA_NOPE + 2 * hm],
                           cos_t[0:hm], sin_t[0:hm])
        slab_t = jnp.concatenate([qt[r0:r0 + MLA_NOPE], y1, y2, qt[r0 + MLA_NOPE + 2 * hm:r0 + LANES]], axis=0)
        qm_ref[0, hd] = (slab_t * mla_scale).astype(BF16)
    ckvn = _rms(p[:, ql:ql + kl], kvg_ref[...])
    kv = _dot(ckvn.astype(BF16), wukv_ref[...])
    kr = _rope(p[:, c_kr:c_kr + LANES], cos_m, sup_m, sdn_m, hm)
    vt = _dot(wvt_ref[...], ckvn.T.astype(BF16))
    for hd in range(MLA_HEADS):
        km_ref[0, hd] = jnp.where(lane < MLA_NOPE, kv[:, hd * LANES:(hd + 1) * LANES], kr).astype(BF16)
        vmt_ref[0, hd] = with_sum_rows(vt[hd * MLA_V:(hd + 1) * MLA_V])
    zpad = jnp.zeros((LANES - NSA_DIM, tm), BF16)
    for pr in range(NSA_HEADS // 2):
        slab_t = p[:, c_nq + pr * LANES:c_nq + (pr + 1) * LANES].T
        for half in range(2):
            r0 = half * NSA_DIM
            y1, y2 = rope_rows(slab_t[r0:r0 + hn], slab_t[r0 + hn:r0 + 2 * hn],
                               cos_t[hm:hm + hn], sin_t[hm:hm + hn])
            head_t = jnp.concatenate([y1, y2, slab_t[r0 + 2 * hn:r0 + NSA_DIM]], axis=0) * nsa_scale
            nq_ref[0, 2 * pr + half, 0:NSA_DIM, :] = head_t.astype(BF16)
            nq_ref[0, 2 * pr + half, NSA_DIM:LANES, :] = zpad
    nsa = lambda idx: p[:, c_kv + idx * LANES:c_kv + (idx + 1) * LANES]
    def store_chunks(slab, stage_ref, out_ref):
        stage_ref[...] = slab
        nrow = tm // CMP_STRIDE
        first = lax.broadcasted_iota(jnp.int32, (nrow, LANES), 1) < NSA_DIM
        for r in range(0, CMP_STRIDE, 2):
            a = stage_ref[pl.ds(r, nrow, stride=CMP_STRIDE), :]
            b = stage_ref[pl.ds(r + 1, nrow, stride=CMP_STRIDE), :]
            cols = slice(r * NSA_DIM, (r + 2) * NSA_DIM)
            out_ref[0, :, cols] = jnp.where(first, a, pltpu.roll(b, NSA_DIM, axis=1)).astype(BF16)
            out_ref[1, :, cols] = jnp.where(first, pltpu.roll(a, NSA_DIM, axis=1), b).astype(BF16)

    store_chunks(_rope(nsa(0), cos_n, sup_n, sdn_n, hn), kstage_ref, kc_ref)
    store_chunks(nsa(1), vstage_ref, vc_ref)
    t_seq = (pl.program_id(0) % tiles_per_seq) * tm + lax.broadcasted_iota(jnp.int32, (tm, LANES), 0)
    slot = NSA_DIM + (lax.shift_right_logical(t_seq, int(math.log2(SEL_LEN))) & (SEL_SLOTS // 2 - 1))
    onehot = jnp.where(lane == slot, 1.0, 0.0)
    for g, kg in enumerate(halves(_rope(nsa(2), cos_n, sup_n, sdn_n, hn))):
        ksa_ref[0, g] = (kg + onehot).astype(BF16)
    vst = nsa(3).T
    vst_ref[0, 0] = with_sum_rows(vst[:NSA_DIM])
    vst_ref[0, 1] = with_sum_rows(vst[NSA_DIM:])
    kws = _rope(nsa(4), cos_n, sup_n, sdn_n, hn)
    kw_ref[0, 0] = kws[:, :NSA_DIM].astype(BF16)
    kw_ref[0, 1] = pltpu.roll(kws, NSA_DIM, axis=1)[:, :NSA_DIM].astype(BF16)
    vwt = nsa(5).T
    vwt_ref[0, 0] = with_sum_rows(vwt[:NSA_DIM])
    vwt_ref[0, 1] = with_sum_rows(vwt[NSA_DIM:])
    gl_ref[...] = p[:, c_gl:c_gl + LANES]


def _att_columns(ql, kl):
    c_kr = ql + kl
    c_nq = c_kr + LANES
    c_kv = c_nq + NSA_HEADS * NSA_DIM
    return c_kr, c_nq, c_kv, c_kr


def _rope_consts():
    inv_freq = lambda half: ROPE_THETA ** (-jnp.arange(half, dtype=F32) / half)
    hm, hn = MLA_ROPE // 2, NSA_ROT // 2
    col = jnp.concatenate([inv_freq(hm), inv_freq(hn), jnp.zeros((ROPE_ROWS - hm - hn,), F32)])
    return jnp.broadcast_to(col[:, None], (ROPE_ROWS, LANES))


def _project(x2, pos2, g_pre, w_att, q_g, w_uq_t, kv_g, w_ukv, w_v_t, rc, S, tm):
    T, D = x2.shape
    n_att = w_att.shape[1]
    tps = S // tm
    B = T // S
    brows = lambda n, d: (jax.ShapeDtypeStruct((B, n, S, d), BF16),
                          pl.BlockSpec((1, n, tm, d), lambda i: (i // tps, 0, i % tps, 0)))
    bcols = lambda n, d: (jax.ShapeDtypeStruct((B, n, d, S), BF16),
                          pl.BlockSpec((1, n, d, tm), lambda i: (i // tps, 0, 0, i % tps)))
    chunked = (jax.ShapeDtypeStruct((NSA_GROUPS, T // CMP_STRIDE, CMP_STRIDE * NSA_DIM), BF16),
               pl.BlockSpec((NSA_GROUPS, tm // CMP_STRIDE, CMP_STRIDE * NSA_DIM), lambda i: (0, i, 0)))
    outs = [bcols(MLA_HEADS, LANES), brows(MLA_HEADS, LANES), bcols(MLA_HEADS, MLA_V + SUM_ROWS),
            bcols(NSA_HEADS, LANES), chunked, chunked,
            brows(NSA_GROUPS, LANES), bcols(NSA_GROUPS, NSA_DIM + SUM_ROWS), brows(NSA_GROUPS, NSA_DIM),
            bcols(NSA_GROUPS, NSA_DIM + SUM_ROWS),
            (jax.ShapeDtypeStruct((T, LANES), F32), pl.BlockSpec((tm, LANES), lambda i: (i, 0)))]
    kern = functools.partial(_proj_kernel,
                             mla_scale=float((MLA_NOPE + MLA_ROPE) ** -0.5 * LOG2E),
                             nsa_scale=float(NSA_DIM ** -0.5 * LOG2E),
                             tiles_per_seq=S // tm)
    return pl.pallas_call(
        kern,
        grid=(T // tm,),
        in_specs=[
            pl.BlockSpec((tm, D), lambda i: (i, 0)),
            pl.BlockSpec((1, tm), lambda i: (0, i)),
            _const_spec((1, D)),
            _const_spec((D, n_att)),
            _const_spec((1, q_g.shape[1])),
            _const_spec(w_uq_t.shape),
            _const_spec((1, kv_g.shape[1])),
            _const_spec(w_ukv.shape),
            _const_spec(w_v_t.shape),
            _const_spec((ROPE_ROWS, LANES)),
        ],
        out_specs=[o[1] for o in outs],
        out_shape=[o[0] for o in outs],
        scratch_shapes=[pltpu.VMEM((tm, LANES), F32), pltpu.VMEM((tm, LANES), F32)],
        compiler_params=_cparams(("parallel",)),
        name="proj",
    )(x2, pos2, g_pre, w_att, q_g, w_uq_t, kv_g, w_ukv, w_v_t, rc)


def _compress_kernel(xk_ref, xv_ref, pk_ref, pv_ref, wk_ref, wv_ref, ok_ref, ovt_ref):
    n = xk_ref.shape[1]
    vboth = None
    for g in range(NSA_GROUPS):
        xk = xk_ref[g].astype(F32)
        lo = _dot((xk + pk_ref[0:1]).astype(BF16), wk_ref[0])
        hi = _dot((xk + pk_ref[1:2]).astype(BF16), wk_ref[1])
        ok_ref[g] = (lo + pltpu.roll(hi, n - 1, axis=0)).astype(BF16)
        xv = xv_ref[g].astype(F32)
        lo = _dot((xv + pv_ref[0:1]).astype(BF16), wv_ref[g, 0])
        hi = _dot((xv + pv_ref[1:2]).astype(BF16), wv_ref[g, 1])
        vg = lo + pltpu.roll(hi, n - 1, axis=0)
        vboth = vg if vboth is None else vboth + vg
    ovt_ref[0] = vboth.T.astype(BF16)


def _compress(xk, xv, pk, pv, wk, wv, B):
    G, TC, CW = xk.shape
    nch = TC // B
    xspec = pl.BlockSpec((G, nch, CW), lambda b: (0, b, 0))
    return pl.pallas_call(
        _compress_kernel,
        grid=(B,),
        in_specs=[xspec, xspec, _const_spec(pk.shape), _const_spec(pv.shape),
                  _const_spec(wk.shape), _const_spec(wv.shape)],
        out_specs=[pl.BlockSpec((G, nch, NSA_DIM), lambda b: (0, b, 0)),
                   pl.BlockSpec((1, G * NSA_DIM, nch), lambda b: (b, 0, 0))],
        out_shape=[jax.ShapeDtypeStruct((G, TC, NSA_DIM), BF16),
                   jax.ShapeDtypeStruct((B, G * NSA_DIM, nch), BF16)],
        compiler_params=_cparams(("parallel",)),
        name="compress",
    )(xk, xv, pk, pv, wk, wv)


def _cmp_select_kernel(q_ref, kc_ref, vct_ref, ovt_ref, o_ref, bias_ref, *, tq, n_pick, nqt, n_var):
    i = pl.program_id(1)
    q0 = i * tq
    nch_all = kc_ref.shape[1]
    nblk_all = ovt_ref.shape[0]
    tiles_per_var = nqt // n_var

    def variant(nch, nblk):
        for c0 in range(0, tq, tq // 2):
            half(nch, nblk, c0, tq // 2)

    def half(nch, nblk, c0, w):
        cols = slice(c0, c0 + w)
        t_c = q0 + c0 + lax.broadcasted_iota(jnp.int32, (nch, w), 1)
        n_c = lax.broadcasted_iota(jnp.int32, (nch, w), 0)
        cmask = (n_c * CMP_STRIDE + (CMP_LEN - 1)) <= t_c
        t_b = q0 + c0 + lax.broadcasted_iota(jnp.int32, (nblk, w), 1)
        blk = lax.broadcasted_iota(jnp.int32, (nblk, w), 0)
        blk_f = blk.astype(F32)
        cur = lax.shift_right_logical(t_b, int(math.log2(SEL_LEN)))
        forced = (blk == 0) | (blk == cur) | (blk == cur - 1)
        causal = blk * SEL_LEN <= t_b
        outs = []
        for g in range(NSA_GROUPS):
            psum = None
            scores = [_dot(kc_ref[g, :nch, :], q_ref[0, g * NSA_HPG + j][:NSA_DIM, cols])
                      for j in range(NSA_HPG)]
            for j in range(NSA_HPG):
                s = jnp.where(cmask, scores[j], MASKED)
                m = jnp.maximum(jnp.max(s, axis=0, keepdims=True), M_FLOOR)
                p = jnp.exp2(s - m)
                l = jnp.sum(p, axis=0, keepdims=True)
                pn = p * jnp.where(l > 0.0, 1.0 / l, 0.0)
                outs.append(_dot(vct_ref[0, g * NSA_DIM:(g + 1) * NSA_DIM, :nch], pn.astype(BF16)))
                psum = pn if psum is None else psum + pn
            imp = None
            for piece in _split_bf16(psum, 3):
                d = _dot(ovt_ref[:nblk, :nch], piece)
                imp = d if imp is None else imp + d
            v0 = jnp.where(forced | jnp.logical_not(causal), -1.0, imp)
            v = v0
            for _ in range(n_pick):
                mx = jnp.max(v, axis=0, keepdims=True)
                first = jnp.min(jnp.where(v == mx, blk_f, float(nblk)), axis=0, keepdims=True)
                v = jnp.where(blk_f == first, -2.0, v)
            sel = forced | ((v == -2.0) & (v0 >= 0.0))
            bias = jnp.where(sel, 0.0, SEL_BIAS)
            per_tile = SEL_SLOTS // 2
            pad = jnp.zeros((SEL_SLOTS - per_tile, w), F32)
            pieces = []
            for kt in range(nblk_all // per_tile):
                if kt * per_tile < nblk:
                    pieces += [bias[kt * per_tile:(kt + 1) * per_tile], pad]
                else:
                    pieces += [jnp.full((per_tile, w), SEL_BIAS, F32), pad]
            bias_ref[0, g, :, cols] = jnp.concatenate(pieces, axis=0).astype(BF16)
        o_ref[cols, :] = jnp.concatenate(outs, axis=0).T.astype(BF16)

    for k in range(n_var):
        @pl.when((i >= k * tiles_per_var) & (i < (k + 1) * tiles_per_var))
        def _():
            variant(nch_all * (k + 1) // n_var, nblk_all * (k + 1) // n_var)


def _cmp_select(nq, kc, vct, ovt, B, S, tq):
    T = B * S
    nch = kc.shape[1] // B
    nblk = ovt.shape[0]
    nqt = S // tq
    n_var = 4 if nqt % 4 == 0 and (nch // 4) % LANES == 0 and (nblk // 4) % SUBLANES == 0 else 1
    kern = functools.partial(_cmp_select_kernel, tq=tq, n_pick=N_SEL - 3, nqt=nqt, n_var=n_var)
    return pl.pallas_call(
        kern,
        grid=(B, nqt),
        in_specs=[
            pl.BlockSpec((1, NSA_HEADS, LANES, tq), lambda b, i: (b, 0, 0, i)),
            pl.BlockSpec((NSA_GROUPS, nch, NSA_DIM), lambda b, i: (0, b, 0)),
            pl.BlockSpec((1, NSA_GROUPS * NSA_DIM, nch), lambda b, i: (b, 0, 0)),
            _const_spec(ovt.shape),
        ],
        out_specs=[pl.BlockSpec((tq, NSA_HEADS * NSA_DIM), lambda b, i: (b * nqt + i, 0)),
                   pl.BlockSpec((1, NSA_GROUPS, 2 * nblk, tq), lambda b, i: (b, 0, 0, i))],
        out_shape=[jax.ShapeDtypeStruct((T, NSA_HEADS * NSA_DIM), BF16),
                   jax.ShapeDtypeStruct((B, NSA_GROUPS, 2 * nblk, S), BF16)],
        compiler_params=_cparams(("parallel", "parallel")),
        name="cmp_select",
    )(nq, kc, vct, ovt)


def _tile_parts(kind, tq):
    hq = tq // 2
    if kind == "full":
        return [(slice(0, tq), slice(0, hq), None), (slice(0, tq), slice(hq, tq), None)]
    if kind == "full_split":
        keys = (slice(0, hq), slice(hq, tq))
        return [(keys, slice(0, hq), None), (keys, slice(hq, tq), None)]

    def mask(rows, cols, below):
        key = rows.start + lax.broadcasted_iota(jnp.int32, (rows.stop - rows.start, hq), 0)
        qry = cols.start + lax.broadcasted_iota(jnp.int32, (rows.stop - rows.start, hq), 1)
        return (key <= qry) if below else (key > qry)

    lo, hi, all_ = slice(0, hq), slice(hq, tq), slice(0, tq)
    if kind == "diag":
        return [(lo, lo, mask(lo, lo, True)), (all_, hi, mask(all_, hi, True))]
    return [(all_, lo, mask(all_, lo, False)), (hi, hi, mask(hi, hi, False))]


def _flash_heads(nheads, kqv, parts, src, dst, lazy=False, unroll=True):
    def row_slices(part):
        return part[0] if isinstance(part[0], tuple) else (part[0],)

    def scores(hd, part, rows):
        k, qt, _ = kqv(hd)
        return _dot(k[rows], qt[:, part[1]])

    def update(hd, part, ss):
        _, cols, mask = part
        rows_l = row_slices(part)
        if mask is not None:
            ss = [jnp.where(mask, ss[0], MASKED)]
        tmax = functools.reduce(jnp.maximum, [jnp.max(s, axis=0, keepdims=True) for s in ss])
        vt = kqv(hd)[2]
        pv = lambda ref: functools.reduce(
            lambda a, b: a + b, [_dot(vt[:, r], jnp.exp2(s - ref).astype(BF16)) for r, s in zip(rows_l, ss)])
        rise = None
        if src is None:
            m_new = jnp.maximum(tmax, M_FLOOR)
            acc = pv(m_new)
        else:
            m_old = src[0][hd, :, cols]
            m_new = jnp.maximum(m_old, tmax)
            alpha = jnp.exp2(m_old - m_new)
            if lazy:
                acc = (src[1][hd, :, cols] + pv(m_old)) * alpha
                rise = tmax - m_old
            else:
                acc = alpha * src[1][hd, :, cols] + pv(m_new)
        dst[0][hd, :, cols] = m_new
        dst[1][hd, :, cols] = acc
        return rise

    if not unroll:
        def body(hd, carry):
            for part in parts:
                update(hd, part, [scores(hd, part, r) for r in row_slices(part)])
            return carry
        lax.fori_loop(0, nheads, body, 0)
        return None
    todo = [(hd, part) for hd in range(nheads) for part in parts]
    micro = [(hd, part, r) for hd, part in todo for r in row_slices(part)]
    ahead = (2 if lazy else 4) * len(row_slices(parts[0]))
    pending = [scores(*t) for t in micro[:ahead]]
    issued = len(pending)
    worst = {}
    for hd, part in todo:
        ss = [pending.pop(0) for _ in row_slices(part)]
        while len(pending) < ahead and issued < len(micro):
            pending.append(scores(*micro[issued]))
            issued += 1
        rise = update(hd, part, ss)
        if rise is not None:
            key = rise.shape
            worst[key] = rise if key not in worst else jnp.maximum(worst[key], rise)
    if not worst:
        return None
    tops = [jnp.max(w, axis=1, keepdims=True) for w in worst.values()]
    return functools.reduce(jnp.maximum, tops)


def _flash_finalize(o_ref, acc_ref):
    nv, rows, tq = acc_ref.shape
    dv = rows - SUM_ROWS
    nh = nv // o_ref.shape[0]
    for bb in range(o_ref.shape[0]):
        sl = slice(bb * nh, (bb + 1) * nh)
        o = acc_ref[sl, :dv, :] * (1.0 / acc_ref[sl, dv:dv + 1, :])
        o_ref[bb] = o.reshape(nh * dv, tq).T.astype(BF16)


def _flash_scratch(nv, dv, tq):
    return [pltpu.VMEM((2, nv, 1, tq), F32), pltpu.VMEM((2, nv, dv + SUM_ROWS, tq), F32)]


def _causal_pairs(nqt):
    pairs = [(i, j) for i in range(nqt) for j in [i] + list(range(i))]
    return (jnp.asarray([p[0] for p in pairs], jnp.int32), jnp.asarray([p[1] for p in pairs], jnp.int32))


def _lazy_flash_step(nheads, kqv, parts, m_ref, acc_ref, cur):
    src = (m_ref.at[cur], acc_ref.at[cur])
    dst = (m_ref.at[1 - cur], acc_ref.at[1 - cur])
    worst = _flash_heads(nheads, kqv, parts, src, dst, lazy=True)

    @pl.when(jnp.max(worst) > LAZY_SPAN)
    def _():
        _flash_heads(nheads, kqv, parts, src, dst, unroll=False)


def _causal_flash_step(i, j, nheads, kqv, tq, o_ref, m_ref, acc_ref, slot_ref):
    @pl.when(j == i)
    def _():
        _flash_heads(nheads, kqv, _tile_parts("diag", tq), None, (m_ref.at[0], acc_ref.at[0]))
        slot_ref[0] = 0

    @pl.when(j < i)
    def _():
        cur = slot_ref[0]
        _lazy_flash_step(nheads, kqv, _tile_parts("full_split", tq), m_ref, acc_ref, cur)
        slot_ref[0] = 1 - cur

    @pl.when((j == i - 1) | (i == 0))
    def _():
        _flash_finalize(o_ref, acc_ref.at[slot_ref[0]])


def _split_virtual(v, nheads):
    if isinstance(v, int):
        return v // nheads, v % nheads
    return lax.div(v, nheads), lax.rem(v, nheads)


def _mla_kernel(qi_ref, ki_ref, q_ref, k_ref, vt_ref, o_ref, m_ref, acc_ref, slot_ref, *, tq):
    i, j = qi_ref[pl.program_id(0)], ki_ref[pl.program_id(0)]

    def kqv(v):
        bb, hd = _split_virtual(v, MLA_HEADS)
        return k_ref[bb, hd], q_ref[bb, hd], vt_ref[bb, hd]

    _causal_flash_step(i, j, q_ref.shape[0] * MLA_HEADS, kqv, tq, o_ref, m_ref, acc_ref, slot_ref)


def _mla_attention(qm, km, vmt, B, S, tq):
    nqt = S // tq
    kern = functools.partial(_mla_kernel, tq=tq)
    qi, ki = _causal_pairs(nqt)
    return pl.pallas_call(
        kern,
        grid_spec=pltpu.PrefetchScalarGridSpec(
            num_scalar_prefetch=2,
            grid=(qi.shape[0],),
            in_specs=[pl.BlockSpec((B, MLA_HEADS, LANES, tq), lambda p, qi, ki: (0, 0, 0, qi[p])),
                      pl.BlockSpec((B, MLA_HEADS, tq, LANES), lambda p, qi, ki: (0, 0, ki[p], 0)),
                      pl.BlockSpec((B, MLA_HEADS, MLA_V + SUM_ROWS, tq), lambda p, qi, ki: (0, 0, 0, ki[p]))],
            out_specs=pl.BlockSpec((B, tq, MLA_HEADS * MLA_V), lambda p, qi, ki: (0, qi[p], 0)),
            scratch_shapes=_flash_scratch(B * MLA_HEADS, MLA_V, tq) + [pltpu.SMEM((1,), jnp.int32)]),
        out_shape=jax.ShapeDtypeStruct((B, S, MLA_HEADS * MLA_V), BF16),
        compiler_params=_cparams(("arbitrary",)),
        name="mla_attn",
    )(qi, ki, qm, km, vmt).reshape(B * S, MLA_HEADS * MLA_V)


def _slc_kernel(qi_ref, ki_ref, q_ref, bias_ref, k_ref, vt_ref, o_ref, qa_ref, m_ref, acc_ref,
                slot_ref, *, tq):
    i, j = qi_ref[pl.program_id(0)], ki_ref[pl.program_id(0)]
    nb = q_ref.shape[0]

    @pl.when(j == i)
    def _():
        for bb in range(nb):
            for hd in range(NSA_HEADS):
                qa_ref[bb * NSA_HEADS + hd] = q_ref[bb, hd]

    rows = pl.ds(pl.multiple_of(j * SEL_SLOTS, SEL_SLOTS), SEL_SLOTS)
    for bb in range(nb):
        for g in range(NSA_GROUPS):
            tile_bias = bias_ref[bb, g, rows, :]
            for hd in range(g * NSA_HPG, (g + 1) * NSA_HPG):
                qa_ref[bb * NSA_HEADS + hd, NSA_DIM:NSA_DIM + SEL_SLOTS, :] = tile_bias

    def kqv(v):
        bb, hd = _split_virtual(v, NSA_HEADS)
        g = hd // NSA_HPG if isinstance(hd, int) else lax.div(hd, NSA_HPG)
        return k_ref[bb, g], qa_ref[v], vt_ref[bb, g]

    _causal_flash_step(i, j, nb * NSA_HEADS, kqv, tq, o_ref, m_ref, acc_ref, slot_ref)


def _slc_attention(nq, bias, ksa, vst, B, S, tq):
    nqt = S // tq
    nrow = bias.shape[2]
    assert tq // SEL_LEN == SEL_SLOTS // 2
    kern = functools.partial(_slc_kernel, tq=tq)
    qi, ki = _causal_pairs(nqt)
    qmap = lambda p, qi, ki: (0, 0, 0, qi[p])
    return pl.pallas_call(
        kern,
        grid_spec=pltpu.PrefetchScalarGridSpec(
            num_scalar_prefetch=2,
            grid=(qi.shape[0],),
            in_specs=[pl.BlockSpec((B, NSA_HEADS, LANES, tq), qmap),
                      pl.BlockSpec((B, NSA_GROUPS, nrow, tq), qmap),
                      pl.BlockSpec((B, NSA_GROUPS, tq, LANES), lambda p, qi, ki: (0, 0, ki[p], 0)),
                      pl.BlockSpec((B, NSA_GROUPS, NSA_DIM + SUM_ROWS, tq), lambda p, qi, ki: (0, 0, 0, ki[p]))],
            out_specs=pl.BlockSpec((B, tq, NSA_HEADS * NSA_DIM), lambda p, qi, ki: (0, qi[p], 0)),
            scratch_shapes=[pltpu.VMEM((B * NSA_HEADS, LANES, tq), BF16)]
                           + _flash_scratch(B * NSA_HEADS, NSA_DIM, tq)
                           + [pltpu.SMEM((1,), jnp.int32)]),
        out_shape=jax.ShapeDtypeStruct((B, S, NSA_HEADS * NSA_DIM), BF16),
        compiler_params=_cparams(("arbitrary",)),
        name="slc_attn",
    )(qi, ki, nq, bias, ksa, vst).reshape(B * S, NSA_HEADS * NSA_DIM)


def _win_kernel(q_ref, k_ref, vt_ref, o_ref, m_ref, acc_ref, *, tq):
    i, j = pl.program_id(0), pl.program_id(1)

    def kqv(v):
        bb, hd = _split_virtual(v, NSA_HEADS)
        g = hd // NSA_HPG if isinstance(hd, int) else lax.div(hd, NSA_HPG)
        return k_ref[bb, g], q_ref[bb, hd][:NSA_DIM, :], vt_ref[bb, g]

    nv = q_ref.shape[0] * NSA_HEADS

    @pl.when(j == 0)
    def _():
        _flash_heads(nv, kqv, _tile_parts("diag", tq), None, (m_ref.at[0], acc_ref.at[0]))

    @pl.when((j == 1) & (i > 0))
    def _():
        _lazy_flash_step(nv, kqv, _tile_parts("prev", tq), m_ref, acc_ref, 0)

    @pl.when(j == 1)
    def _():
        _flash_finalize(o_ref, acc_ref.at[jnp.where(i > 0, 1, 0)])


def _win_attention(nq, kw, vwt, B, S, tq):
    nqt = S // tq
    kern = functools.partial(_win_kernel, tq=tq)
    kidx = lambda i, j: jnp.maximum(i - j, 0)
    return pl.pallas_call(
        kern,
        grid=(nqt, 2),
        in_specs=[pl.BlockSpec((B, NSA_HEADS, LANES, tq), lambda i, j: (0, 0, 0, i)),
                  pl.BlockSpec((B, NSA_GROUPS, tq, NSA_DIM), lambda i, j: (0, 0, kidx(i, j), 0)),
                  pl.BlockSpec((B, NSA_GROUPS, NSA_DIM + SUM_ROWS, tq), lambda i, j: (0, 0, 0, kidx(i, j)))],
        out_specs=pl.BlockSpec((B, tq, NSA_HEADS * NSA_DIM), lambda i, j: (0, i, 0)),
        out_shape=jax.ShapeDtypeStruct((B, S, NSA_HEADS * NSA_DIM), BF16),
        scratch_shapes=_flash_scratch(B * NSA_HEADS, NSA_DIM, tq),
        compiler_params=_cparams(("parallel", "arbitrary")),
        name="win_attn",
    )(nq, kw, vwt).reshape(B * S, NSA_HEADS * NSA_DIM)


def _causal_conv(u, carry, w_ref):
    assert CONV_WIDTH == 3
    row =lax.broadcasted_iota(jnp.int32, u.shape, 0)
    c1 = carry[SUBLANES - 1:SUBLANES]
    c2 = carry[SUBLANES - 2:SUBLANES - 1]
    u1 = jnp.where(row == 0, c1, pltpu.roll(u, 1, axis=0))
    u2 = jnp.where(row == 0, c2, jnp.where(row == 1, c1, pltpu.roll(u, 2, axis=0)))
    return w_ref[0:1] * u2 + w_ref[1:2] * u1 + w_ref[2:3] * u


def _merge_kernel(x_ref, g_ref, wgc_ref, cw_ref, ym_ref, oc_ref, os_ref, ow_ref, gl_ref,
                  wbc_ref, wbm_ref, wbn_ref, wo_ref, gp_ref, out_ref, carry_ref, *, tiles_per_seq):
    D = x_ref.shape[1]
    DC = wbc_ref.shape[0]

    @pl.when(pl.program_id(0) % tiles_per_seq == 0)
    def _():
        carry_ref[...] = jnp.zeros(carry_ref.shape, F32)

    x = x_ref[...]
    h = _rms(x, g_ref[...]).astype(BF16)
    p = _dot(h, wgc_ref[...])
    u = p[:, 3 * D + DC:3 * D + 2 * DC] * p[:, 3 * D + 2 * DC:3 * D + 3 * DC]
    y_conv = p[:, 3 * D:3 * D + DC] * _causal_conv(u, carry_ref[...], cw_ref)
    carry_ref[...] = u[u.shape[0] - SUBLANES:]
    gsig = jax.nn.sigmoid(gl_ref[...])
    first = lax.broadcasted_iota(jnp.int32, (x.shape[0], LANES), 1) < NSA_DIM

    def branch_gate(c):
        col = lambda hd: jnp.broadcast_to(gsig[:, 3 * hd + c:3 * hd + c + 1], (x.shape[0], LANES))
        return jnp.concatenate([jnp.where(first, col(hd), col(hd + 1)) for hd in range(0, NSA_HEADS, 2)],
                               axis=1)

    y_nsa = (branch_gate(0) * oc_ref[...].astype(F32) + branch_gate(1) * os_ref[...].astype(F32)
             + branch_gate(2) * ow_ref[...].astype(F32))
    merged = (jax.nn.sigmoid(p[:, 0:D]) * _dot(y_conv.astype(BF16), wbc_ref[...])
              + jax.nn.sigmoid(p[:, D:2 * D]) * _dot(ym_ref[...], wbm_ref[...])
              + jax.nn.sigmoid(p[:, 2 * D:3 * D]) * _dot(y_nsa.astype(BF16), wbn_ref[...]))
    out_ref[...] = x + _rms(_dot(merged.astype(BF16), wo_ref[...]), gp_ref[...])


def _merge(x2, g_pre, w_gc, conv_w, y_mla, o_cmp, o_slc, o_win, gl, wbc, wbm, wbn, wo, g_post,
           S, tm):
    T, D = x2.shape
    HD = y_mla.shape[1]
    row = lambda w: pl.BlockSpec((tm, w), lambda i: (i, 0))
    kern = functools.partial(_merge_kernel, tiles_per_seq=S // tm)
    return pl.pallas_call(
        kern,
        grid=(T // tm,),
        in_specs=[row(D), _const_spec((1, D)), _const_spec(w_gc.shape), _const_spec(conv_w.shape),
                  row(HD), row(HD), row(HD), row(HD), row(LANES),
                  _const_spec(wbc.shape), _const_spec(wbm.shape), _const_spec(wbn.shape),
                  _const_spec(wo.shape), _const_spec((1, D))],
        out_specs=row(D),
        out_shape=jax.ShapeDtypeStruct((T, D), F32),
        scratch_shapes=[pltpu.VMEM((SUBLANES, wbc.shape[0]), F32)],
        compiler_params=_cparams(("arbitrary",)),
        name="merge",
    )(x2, g_pre, w_gc, conv_w, y_mla, o_cmp, o_slc, o_win, gl, wbc, wbm, wbn, wo, g_post)


def _ffn_kernel(x_ref, g_ref, wup_ref, cw_ref, cb_ref, wdn_ref, gp_ref, out_ref, carry_ref,
                *, tiles_per_seq, chunk):
    F = wdn_ref.shape[0]

    @pl.when(pl.program_id(0) % tiles_per_seq == 0)
    def _():
        carry_ref[...] = jnp.zeros(carry_ref.shape, F32)

    x = x_ref[...]
    h = _rms(x, g_ref[...]).astype(BF16)
    acc = None
    for c0 in range(0, F, chunk):
        a = _dot(h, wup_ref[:, c0:c0 + chunk])
        b = _dot(h, wup_ref[:, F + c0:F + c0 + chunk])
        ac = _causal_conv(a, carry_ref[:, c0:c0 + chunk], cw_ref.at[:, c0:c0 + chunk])
        ac = ac + cb_ref[:, c0:c0 + chunk]
        carry_ref[:, c0:c0 + chunk] = a[a.shape[0] - SUBLANES:]
        d = _dot((jax.nn.gelu(ac) * b).astype(BF16), wdn_ref[c0:c0 + chunk, :])
        acc = d if acc is None else acc + d
    out_ref[...] = x + _rms(acc, gp_ref[...])


def _ffn(x2, g_pre, w_up, conv_w, conv_b, w_dn, g_post, S, tm, chunk):
    T, D = x2.shape
    F = w_dn.shape[0]
    row = pl.BlockSpec((tm, D), lambda i: (i, 0))
    kern = functools.partial(_ffn_kernel, tiles_per_seq=S // tm, chunk=chunk)
    return pl.pallas_call(
        kern,
        grid=(T // tm,),
        in_specs=[row, _const_spec((1, D)), _const_spec(w_up.shape), _const_spec(conv_w.shape),
                  _const_spec((1, F)), _const_spec(w_dn.shape), _const_spec((1, D))],
        out_specs=row,
        out_shape=jax.ShapeDtypeStruct((T, D), F32),
        scratch_shapes=[pltpu.VMEM((SUBLANES, F), F32)],
        compiler_params=_cparams(("arbitrary",)),
        name="ffn",
    )(x2, g_pre, w_up, conv_w, conv_b, w_dn, g_post)


def _overlap_matrix_t(S):
    nch = S // CMP_STRIDE
    n_cmp = (S - CMP_LEN) // CMP_STRIDE + 1
    cs = np.arange(nch)[None, :] * CMP_STRIDE
    ss = np.arange(S // SEL_LEN)[:, None] * SEL_LEN
    ov = (cs <= ss + SEL_LEN - 1) & (cs + CMP_LEN - 1 >= ss) & (np.arange(nch)[None, :] < n_cmp)
    return ov.astype(np.float32)


def _pad_rows(w, rows):
    return jnp.concatenate([w, jnp.zeros((rows - w.shape[0],) + w.shape[1:], w.dtype)], axis=0)


def kernel(x, positions, norm_mix_pre, norm_mix_post, w_in, conv_w, mla_q_norm, mla_w_uq, mla_kv_norm, mla_w_ukv, nsa_cmp_pos_k, nsa_cmp_pos_v, nsa_cmp_w_k, nsa_cmp_w_v, w_branch_conv, w_branch_mla, w_branch_nsa, w_out, norm_ffn_pre, norm_ffn_post, ffn_w_up, ffn_conv_w, ffn_conv_b, ffn_w_down):
    B, S, D = x.shape
    T = B * S
    depth = w_in.shape[0]
    DC = conv_w.shape[2]
    F = ffn_w_down.shape[1]
    QL = mla_q_norm.shape[1]
    KL = mla_kv_norm.shape[1]
    tq = 512
    assert S % tq == 0 and WINDOW == tq and N_SEL <= S // SEL_LEN <= LANES
    tm_proj, tm_merge, tm_ffn, ffn_chunk = 512, 512, 512, 1408
    assert F % ffn_chunk == 0 and ffn_chunk % LANES == 0

    x2 = x.reshape(T, D)
    pos2 = positions.reshape(1, T)
    rc = _rope_consts()
    ovt = jnp.asarray(_overlap_matrix_t(S), BF16)
    half = CMP_LEN // 2 * NSA_DIM
    o_att = 3 * D + 3 * DC

    for l in range(depth):
        w = w_in[l]
        o_kr = o_att + QL + KL
        z = lambda n: jnp.zeros((D, n), F32)
        w_att = jnp.concatenate(
            [w[:, o_att:o_kr], w[:, w.shape[1] - 3 * NSA_HEADS:], z(MLA_NOPE - 3 * NSA_HEADS),
             w[:, o_kr:o_kr + MLA_ROPE], z(LANES - MLA_NOPE - MLA_ROPE),
             w[:, o_kr + MLA_ROPE:o_kr + MLA_ROPE + NSA_HEADS * NSA_DIM + 6 * NSA_GROUPS * NSA_DIM]],
            axis=1).astype(BF16)
        dq = MLA_NOPE + MLA_ROPE
        w_uq = jnp.pad(mla_w_uq[l].reshape(QL, MLA_HEADS, dq),
                       ((0, 0), (0, 0), (0, LANES - dq))).reshape(QL, MLA_HEADS * LANES).astype(BF16)
        w_v_t = mla_w_ukv[l].reshape(KL, MLA_HEADS, MLA_NOPE + MLA_V)[:, :, MLA_NOPE:].reshape(
            KL, MLA_HEADS * MLA_V).T.astype(BF16)
        (qm, km, vmt, nq, kc_in, vc_in, ksa, vst, kw, vwt, gl) = _project(
            x2, pos2, norm_mix_pre[l][None], w_att, mla_q_norm[l][None], w_uq.T,
            mla_kv_norm[l][None], mla_w_ukv[l].astype(BF16), w_v_t, rc, S, tm_proj)

        wv = nsa_cmp_w_v[l].reshape(2, half, NSA_DIM)
        zv = jnp.zeros_like(wv)
        wv_pad = jnp.stack([jnp.concatenate([wv, zv], axis=2), jnp.concatenate([zv, wv], axis=2)])
        kc, vct = _compress(kc_in, vc_in,
                            nsa_cmp_pos_k[l].reshape(2, half), nsa_cmp_pos_v[l].reshape(2, half),
                            nsa_cmp_w_k[l].reshape(2, half, NSA_DIM).astype(BF16),
                            wv_pad.astype(BF16), B)

        y_mla = _mla_attention(qm, km, vmt, B, S, tq)
        o_cmp, bias = _cmp_select(nq, kc, vct, ovt, B, S, tq)
        o_slc = _slc_attention(nq, bias, ksa, vst, B, S, tq)
        o_win = _win_attention(nq, kw, vwt, B, S, tq)

        x2 = _merge(x2, norm_mix_pre[l][None], w[:, :o_att].astype(BF16),
                    _pad_rows(conv_w[l], SUBLANES), y_mla, o_cmp, o_slc, o_win, gl,
                    w_branch_conv[l].astype(BF16), w_branch_mla[l].astype(BF16),
                    w_branch_nsa[l].astype(BF16), w_out[l].astype(BF16), norm_mix_post[l][None],
                    S, tm_merge)
        x2 = _ffn(x2, norm_ffn_pre[l][None], ffn_w_up[l].astype(BF16),
                  _pad_rows(ffn_conv_w[l], SUBLANES), ffn_conv_b[l][None],
                  ffn_w_down[l].astype(BF16), norm_ffn_post[l][None], S, tm_ffn, ffn_chunk)
    return x2.reshape(B, S, D)
```
